```python
import jax, jax.numpy as jnp
from jax import lax
import numpy as np

D_MODEL = 1024
BATCH = 4
SEQ = 8192
DEPTH = 1
DEC_BATCH = 32
DEC_SEQ = 64
PAST_LEN = 1024

CHUNK = 64
PLE_DIM = 256
N_HEADS = 8
QK_NOPE = 64
QK_ROPE = 32
QK_DIM = QK_NOPE + QK_ROPE
V_DIM = 64
Q_LORA = 256
KV_LORA = 256
ROPE_THETA = 10000.0
ATTN_SCALE = QK_DIM ** -0.5
CONV_DIM = 512
CONV_W = 3
N_GROUPS = 4
EXPERTS_PER_GROUP = 8
N_EXPERTS = N_GROUPS * EXPERTS_PER_GROUP
TOP_K = 2
EXPERT_HIDDEN = 256
MOE_BLOCK = 128
Q_BLOCK = 128
EPS = 1e-6
IN_COLS = Q_LORA + KV_LORA + QK_ROPE + 3 * CONV_DIM + 2 * D_MODEL

kernel_name = "mla_shortconv_gated_hier_moe_stream_step"


def rmsnorm(x, g):
    xf = x.astype(jnp.float32)
    y = xf * lax.rsqrt(jnp.mean(xf * xf, axis=-1, keepdims=True) + EPS)
    return (y * g.astype(jnp.float32)).astype(x.dtype)


def rope_tables(pos):
    inv = 1.0 / (ROPE_THETA ** (jnp.arange(0, QK_ROPE, 2, dtype=jnp.float32) / QK_ROPE))
    ang = pos.astype(jnp.float32)[:, None] * inv[None, :]
    return jnp.cos(ang), jnp.sin(ang)


def apply_rope(x, cos, sin):
    xf = x.astype(jnp.float32)
    x1, x2 = jnp.split(xf, 2, axis=-1)
    return jnp.concatenate([x1 * cos - x2 * sin, x1 * sin + x2 * cos], axis=-1).astype(x.dtype)


def split_cols(z):
    sizes = (Q_LORA, KV_LORA, QK_ROPE, CONV_DIM, CONV_DIM, CONV_DIM, D_MODEL, D_MODEL)
    out, off = [], 0
    for n in sizes:
        out.append(z[..., off:off + n])
        off += n
    return out


def attn_probs(s, dtype):
    return jax.nn.softmax(s.astype(jnp.float32) * ATTN_SCALE, axis=-1).astype(dtype)


def attention_prompt(q_nope, q_pe, k_nope, k_pe, v):
    B, S = q_nope.shape[0], q_nope.shape[1]
    nq = S // Q_BLOCK
    qn_b = q_nope.reshape(B, nq, Q_BLOCK, N_HEADS, QK_NOPE).transpose(1, 0, 2, 3, 4)
    qp_b = q_pe.reshape(B, nq, Q_BLOCK, N_HEADS, QK_ROPE).transpose(1, 0, 2, 3, 4)
    key_chunk = jnp.arange(S) // CHUNK

    def one_block(args):
        qn, qp, i = args
        s = (jnp.einsum('bqhd,bkhd->bhqk', qn, k_nope)
             + jnp.einsum('bqhr,bkr->bhqk', qp, k_pe)).astype(jnp.float32)
        q_chunk = (i * Q_BLOCK + jnp.arange(Q_BLOCK)) // CHUNK
        mask = key_chunk[None, :] <= q_chunk[:, None]
        s = jnp.where(mask[None, None], s, -jnp.inf)
        p = attn_probs(s, v.dtype)
        return jnp.einsum('bhqk,bkhd->bqhd', p, v)

    o = lax.map(one_block, (qn_b, qp_b, jnp.arange(nq)))
    return o.transpose(1, 0, 2, 3, 4).reshape(B, S, N_HEADS * V_DIM)


def attention_sample(q_nope, q_pe, k_nope, k_pe, v):
    B, S = q_nope.shape[0], q_nope.shape[1]
    s = (jnp.einsum('bqhd,bkhd->bhqk', q_nope, k_nope)
         + jnp.einsum('bqhr,bkr->bhqk', q_pe, k_pe))
    p = attn_probs(s, v.dtype)
    return jnp.einsum('bhqk,bkhd->bqhd', p, v).reshape(B, S, N_HEADS * V_DIM)


def hier_moe(h, w_rg, b_rg, w_re, b_re, w1, w3, w2):
    N, D = h.shape
    g_logit = (h @ w_rg).astype(jnp.float32) + b_rg.astype(jnp.float32)
    g_prob = jax.nn.softmax(g_logit, axis=-1)
    g_idx = jnp.argmax(g_logit, axis=-1).astype(jnp.int32)
    g_p = jnp.take_along_axis(g_prob, g_idx[:, None], axis=-1)[:, 0]
    e_logit = ((h @ w_re).astype(jnp.float32) + b_re.astype(jnp.float32)).reshape(N, N_GROUPS, EXPERTS_PER_GROUP)
    e_logit = jnp.take_along_axis(e_logit, g_idx[:, None, None], axis=1)[:, 0]
    e_prob = jax.nn.softmax(e_logit, axis=-1)
    top_p, top_i = lax.top_k(e_prob, TOP_K)
    gate = g_p[:, None] * top_p / jnp.sum(top_p, axis=-1, keepdims=True)
    expert = g_idx[:, None] * EXPERTS_PER_GROUP + top_i.astype(jnp.int32)

    A = N * TOP_K
    e_flat = expert.reshape(A)
    w_flat = gate.reshape(A)
    tok = jnp.arange(A, dtype=jnp.int32) // TOP_K
    order = jnp.argsort(e_flat)
    e_sorted = e_flat[order]
    counts = jnp.zeros((N_EXPERTS,), jnp.int32).at[e_flat].add(1)
    padded = (counts + MOE_BLOCK - 1) // MOE_BLOCK * MOE_BLOCK
    start = jnp.cumsum(counts) - counts
    pend = jnp.cumsum(padded)
    pstart = pend - padded
    dest = pstart[e_sorted] + jnp.arange(A, dtype=jnp.int32) - start[e_sorted]
    NB = -(-A // MOE_BLOCK) + N_EXPERTS
    P = NB * MOE_BLOCK
    buf_tok = jnp.full((P,), N, jnp.int32).at[dest].set(tok[order])
    buf_w = jnp.zeros((P,), jnp.float32).at[dest].set(w_flat[order])
    blk_e = jnp.minimum(jnp.searchsorted(pend, jnp.arange(NB, dtype=jnp.int32) * MOE_BLOCK, side='right'),
                        N_EXPERTS - 1).astype(jnp.int32)
    h_pad = jnp.concatenate([h, jnp.zeros((1, D), h.dtype)], axis=0)
    xb = h_pad[buf_tok].reshape(NB, MOE_BLOCK, D)

    def expert_block(args):
        xblk, eid = args
        a = xblk @ w1[eid]
        b = xblk @ w3[eid]
        return (jax.nn.silu(a) * b) @ w2[eid]

    yb = lax.map(expert_block, (xb, blk_e)).reshape(P, D)
    y = jax.ops.segment_sum(yb * buf_w[:, None].astype(yb.dtype), buf_tok, num_segments=N + 1)[:N]
    return y.astype(h.dtype)


def layer(x, p, past_lat, past_kpe, past_conv,
          g_mix, w_in, g_cq, w_uq, g_qn, g_qr, g_ckv, w_ukv, g_kn, g_kr, w_oa,
          conv_w, w_oc, w_o, g_ffn, w_rg, b_rg, w_re, b_re, w1, w3, w2,
          g_ple, w_pg, w_ple):
    B, S, D = x.shape
    past_len = 0 if past_lat is None else past_lat.shape[1]
    pos = past_len + jnp.arange(S)
    h = rmsnorm(x, g_mix)
    c_q, c_kv, k_r, conv_b, conv_c, conv_x, gate_a, gate_c = split_cols(h @ w_in)

    cos, sin = rope_tables(pos)
    q = (rmsnorm(c_q, g_cq) @ w_uq).reshape(B, S, N_HEADS, QK_DIM)
    q_nope = rmsnorm(q[..., :QK_NOPE], g_qn)
    q_pe = apply_rope(rmsnorm(q[..., QK_NOPE:], g_qr), cos[:, None, :], sin[:, None, :])
    lat_new = rmsnorm(c_kv, g_ckv)
    kpe_new = apply_rope(rmsnorm(k_r, g_kr), cos, sin)
    if past_lat is None:
        lat_all, kpe_all = lat_new, kpe_new
    else:
        lat_all = jnp.concatenate([past_lat, lat_new], axis=1)
        kpe_all = jnp.concatenate([past_kpe, kpe_new], axis=1)
    T = lat_all.shape[1]
    kv = (lat_all @ w_ukv).reshape(B, T, N_HEADS, QK_NOPE + V_DIM)
    k_nope = rmsnorm(kv[..., :QK_NOPE], g_kn)
    v = kv[..., QK_NOPE:]
    if past_lat is None:
        attn = attention_prompt(q_nope, q_pe, k_nope, kpe_all, v)
    else:
        attn = attention_sample(q_nope, q_pe, k_nope, kpe_all, v)
    y_a = attn @ w_oa

    u = conv_c * conv_x
    if past_conv is None:
        up = jnp.pad(u, ((0, 0), (CONV_W - 1, 0), (0, 0)))
    else:
        up = jnp.concatenate([past_conv, u], axis=1)
    cv = sum(conv_w[k] * up[:, k:k + S] for k in range(CONV_W))
    y_c = (conv_b * cv) @ w_oc
    conv_new = up[:, -(CONV_W - 1):]

    m = jax.nn.sigmoid(gate_a) * y_a + jax.nn.sigmoid(gate_c) * y_c
    x = x + m @ w_o

    h2 = rmsnorm(x, g_ffn).reshape(B * S, D)
    x = x + hier_moe(h2, w_rg, b_rg, w_re, b_re, w1, w3, w2).reshape(B, S, D)

    x = x + jax.nn.sigmoid(rmsnorm(x, g_ple) @ w_pg) * (p @ w_ple)
    return x, lat_new, kpe_new, conv_new


def _normal(key, shape, scale):
    return jax.random.normal(key, shape, jnp.float32) * scale


def _gain(key, shape):
    return 1.0 + 0.05 * jax.random.normal(key, shape, jnp.float32)


def setup_inputs(seed: int = 0) -> dict:
    key = jax.random.key(seed)
    ks = jax.random.split(key, 32)
    L = DEPTH
    return {
        "x_prompt": _normal(ks[0], (BATCH, SEQ, D_MODEL), 1.0),
        "x_sample": _normal(ks[1], (DEC_BATCH, DEC_SEQ, D_MODEL), 1.0),
        "cache_kv_latent": _normal(ks[2], (L, DEC_BATCH, PAST_LEN, KV_LORA), 1.0),
        "cache_k_rope": _normal(ks[3], (L, DEC_BATCH, PAST_LEN, QK_ROPE), 1.0),
        "state_conv": _normal(ks[4], (L, DEC_BATCH, CONV_W - 1, CONV_DIM), 1.0),
        "p_prompt": _normal(ks[5], (L, BATCH, SEQ, PLE_DIM), 1.0),
        "p_sample": _normal(ks[6], (L, DEC_BATCH, DEC_SEQ, PLE_DIM), 1.0),
        "g_mix": _gain(ks[7], (L, D_MODEL)),
        "w_in": _normal(ks[8], (L, D_MODEL, IN_COLS), D_MODEL ** -0.5),
        "g_cq": _gain(ks[9], (L, Q_LORA)),
        "w_uq": _normal(ks[10], (L, Q_LORA, N_HEADS * QK_DIM), Q_LORA ** -0.5),
        "g_qn": _gain(ks[11], (L, QK_NOPE)),
        "g_qr": _gain(ks[12], (L, QK_ROPE)),
        "g_ckv": _gain(ks[13], (L, KV_LORA)),
        "w_ukv": _normal(ks[14], (L, KV_LORA, N_HEADS * (QK_NOPE + V_DIM)), KV_LORA ** -0.5),
        "g_kn": _gain(ks[15], (L, QK_NOPE)),
        "g_kr": _gain(ks[16], (L, QK_ROPE)),
        "w_oa": _normal(ks[17], (L, N_HEADS * V_DIM, D_MODEL), (N_HEADS * V_DIM) ** -0.5),
        "conv_w": _normal(ks[18], (L, CONV_W, CONV_DIM), CONV_W ** -0.5),
        "w_oc": _normal(ks[19], (L, CONV_DIM, D_MODEL), CONV_DIM ** -0.5),
        "w_o": _normal(ks[20], (L, D_MODEL, D_MODEL), D_MODEL ** -0.5),
        "g_ffn": _gain(ks[21], (L, D_MODEL)),
        "w_rg": _normal(ks[22], (L, D_MODEL, N_GROUPS), D_MODEL ** -0.5),
        "b_rg": _normal(ks[23], (L, N_GROUPS), 0.01),
        "w_re": _normal(ks[24], (L, D_MODEL, N_EXPERTS), D_MODEL ** -0.5),
        "b_re": _normal(ks[25], (L, N_EXPERTS), 0.01),
        "w1": _normal(ks[26], (L, N_EXPERTS, D_MODEL, EXPERT_HIDDEN), D_MODEL ** -0.5),
        "w3": _normal(ks[27], (L, N_EXPERTS, D_MODEL, EXPERT_HIDDEN), D_MODEL ** -0.5),
        "w2": _normal(ks[28], (L, N_EXPERTS, EXPERT_HIDDEN, D_MODEL), EXPERT_HIDDEN ** -0.5),
        "g_ple": _gain(ks[29], (L, D_MODEL)),
        "w_pg": _normal(ks[30], (L, D_MODEL, D_MODEL), D_MODEL ** -0.5),
        "w_ple": _normal(ks[31], (L, PLE_DIM, D_MODEL), PLE_DIM ** -0.5),
    }


def reference(x_prompt, x_sample, cache_kv_latent, cache_k_rope, state_conv, p_prompt, p_sample,
              g_mix, w_in, g_cq, w_uq, g_qn, g_qr, g_ckv, w_ukv, g_kn, g_kr, w_oa,
              conv_w, w_oc, w_o, g_ffn, w_rg, b_rg, w_re, b_re, w1, w3, w2,
              g_ple, w_pg, w_ple):
    xp, xs = x_prompt, x_sample
    lat_p, kpe_p, conv_p, lat_s, kpe_s, conv_s = [], [], [], [], [], []
    for i in range(DEPTH):
        wts = (g_mix[i], w_in[i], g_cq[i], w_uq[i], g_qn[i], g_qr[i], g_ckv[i], w_ukv[i], g_kn[i], g_kr[i],
               w_oa[i], conv_w[i], w_oc[i], w_o[i], g_ffn[i], w_rg[i], b_rg[i], w_re[i], b_re[i],
               w1[i], w3[i], w2[i], g_ple[i], w_pg[i], w_ple[i])
        xp, a, b, c = layer(xp, p_prompt[i], None, None, None, *wts)
        lat_p.append(a); kpe_p.append(b); conv_p.append(c)
        xs, a, b, c = layer(xs, p_sample[i], cache_kv_latent[i], cache_k_rope[i], state_conv[i], *wts)
        lat_s.append(a); kpe_s.append(b); conv_s.append(c)
    new_kv_latent_prompt = jnp.stack(lat_p, axis=0)
    new_k_rope_prompt = jnp.stack(kpe_p, axis=0)
    new_conv_prompt = jnp.stack(conv_p, axis=0)
    new_kv_latent_sample = jnp.stack(lat_s, axis=0)
    new_k_rope_sample = jnp.stack(kpe_s, axis=0)
    new_conv_sample = jnp.stack(conv_s, axis=0)
    return (xp, xs, new_kv_latent_prompt, new_k_rope_prompt, new_conv_prompt,
            new_kv_latent_sample, new_k_rope_sample, new_conv_sample)
```

```python
import functools
import math

import jax
import jax.numpy as jnp
from jax import lax
from jax.experimental import pallas as pl
from jax.experimental.pallas import tpu as pltpu

F32 = jnp.float32
BF16 = jnp.bfloat16

LANE = 128
CHUNK = 64
N_HEADS = 8
QK_NOPE = 64
QK_ROPE = 32
V_DIM = 64
Q_LORA = 256
KV_LORA = 256
CONV_DIM = 512
N_GROUPS = 4
EXPERTS_PER_GROUP = 8
N_EXPERTS = N_GROUPS * EXPERTS_PER_GROUP
EXPERT_HIDDEN = 256
ROPE_THETA = 10000.0
EPS = 1e-6
ATTN_SCALE = (QK_NOPE + QK_ROPE) ** -0.5
LOG2E = math.log2(math.e)
NEG_BIG = -1e30
MOE_ROWS = 128
N_PAIR_BUCKETS = N_GROUPS * (EXPERTS_PER_GROUP * (EXPERTS_PER_GROUP - 1) // 2)
VMEM_LIMIT = 56 * 1024 * 1024


def _rms_scale(x, n):
    return lax.rsqrt(jnp.sum(x * x, axis=-1, keepdims=True) * (1.0 / n) + EPS)


def _lane_iota(shape):
    return lax.broadcasted_iota(jnp.int32, shape, len(shape) - 1)


def _rope(t, rc, rs1, rs2):
    return t * rc + pltpu.roll(t, LANE - QK_ROPE // 2, 1) * rs1 + pltpu.roll(t, QK_ROPE // 2, 1) * rs2


def _pre_kernel(x_ref, gmix_ref, wa_ref, gcq_ref, gckv_ref, gkr_ref, wuq_ref, wuk_ref, wuv_ref,
                gq_ref, gk_ref, rc_ref, rs1_ref, rs2_ref,
                q_ref, k_ref, v_ref, lat_ref, kpe_ref):
    x = x_ref[0]
    h = x * _rms_scale(x, x.shape[-1]) * gmix_ref[...]
    z = jnp.dot(h.astype(BF16), wa_ref[...], preferred_element_type=F32)
    cq = z[:, :Q_LORA]
    ckv = z[:, Q_LORA:Q_LORA + KV_LORA]
    kr = z[:, Q_LORA + KV_LORA:]
    rc, rs1, rs2 = rc_ref[...], rs1_ref[...], rs2_ref[...]
    lane = _lane_iota(kr.shape)
    is_nope = lane < QK_NOPE

    cqn = cq * _rms_scale(cq, Q_LORA) * gcq_ref[...]
    lat = ckv * _rms_scale(ckv, KV_LORA) * gckv_ref[...]
    lat_ref[0] = lat
    krn = kr * _rms_scale(kr, QK_ROPE) * gkr_ref[...]
    kpe = _rope(krn, rc, rs1, rs2)
    kpe_ref[0] = kpe[:, QK_NOPE:QK_NOPE + QK_ROPE]

    latb = lat.astype(BF16)
    qf = jnp.dot(cqn.astype(BF16), wuq_ref[...], preferred_element_type=F32)
    kf = jnp.dot(latb, wuk_ref[...], preferred_element_type=F32)
    vf = jnp.dot(latb, wuv_ref[...], preferred_element_type=F32)
    gq, gk = gq_ref[...], gk_ref[...]
    ones_hi = jnp.where(is_nope, 0.0, 1.0)
    for hd in range(N_HEADS):
        sl = slice(hd * LANE, (hd + 1) * LANE)
        qs = qf[:, sl]
        sq = qs * qs
        ss_all = jnp.sum(sq, axis=-1, keepdims=True)
        ss_n = jnp.sum(jnp.where(is_nope, sq, 0.0), axis=-1, keepdims=True)
        r = jnp.where(is_nope, lax.rsqrt(ss_n * (1.0 / QK_NOPE) + EPS),
                      lax.rsqrt((ss_all - ss_n) * (1.0 / QK_ROPE) + EPS))
        q_ref[0, :, sl] = _rope(qs * r * gq, rc, rs1, rs2).astype(BF16)
        ks = kf[:, sl]
        k_ref[0, :, sl] = (ks * _rms_scale(ks, QK_NOPE) * gk + kpe).astype(BF16)
        v_ref[0, :, sl] = (vf[:, sl] + ones_hi).astype(BF16)


def _kvpast_kernel(lat_ref, kpe_ref, wuk_ref, wuv_ref, gk_ref, k_ref, v_ref):
    latb = lat_ref[0].astype(BF16)
    kf = jnp.dot(latb, wuk_ref[...], preferred_element_type=F32)
    vf = jnp.dot(latb, wuv_ref[...], preferred_element_type=F32)
    kpe = kpe_ref[0]
    lane = _lane_iota(kpe.shape)
    ones_hi = jnp.where(lane < QK_NOPE, 0.0, 1.0)
    gk = gk_ref[...]
    for hd in range(N_HEADS):
        sl = slice(hd * LANE, (hd + 1) * LANE)
        ks = kf[:, sl]
        k_ref[0, :, sl] = (ks * _rms_scale(ks, QK_NOPE) * gk + kpe).astype(BF16)
        v_ref[0, :, sl] = (vf[:, sl] + ones_hi).astype(BF16)


def _softmax_step(q, k, v, m, acc, mask):
    s = lax.dot_general(q, k, (((1,), (1,)), ((), ())), preferred_element_type=F32)
    if mask is not None:
        s = jnp.where(mask, s, NEG_BIG)
    m_new = jnp.maximum(m, jnp.max(s, axis=-1, keepdims=True))
    alpha = jnp.exp2(m - m_new)
    p = jnp.exp2(s - m_new)
    acc = acc * alpha + jnp.dot(p.astype(BF16), v, preferred_element_type=F32)
    return m_new, acc


def _finish_pair(accs):
    outs = [a / pltpu.roll(a, V_DIM, 1) for a in accs]
    lane = _lane_iota(outs[0].shape)
    return jnp.where(lane < V_DIM, outs[0], pltpu.roll(outs[1], V_DIM, 1))


def _attn_prompt_kernel(q_ref, k_ref, v_ref, o_ref, *, tq):
    qi = pl.program_id(2)
    row = lax.broadcasted_iota(jnp.int32, (tq, tq), 0)
    col = lax.broadcasted_iota(jnp.int32, (tq, tq), 1)
    diag_mask = (col // CHUNK) <= (row // CHUNK)
    accs = []
    for hh in range(2):
        sl = slice(hh * LANE, (hh + 1) * LANE)
        q = q_ref[0, :, sl]

        def body(ki, carry, sl=sl, q=q):
            start = pl.multiple_of(ki * tq, tq)
            return _softmax_step(q, k_ref[0, pl.ds(start, tq), sl], v_ref[0, pl.ds(start, tq), sl],
                                 carry[0], carry[1], None)

        m0 = jnp.full((tq, 1), NEG_BIG, F32)
        acc0 = jnp.zeros((tq, LANE), F32)
        m, acc = lax.fori_loop(0, qi, body, (m0, acc0))
        start = pl.multiple_of(qi * tq, tq)
        m, acc = _softmax_step(q, k_ref[0, pl.ds(start, tq), sl], v_ref[0, pl.ds(start, tq), sl],
                               m, acc, diag_mask)
        accs.append(acc)
    o_ref[0] = _finish_pair(accs).astype(o_ref.dtype)


def _attn_sample_kernel(q_ref, kp_ref, vp_ref, kn_ref, vn_ref, o_ref):
    accs = []
    for hh in range(2):
        sl = slice(hh * LANE, (hh + 1) * LANE)
        q = q_ref[0, :, sl]
        m0 = jnp.full((q.shape[0], 1), NEG_BIG, F32)
        acc0 = jnp.zeros((q.shape[0], LANE), F32)
        m, acc = _softmax_step(q, kp_ref[0, :, sl], vp_ref[0, :, sl], m0, acc0, None)
        m, acc = _softmax_step(q, kn_ref[0, :, sl], vn_ref[0, :, sl], m, acc, None)
        accs.append(acc)
    o_ref[0] = _finish_pair(accs).astype(o_ref.dtype)


def _post_kernel(x_ref, attn_ref, cinit_ref, gmix_ref, wb_ref, convw_ref, woa_ref, woc_ref, wo_ref,
                 gffn_ref, wr_ref, br_ref,
                 x1_ref, route_ref, cnew_ref, carry_ref):
    si = pl.program_id(1)
    tm = x_ref.shape[1]

    @pl.when(si == 0)
    def _():
        carry_ref[...] = cinit_ref[0]

    x = x_ref[0]
    h = x * _rms_scale(x, x.shape[-1]) * gmix_ref[...]
    z = jnp.dot(h.astype(BF16), wb_ref[...], preferred_element_type=F32)
    conv_b = z[:, :CONV_DIM]
    u = z[:, CONV_DIM:2 * CONV_DIM] * z[:, 2 * CONV_DIM:3 * CONV_DIM]
    d = x.shape[-1]
    gate_a = z[:, 3 * CONV_DIM:3 * CONV_DIM + d]
    gate_c = z[:, 3 * CONV_DIM + d:]

    carry = carry_ref[...]
    c1 = carry[7:8, :]
    c2 = carry[6:7, :]
    row = lax.broadcasted_iota(jnp.int32, u.shape, 0)
    u_m1 = jnp.where(row == 0, c1, pltpu.roll(u, 1, 0))
    u_m2 = jnp.where(row == 0, c2, jnp.where(row == 1, c1, pltpu.roll(u, 2, 0)))
    cw = convw_ref[...]
    cv = cw[0:1, :] * u_m2 + cw[1:2, :] * u_m1 + cw[2:3, :] * u
    carry_ref[...] = u[tm - 8:, :]
    cnew_ref[0] = u[tm - 2:, :]

    y_c = jnp.dot((conv_b * cv).astype(BF16), woc_ref[...], preferred_element_type=F32)
    y_a = jnp.dot(attn_ref[0], woa_ref[...], preferred_element_type=F32)
    mrg = jax.nn.sigmoid(gate_a) * y_a + jax.nn.sigmoid(gate_c) * y_c
    x1 = x + jnp.dot(mrg.astype(BF16), wo_ref[...], preferred_element_type=F32)
    x1_ref[0] = x1

    h2 = x1 * _rms_scale(x1, d) * gffn_ref[...]
    logit = jnp.dot(h2, wr_ref[...], preferred_element_type=F32, precision=lax.Precision.HIGHEST) + br_ref[...]
    lane = _lane_iota(logit.shape)
    lane_f = lane.astype(F32)
    big = float(LANE)
    is_g = lane < N_GROUPS
    gl = jnp.where(is_g, logit, NEG_BIG)
    gmax = jnp.max(gl, axis=-1, keepdims=True)
    g_idx = jnp.min(jnp.where(is_g & (gl == gmax), lane_f, big), axis=-1, keepdims=True)
    g_p = 1.0 / jnp.sum(jnp.where(is_g, jnp.exp(gl - gmax), 0.0), axis=-1, keepdims=True)
    lo = N_GROUPS + EXPERTS_PER_GROUP * g_idx
    in_grp = (lane_f >= lo) & (lane_f < lo + EXPERTS_PER_GROUP)
    el = jnp.where(in_grp, logit, NEG_BIG)
    e1max = jnp.max(el, axis=-1, keepdims=True)
    i1 = jnp.min(jnp.where(in_grp & (el == e1max), lane_f, big), axis=-1, keepdims=True)
    rest = in_grp & (lane_f != i1)
    el2 = jnp.where(rest, logit, NEG_BIG)
    e2max = jnp.max(el2, axis=-1, keepdims=True)
    i2 = jnp.min(jnp.where(rest & (el2 == e2max), lane_f, big), axis=-1, keepdims=True)
    t = jnp.exp(e2max - e1max)
    w1 = g_p / (1.0 + t)
    w2 = g_p * t / (1.0 + t)
    route_ref[0] = jnp.where(lane == 0, i1 - N_GROUPS,
                             jnp.where(lane == 1, i2 - N_GROUPS,
                                       jnp.where(lane == 2, w1, jnp.where(lane == 3, w2, 0.0))))


def _moe_kernel(lo_ref, hi_ref, cnt_ref, tok_ref,
                x1_hbm, gate_ref, gffn_ref, w13lo_ref, w13hi_ref, w2lo_ref, w2hi_ref,
                x2_hbm, xbuf, obuf, gsem, ssem):
    nb = pl.program_id(0)
    cnt = cnt_ref[nb]
    base = nb * MOE_ROWS
    nchunk = xbuf.shape[0] // MOE_ROWS

    @pl.when(nb == 0)
    def _():
        xbuf[...] = jnp.zeros_like(xbuf)

    def gather(i):
        return pltpu.make_async_copy(x1_hbm.at[tok_ref[base + i]], xbuf.at[pl.ds(i * nchunk, nchunk)], gsem)

    def scatter(i):
        return pltpu.make_async_copy(obuf.at[pl.ds(i * nchunk, nchunk)], x2_hbm.at[tok_ref[base + i]], ssem)

    @pl.when(cnt > 0)
    def _():
        def start_g(i, c):
            gather(i).start()
            return c
        lax.fori_loop(0, cnt, start_g, 0)

        def wait_g(i, c):
            gather(i).wait()
            return c
        lax.fori_loop(0, cnt, wait_g, 0)

        xg = jnp.concatenate([xbuf[pl.ds(c, MOE_ROWS, stride=nchunk), :] for c in range(nchunk)], axis=1)
        h = (xg * _rms_scale(xg, xg.shape[-1]) * gffn_ref[...]).astype(BF16)
        g = gate_ref[...]
        y = xg
        for w13_ref, w2_ref, col in ((w13lo_ref, w2lo_ref, 0), (w13hi_ref, w2hi_ref, 1)):
            ab = jnp.dot(h, w13_ref[0], preferred_element_type=F32)
            hid = jax.nn.silu(ab[:, :EXPERT_HIDDEN]) * ab[:, EXPERT_HIDDEN:]
            y = y + g[:, col:col + 1] * jnp.dot(hid.astype(BF16), w2_ref[0], preferred_element_type=F32)
        for c in range(nchunk):
            obuf[pl.ds(c, MOE_ROWS, stride=nchunk), :] = y[:, c * LANE:(c + 1) * LANE]

        def start_s(i, c):
            scatter(i).start()
            return c
        lax.fori_loop(0, cnt, start_s, 0)

        def wait_s(i, c):
            scatter(i).wait()
            return c
        lax.fori_loop(0, cnt, wait_s, 0)


def _ple_kernel(x_ref, p_ref, gple_ref, wpg_ref, wple_ref, o_ref):
    x = x_ref[...]
    hp = (x * _rms_scale(x, x.shape[-1]) * gple_ref[...]).astype(BF16)
    gate = jax.nn.sigmoid(jnp.dot(hp, wpg_ref[...], preferred_element_type=F32))
    emb = jnp.dot(p_ref[...].astype(BF16), wple_ref[...], preferred_element_type=F32)
    o_ref[...] = x + gate * emb


def _const(shape):
    nd = len(shape)
    return pl.BlockSpec(shape, lambda *_: (0,) * nd)


def _head_slab_cols(w, width, offset=0):
    k = w.shape[0]
    w = w.reshape(k, N_HEADS, width)
    w = jnp.pad(w, ((0, 0), (0, 0), (offset, LANE - width - offset)))
    return w.reshape(k, N_HEADS * LANE)


def _prep_weights(g_mix, w_in, g_cq, w_uq, g_qn, g_qr, g_ckv, w_ukv, g_kn, g_kr, w_oa,
                  conv_w, w_oc, w_o, g_ffn, w_rg, b_rg, w_re, b_re, w1, w3, w2, g_ple, w_pg, w_ple):
    d = w_in.shape[0]
    n_mla = Q_LORA + KV_LORA
    kr_cols = jnp.pad(w_in[:, n_mla:n_mla + QK_ROPE], ((0, 0), (QK_NOPE, LANE - QK_NOPE - QK_ROPE)))
    w = {}
    w["wa"] = jnp.concatenate([w_in[:, :n_mla], kr_cols], axis=1).astype(BF16)
    w["wb"] = w_in[:, n_mla + QK_ROPE:].astype(BF16)
    w["wuq"] = _head_slab_cols(w_uq, QK_NOPE + QK_ROPE).astype(BF16)
    ukv = w_ukv.reshape(KV_LORA, N_HEADS, QK_NOPE + V_DIM)
    w["wuk"] = _head_slab_cols(ukv[:, :, :QK_NOPE].reshape(KV_LORA, -1), QK_NOPE).astype(BF16)
    w["wuv"] = _head_slab_cols(ukv[:, :, QK_NOPE:].reshape(KV_LORA, -1), V_DIM).astype(BF16)
    pad_hi = LANE - QK_NOPE - QK_ROPE
    w["gq"] = (jnp.pad(jnp.concatenate([g_qn, g_qr]), (0, pad_hi)) * (ATTN_SCALE * LOG2E))[None]
    w["gk"] = jnp.pad(g_kn, (0, LANE - QK_NOPE))[None]
    w["gkr"] = jnp.pad(g_kr, (QK_NOPE, pad_hi))[None]
    w["gmix"], w["gcq"], w["gckv"] = g_mix[None], g_cq[None], g_ckv[None]
    w["gffn"], w["gple"] = g_ffn[None], g_ple[None]
    w["convw"] = jnp.pad(conv_w, ((0, 8 - conv_w.shape[0]), (0, 0)))
    w["woa"], w["woc"], w["wo"] = w_oa.astype(BF16), w_oc.astype(BF16), w_o.astype(BF16)
    n_r = N_GROUPS + N_EXPERTS
    w["wr"] = jnp.pad(jnp.concatenate([w_rg, w_re], axis=1), ((0, 0), (0, LANE - n_r)))
    w["br"] = jnp.pad(jnp.concatenate([b_rg, b_re]), (0, LANE - n_r))[None]
    w["w13"] = jnp.concatenate([w1, w3], axis=2).astype(BF16)
    w["w2"] = w2.astype(BF16)
    w["wpg"], w["wple"] = w_pg.astype(BF16), w_ple.astype(BF16)
    return w


def _rope_slabs(pos):
    inv = 1.0 / (ROPE_THETA ** (jnp.arange(0, QK_ROPE, 2, dtype=F32) / QK_ROPE))
    ang = pos.astype(F32)[:, None] * inv[None, :]
    cos, sin = jnp.cos(ang), jnp.sin(ang)
    n = pos.shape[0]
    half = QK_ROPE // 2
    z = lambda k: jnp.zeros((n, k), F32)
    pad_hi = LANE - QK_NOPE - QK_ROPE
    rc = jnp.concatenate([jnp.ones((n, QK_NOPE), F32), cos, cos, z(pad_hi)], axis=1)
    rs1 = jnp.concatenate([z(QK_NOPE), -sin, z(half), z(pad_hi)], axis=1)
    rs2 = jnp.concatenate([z(QK_NOPE), z(half), sin, z(pad_hi)], axis=1)
    return rc, rs1, rs2


def _params(sem):
    return pltpu.CompilerParams(dimension_semantics=sem, vmem_limit_bytes=VMEM_LIMIT)


def _mla_pre(x, w, rope, tm):
    b, s, d = x.shape
    hw = N_HEADS * LANE
    tok = lambda width: pl.BlockSpec((1, tm, width), lambda i, j: (i, j, 0))
    rope_spec = pl.BlockSpec((tm, LANE), lambda i, j: (j, 0))
    consts = [w["gmix"], w["wa"], w["gcq"], w["gckv"], w["gkr"], w["wuq"], w["wuk"], w["wuv"], w["gq"], w["gk"]]
    return pl.pallas_call(
        _pre_kernel,
        grid=(b, s // tm),
        in_specs=[tok(d)] + [_const(c.shape) for c in consts] + [rope_spec] * 3,
        out_specs=[tok(hw), tok(hw), tok(hw), tok(KV_LORA), tok(QK_ROPE)],
        out_shape=[jax.ShapeDtypeStruct((b, s, hw), BF16)] * 3
        + [jax.ShapeDtypeStruct((b, s, KV_LORA), F32), jax.ShapeDtypeStruct((b, s, QK_ROPE), F32)],
        compiler_params=_params(("parallel", "parallel")),
        name="mla_pre",
    )(x, *consts, *rope)


def _kv_past(past_lat, past_kpe, w, tm):
    b, t, _ = past_lat.shape
    hw = N_HEADS * LANE
    kpe_slab = jnp.pad(past_kpe, ((0, 0), (0, 0), (QK_NOPE, LANE - QK_NOPE - QK_ROPE)))
    tok = lambda width: pl.BlockSpec((1, tm, width), lambda i, j: (i, j, 0))
    consts = [w["wuk"], w["wuv"], w["gk"]]
    return pl.pallas_call(
        _kvpast_kernel,
        grid=(b, t // tm),
        in_specs=[tok(KV_LORA), tok(LANE)] + [_const(c.shape) for c in consts],
        out_specs=[tok(hw), tok(hw)],
        out_shape=[jax.ShapeDtypeStruct((b, t, hw), BF16)] * 2,
        compiler_params=_params(("parallel", "parallel")),
        name="kv_past",
    )(past_lat, kpe_slab, *consts)


def _attn_prompt(q, k, v, tq):
    b, s, _ = q.shape
    pair = 2 * LANE
    return pl.pallas_call(
        functools.partial(_attn_prompt_kernel, tq=tq),
        grid=(b, N_HEADS // 2, s // tq),
        in_specs=[pl.BlockSpec((1, tq, pair), lambda i, h, j: (i, j, h)),
                  pl.BlockSpec((1, s, pair), lambda i, h, j: (i, 0, h)),
                  pl.BlockSpec((1, s, pair), lambda i, h, j: (i, 0, h))],
        out_specs=pl.BlockSpec((1, tq, LANE), lambda i, h, j: (i, j, h)),
        out_shape=jax.ShapeDtypeStruct((b, s, N_HEADS * V_DIM), BF16),
        compiler_params=_params(("parallel", "parallel", "arbitrary")),
        name="attn_prompt",
    )(q, k, v)


def _attn_sample(q, kp, vp, kn, vn):
    b, s, _ = q.shape
    t = kp.shape[1]
    pair = 2 * LANE
    blk = lambda rows: pl.BlockSpec((1, rows, pair), lambda i, h: (i, 0, h))
    return pl.pallas_call(
        _attn_sample_kernel,
        grid=(b, N_HEADS // 2),
        in_specs=[blk(s), blk(t), blk(t), blk(s), blk(s)],
        out_specs=pl.BlockSpec((1, s, LANE), lambda i, h: (i, 0, h)),
        out_shape=jax.ShapeDtypeStruct((b, s, N_HEADS * V_DIM), BF16),
        compiler_params=_params(("parallel", "parallel")),
        name="attn_sample",
    )(q, kp, vp, kn, vn)


def _post(x, attn, conv_init, w, tm):
    b, s, d = x.shape
    tok = lambda width: pl.BlockSpec((1, tm, width), lambda i, j: (i, j, 0))
    per_b = lambda rows: pl.BlockSpec((1, rows, CONV_DIM), lambda i, j: (i, 0, 0))
    consts = [w["gmix"], w["wb"], w["convw"], w["woa"], w["woc"], w["wo"], w["gffn"], w["wr"], w["br"]]
    return pl.pallas_call(
        _post_kernel,
        grid=(b, s // tm),
        in_specs=[tok(d), tok(N_HEADS * V_DIM), per_b(8)] + [_const(c.shape) for c in consts],
        out_specs=[tok(d), tok(LANE), per_b(2)],
        out_shape=[jax.ShapeDtypeStruct((b, s, d), F32), jax.ShapeDtypeStruct((b, s, LANE), F32),
                   jax.ShapeDtypeStruct((b, 2, CONV_DIM), F32)],
        scratch_shapes=[pltpu.VMEM((8, CONV_DIM), F32)],
        compiler_params=_params(("parallel", "arbitrary")),
        name="post",
    )(x, attn, conv_init, *consts)


def _route_tables(route, n):
    e1 = route[:, 0].astype(jnp.int32)
    e2 = route[:, 1].astype(jnp.int32)
    first_lo = e1 < e2
    lo = jnp.where(first_lo, e1, e2)
    hi = jnp.where(first_lo, e2, e1)
    g_lo = jnp.where(first_lo, route[:, 2], route[:, 3])
    g_hi = jnp.where(first_lo, route[:, 3], route[:, 2])
    n_keys = N_EXPERTS * EXPERTS_PER_GROUP
    key = lo * EXPERTS_PER_GROUP + hi % EXPERTS_PER_GROUP
    order = jnp.argsort(key).astype(jnp.int32)
    key_s = key[order]
    counts = jnp.zeros((n_keys,), jnp.int32).at[key].add(1)
    padded = (counts + MOE_ROWS - 1) // MOE_ROWS * MOE_ROWS
    start = jnp.cumsum(counts) - counts
    pend = jnp.cumsum(padded)
    pstart = pend - padded
    dest = pstart[key_s] + jnp.arange(n, dtype=jnp.int32) - start[key_s]
    nblk = n // MOE_ROWS + N_PAIR_BUCKETS
    npos = nblk * MOE_ROWS
    tok = jnp.zeros((npos,), jnp.int32).at[dest].set(order)
    gates = jnp.zeros((npos, 2), F32).at[dest].set(jnp.stack([g_lo, g_hi], axis=1)[order])
    blk_start = jnp.arange(nblk, dtype=jnp.int32) * MOE_ROWS
    blk_key = jnp.minimum(jnp.searchsorted(pend, blk_start, side="right"), n_keys - 1).astype(jnp.int32)
    blk_cnt = jnp.clip(counts[blk_key] - (blk_start - pstart[blk_key]), 0, MOE_ROWS).astype(jnp.int32)
    blk_lo = blk_key // EXPERTS_PER_GROUP
    blk_hi = blk_lo // EXPERTS_PER_GROUP * EXPERTS_PER_GROUP + blk_key % EXPERTS_PER_GROUP
    return blk_lo, blk_hi, blk_cnt, tok, gates


def _moe(x1, route, w):
    n, d = x1.shape
    nchunk = d // LANE
    blk_lo, blk_hi, blk_cnt, tok, gates = _route_tables(route, n)
    nblk = blk_lo.shape[0]
    x1r = x1.reshape(n, nchunk, LANE)
    w13_spec = lambda ref_idx: pl.BlockSpec((1, d, 2 * EXPERT_HIDDEN),
                                            lambda i, lo, hi, cnt, tok: ((lo, hi)[ref_idx][i], 0, 0))
    w2_spec = lambda ref_idx: pl.BlockSpec((1, EXPERT_HIDDEN, d),
                                           lambda i, lo, hi, cnt, tok: ((lo, hi)[ref_idx][i], 0, 0))
    grid_spec = pltpu.PrefetchScalarGridSpec(
        num_scalar_prefetch=4,
        grid=(nblk,),
        in_specs=[pl.BlockSpec(memory_space=pl.ANY),
                  pl.BlockSpec((MOE_ROWS, 2), lambda i, *_: (i, 0)),
                  pl.BlockSpec((1, d), lambda i, *_: (0, 0)),
                  w13_spec(0), w13_spec(1), w2_spec(0), w2_spec(1)],
        out_specs=pl.BlockSpec(memory_space=pl.ANY),
        scratch_shapes=[pltpu.VMEM((MOE_ROWS * nchunk, LANE), F32), pltpu.VMEM((MOE_ROWS * nchunk, LANE), F32),
                        pltpu.SemaphoreType.DMA, pltpu.SemaphoreType.DMA],
    )
    x2 = pl.pallas_call(
        _moe_kernel,
        grid_spec=grid_spec,
        out_shape=jax.ShapeDtypeStruct((n, nchunk, LANE), F32),
        compiler_params=_params(("arbitrary",)),
        name="moe",
    )(blk_lo, blk_hi, blk_cnt, tok, x1r, gates, w["gffn"], w["w13"], w["w13"], w["w2"], w["w2"])
    return x2.reshape(n, d)


def _ple(x2, p, w, tm):
    n, d = x2.shape
    pd = p.shape[1]
    consts = [w["gple"], w["wpg"], w["wple"]]
    return pl.pallas_call(
        _ple_kernel,
        grid=(n // tm,),
        in_specs=[pl.BlockSpec((tm, d), lambda i: (i, 0)), pl.BlockSpec((tm, pd), lambda i: (i, 0))]
        + [_const(c.shape) for c in consts],
        out_specs=pl.BlockSpec((tm, d), lambda i: (i, 0)),
        out_shape=jax.ShapeDtypeStruct((n, d), F32),
        compiler_params=_params(("parallel",)),
        name="ple",
    )(x2, p, *consts)


def _stream(x, p, past_lat, past_kpe, past_conv, w):
    b, s, d = x.shape
    tm = min(512, s)
    past_len = 0 if past_lat is None else past_lat.shape[1]
    rope = _rope_slabs(past_len + jnp.arange(s))
    q, k, v, lat_new, kpe_new = _mla_pre(x, w, rope, tm)
    if past_lat is None:
        attn = _attn_prompt(q, k, v, min(256, s))
        conv_init = jnp.zeros((b, 8, CONV_DIM), F32)
    else:
        kp, vp = _kv_past(past_lat, past_kpe, w, min(512, past_len))
        attn = _attn_sample(q, kp, vp, k, v)
        conv_init = jnp.pad(past_conv, ((0, 0), (8 - past_conv.shape[1], 0), (0, 0)))
    x1, route, conv_new = _post(x, attn, conv_init, w, tm)
    n = b * s
    x2 = _moe(x1.reshape(n, d), route.reshape(n, LANE), w)
    y = _ple(x2, p.reshape(n, -1), w, min(512, n))
    return y.reshape(b, s, d), lat_new, kpe_new, conv_new


def kernel(x_prompt, x_sample, cache_kv_latent, cache_k_rope, state_conv, p_prompt, p_sample,
           g_mix, w_in, g_cq, w_uq, g_qn, g_qr, g_ckv, w_ukv, g_kn, g_kr, w_oa,
           conv_w, w_oc, w_o, g_ffn, w_rg, b_rg, w_re, b_re, w1, w3, w2, g_ple, w_pg, w_ple):
    depth = g_mix.shape[0]
    xp, xs = x_prompt, x_sample
    outs = [[] for _ in range(6)]
    for i in range(depth):
        w = _prep_weights(g_mix[i], w_in[i], g_cq[i], w_uq[i], g_qn[i], g_qr[i], g_ckv[i], w_ukv[i],
                          g_kn[i], g_kr[i], w_oa[i], conv_w[i], w_oc[i], w_o[i], g_ffn[i], w_rg[i], b_rg[i],
                          w_re[i], b_re[i], w1[i], w3[i], w2[i], g_ple[i], w_pg[i], w_ple[i])
        xp, a, bb, c = _stream(xp, p_prompt[i], None, None, None, w)
        outs[0].append(a); outs[1].append(bb); outs[2].append(c)
        xs, a, bb, c = _stream(xs, p_sample[i], cache_kv_latent[i], cache_k_rope[i], state_conv[i], w)
        outs[3].append(a); outs[4].append(bb); outs[5].append(c)
    return (xp, xs) + tuple(jnp.stack(o, axis=0) for o in outs)
```

```python
import functools
import math

import jax
import jax.numpy as jnp
from jax import lax
from jax.experimental import pallas as pl
from jax.experimental.pallas import tpu as pltpu

F32 = jnp.float32
BF16 = jnp.bfloat16

LANE = 128
CHUNK = 64
N_HEADS = 8
QK_NOPE = 64
QK_ROPE = 32
V_DIM = 64
Q_LORA = 256
KV_LORA = 256
CONV_DIM = 512
N_GROUPS = 4
EXPERTS_PER_GROUP = 8
N_EXPERTS = N_GROUPS * EXPERTS_PER_GROUP
EXPERT_HIDDEN = 256
ROPE_THETA = 10000.0
EPS = 1e-6
ATTN_SCALE = (QK_NOPE + QK_ROPE) ** -0.5
LOG2E = math.log2(math.e)
NEG_BIG = -1e30
MOE_ROWS = 128
N_PAIR_BUCKETS = N_GROUPS * (EXPERTS_PER_GROUP * (EXPERTS_PER_GROUP - 1) // 2)
VMEM_LIMIT = 56 * 1024 * 1024


def _rms_scale(x, n):
    return lax.rsqrt(jnp.sum(x * x, axis=-1, keepdims=True) * (1.0 / n) + EPS)


def _lane_iota(shape):
    return lax.broadcasted_iota(jnp.int32, shape, len(shape) - 1)


def _rope(t, rc, rs1, rs2):
    return t * rc + pltpu.roll(t, LANE - QK_ROPE // 2, 1) * rs1 + pltpu.roll(t, QK_ROPE // 2, 1) * rs2


def _pre_kernel(x_ref, gmix_ref, wa_ref, gcq_ref, gckv_ref, gkr_ref, wuq_ref, wuk_ref, wuv_ref,
                gq_ref, gk_ref, rc_ref, rs1_ref, rs2_ref,
                q_ref, k_ref, v_ref, lat_ref, kpe_ref):
    x = x_ref[0]
    h = x * _rms_scale(x, x.shape[-1]) * gmix_ref[...]
    z = jnp.dot(h.astype(BF16), wa_ref[...], preferred_element_type=F32)
    cq = z[:, :Q_LORA]
    ckv = z[:, Q_LORA:Q_LORA + KV_LORA]
    kr = z[:, Q_LORA + KV_LORA:]
    rc, rs1, rs2 = rc_ref[...], rs1_ref[...], rs2_ref[...]
    lane = _lane_iota(kr.shape)
    is_nope = lane < QK_NOPE

    cqn = cq * _rms_scale(cq, Q_LORA) * gcq_ref[...]
    lat = ckv * _rms_scale(ckv, KV_LORA) * gckv_ref[...]
    lat_ref[0] = lat
    krn = kr * _rms_scale(kr, QK_ROPE) * gkr_ref[...]
    kpe = _rope(krn, rc, rs1, rs2)
    kpe_ref[0] = kpe[:, QK_NOPE:QK_NOPE + QK_ROPE]

    latb = lat.astype(BF16)
    qf = jnp.dot(cqn.astype(BF16), wuq_ref[...], preferred_element_type=F32)
    kf = jnp.dot(latb, wuk_ref[...], preferred_element_type=F32)
    vf = jnp.dot(latb, wuv_ref[...], preferred_element_type=F32)
    gq, gk = gq_ref[...], gk_ref[...]
    ones_hi = jnp.where(is_nope, 0.0, 1.0)
    for hd in range(N_HEADS):
        sl = slice(hd * LANE, (hd + 1) * LANE)
        qs = qf[:, sl]
        sq = qs * qs
        ss_all = jnp.sum(sq, axis=-1, keepdims=True)
        ss_n = jnp.sum(jnp.where(is_nope, sq, 0.0), axis=-1, keepdims=True)
        r = jnp.where(is_nope, lax.rsqrt(ss_n * (1.0 / QK_NOPE) + EPS),
                      lax.rsqrt((ss_all - ss_n) * (1.0 / QK_ROPE) + EPS))
        q_ref[0, :, sl] = _rope(qs * r * gq, rc, rs1, rs2).astype(BF16)
        ks = kf[:, sl]
        k_ref[0, :, sl] = (ks * _rms_scale(ks, QK_NOPE) * gk + kpe).astype(BF16)
        v_ref[0, :, sl] = (vf[:, sl] + ones_hi).astype(BF16)


def _kvpast_kernel(lat_ref, kpe_ref, wuk_ref, wuv_ref, gk_ref, k_ref, v_ref):
    latb = lat_ref[0].astype(BF16)
    kf = jnp.dot(latb, wuk_ref[...], preferred_element_type=F32)
    vf = jnp.dot(latb, wuv_ref[...], preferred_element_type=F32)
    kpe = kpe_ref[0]
    lane = _lane_iota(kpe.shape)
    ones_hi = jnp.where(lane < QK_NOPE, 0.0, 1.0)
    gk = gk_ref[...]
    for hd in range(N_HEADS):
        sl = slice(hd * LANE, (hd + 1) * LANE)
        ks = kf[:, sl]
        k_ref[0, :, sl] = (ks * _rms_scale(ks, QK_NOPE) * gk + kpe).astype(BF16)
        v_ref[0, :, sl] = (vf[:, sl] + ones_hi).astype(BF16)


def _softmax_step(q, k, v, m, acc, mask):
    s = lax.dot_general(q, k, (((1,), (1,)), ((), ())), preferred_element_type=F32)
    if mask is not None:
        s = jnp.where(mask, s, NEG_BIG)
    m_new = jnp.maximum(m, jnp.max(s, axis=-1, keepdims=True))
    alpha = jnp.exp2(m - m_new)
    p = jnp.exp2(s - m_new)
    acc = acc * alpha + jnp.dot(p.astype(BF16), v, preferred_element_type=F32)
    return m_new, acc


def _finish_pair(accs):
    outs = [a / pltpu.roll(a, V_DIM, 1) for a in accs]
    lane = _lane_iota(outs[0].shape)
    return jnp.where(lane < V_DIM, outs[0], pltpu.roll(outs[1], V_DIM, 1))


def _attn_prompt_kernel(q_ref, k_ref, v_ref, o_ref, s_ref, *, tq):
    qi = pl.program_id(2)
    row = lax.broadcasted_iota(jnp.int32, (tq, tq), 0)
    col = lax.broadcasted_iota(jnp.int32, (tq, tq), 1)
    diag_mask = (col // CHUNK) <= (row // CHUNK)
    heads = [slice(hh * LANE, (hh + 1) * LANE) for hh in range(2)]

    def scores(ki, sl):
        start = pl.multiple_of(ki * tq, tq)
        return lax.dot_general(q_ref[0, :, sl], k_ref[0, pl.ds(start, tq), sl], (((1,), (1,)), ((), ())),
                               preferred_element_type=F32)

    def update(ki, s, m, acc, sl, mask):
        start = pl.multiple_of(ki * tq, tq)
        if mask is not None:
            s = jnp.where(mask, s, NEG_BIG)
        m_new = jnp.maximum(m, jnp.max(s, axis=-1, keepdims=True))
        p = jnp.exp2(s - m_new)
        pv = jnp.dot(p.astype(BF16), v_ref[0, pl.ds(start, tq), sl], preferred_element_type=F32)
        return m_new, acc * jnp.exp2(m - m_new) + pv

    for hh, sl in enumerate(heads):
        s_ref[0, hh] = scores(0, sl)

    def body(ki, carry):
        slot = ki % 2
        out = []
        for hh, (sl, (m, acc)) in enumerate(zip(heads, carry)):
            s = s_ref[slot, hh]
            s_ref[1 - slot, hh] = scores(ki + 1, sl)
            out.append(update(ki, s, m, acc, sl, None))
        return tuple(out)

    init = tuple((jnp.full((tq, 1), NEG_BIG, F32), jnp.zeros((tq, LANE), F32)) for _ in heads)
    carry = lax.fori_loop(0, qi, body, init)
    last = qi % 2
    accs = [update(qi, s_ref[last, hh], m, acc, sl, diag_mask)[1]
            for hh, (sl, (m, acc)) in enumerate(zip(heads, carry))]
    o_ref[0] = _finish_pair(accs).astype(o_ref.dtype)


def _attn_sample_kernel(q_ref, kp_ref, vp_ref, kn_ref, vn_ref, o_ref):
    accs = []
    for hh in range(2):
        sl = slice(hh * LANE, (hh + 1) * LANE)
        q = q_ref[0, :, sl]
        m0 = jnp.full((q.shape[0], 1), NEG_BIG, F32)
        acc0 = jnp.zeros((q.shape[0], LANE), F32)
        m, acc = _softmax_step(q, kp_ref[0, :, sl], vp_ref[0, :, sl], m0, acc0, None)
        m, acc = _softmax_step(q, kn_ref[0, :, sl], vn_ref[0, :, sl], m, acc, None)
        accs.append(acc)
    o_ref[0] = _finish_pair(accs).astype(o_ref.dtype)


def _post_kernel(x_ref, attn_ref, cinit_ref, gmix_ref, wb_ref, convw_ref, woa_ref, woc_ref, wo_ref,
                 gffn_ref, wr_ref, br_ref,
                 x1_ref, route_ref, cnew_ref, carry_ref):
    si = pl.program_id(1)
    tm = x_ref.shape[1]

    @pl.when(si == 0)
    def _():
        carry_ref[...] = cinit_ref[0]

    x = x_ref[0]
    h = x * _rms_scale(x, x.shape[-1]) * gmix_ref[...]
    z = jnp.dot(h.astype(BF16), wb_ref[...], preferred_element_type=F32)
    conv_b = z[:, :CONV_DIM]
    u = z[:, CONV_DIM:2 * CONV_DIM] * z[:, 2 * CONV_DIM:3 * CONV_DIM]
    d = x.shape[-1]
    gate_a = z[:, 3 * CONV_DIM:3 * CONV_DIM + d]
    gate_c = z[:, 3 * CONV_DIM + d:]

    carry = carry_ref[...]
    c1 = carry[7:8, :]
    c2 = carry[6:7, :]
    row = lax.broadcasted_iota(jnp.int32, u.shape, 0)
    u_m1 = jnp.where(row == 0, c1, pltpu.roll(u, 1, 0))
    u_m2 = jnp.where(row == 0, c2, jnp.where(row == 1, c1, pltpu.roll(u, 2, 0)))
    cw = convw_ref[...]
    cv = cw[0:1, :] * u_m2 + cw[1:2, :] * u_m1 + cw[2:3, :] * u
    carry_ref[...] = u[tm - 8:, :]
    cnew_ref[0] = u[tm - 2:, :]

    y_c = jnp.dot((conv_b * cv).astype(BF16), woc_ref[...], preferred_element_type=F32)
    y_a = jnp.dot(attn_ref[0], woa_ref[...], preferred_element_type=F32)
    mrg = jax.nn.sigmoid(gate_a) * y_a + jax.nn.sigmoid(gate_c) * y_c
    x1 = x + jnp.dot(mrg.astype(BF16), wo_ref[...], preferred_element_type=F32)
    x1_ref[0] = x1

    h2 = x1 * _rms_scale(x1, d) * gffn_ref[...]
    logit = jnp.dot(h2, wr_ref[...], preferred_element_type=F32, precision=lax.Precision.HIGHEST) + br_ref[...]
    lane = _lane_iota(logit.shape)
    lane_f = lane.astype(F32)
    big = float(LANE)
    is_g = lane < N_GROUPS
    gl = jnp.where(is_g, logit, NEG_BIG)
    gmax = jnp.max(gl, axis=-1, keepdims=True)
    g_idx = jnp.min(jnp.where(is_g & (gl == gmax), lane_f, big), axis=-1, keepdims=True)
    g_p = 1.0 / jnp.sum(jnp.where(is_g, jnp.exp(gl - gmax), 0.0), axis=-1, keepdims=True)
    lo = N_GROUPS + EXPERTS_PER_GROUP * g_idx
    in_grp = (lane_f >= lo) & (lane_f < lo + EXPERTS_PER_GROUP)
    el = jnp.where(in_grp, logit, NEG_BIG)
    e1max = jnp.max(el, axis=-1, keepdims=True)
    i1 = jnp.min(jnp.where(in_grp & (el == e1max), lane_f, big), axis=-1, keepdims=True)
    rest = in_grp & (lane_f != i1)
    el2 = jnp.where(rest, logit, NEG_BIG)
    e2max = jnp.max(el2, axis=-1, keepdims=True)
    i2 = jnp.min(jnp.where(rest & (el2 == e2max), lane_f, big), axis=-1, keepdims=True)
    t = jnp.exp(e2max - e1max)
    w1 = g_p / (1.0 + t)
    w2 = g_p * t / (1.0 + t)
    route_ref[0] = jnp.where(lane == 0, i1 - N_GROUPS,
                             jnp.where(lane == 1, i2 - N_GROUPS,
                                       jnp.where(lane == 2, w1, jnp.where(lane == 3, w2, 0.0))))


def _moe_kernel(lo_ref, hi_ref, cnt_ref, tok_ref,
                x1_hbm, gate_ref, gffn_ref, w13lo_ref, w13hi_ref, w2lo_ref, w2hi_ref,
                x2_hbm, xbuf, obuf, gsem, ssem):
    nb = pl.program_id(0)
    cnt = cnt_ref[nb]
    base = nb * MOE_ROWS
    nchunk = xbuf.shape[0] // MOE_ROWS

    @pl.when(nb == 0)
    def _():
        xbuf[...] = jnp.zeros_like(xbuf)

    def gather(i):
        return pltpu.make_async_copy(x1_hbm.at[tok_ref[base + i]], xbuf.at[pl.ds(i * nchunk, nchunk)], gsem)

    def scatter(i):
        return pltpu.make_async_copy(obuf.at[pl.ds(i * nchunk, nchunk)], x2_hbm.at[tok_ref[base + i]], ssem)

    @pl.when(cnt > 0)
    def _():
        def start_g(i, c):
            gather(i).start()
            return c
        lax.fori_loop(0, cnt, start_g, 0)

        def wait_g(i, c):
            gather(i).wait()
            return c
        lax.fori_loop(0, cnt, wait_g, 0)

        xg = jnp.concatenate([xbuf[pl.ds(c, MOE_ROWS, stride=nchunk), :] for c in range(nchunk)], axis=1)
        h = (xg * _rms_scale(xg, xg.shape[-1]) * gffn_ref[...]).astype(BF16)
        g = gate_ref[...]
        y = xg
        for w13_ref, w2_ref, col in ((w13lo_ref, w2lo_ref, 0), (w13hi_ref, w2hi_ref, 1)):
            ab = jnp.dot(h, w13_ref[0], preferred_element_type=F32)
            hid = jax.nn.silu(ab[:, :EXPERT_HIDDEN]) * ab[:, EXPERT_HIDDEN:]
            y = y + g[:, col:col + 1] * jnp.dot(hid.astype(BF16), w2_ref[0], preferred_element_type=F32)
        for c in range(nchunk):
            obuf[pl.ds(c, MOE_ROWS, stride=nchunk), :] = y[:, c * LANE:(c + 1) * LANE]

        def start_s(i, c):
            scatter(i).start()
            return c
        lax.fori_loop(0, cnt, start_s, 0)

        def wait_s(i, c):
            scatter(i).wait()
            return c
        lax.fori_loop(0, cnt, wait_s, 0)


def _ple_kernel(x_ref, p_ref, gple_ref, wpg_ref, wple_ref, o_ref):
    x = x_ref[...]
    hp = (x * _rms_scale(x, x.shape[-1]) * gple_ref[...]).astype(BF16)
    gate = jax.nn.sigmoid(jnp.dot(hp, wpg_ref[...], preferred_element_type=F32))
    emb = jnp.dot(p_ref[...].astype(BF16), wple_ref[...], preferred_element_type=F32)
    o_ref[...] = x + gate * emb


def _const(shape):
    nd = len(shape)
    return pl.BlockSpec(shape, lambda *_: (0,) * nd)


def _head_slab_cols(w, width, offset=0):
    k = w.shape[0]
    w = w.reshape(k, N_HEADS, width)
    w = jnp.pad(w, ((0, 0), (0, 0), (offset, LANE - width - offset)))
    return w.reshape(k, N_HEADS * LANE)


def _prep_weights(g_mix, w_in, g_cq, w_uq, g_qn, g_qr, g_ckv, w_ukv, g_kn, g_kr, w_oa,
                  conv_w, w_oc, w_o, g_ffn, w_rg, b_rg, w_re, b_re, w1, w3, w2, g_ple, w_pg, w_ple):
    d = w_in.shape[0]
    n_mla = Q_LORA + KV_LORA
    kr_cols = jnp.pad(w_in[:, n_mla:n_mla + QK_ROPE], ((0, 0), (QK_NOPE, LANE - QK_NOPE - QK_ROPE)))
    w = {}
    w["wa"] = jnp.concatenate([w_in[:, :n_mla], kr_cols], axis=1).astype(BF16)
    w["wb"] = w_in[:, n_mla + QK_ROPE:].astype(BF16)
    w["wuq"] = _head_slab_cols(w_uq, QK_NOPE + QK_ROPE).astype(BF16)
    ukv = w_ukv.reshape(KV_LORA, N_HEADS, QK_NOPE + V_DIM)
    w["wuk"] = _head_slab_cols(ukv[:, :, :QK_NOPE].reshape(KV_LORA, -1), QK_NOPE).astype(BF16)
    w["wuv"] = _head_slab_cols(ukv[:, :, QK_NOPE:].reshape(KV_LORA, -1), V_DIM).astype(BF16)
    pad_hi = LANE - QK_NOPE - QK_ROPE
    w["gq"] = (jnp.pad(jnp.concatenate([g_qn, g_qr]), (0, pad_hi)) * (ATTN_SCALE * LOG2E))[None]
    w["gk"] = jnp.pad(g_kn, (0, LANE - QK_NOPE))[None]
    w["gkr"] = jnp.pad(g_kr, (QK_NOPE, pad_hi))[None]
    w["gmix"], w["gcq"], w["gckv"] = g_mix[None], g_cq[None], g_ckv[None]
    w["gffn"], w["gple"] = g_ffn[None], g_ple[None]
    w["convw"] = jnp.pad(conv_w, ((0, 8 - conv_w.shape[0]), (0, 0)))
    w["woa"], w["woc"], w["wo"] = w_oa.astype(BF16), w_oc.astype(BF16), w_o.astype(BF16)
    n_r = N_GROUPS + N_EXPERTS
    w["wr"] = jnp.pad(jnp.concatenate([w_rg, w_re], axis=1), ((0, 0), (0, LANE - n_r)))
    w["br"] = jnp.pad(jnp.concatenate([b_rg, b_re]), (0, LANE - n_r))[None]
    w["w13"] = jnp.concatenate([w1, w3], axis=2).astype(BF16)
    w["w2"] = w2.astype(BF16)
    w["wpg"], w["wple"] = w_pg.astype(BF16), w_ple.astype(BF16)
    return w


def _rope_slabs(pos):
    inv = 1.0 / (ROPE_THETA ** (jnp.arange(0, QK_ROPE, 2, dtype=F32) / QK_ROPE))
    ang = pos.astype(F32)[:, None] * inv[None, :]
    cos, sin = jnp.cos(ang), jnp.sin(ang)
    n = pos.shape[0]
    half = QK_ROPE // 2
    z = lambda k: jnp.zeros((n, k), F32)
    pad_hi = LANE - QK_NOPE - QK_ROPE
    rc = jnp.concatenate([jnp.ones((n, QK_NOPE), F32), cos, cos, z(pad_hi)], axis=1)
    rs1 = jnp.concatenate([z(QK_NOPE), -sin, z(half), z(pad_hi)], axis=1)
    rs2 = jnp.concatenate([z(QK_NOPE), z(half), sin, z(pad_hi)], axis=1)
    return rc, rs1, rs2


def _params(sem):
    return pltpu.CompilerParams(dimension_semantics=sem, vmem_limit_bytes=VMEM_LIMIT)


def _mla_pre(x, w, rope, tm):
    b, s, d = x.shape
    hw = N_HEADS * LANE
    tok = lambda width: pl.BlockSpec((1, tm, width), lambda i, j: (i, j, 0))
    rope_spec = pl.BlockSpec((tm, LANE), lambda i, j: (j, 0))
    consts = [w["gmix"], w["wa"], w["gcq"], w["gckv"], w["gkr"], w["wuq"], w["wuk"], w["wuv"], w["gq"], w["gk"]]
    return pl.pallas_call(
        _pre_kernel,
        grid=(b, s // tm),
        in_specs=[tok(d)] + [_const(c.shape) for c in consts] + [rope_spec] * 3,
        out_specs=[tok(hw), tok(hw), tok(hw), tok(KV_LORA), tok(QK_ROPE)],
        out_shape=[jax.ShapeDtypeStruct((b, s, hw), BF16)] * 3
        + [jax.ShapeDtypeStruct((b, s, KV_LORA), F32), jax.ShapeDtypeStruct((b, s, QK_ROPE), F32)],
        compiler_params=_params(("parallel", "parallel")),
        name="mla_pre",
    )(x, *consts, *rope)


def _kv_past(past_lat, past_kpe, w, tm):
    b, t, _ = past_lat.shape
    hw = N_HEADS * LANE
    kpe_slab = jnp.pad(past_kpe, ((0, 0), (0, 0), (QK_NOPE, LANE - QK_NOPE - QK_ROPE)))
    tok = lambda width: pl.BlockSpec((1, tm, width), lambda i, j: (i, j, 0))
    consts = [w["wuk"], w["wuv"], w["gk"]]
    return pl.pallas_call(
        _kvpast_kernel,
        grid=(b, t // tm),
        in_specs=[tok(KV_LORA), tok(LANE)] + [_const(c.shape) for c in consts],
        out_specs=[tok(hw), tok(hw)],
        out_shape=[jax.ShapeDtypeStruct((b, t, hw), BF16)] * 2,
        compiler_params=_params(("parallel", "parallel")),
        name="kv_past",
    )(past_lat, kpe_slab, *consts)


def _attn_prompt(q, k, v, tq):
    b, s, _ = q.shape
    pair = 2 * LANE
    return pl.pallas_call(
        functools.partial(_attn_prompt_kernel, tq=tq),
        grid=(b, N_HEADS // 2, s // tq),
        in_specs=[pl.BlockSpec((1, tq, pair), lambda i, h, j: (i, j, h)),
                  pl.BlockSpec((1, s, pair), lambda i, h, j: (i, 0, h)),
                  pl.BlockSpec((1, s, pair), lambda i, h, j: (i, 0, h))],
        out_specs=pl.BlockSpec((1, tq, LANE), lambda i, h, j: (i, j, h)),
        out_shape=jax.ShapeDtypeStruct((b, s, N_HEADS * V_DIM), BF16),
        scratch_shapes=[pltpu.VMEM((2, 2, tq, tq), F32)],
        compiler_params=_params(("parallel", "parallel", "arbitrary")),
        name="attn_prompt",
    )(q, k, v)


def _attn_sample(q, kp, vp, kn, vn):
    b, s, _ = q.shape
    t = kp.shape[1]
    pair = 2 * LANE
    blk = lambda rows: pl.BlockSpec((1, rows, pair), lambda i, h: (i, 0, h))
    return pl.pallas_call(
        _attn_sample_kernel,
        grid=(b, N_HEADS // 2),
        in_specs=[blk(s), blk(t), blk(t), blk(s), blk(s)],
        out_specs=pl.BlockSpec((1, s, LANE), lambda i, h: (i, 0, h)),
        out_shape=jax.ShapeDtypeStruct((b, s, N_HEADS * V_DIM), BF16),
        compiler_params=_params(("parallel", "parallel")),
        name="attn_sample",
    )(q, kp, vp, kn, vn)


def _post(x, attn, conv_init, w, tm):
    b, s, d = x.shape
    tok = lambda width: pl.BlockSpec((1, tm, width), lambda i, j: (i, j, 0))
    per_b = lambda rows: pl.BlockSpec((1, rows, CONV_DIM), lambda i, j: (i, 0, 0))
    consts = [w["gmix"], w["wb"], w["convw"], w["woa"], w["woc"], w["wo"], w["gffn"], w["wr"], w["br"]]
    return pl.pallas_call(
        _post_kernel,
        grid=(b, s // tm),
        in_specs=[tok(d), tok(N_HEADS * V_DIM), per_b(8)] + [_const(c.shape) for c in consts],
        out_specs=[tok(d), tok(LANE), per_b(2)],
        out_shape=[jax.ShapeDtypeStruct((b, s, d), F32), jax.ShapeDtypeStruct((b, s, LANE), F32),
                   jax.ShapeDtypeStruct((b, 2, CONV_DIM), F32)],
        scratch_shapes=[pltpu.VMEM((8, CONV_DIM), F32)],
        compiler_params=_params(("parallel", "arbitrary")),
        name="post",
    )(x, attn, conv_init, *consts)


def _route_tables(route, n):
    e1 = route[:, 0].astype(jnp.int32)
    e2 = route[:, 1].astype(jnp.int32)
    first_lo = e1 < e2
    lo = jnp.where(first_lo, e1, e2)
    hi = jnp.where(first_lo, e2, e1)
    g_lo = jnp.where(first_lo, route[:, 2], route[:, 3])
    g_hi = jnp.where(first_lo, route[:, 3], route[:, 2])
    n_keys = N_EXPERTS * EXPERTS_PER_GROUP
    key = lo * EXPERTS_PER_GROUP + hi % EXPERTS_PER_GROUP
    order = jnp.argsort(key).astype(jnp.int32)
    key_s = key[order]
    counts = jnp.zeros((n_keys,), jnp.int32).at[key].add(1)
    padded = (counts + MOE_ROWS - 1) // MOE_ROWS * MOE_ROWS
    start = jnp.cumsum(counts) - counts
    pend = jnp.cumsum(padded)
    pstart = pend - padded
    dest = pstart[key_s] + jnp.arange(n, dtype=jnp.int32) - start[key_s]
    nblk = n // MOE_ROWS + N_PAIR_BUCKETS
    npos = nblk * MOE_ROWS
    tok = jnp.zeros((npos,), jnp.int32).at[dest].set(order)
    gates = jnp.zeros((npos, 2), F32).at[dest].set(jnp.stack([g_lo, g_hi], axis=1)[order])
    blk_start = jnp.arange(nblk, dtype=jnp.int32) * MOE_ROWS
    blk_key = jnp.minimum(jnp.searchsorted(pend, blk_start, side="right"), n_keys - 1).astype(jnp.int32)
    blk_cnt = jnp.clip(counts[blk_key] - (blk_start - pstart[blk_key]), 0, MOE_ROWS).astype(jnp.int32)
    blk_lo = blk_key // EXPERTS_PER_GROUP
    blk_hi = blk_lo // EXPERTS_PER_GROUP * EXPERTS_PER_GROUP + blk_key % EXPERTS_PER_GROUP
    return blk_lo, blk_hi, blk_cnt, tok, gates


def _moe(x1, route, w):
    n, d = x1.shape
    nchunk = d // LANE
    blk_lo, blk_hi, blk_cnt, tok, gates = _route_tables(route, n)
    nblk = blk_lo.shape[0]
    x1r = x1.reshape(n, nchunk, LANE)
    w13_spec = lambda ref_idx: pl.BlockSpec((1, d, 2 * EXPERT_HIDDEN),
                                            lambda i, lo, hi, cnt, tok: ((lo, hi)[ref_idx][i], 0, 0))
    w2_spec = lambda ref_idx: pl.BlockSpec((1, EXPERT_HIDDEN, d),
                                           lambda i, lo, hi, cnt, tok: ((lo, hi)[ref_idx][i], 0, 0))
    grid_spec = pltpu.PrefetchScalarGridSpec(
        num_scalar_prefetch=4,
        grid=(nblk,),
        in_specs=[pl.BlockSpec(memory_space=pl.ANY),
                  pl.BlockSpec((MOE_ROWS, 2), lambda i, *_: (i, 0)),
                  pl.BlockSpec((1, d), lambda i, *_: (0, 0)),
                  w13_spec(0), w13_spec(1), w2_spec(0), w2_spec(1)],
        out_specs=pl.BlockSpec(memory_space=pl.ANY),
        scratch_shapes=[pltpu.VMEM((MOE_ROWS * nchunk, LANE), F32), pltpu.VMEM((MOE_ROWS * nchunk, LANE), F32),
                        pltpu.SemaphoreType.DMA, pltpu.SemaphoreType.DMA],
    )
    x2 = pl.pallas_call(
        _moe_kernel,
        grid_spec=grid_spec,
        out_shape=jax.ShapeDtypeStruct((n, nchunk, LANE), F32),
        compiler_params=_params(("arbitrary",)),
        name="moe",
    )(blk_lo, blk_hi, blk_cnt, tok, x1r, gates, w["gffn"], w["w13"], w["w13"], w["w2"], w["w2"])
    return x2.reshape(n, d)


def _ple(x2, p, w, tm):
    n, d = x2.shape
    pd = p.shape[1]
    consts = [w["gple"], w["wpg"], w["wple"]]
    return pl.pallas_call(
        _ple_kernel,
        grid=(n // tm,),
        in_specs=[pl.BlockSpec((tm, d), lambda i: (i, 0)), pl.BlockSpec((tm, pd), lambda i: (i, 0))]
        + [_const(c.shape) for c in consts],
        out_specs=pl.BlockSpec((tm, d), lambda i: (i, 0)),
        out_shape=jax.ShapeDtypeStruct((n, d), F32),
        compiler_params=_params(("parallel",)),
        name="ple",
    )(x2, p, *consts)


def _stream(x, p, past_lat, past_kpe, past_conv, w):
    b, s, d = x.shape
    tm = min(512, s)
    past_len = 0 if past_lat is None else past_lat.shape[1]
    rope = _rope_slabs(past_len + jnp.arange(s))
    q, k, v, lat_new, kpe_new = _mla_pre(x, w, rope, tm)
    if past_lat is None:
        attn = _attn_prompt(q, k, v, min(512, s))
        conv_init = jnp.zeros((b, 8, CONV_DIM), F32)
    else:
        kp, vp = _kv_past(past_lat, past_kpe, w, min(512, past_len))
        attn = _attn_sample(q, kp, vp, k, v)
        conv_init = jnp.pad(past_conv, ((0, 0), (8 - past_conv.shape[1], 0), (0, 0)))
    x1, route, conv_new = _post(x, attn, conv_init, w, tm)
    n = b * s
    x2 = _moe(x1.reshape(n, d), route.reshape(n, LANE), w)
    y = _ple(x2, p.reshape(n, -1), w, min(512, n))
    return y.reshape(b, s, d), lat_new, kpe_new, conv_new


def kernel(x_prompt, x_sample, cache_kv_latent, cache_k_rope, state_conv, p_prompt, p_sample,
           g_mix, w_in, g_cq, w_uq, g_qn, g_qr, g_ckv, w_ukv, g_kn, g_kr, w_oa,
           conv_w, w_oc, w_o, g_ffn, w_rg, b_rg, w_re, b_re, w1, w3, w2, g_ple, w_pg, w_ple):
    depth = g_mix.shape[0]
    xp, xs = x_prompt, x_sample
    outs = [[] for _ in range(6)]
    for i in range(depth):
        w = _prep_weights(g_mix[i], w_in[i], g_cq[i], w_uq[i], g_qn[i], g_qr[i], g_ckv[i], w_ukv[i],
                          g_kn[i], g_kr[i], w_oa[i], conv_w[i], w_oc[i], w_o[i], g_ffn[i], w_rg[i], b_rg[i],
                          w_re[i], b_re[i], w1[i], w3[i], w2[i], g_ple[i], w_pg[i], w_ple[i])
        xp, a, bb, c = _stream(xp, p_prompt[i], None, None, None, w)
        outs[0].append(a); outs[1].append(bb); outs[2].append(c)
        xs, a, bb, c = _stream(xs, p_sample[i], cache_kv_latent[i], cache_k_rope[i], state_conv[i], w)
        outs[3].append(a); outs[4].append(bb); outs[5].append(c)
    return (xp, xs) + tuple(jnp.stack(o, axis=0) for o in outs)
```

```python
import functools
import math

import jax
import jax.numpy as jnp
from jax import lax
from jax.experimental import pallas as pl
from jax.experimental.pallas import tpu as pltpu

F32 = jnp.float32
BF16 = jnp.bfloat16

LANE = 128
CHUNK = 64
N_HEADS = 8
QK_NOPE = 64
QK_ROPE = 32
V_DIM = 64
Q_LORA = 256
KV_LORA = 256
CONV_DIM = 512
N_GROUPS = 4
EXPERTS_PER_GROUP = 8
N_EXPERTS = N_GROUPS * EXPERTS_PER_GROUP
EXPERT_HIDDEN = 256
ROPE_THETA = 10000.0
EPS = 1e-6
ATTN_SCALE = (QK_NOPE + QK_ROPE) ** -0.5
LOG2E = math.log2(math.e)
NEG_BIG = -1e30
MOE_ROWS = 128
N_PAIR_BUCKETS = N_GROUPS * (EXPERTS_PER_GROUP * (EXPERTS_PER_GROUP - 1) // 2)
VMEM_LIMIT = 56 * 1024 * 1024


def _rms_scale(x, n):
    return lax.rsqrt(jnp.sum(x * x, axis=-1, keepdims=True) * (1.0 / n) + EPS)


def _lane_iota(shape):
    return lax.broadcasted_iota(jnp.int32, shape, len(shape) - 1)


def _rope(t, rc, rs1, rs2):
    return t * rc + pltpu.roll(t, LANE - QK_ROPE // 2, 1) * rs1 + pltpu.roll(t, QK_ROPE // 2, 1) * rs2


def _pre_kernel(x_ref, gmix_ref, wa_ref, gcq_ref, gckv_ref, gkr_ref, wuq_ref, wuk_ref, wuv_ref,
                gq_ref, gk_ref, rc_ref, rs1_ref, rs2_ref,
                q_ref, k_ref, v_ref, lat_ref, kpe_ref):
    x = x_ref[0]
    h = x * _rms_scale(x, x.shape[-1]) * gmix_ref[...]
    z = jnp.dot(h.astype(BF16), wa_ref[...], preferred_element_type=F32)
    cq = z[:, :Q_LORA]
    ckv = z[:, Q_LORA:Q_LORA + KV_LORA]
    kr = z[:, Q_LORA + KV_LORA:]
    rc, rs1, rs2 = rc_ref[...], rs1_ref[...], rs2_ref[...]
    lane = _lane_iota(kr.shape)
    is_nope = lane < QK_NOPE

    cqn = cq * _rms_scale(cq, Q_LORA) * gcq_ref[...]
    lat = ckv * _rms_scale(ckv, KV_LORA) * gckv_ref[...]
    lat_ref[0] = lat
    krn = kr * _rms_scale(kr, QK_ROPE) * gkr_ref[...]
    kpe = _rope(krn, rc, rs1, rs2)
    kpe_ref[0] = kpe[:, QK_NOPE:QK_NOPE + QK_ROPE]

    latb = lat.astype(BF16)
    qf = jnp.dot(cqn.astype(BF16), wuq_ref[...], preferred_element_type=F32)
    kf = jnp.dot(latb, wuk_ref[...], preferred_element_type=F32)
    vf = jnp.dot(latb, wuv_ref[...], preferred_element_type=F32)
    gq, gk = gq_ref[...], gk_ref[...]
    ones_hi = jnp.where(is_nope, 0.0, 1.0)
    for hd in range(N_HEADS):
        sl = slice(hd * LANE, (hd + 1) * LANE)
        qs = qf[:, sl]
        sq = qs * qs
        ss_all = jnp.sum(sq, axis=-1, keepdims=True)
        ss_n = jnp.sum(jnp.where(is_nope, sq, 0.0), axis=-1, keepdims=True)
        r = jnp.where(is_nope, lax.rsqrt(ss_n * (1.0 / QK_NOPE) + EPS),
                      lax.rsqrt((ss_all - ss_n) * (1.0 / QK_ROPE) + EPS))
        q_ref[0, :, sl] = _rope(qs * r * gq, rc, rs1, rs2).astype(BF16)
        ks = kf[:, sl]
        k_ref[0, :, sl] = (ks * _rms_scale(ks, QK_NOPE) * gk + kpe).astype(BF16)
        v_ref[0, :, sl] = (vf[:, sl] + ones_hi).astype(BF16)


def _kvpast_kernel(lat_ref, kpe_ref, wuk_ref, wuv_ref, gk_ref, k_ref, v_ref):
    latb = lat_ref[0].astype(BF16)
    kf = jnp.dot(latb, wuk_ref[...], preferred_element_type=F32)
    vf = jnp.dot(latb, wuv_ref[...], preferred_element_type=F32)
    kpe = kpe_ref[0]
    lane = _lane_iota(kpe.shape)
    ones_hi = jnp.where(lane < QK_NOPE, 0.0, 1.0)
    gk = gk_ref[...]
    for hd in range(N_HEADS):
        sl = slice(hd * LANE, (hd + 1) * LANE)
        ks = kf[:, sl]
        k_ref[0, :, sl] = (ks * _rms_scale(ks, QK_NOPE) * gk + kpe).astype(BF16)
        v_ref[0, :, sl] = (vf[:, sl] + ones_hi).astype(BF16)


def _softmax_step(q, k, v, m, acc, mask):
    s = lax.dot_general(q, k, (((1,), (1,)), ((), ())), preferred_element_type=F32)
    if mask is not None:
        s = jnp.where(mask, s, NEG_BIG)
    m_new = jnp.maximum(m, jnp.max(s, axis=-1, keepdims=True))
    alpha = jnp.exp2(m - m_new)
    p = jnp.exp2(s - m_new)
    acc = acc * alpha + jnp.dot(p.astype(BF16), v, preferred_element_type=F32)
    return m_new, acc


def _finish_pair(accs):
    outs = [a / pltpu.roll(a, V_DIM, 1) for a in accs]
    lane = _lane_iota(outs[0].shape)
    return jnp.where(lane < V_DIM, outs[0], pltpu.roll(outs[1], V_DIM, 1))


def _attn_prompt_kernel(q_ref, k_ref, v_ref, o_ref, s_ref, *, tq):
    qi = pl.program_id(2)
    row = lax.broadcasted_iota(jnp.int32, (tq, tq), 0)
    col = lax.broadcasted_iota(jnp.int32, (tq, tq), 1)
    diag_mask = (col // CHUNK) <= (row // CHUNK)
    heads = [slice(hh * LANE, (hh + 1) * LANE) for hh in range(2)]

    def scores(ki, sl):
        start = pl.multiple_of(ki * tq, tq)
        return lax.dot_general(q_ref[0, :, sl], k_ref[0, pl.ds(start, tq), sl], (((1,), (1,)), ((), ())),
                               preferred_element_type=F32)

    def update(ki, s, m, acc, sl, mask):
        start = pl.multiple_of(ki * tq, tq)
        if mask is not None:
            s = jnp.where(mask, s, NEG_BIG)
        m_new = jnp.maximum(m, jnp.max(s, axis=-1, keepdims=True))
        p = jnp.exp2(s - m_new)
        pv = jnp.dot(p.astype(BF16), v_ref[0, pl.ds(start, tq), sl], preferred_element_type=F32)
        return m_new, acc * jnp.exp2(m - m_new) + pv

    for hh, sl in enumerate(heads):
        s_ref[0, hh] = scores(0, sl)

    def body(ki, carry):
        slot = ki % 2
        out = []
        for hh, (sl, (m, acc)) in enumerate(zip(heads, carry)):
            s = s_ref[slot, hh]
            s_ref[1 - slot, hh] = scores(ki + 1, sl)
            out.append(update(ki, s, m, acc, sl, None))
        return tuple(out)

    init = tuple((jnp.full((tq, 1), NEG_BIG, F32), jnp.zeros((tq, LANE), F32)) for _ in heads)
    carry = lax.fori_loop(0, qi, body, init)
    last = qi % 2
    accs = [update(qi, s_ref[last, hh], m, acc, sl, diag_mask)[1]
            for hh, (sl, (m, acc)) in enumerate(zip(heads, carry))]
    o_ref[0] = _finish_pair(accs).astype(o_ref.dtype)


def _attn_sample_kernel(q_ref, kp_ref, vp_ref, kn_ref, vn_ref, o_ref):
    accs = []
    for hh in range(2):
        sl = slice(hh * LANE, (hh + 1) * LANE)
        q = q_ref[0, :, sl]
        m0 = jnp.full((q.shape[0], 1), NEG_BIG, F32)
        acc0 = jnp.zeros((q.shape[0], LANE), F32)
        m, acc = _softmax_step(q, kp_ref[0, :, sl], vp_ref[0, :, sl], m0, acc0, None)
        m, acc = _softmax_step(q, kn_ref[0, :, sl], vn_ref[0, :, sl], m, acc, None)
        accs.append(acc)
    o_ref[0] = _finish_pair(accs).astype(o_ref.dtype)


def _post_kernel(x_ref, attn_ref, cinit_ref, cnt0_ref, gmix_ref, wb_ref, convw_ref, woa_ref, woc_ref, wo_ref,
                 gffn_ref, wrt_ref, brt_ref, tri_ref, x1_all_ref,
                 x1_ref, route_ref, cnew_ref, cnt_ref, carry_ref):
    del x1_all_ref
    si = pl.program_id(1)
    tm = x_ref.shape[1]

    @pl.when(si == 0)
    def _():
        carry_ref[...] = cinit_ref[0]

    @pl.when((si == 0) & (pl.program_id(0) == 0))
    def _():
        cnt_ref[...] = cnt0_ref[...]

    x = x_ref[0]
    h = x * _rms_scale(x, x.shape[-1]) * gmix_ref[...]
    z = jnp.dot(h.astype(BF16), wb_ref[...], preferred_element_type=F32)
    conv_b = z[:, :CONV_DIM]
    u = z[:, CONV_DIM:2 * CONV_DIM] * z[:, 2 * CONV_DIM:3 * CONV_DIM]
    d = x.shape[-1]
    gate_a = z[:, 3 * CONV_DIM:3 * CONV_DIM + d]
    gate_c = z[:, 3 * CONV_DIM + d:]

    carry = carry_ref[...]
    c1 = carry[7:8, :]
    c2 = carry[6:7, :]
    row = lax.broadcasted_iota(jnp.int32, u.shape, 0)
    u_m1 = jnp.where(row == 0, c1, pltpu.roll(u, 1, 0))
    u_m2 = jnp.where(row == 0, c2, jnp.where(row == 1, c1, pltpu.roll(u, 2, 0)))
    cw = convw_ref[...]
    cv = cw[0:1, :] * u_m2 + cw[1:2, :] * u_m1 + cw[2:3, :] * u
    carry_ref[...] = u[tm - 8:, :]
    cnew_ref[0] = u[tm - 2:, :]

    y_c = jnp.dot((conv_b * cv).astype(BF16), woc_ref[...], preferred_element_type=F32)
    y_a = jnp.dot(attn_ref[0], woa_ref[...], preferred_element_type=F32)
    mrg = jax.nn.sigmoid(gate_a) * y_a + jax.nn.sigmoid(gate_c) * y_c
    x1 = x + jnp.dot(mrg.astype(BF16), wo_ref[...], preferred_element_type=F32)
    nchunk = d // LANE
    for c in range(nchunk):
        x1_ref[pl.ds(c, tm, stride=nchunk), :] = x1[:, c * LANE:(c + 1) * LANE]

    h2 = x1 * _rms_scale(x1, d) * gffn_ref[...]
    h2_hi = h2.astype(BF16)
    h2_lo = (h2 - h2_hi.astype(F32)).astype(BF16)
    nt = (((1,), (1,)), ((), ()))
    n_rows = brt_ref.shape[0]
    lt2 = lax.dot_general(wrt_ref[...], h2_hi, nt, preferred_element_type=F32)
    lt = (lt2[:n_rows] + lt2[n_rows:]
          + lax.dot_general(wrt_ref[:n_rows, :], h2_lo, nt, preferred_element_type=F32) + brt_ref[...])
    sub = lax.broadcasted_iota(jnp.int32, (EXPERTS_PER_GROUP, tm), 0).astype(F32)
    none = float(EXPERTS_PER_GROUP)

    def first_argmax(v):
        vmax = jnp.max(v, axis=0, keepdims=True)
        return jnp.min(jnp.where(v == vmax, sub, none), axis=0, keepdims=True)

    g_idx = first_argmax(lt[:EXPERTS_PER_GROUP])
    el = lt[EXPERTS_PER_GROUP * N_GROUPS:]
    for g in range(N_GROUPS - 2, -1, -1):
        el = jnp.where(g_idx == g, lt[EXPERTS_PER_GROUP * (g + 1):EXPERTS_PER_GROUP * (g + 2)], el)
    i1 = first_argmax(el)
    i2 = first_argmax(jnp.where(sub == i1, NEG_BIG, el))
    key = (g_idx * (EXPERTS_PER_GROUP * EXPERTS_PER_GROUP) + jnp.minimum(i1, i2) * EXPERTS_PER_GROUP
           + jnp.maximum(i1, i2))

    n_keys = cnt_ref.shape[0]
    keys = lax.broadcasted_iota(jnp.int32, (n_keys, tm), 0).astype(F32)
    onehot = jnp.where(keys == key, 1.0, 0.0)
    before = jnp.dot(onehot.astype(BF16), tri_ref[...], preferred_element_type=F32)
    rank = jnp.sum(onehot * (before + cnt_ref[...]), axis=0, keepdims=True)
    cnt_ref[...] += jnp.sum(onehot, axis=1, keepdims=True)
    row8 = lax.broadcasted_iota(jnp.int32, (8, tm), 0)
    route_ref[0] = jnp.where(row8 == 0, key, jnp.where(row8 == 1, rank, 0.0))


def _rows_loop(n, fn, unroll=8):
    def group(j, c):
        for u in range(unroll):
            fn(j * unroll + u)
        return c
    lax.fori_loop(0, n // unroll, group, 0)

    def single(i, c):
        fn(i)
        return c
    lax.fori_loop(n // unroll * unroll, n, single, 0)


def _moe_kernel(lo_ref, hi_ref, cnt_ref, dest_ref,
                x1_hbm, gffn_ref, wr_ref, br_ref, w13lo_ref, w13hi_ref, w2lo_ref, w2hi_ref,
                x2_hbm, tok_ref, xbuf, obuf, gsem, ssem):
    nb = pl.program_id(0)
    nblk = pl.num_programs(0)
    nchunk = xbuf.shape[1] // MOE_ROWS
    slot = nb % 2
    cnt = cnt_ref[nb]

    def start_gathers(blk, sl):
        base = blk * MOE_ROWS

        def one(i):
            src = x1_hbm.at[pl.ds(tok_ref[base + i] * nchunk, nchunk)]
            pltpu.make_async_copy(src, xbuf.at[sl, pl.ds(i * nchunk, nchunk)], gsem.at[sl]).start()
        _rows_loop(cnt_ref[blk], one)

    def start_scatters(blk, sl):
        base = blk * MOE_ROWS

        def one(i):
            dst = x2_hbm.at[pl.ds(tok_ref[base + i] * nchunk, nchunk)]
            pltpu.make_async_copy(obuf.at[sl, pl.ds(i * nchunk, nchunk)], dst, ssem.at[sl]).start()
        _rows_loop(cnt_ref[blk], one)

    def wait_gathers(blk, sl):
        rows = cnt_ref[blk] * nchunk
        pltpu.make_async_copy(x1_hbm.at[pl.ds(0, rows)], xbuf.at[sl, pl.ds(0, rows)], gsem.at[sl]).wait()

    def wait_scatters(blk, sl):
        rows = cnt_ref[blk] * nchunk
        pltpu.make_async_copy(obuf.at[sl, pl.ds(0, rows)], x2_hbm.at[pl.ds(0, rows)], ssem.at[sl]).wait()

    @pl.when(nb == 0)
    def _():
        def fill(i, c):
            tok_ref[i] = 0
            return c
        lax.fori_loop(0, tok_ref.shape[0], fill, 0, unroll=8)

        def invert(i, c):
            tok_ref[dest_ref[i]] = i
            return c
        lax.fori_loop(0, dest_ref.shape[0], invert, 0, unroll=8)
        xbuf[...] = jnp.zeros_like(xbuf)

        @pl.when(cnt > 0)
        def _():
            start_gathers(0, 0)

    @pl.when(nb + 1 < nblk)
    def _():
        @pl.when(cnt_ref[nb + 1] > 0)
        def _():
            start_gathers(nb + 1, 1 - slot)

    @pl.when(nb >= 2)
    def _():
        @pl.when(cnt_ref[nb - 2] > 0)
        def _():
            wait_scatters(nb - 2, slot)

    @pl.when(cnt > 0)
    def _():
        wait_gathers(nb, slot)
        xg = jnp.concatenate([xbuf[slot, pl.ds(c, MOE_ROWS, stride=nchunk), :] for c in range(nchunk)], axis=1)
        h = (xg * _rms_scale(xg, xg.shape[-1]) * gffn_ref[...]).astype(BF16)

        lo, hi = lo_ref[nb], hi_ref[nb]
        logit = jnp.dot(h, wr_ref[...], preferred_element_type=F32) + br_ref[...]
        lane = _lane_iota(logit.shape)
        pick = lambda j: jnp.sum(jnp.where(lane == j, logit, 0.0), axis=-1, keepdims=True)
        is_g = lane < N_GROUPS
        gmax = jnp.max(jnp.where(is_g, logit, NEG_BIG), axis=-1, keepdims=True)
        g_den = jnp.sum(jnp.where(is_g, jnp.exp(logit - gmax), 0.0), axis=-1, keepdims=True)
        g_p = jnp.exp(pick(lo // EXPERTS_PER_GROUP) - gmax) / g_den
        l_lo, l_hi = pick(N_GROUPS + lo), pick(N_GROUPS + hi)
        gates = (g_p * jax.nn.sigmoid(l_lo - l_hi), g_p * jax.nn.sigmoid(l_hi - l_lo))

        y = xg
        for w13_ref, w2_ref, gate in ((w13lo_ref, w2lo_ref, gates[0]), (w13hi_ref, w2hi_ref, gates[1])):
            ab = jnp.dot(h, w13_ref[0], preferred_element_type=F32)
            hid = jax.nn.silu(ab[:, :EXPERT_HIDDEN]) * ab[:, EXPERT_HIDDEN:]
            y = y + gate * jnp.dot(hid.astype(BF16), w2_ref[0], preferred_element_type=F32)
        for c in range(nchunk):
            obuf[slot, pl.ds(c, MOE_ROWS, stride=nchunk), :] = y[:, c * LANE:(c + 1) * LANE]
        start_scatters(nb, slot)

    @pl.when(nb == nblk - 1)
    def _():
        @pl.when(nb >= 1)
        def _():
            @pl.when(cnt_ref[nb - 1] > 0)
            def _():
                wait_scatters(nb - 1, 1 - slot)

        @pl.when(cnt > 0)
        def _():
            wait_scatters(nb, slot)


def _ple_kernel(x_ref, p_ref, gple_ref, wpg_ref, wple_ref, o_ref):
    tm = o_ref.shape[0]
    nchunk = o_ref.shape[1] // LANE
    x = jnp.concatenate([x_ref[pl.ds(c, tm, stride=nchunk), :] for c in range(nchunk)], axis=1)
    hp = (x * _rms_scale(x, x.shape[-1]) * gple_ref[...]).astype(BF16)
    gate = jax.nn.sigmoid(jnp.dot(hp, wpg_ref[...], preferred_element_type=F32))
    emb = jnp.dot(p_ref[...].astype(BF16), wple_ref[...], preferred_element_type=F32)
    o_ref[...] = x + gate * emb


def _const(shape):
    nd = len(shape)
    return pl.BlockSpec(shape, lambda *_: (0,) * nd)


def _head_slab_cols(w, width, offset=0):
    k = w.shape[0]
    w = w.reshape(k, N_HEADS, width)
    w = jnp.pad(w, ((0, 0), (0, 0), (offset, LANE - width - offset)))
    return w.reshape(k, N_HEADS * LANE)


def _prep_weights(g_mix, w_in, g_cq, w_uq, g_qn, g_qr, g_ckv, w_ukv, g_kn, g_kr, w_oa,
                  conv_w, w_oc, w_o, g_ffn, w_rg, b_rg, w_re, b_re, w1, w3, w2, g_ple, w_pg, w_ple):
    d = w_in.shape[0]
    n_mla = Q_LORA + KV_LORA
    kr_cols = jnp.pad(w_in[:, n_mla:n_mla + QK_ROPE], ((0, 0), (QK_NOPE, LANE - QK_NOPE - QK_ROPE)))
    w = {}
    w["wa"] = jnp.concatenate([w_in[:, :n_mla], kr_cols], axis=1).astype(BF16)
    w["wb"] = w_in[:, n_mla + QK_ROPE:].astype(BF16)
    w["wuq"] = _head_slab_cols(w_uq, QK_NOPE + QK_ROPE).astype(BF16)
    ukv = w_ukv.reshape(KV_LORA, N_HEADS, QK_NOPE + V_DIM)
    w["wuk"] = _head_slab_cols(ukv[:, :, :QK_NOPE].reshape(KV_LORA, -1), QK_NOPE).astype(BF16)
    w["wuv"] = _head_slab_cols(ukv[:, :, QK_NOPE:].reshape(KV_LORA, -1), V_DIM).astype(BF16)
    pad_hi = LANE - QK_NOPE - QK_ROPE
    w["gq"] = (jnp.pad(jnp.concatenate([g_qn, g_qr]), (0, pad_hi)) * (ATTN_SCALE * LOG2E))[None]
    w["gk"] = jnp.pad(g_kn, (0, LANE - QK_NOPE))[None]
    w["gkr"] = jnp.pad(g_kr, (QK_NOPE, pad_hi))[None]
    w["gmix"], w["gcq"], w["gckv"] = g_mix[None], g_cq[None], g_ckv[None]
    w["gffn"], w["gple"] = g_ffn[None], g_ple[None]
    w["convw"] = jnp.pad(conv_w, ((0, 8 - conv_w.shape[0]), (0, 0)))
    w["woa"], w["woc"], w["wo"] = w_oa.astype(BF16), w_oc.astype(BF16), w_o.astype(BF16)
    n_r = N_GROUPS + N_EXPERTS
    w["wr"] = jnp.pad(jnp.concatenate([w_rg, w_re], axis=1), ((0, 0), (0, LANE - n_r))).astype(BF16)
    w["br"] = jnp.pad(jnp.concatenate([b_rg, b_re]), (0, LANE - n_r))[None]
    pad_g = EXPERTS_PER_GROUP - N_GROUPS
    wrt = jnp.concatenate([jnp.pad(w_rg.T, ((0, pad_g), (0, 0))), w_re.T], axis=0)
    wrt_hi = wrt.astype(BF16)
    w["wrt"] = jnp.concatenate([wrt_hi, (wrt - wrt_hi.astype(F32)).astype(BF16)], axis=0)
    w["brt"] = jnp.concatenate([b_rg, jnp.full((pad_g,), NEG_BIG, F32), b_re])[:, None]
    w["w13"] = jnp.concatenate([w1, w3], axis=2).astype(BF16)
    w["w2"] = w2.astype(BF16)
    w["wpg"], w["wple"] = w_pg.astype(BF16), w_ple.astype(BF16)
    return w


def _rope_slabs(pos):
    inv = 1.0 / (ROPE_THETA ** (jnp.arange(0, QK_ROPE, 2, dtype=F32) / QK_ROPE))
    ang = pos.astype(F32)[:, None] * inv[None, :]
    cos, sin = jnp.cos(ang), jnp.sin(ang)
    n = pos.shape[0]
    half = QK_ROPE // 2
    z = lambda k: jnp.zeros((n, k), F32)
    pad_hi = LANE - QK_NOPE - QK_ROPE
    rc = jnp.concatenate([jnp.ones((n, QK_NOPE), F32), cos, cos, z(pad_hi)], axis=1)
    rs1 = jnp.concatenate([z(QK_NOPE), -sin, z(half), z(pad_hi)], axis=1)
    rs2 = jnp.concatenate([z(QK_NOPE), z(half), sin, z(pad_hi)], axis=1)
    return rc, rs1, rs2


def _params(sem):
    return pltpu.CompilerParams(dimension_semantics=sem, vmem_limit_bytes=VMEM_LIMIT)


def _mla_pre(x, w, rope, tm):
    b, s, d = x.shape
    hw = N_HEADS * LANE
    tok = lambda width: pl.BlockSpec((1, tm, width), lambda i, j: (i, j, 0))
    rope_spec = pl.BlockSpec((tm, LANE), lambda i, j: (j, 0))
    consts = [w["gmix"], w["wa"], w["gcq"], w["gckv"], w["gkr"], w["wuq"], w["wuk"], w["wuv"], w["gq"], w["gk"]]
    return pl.pallas_call(
        _pre_kernel,
        grid=(b, s // tm),
        in_specs=[tok(d)] + [_const(c.shape) for c in consts] + [rope_spec] * 3,
        out_specs=[tok(hw), tok(hw), tok(hw), tok(KV_LORA), tok(QK_ROPE)],
        out_shape=[jax.ShapeDtypeStruct((b, s, hw), BF16)] * 3
        + [jax.ShapeDtypeStruct((b, s, KV_LORA), F32), jax.ShapeDtypeStruct((b, s, QK_ROPE), F32)],
        compiler_params=_params(("parallel", "parallel")),
        name="mla_pre",
    )(x, *consts, *rope)


def _kv_past(past_lat, past_kpe, w, tm):
    b, t, _ = past_lat.shape
    hw = N_HEADS * LANE
    kpe_slab = jnp.pad(past_kpe, ((0, 0), (0, 0), (QK_NOPE, LANE - QK_NOPE - QK_ROPE)))
    tok = lambda width: pl.BlockSpec((1, tm, width), lambda i, j: (i, j, 0))
    consts = [w["wuk"], w["wuv"], w["gk"]]
    return pl.pallas_call(
        _kvpast_kernel,
        grid=(b, t // tm),
        in_specs=[tok(KV_LORA), tok(LANE)] + [_const(c.shape) for c in consts],
        out_specs=[tok(hw), tok(hw)],
        out_shape=[jax.ShapeDtypeStruct((b, t, hw), BF16)] * 2,
        compiler_params=_params(("parallel", "parallel")),
        name="kv_past",
    )(past_lat, kpe_slab, *consts)


def _attn_prompt(q, k, v, tq):
    b, s, _ = q.shape
    pair = 2 * LANE
    return pl.pallas_call(
        functools.partial(_attn_prompt_kernel, tq=tq),
        grid=(b, N_HEADS // 2, s // tq),
        in_specs=[pl.BlockSpec((1, tq, pair), lambda i, h, j: (i, j, h)),
                  pl.BlockSpec((1, s, pair), lambda i, h, j: (i, 0, h)),
                  pl.BlockSpec((1, s, pair), lambda i, h, j: (i, 0, h))],
        out_specs=pl.BlockSpec((1, tq, LANE), lambda i, h, j: (i, j, h)),
        out_shape=jax.ShapeDtypeStruct((b, s, N_HEADS * V_DIM), BF16),
        scratch_shapes=[pltpu.VMEM((2, 2, tq, tq), F32)],
        compiler_params=_params(("parallel", "parallel", "arbitrary")),
        name="attn_prompt",
    )(q, k, v)


def _attn_sample(q, kp, vp, kn, vn):
    b, s, _ = q.shape
    t = kp.shape[1]
    pair = 2 * LANE
    blk = lambda rows: pl.BlockSpec((1, rows, pair), lambda i, h: (i, 0, h))
    return pl.pallas_call(
        _attn_sample_kernel,
        grid=(b, N_HEADS // 2),
        in_specs=[blk(s), blk(t), blk(t), blk(s), blk(s)],
        out_specs=pl.BlockSpec((1, s, LANE), lambda i, h: (i, 0, h)),
        out_shape=jax.ShapeDtypeStruct((b, s, N_HEADS * V_DIM), BF16),
        compiler_params=_params(("parallel", "parallel")),
        name="attn_sample",
    )(q, kp, vp, kn, vn)


def _post(x, attn, conv_init, cnt0, x1_all, row_off, n_all, w, tm):
    b, s, d = x.shape
    nchunk = d // LANE
    n_keys = cnt0.shape[0]
    tiles_per_b = s // tm
    off = row_off // tm
    tok = lambda width: pl.BlockSpec((1, tm, width), lambda i, j: (i, j, 0))
    per_b = lambda rows: pl.BlockSpec((1, rows, CONV_DIM), lambda i, j: (i, 0, 0))
    tri = (jnp.arange(tm)[:, None] < jnp.arange(tm)[None, :]).astype(BF16)
    consts = [cnt0, w["gmix"], w["wb"], w["convw"], w["woa"], w["woc"], w["wo"], w["gffn"], w["wrt"], w["brt"], tri]
    in_specs = ([tok(d), tok(N_HEADS * V_DIM), per_b(8)] + [_const(c.shape) for c in consts]
                + [pl.BlockSpec(memory_space=pl.ANY)])
    args = [x, attn, conv_init] + consts + [x1_all]
    return pl.pallas_call(
        _post_kernel,
        grid=(b, tiles_per_b),
        in_specs=in_specs,
        out_specs=[pl.BlockSpec((tm * nchunk, LANE), lambda i, j: (off + i * tiles_per_b + j, 0)),
                   pl.BlockSpec((1, 8, tm), lambda i, j: (i * tiles_per_b + j, 0, 0)),
                   per_b(2), _const((n_keys, 1))],
        out_shape=[jax.ShapeDtypeStruct((n_all * nchunk, LANE), F32),
                   jax.ShapeDtypeStruct((b * tiles_per_b, 8, tm), F32),
                   jax.ShapeDtypeStruct((b, 2, CONV_DIM), F32), jax.ShapeDtypeStruct((n_keys, 1), F32)],
        scratch_shapes=[pltpu.VMEM((8, CONV_DIM), F32)],
        input_output_aliases={len(args) - 1: 0},
        compiler_params=_params(("arbitrary", "arbitrary")),
        name="post",
    )(*args)


def _route_tables(key, rank, counts, n):
    n_keys = counts.shape[0]
    padded = (counts + MOE_ROWS - 1) // MOE_ROWS * MOE_ROWS
    pend = jnp.cumsum(padded)
    pstart = pend - padded
    ids = jnp.arange(n_keys, dtype=jnp.int32)
    dest = rank + jnp.sum(jnp.where(key[:, None] == ids[None, :], pstart[None, :], 0), axis=1)
    nblk = n // MOE_ROWS + N_PAIR_BUCKETS
    blk_start = jnp.arange(nblk, dtype=jnp.int32) * MOE_ROWS
    blk_hot = (blk_start[:, None] >= pstart[None, :]) & (blk_start[:, None] < pend[None, :])
    blk_key = jnp.sum(jnp.where(blk_hot, ids[None, :], 0), axis=1)
    blk_cnt = jnp.sum(jnp.where(blk_hot, jnp.minimum(counts[None, :] - (blk_start[:, None] - pstart[None, :]),
                                                      MOE_ROWS), 0), axis=1)
    any_hot = jnp.any(blk_hot, axis=1)
    blk_key = jnp.where(any_hot, blk_key, n_keys - 1)
    blk_lo = blk_key // EXPERTS_PER_GROUP
    blk_hi = blk_lo // EXPERTS_PER_GROUP * EXPERTS_PER_GROUP + blk_key % EXPERTS_PER_GROUP
    return blk_lo, blk_hi, blk_cnt.astype(jnp.int32), dest.astype(jnp.int32)


def _moe(x1_all, key, rank, counts, w):
    rows, _ = x1_all.shape
    d = w["gffn"].shape[1]
    nchunk = d // LANE
    n = rows // nchunk
    blk_lo, blk_hi, blk_cnt, dest = _route_tables(key, rank, counts, n)
    nblk = blk_lo.shape[0]
    w13_spec = lambda ref_idx: pl.BlockSpec((1, d, 2 * EXPERT_HIDDEN),
                                            lambda i, lo, hi, cnt, dst: ((lo, hi)[ref_idx][i], 0, 0))
    w2_spec = lambda ref_idx: pl.BlockSpec((1, EXPERT_HIDDEN, d),
                                           lambda i, lo, hi, cnt, dst: ((lo, hi)[ref_idx][i], 0, 0))
    buf = pltpu.VMEM((2, MOE_ROWS * nchunk, LANE), F32)
    grid_spec = pltpu.PrefetchScalarGridSpec(
        num_scalar_prefetch=4,
        grid=(nblk,),
        in_specs=[pl.BlockSpec(memory_space=pl.ANY),
                  pl.BlockSpec((1, d), lambda i, *_: (0, 0)),
                  pl.BlockSpec(w["wr"].shape, lambda i, *_: (0, 0)),
                  pl.BlockSpec(w["br"].shape, lambda i, *_: (0, 0)),
                  w13_spec(0), w13_spec(1), w2_spec(0), w2_spec(1)],
        out_specs=pl.BlockSpec(memory_space=pl.ANY),
        scratch_shapes=[pltpu.SMEM((nblk * MOE_ROWS,), jnp.int32), buf, buf,
                        pltpu.SemaphoreType.DMA((2,)), pltpu.SemaphoreType.DMA((2,))],
    )
    return pl.pallas_call(
        _moe_kernel,
        grid_spec=grid_spec,
        out_shape=jax.ShapeDtypeStruct((rows, LANE), F32),
        compiler_params=_params(("arbitrary",)),
        name="moe",
    )(blk_lo, blk_hi, blk_cnt, dest, x1_all, w["gffn"], w["wr"], w["br"], w["w13"], w["w13"], w["w2"], w["w2"])


def _ple(x2_all, row_off, p, w, tm):
    n, pd = p.shape
    d = w["gple"].shape[1]
    nchunk = d // LANE
    off = row_off // tm
    consts = [w["gple"], w["wpg"], w["wple"]]
    return pl.pallas_call(
        _ple_kernel,
        grid=(n // tm,),
        in_specs=[pl.BlockSpec((tm * nchunk, LANE), lambda i: (off + i, 0)), pl.BlockSpec((tm, pd), lambda i: (i, 0))]
        + [_const(c.shape) for c in consts],
        out_specs=pl.BlockSpec((tm, d), lambda i: (i, 0)),
        out_shape=jax.ShapeDtypeStruct((n, d), F32),
        compiler_params=_params(("parallel",)),
        name="ple",
    )(x2_all, p, *consts)


def _mixer(x, past_lat, past_kpe, past_conv, w):
    b, s, _ = x.shape
    tm = min(512, s)
    past_len = 0 if past_lat is None else past_lat.shape[1]
    rope = _rope_slabs(past_len + jnp.arange(s))
    q, k, v, lat_new, kpe_new = _mla_pre(x, w, rope, tm)
    if past_lat is None:
        attn = _attn_prompt(q, k, v, min(512, s))
        conv_init = jnp.zeros((b, 8, CONV_DIM), F32)
    else:
        kp, vp = _kv_past(past_lat, past_kpe, w, min(512, past_len))
        attn = _attn_sample(q, kp, vp, k, v)
        conv_init = jnp.pad(past_conv, ((0, 0), (8 - past_conv.shape[1], 0), (0, 0)))
    return attn, conv_init, lat_new, kpe_new, tm


def _layer(xp, xs, pp, ps, past_lat, past_kpe, past_conv, w):
    bp, sp, d = xp.shape
    bs, ss, _ = xs.shape
    n_p, n_s = bp * sp, bs * ss
    n_all = n_p + n_s
    n_keys = N_EXPERTS * EXPERTS_PER_GROUP
    attn_p, cinit_p, lat_p, kpe_p, tm_p = _mixer(xp, None, None, None, w)
    attn_s, cinit_s, lat_s, kpe_s, tm_s = _mixer(xs, past_lat, past_kpe, past_conv, w)
    x1_all = jnp.zeros((n_all * (d // LANE), LANE), F32)
    x1_all, route_p, conv_p, cnt = _post(xp, attn_p, cinit_p, jnp.zeros((n_keys, 1), F32), x1_all, 0, n_all, w, tm_p)
    x1_all, route_s, conv_s, cnt = _post(xs, attn_s, cinit_s, cnt, x1_all, n_p, n_all, w, tm_s)
    key = jnp.concatenate([route_p[:, 0].reshape(-1), route_s[:, 0].reshape(-1)]).astype(jnp.int32)
    rank = jnp.concatenate([route_p[:, 1].reshape(-1), route_s[:, 1].reshape(-1)]).astype(jnp.int32)
    x2_all = _moe(x1_all, key, rank, cnt[:, 0].astype(jnp.int32), w)
    yp = _ple(x2_all, 0, pp.reshape(n_p, -1), w, min(512, n_p)).reshape(bp, sp, d)
    ys = _ple(x2_all, n_p, ps.reshape(n_s, -1), w, min(512, n_s)).reshape(bs, ss, d)
    return yp, ys, (lat_p, kpe_p, conv_p, lat_s, kpe_s, conv_s)


def kernel(x_prompt, x_sample, cache_kv_latent, cache_k_rope, state_conv, p_prompt, p_sample,
           g_mix, w_in, g_cq, w_uq, g_qn, g_qr, g_ckv, w_ukv, g_kn, g_kr, w_oa,
           conv_w, w_oc, w_o, g_ffn, w_rg, b_rg, w_re, b_re, w1, w3, w2, g_ple, w_pg, w_ple):
    depth = g_mix.shape[0]
    xp, xs = x_prompt, x_sample
    outs = [[] for _ in range(6)]
    for i in range(depth):
        w = _prep_weights(g_mix[i], w_in[i], g_cq[i], w_uq[i], g_qn[i], g_qr[i], g_ckv[i], w_ukv[i],
                          g_kn[i], g_kr[i], w_oa[i], conv_w[i], w_oc[i], w_o[i], g_ffn[i], w_rg[i], b_rg[i],
                          w_re[i], b_re[i], w1[i], w3[i], w2[i], g_ple[i], w_pg[i], w_ple[i])
        xp, xs, new = _layer(xp, xs, p_prompt[i], p_sample[i], cache_kv_latent[i], cache_k_rope[i], state_conv[i], w)
        for o, a in zip(outs, new):
            o.append(a)
    return (xp, xs) + tuple(jnp.stack(o, axis=0) for o in outs)
```

```python
import functools
import math

import jax
import jax.numpy as jnp
from jax import lax
from jax.experimental import pallas as pl
from jax.experimental.pallas import tpu as pltpu

F32 = jnp.float32
BF16 = jnp.bfloat16

LANE = 128
CHUNK = 64
N_HEADS = 8
QK_NOPE = 64
QK_ROPE = 32
V_DIM = 64
Q_LORA = 256
KV_LORA = 256
CONV_DIM = 512
N_GROUPS = 4
EXPERTS_PER_GROUP = 8
N_EXPERTS = N_GROUPS * EXPERTS_PER_GROUP
EXPERT_HIDDEN = 256
ROPE_THETA = 10000.0
EPS = 1e-6
ATTN_SCALE = (QK_NOPE + QK_ROPE) ** -0.5
LOG2E = math.log2(math.e)
NEG_BIG = -1e30
MOE_ROWS = 128
N_PAIR_BUCKETS = N_GROUPS * (EXPERTS_PER_GROUP * (EXPERTS_PER_GROUP - 1) // 2)
VMEM_LIMIT = 56 * 1024 * 1024


def _rms_scale(x, n):
    return lax.rsqrt(jnp.sum(x * x, axis=-1, keepdims=True) * (1.0 / n) + EPS)


def _lane_iota(shape):
    return lax.broadcasted_iota(jnp.int32, shape, len(shape) - 1)


def _rope(t, rc, rs1, rs2):
    return t * rc + pltpu.roll(t, LANE - QK_ROPE // 2, 1) * rs1 + pltpu.roll(t, QK_ROPE // 2, 1) * rs2


def _pre_kernel(x_ref, gmix_ref, wa_ref, gcq_ref, gckv_ref, gkr_ref, wuq_ref, wuk_ref, wuv_ref,
                gq_ref, gk_ref, rc_ref, rs1_ref, rs2_ref,
                q_ref, k_ref, v_ref, lat_ref, kpe_ref):
    x = x_ref[0]
    h = x * _rms_scale(x, x.shape[-1]) * gmix_ref[...]
    z = jnp.dot(h.astype(BF16), wa_ref[...], preferred_element_type=F32)
    cq = z[:, :Q_LORA]
    ckv = z[:, Q_LORA:Q_LORA + KV_LORA]
    kr = z[:, Q_LORA + KV_LORA:]
    rc, rs1, rs2 = rc_ref[...], rs1_ref[...], rs2_ref[...]
    lane = _lane_iota(kr.shape)
    is_nope = lane < QK_NOPE

    cqn = cq * _rms_scale(cq, Q_LORA) * gcq_ref[...]
    lat = ckv * _rms_scale(ckv, KV_LORA) * gckv_ref[...]
    lat_ref[0] = lat
    krn = kr * _rms_scale(kr, QK_ROPE) * gkr_ref[...]
    kpe = _rope(krn, rc, rs1, rs2)
    kpe_ref[0] = kpe[:, QK_NOPE:QK_NOPE + QK_ROPE]

    latb = lat.astype(BF16)
    qf = jnp.dot(cqn.astype(BF16), wuq_ref[...], preferred_element_type=F32)
    kf = jnp.dot(latb, wuk_ref[...], preferred_element_type=F32)
    vf = jnp.dot(latb, wuv_ref[...], preferred_element_type=F32)
    gq, gk = gq_ref[...], gk_ref[...]
    ones_hi = jnp.where(is_nope, 0.0, 1.0)
    for hd in range(N_HEADS):
        sl = slice(hd * LANE, (hd + 1) * LANE)
        qs = qf[:, sl]
        sq = qs * qs
        ss_all = jnp.sum(sq, axis=-1, keepdims=True)
        ss_n = jnp.sum(jnp.where(is_nope, sq, 0.0), axis=-1, keepdims=True)
        r = jnp.where(is_nope, lax.rsqrt(ss_n * (1.0 / QK_NOPE) + EPS),
                      lax.rsqrt((ss_all - ss_n) * (1.0 / QK_ROPE) + EPS))
        q_ref[0, :, sl] = _rope(qs * r * gq, rc, rs1, rs2).astype(BF16)
        ks = kf[:, sl]
        k_ref[0, :, sl] = (ks * _rms_scale(ks, QK_NOPE) * gk + kpe).astype(BF16)
        v_ref[0, :, sl] = (vf[:, sl] + ones_hi).astype(BF16)


def _kvpast_kernel(lat_ref, kpe_ref, wuk_ref, wuv_ref, gk_ref, k_ref, v_ref):
    latb = lat_ref[0].astype(BF16)
    kf = jnp.dot(latb, wuk_ref[...], preferred_element_type=F32)
    vf = jnp.dot(latb, wuv_ref[...], preferred_element_type=F32)
    kpe = kpe_ref[0]
    lane = _lane_iota(kpe.shape)
    ones_hi = jnp.where(lane < QK_NOPE, 0.0, 1.0)
    gk = gk_ref[...]
    for hd in range(N_HEADS):
        sl = slice(hd * LANE, (hd + 1) * LANE)
        ks = kf[:, sl]
        k_ref[0, :, sl] = (ks * _rms_scale(ks, QK_NOPE) * gk + kpe).astype(BF16)
        v_ref[0, :, sl] = (vf[:, sl] + ones_hi).astype(BF16)


def _softmax_step(q, k, v, m, acc, mask):
    s = lax.dot_general(q, k, (((1,), (1,)), ((), ())), preferred_element_type=F32)
    if mask is not None:
        s = jnp.where(mask, s, NEG_BIG)
    m_new = jnp.maximum(m, jnp.max(s, axis=-1, keepdims=True))
    alpha = jnp.exp2(m - m_new)
    p = jnp.exp2(s - m_new)
    acc = acc * alpha + jnp.dot(p.astype(BF16), v, preferred_element_type=F32)
    return m_new, acc


def _finish_pair(accs):
    outs = [a / pltpu.roll(a, V_DIM, 1) for a in accs]
    lane = _lane_iota(outs[0].shape)
    return jnp.where(lane < V_DIM, outs[0], pltpu.roll(outs[1], V_DIM, 1))


def _attn_prompt_kernel(q_ref, k_ref, v_ref, o_ref, s00, s01, s10, s11, p00, p01, p10, p11, *, tq):
    s_refs = ((s00, s01), (s10, s11))
    p_refs = ((p00, p01), (p10, p11))
    qi = pl.program_id(2)
    row = lax.broadcasted_iota(jnp.int32, (tq, tq), 0)
    col = lax.broadcasted_iota(jnp.int32, (tq, tq), 1)
    diag_mask = (col // CHUNK) <= (row // CHUNK)
    heads = [slice(hh * LANE, (hh + 1) * LANE) for hh in range(2)]

    def scores(i, slot, hh):
        start = pl.multiple_of(i * tq, tq)
        s_refs[slot][hh][...] = lax.dot_general(q_ref[0, :, heads[hh]], k_ref[0, pl.ds(start, tq), heads[hh]],
                                                (((1,), (1,)), ((), ())), preferred_element_type=F32)

    def softmax(slot, hh, m, mask):
        s = s_refs[slot][hh][...]
        if mask is not None:
            s = jnp.where(mask, s, NEG_BIG)
        m_new = jnp.maximum(m, jnp.max(s, axis=-1, keepdims=True))
        p_refs[slot][hh][...] = jnp.exp2(s - m_new).astype(BF16)
        return m_new, jnp.exp2(m - m_new)

    def accumulate(i, slot, hh, alpha, acc):
        start = pl.multiple_of(jnp.maximum(i, 0) * tq, tq)
        pv = jnp.dot(p_refs[slot][hh][...], v_ref[0, pl.ds(start, tq), heads[hh]], preferred_element_type=F32)
        return acc * alpha + pv

    def iteration(i, slot, carry):
        out = []
        for hh, (m, alpha, acc) in enumerate(carry):
            acc = accumulate(i - 1, 1 - slot, hh, alpha, acc)
            m, alpha = softmax(slot, hh, m, None)
            scores(i + 1, 1 - slot, hh)
            out.append((m, alpha, acc))
        return tuple(out)

    def last(slot, carry):
        accs = []
        for hh, (m, alpha, acc) in enumerate(carry):
            acc = accumulate(qi - 1, 1 - slot, hh, alpha, acc)
            m, alpha = softmax(slot, hh, m, diag_mask)
            accs.append(accumulate(qi, slot, hh, alpha, acc))
        return tuple(accs)

    for hh in range(2):
        scores(0, 0, hh)
        p_refs[1][hh][...] = jnp.zeros((tq, tq), BF16)

    init = tuple((jnp.full((tq, 1), NEG_BIG, F32), jnp.ones((tq, 1), F32), jnp.zeros((tq, LANE), F32))
                 for _ in heads)
    carry = lax.fori_loop(0, qi // 2, lambda j, c: iteration(2 * j + 1, 1, iteration(2 * j, 0, c)), init)
    accs = lax.cond(qi % 2 == 1, lambda c: last(1, iteration(qi - 1, 0, c)), lambda c: last(0, c), carry)
    o_ref[0] = _finish_pair(list(accs)).astype(o_ref.dtype)


def _attn_sample_kernel(q_ref, kp_ref, vp_ref, kn_ref, vn_ref, o_ref):
    accs = []
    for hh in range(2):
        sl = slice(hh * LANE, (hh + 1) * LANE)
        q = q_ref[0, :, sl]
        m0 = jnp.full((q.shape[0], 1), NEG_BIG, F32)
        acc0 = jnp.zeros((q.shape[0], LANE), F32)
        m, acc = _softmax_step(q, kp_ref[0, :, sl], vp_ref[0, :, sl], m0, acc0, None)
        m, acc = _softmax_step(q, kn_ref[0, :, sl], vn_ref[0, :, sl], m, acc, None)
        accs.append(acc)
    o_ref[0] = _finish_pair(accs).astype(o_ref.dtype)


def _post_kernel(x_ref, attn_ref, cinit_ref, cnt0_ref, gmix_ref, wb_ref, convw_ref, woa_ref, woc_ref, wo_ref,
                 gffn_ref, wrt_ref, brt_ref, tri_ref, x1_all_ref,
                 x1_ref, route_ref, cnew_ref, cnt_ref, carry_ref):
    del x1_all_ref
    si = pl.program_id(1)
    tm = x_ref.shape[1]

    @pl.when(si == 0)
    def _():
        carry_ref[...] = cinit_ref[0]

    @pl.when((si == 0) & (pl.program_id(0) == 0))
    def _():
        cnt_ref[...] = cnt0_ref[...]

    x = x_ref[0]
    h = x * _rms_scale(x, x.shape[-1]) * gmix_ref[...]
    z = jnp.dot(h.astype(BF16), wb_ref[...], preferred_element_type=F32)
    conv_b = z[:, :CONV_DIM]
    u = z[:, CONV_DIM:2 * CONV_DIM] * z[:, 2 * CONV_DIM:3 * CONV_DIM]
    d = x.shape[-1]
    gate_a = z[:, 3 * CONV_DIM:3 * CONV_DIM + d]
    gate_c = z[:, 3 * CONV_DIM + d:]

    carry = carry_ref[...]
    c1 = carry[7:8, :]
    c2 = carry[6:7, :]
    row = lax.broadcasted_iota(jnp.int32, u.shape, 0)
    u_m1 = jnp.where(row == 0, c1, pltpu.roll(u, 1, 0))
    u_m2 = jnp.where(row == 0, c2, jnp.where(row == 1, c1, pltpu.roll(u, 2, 0)))
    cw = convw_ref[...]
    cv = cw[0:1, :] * u_m2 + cw[1:2, :] * u_m1 + cw[2:3, :] * u
    carry_ref[...] = u[tm - 8:, :]
    cnew_ref[0] = u[tm - 2:, :]

    y_c = jnp.dot((conv_b * cv).astype(BF16), woc_ref[...], preferred_element_type=F32)
    y_a = jnp.dot(attn_ref[0], woa_ref[...], preferred_element_type=F32)
    mrg = jax.nn.sigmoid(gate_a) * y_a + jax.nn.sigmoid(gate_c) * y_c
    x1 = x + jnp.dot(mrg.astype(BF16), wo_ref[...], preferred_element_type=F32)
    nchunk = d // LANE
    for c in range(nchunk):
        x1_ref[pl.ds(c, tm, stride=nchunk), :] = x1[:, c * LANE:(c + 1) * LANE]

    h2 = x1 * _rms_scale(x1, d) * gffn_ref[...]
    h2_hi = h2.astype(BF16)
    h2_lo = (h2 - h2_hi.astype(F32)).astype(BF16)
    nt = (((1,), (1,)), ((), ()))
    n_rows = brt_ref.shape[0]
    lt2 = lax.dot_general(wrt_ref[...], h2_hi, nt, preferred_element_type=F32)
    lt = (lt2[:n_rows] + lt2[n_rows:]
          + lax.dot_general(wrt_ref[:n_rows, :], h2_lo, nt, preferred_element_type=F32) + brt_ref[...])
    sub = lax.broadcasted_iota(jnp.int32, (EXPERTS_PER_GROUP, tm), 0).astype(F32)
    none = float(EXPERTS_PER_GROUP)

    def first_argmax(v):
        vmax = jnp.max(v, axis=0, keepdims=True)
        return jnp.min(jnp.where(v == vmax, sub, none), axis=0, keepdims=True)

    g_idx = first_argmax(lt[:EXPERTS_PER_GROUP])
    el = lt[EXPERTS_PER_GROUP * N_GROUPS:]
    for g in range(N_GROUPS - 2, -1, -1):
        el = jnp.where(g_idx == g, lt[EXPERTS_PER_GROUP * (g + 1):EXPERTS_PER_GROUP * (g + 2)], el)
    i1 = first_argmax(el)
    i2 = first_argmax(jnp.where(sub == i1, NEG_BIG, el))
    key = (g_idx * (EXPERTS_PER_GROUP * EXPERTS_PER_GROUP) + jnp.minimum(i1, i2) * EXPERTS_PER_GROUP
           + jnp.maximum(i1, i2))

    n_keys = cnt_ref.shape[0]
    keys = lax.broadcasted_iota(jnp.int32, (n_keys, tm), 0).astype(F32)
    onehot = jnp.where(keys == key, 1.0, 0.0)
    before = jnp.dot(onehot.astype(BF16), tri_ref[...], preferred_element_type=F32)
    rank = jnp.sum(onehot * (before + cnt_ref[...]), axis=0, keepdims=True)
    cnt_ref[...] += jnp.sum(onehot, axis=1, keepdims=True)
    row8 = lax.broadcasted_iota(jnp.int32, (8, tm), 0)
    route_ref[0] = jnp.where(row8 == 0, key, jnp.where(row8 == 1, rank, 0.0))


def _rows_loop(n, fn, unroll=8):
    def group(j, c):
        for u in range(unroll):
            fn(j * unroll + u)
        return c
    lax.fori_loop(0, n // unroll, group, 0)

    def single(i, c):
        fn(i)
        return c
    lax.fori_loop(n // unroll * unroll, n, single, 0)


def _moe_kernel(lo_ref, hi_ref, cnt_ref, dest_ref,
                x1_hbm, gffn_ref, wr_ref, br_ref, w13lo_ref, w13hi_ref, w2lo_ref, w2hi_ref,
                x2_hbm, tok_ref, xbuf, obuf, gsem, ssem):
    nb = pl.program_id(0)
    nblk = pl.num_programs(0)
    nchunk = xbuf.shape[1] // MOE_ROWS
    slot = nb % 2
    cnt = cnt_ref[nb]

    def start_gathers(blk, sl):
        base = blk * MOE_ROWS

        def one(i):
            src = x1_hbm.at[pl.ds(tok_ref[base + i] * nchunk, nchunk)]
            pltpu.make_async_copy(src, xbuf.at[sl, pl.ds(i * nchunk, nchunk)], gsem.at[sl]).start()
        _rows_loop(cnt_ref[blk], one)

    def start_scatters(blk, sl):
        base = blk * MOE_ROWS

        def one(i):
            dst = x2_hbm.at[pl.ds(tok_ref[base + i] * nchunk, nchunk)]
            pltpu.make_async_copy(obuf.at[sl, pl.ds(i * nchunk, nchunk)], dst, ssem.at[sl]).start()
        _rows_loop(cnt_ref[blk], one)

    def wait_gathers(blk, sl):
        rows = cnt_ref[blk] * nchunk
        pltpu.make_async_copy(x1_hbm.at[pl.ds(0, rows)], xbuf.at[sl, pl.ds(0, rows)], gsem.at[sl]).wait()

    def wait_scatters(blk, sl):
        rows = cnt_ref[blk] * nchunk
        pltpu.make_async_copy(obuf.at[sl, pl.ds(0, rows)], x2_hbm.at[pl.ds(0, rows)], ssem.at[sl]).wait()

    @pl.when(nb == 0)
    def _():
        def fill(i, c):
            tok_ref[i] = 0
            return c
        lax.fori_loop(0, tok_ref.shape[0], fill, 0, unroll=8)

        def invert(i, c):
            tok_ref[dest_ref[i]] = i
            return c
        lax.fori_loop(0, dest_ref.shape[0], invert, 0, unroll=8)
        xbuf[...] = jnp.zeros_like(xbuf)

        @pl.when(cnt > 0)
        def _():
            start_gathers(0, 0)

    @pl.when(nb + 1 < nblk)
    def _():
        @pl.when(cnt_ref[nb + 1] > 0)
        def _():
            start_gathers(nb + 1, 1 - slot)

    @pl.when(nb >= 2)
    def _():
        @pl.when(cnt_ref[nb - 2] > 0)
        def _():
            wait_scatters(nb - 2, slot)

    @pl.when(cnt > 0)
    def _():
        wait_gathers(nb, slot)
        xg = jnp.concatenate([xbuf[slot, pl.ds(c, MOE_ROWS, stride=nchunk), :] for c in range(nchunk)], axis=1)
        h = (xg * _rms_scale(xg, xg.shape[-1]) * gffn_ref[...]).astype(BF16)

        lo, hi = lo_ref[nb], hi_ref[nb]
        logit = jnp.dot(h, wr_ref[...], preferred_element_type=F32) + br_ref[...]
        lane = _lane_iota(logit.shape)
        pick = lambda j: jnp.sum(jnp.where(lane == j, logit, 0.0), axis=-1, keepdims=True)
        is_g = lane < N_GROUPS
        gmax = jnp.max(jnp.where(is_g, logit, NEG_BIG), axis=-1, keepdims=True)
        g_den = jnp.sum(jnp.where(is_g, jnp.exp(logit - gmax), 0.0), axis=-1, keepdims=True)
        g_p = jnp.exp(pick(lo // EXPERTS_PER_GROUP) - gmax) / g_den
        l_lo, l_hi = pick(N_GROUPS + lo), pick(N_GROUPS + hi)
        gates = (g_p * jax.nn.sigmoid(l_lo - l_hi), g_p * jax.nn.sigmoid(l_hi - l_lo))

        y = xg
        for w13_ref, w2_ref, gate in ((w13lo_ref, w2lo_ref, gates[0]), (w13hi_ref, w2hi_ref, gates[1])):
            ab = jnp.dot(h, w13_ref[0], preferred_element_type=F32)
            hid = jax.nn.silu(ab[:, :EXPERT_HIDDEN]) * ab[:, EXPERT_HIDDEN:]
            y = y + gate * jnp.dot(hid.astype(BF16), w2_ref[0], preferred_element_type=F32)
        for c in range(nchunk):
            obuf[slot, pl.ds(c, MOE_ROWS, stride=nchunk), :] = y[:, c * LANE:(c + 1) * LANE]
        start_scatters(nb, slot)

    @pl.when(nb == nblk - 1)
    def _():
        @pl.when(nb >= 1)
        def _():
            @pl.when(cnt_ref[nb - 1] > 0)
            def _():
                wait_scatters(nb - 1, 1 - slot)

        @pl.when(cnt > 0)
        def _():
            wait_scatters(nb, slot)


def _ple_kernel(x_ref, p_ref, gple_ref, wpg_ref, wple_ref, o_ref):
    tm = o_ref.shape[0]
    nchunk = o_ref.shape[1] // LANE
    x = jnp.concatenate([x_ref[pl.ds(c, tm, stride=nchunk), :] for c in range(nchunk)], axis=1)
    hp = (x * _rms_scale(x, x.shape[-1]) * gple_ref[...]).astype(BF16)
    gate = jax.nn.sigmoid(jnp.dot(hp, wpg_ref[...], preferred_element_type=F32))
    emb = jnp.dot(p_ref[...].astype(BF16), wple_ref[...], preferred_element_type=F32)
    o_ref[...] = x + gate * emb


def _const(shape):
    nd = len(shape)
    return pl.BlockSpec(shape, lambda *_: (0,) * nd)


def _head_slab_cols(w, width, offset=0):
    k = w.shape[0]
    w = w.reshape(k, N_HEADS, width)
    w = jnp.pad(w, ((0, 0), (0, 0), (offset, LANE - width - offset)))
    return w.reshape(k, N_HEADS * LANE)


def _prep_weights(g_mix, w_in, g_cq, w_uq, g_qn, g_qr, g_ckv, w_ukv, g_kn, g_kr, w_oa,
                  conv_w, w_oc, w_o, g_ffn, w_rg, b_rg, w_re, b_re, w1, w3, w2, g_ple, w_pg, w_ple):
    d = w_in.shape[0]
    n_mla = Q_LORA + KV_LORA
    kr_cols = jnp.pad(w_in[:, n_mla:n_mla + QK_ROPE], ((0, 0), (QK_NOPE, LANE - QK_NOPE - QK_ROPE)))
    w = {}
    w["wa"] = jnp.concatenate([w_in[:, :n_mla], kr_cols], axis=1).astype(BF16)
    w["wb"] = w_in[:, n_mla + QK_ROPE:].astype(BF16)
    w["wuq"] = _head_slab_cols(w_uq, QK_NOPE + QK_ROPE).astype(BF16)
    ukv = w_ukv.reshape(KV_LORA, N_HEADS, QK_NOPE + V_DIM)
    w["wuk"] = _head_slab_cols(ukv[:, :, :QK_NOPE].reshape(KV_LORA, -1), QK_NOPE).astype(BF16)
    w["wuv"] = _head_slab_cols(ukv[:, :, QK_NOPE:].reshape(KV_LORA, -1), V_DIM).astype(BF16)
    pad_hi = LANE - QK_NOPE - QK_ROPE
    w["gq"] = (jnp.pad(jnp.concatenate([g_qn, g_qr]), (0, pad_hi)) * (ATTN_SCALE * LOG2E))[None]
    w["gk"] = jnp.pad(g_kn, (0, LANE - QK_NOPE))[None]
    w["gkr"] = jnp.pad(g_kr, (QK_NOPE, pad_hi))[None]
    w["gmix"], w["gcq"], w["gckv"] = g_mix[None], g_cq[None], g_ckv[None]
    w["gffn"], w["gple"] = g_ffn[None], g_ple[None]
    w["convw"] = jnp.pad(conv_w, ((0, 8 - conv_w.shape[0]), (0, 0)))
    w["woa"], w["woc"], w["wo"] = w_oa.astype(BF16), w_oc.astype(BF16), w_o.astype(BF16)
    n_r = N_GROUPS + N_EXPERTS
    w["wr"] = jnp.pad(jnp.concatenate([w_rg, w_re], axis=1), ((0, 0), (0, LANE - n_r))).astype(BF16)
    w["br"] = jnp.pad(jnp.concatenate([b_rg, b_re]), (0, LANE - n_r))[None]
    pad_g = EXPERTS_PER_GROUP - N_GROUPS
    wrt = jnp.concatenate([jnp.pad(w_rg.T, ((0, pad_g), (0, 0))), w_re.T], axis=0)
    wrt_hi = wrt.astype(BF16)
    w["wrt"] = jnp.concatenate([wrt_hi, (wrt - wrt_hi.astype(F32)).astype(BF16)], axis=0)
    w["brt"] = jnp.concatenate([b_rg, jnp.full((pad_g,), NEG_BIG, F32), b_re])[:, None]
    w["w13"] = jnp.concatenate([w1, w3], axis=2).astype(BF16)
    w["w2"] = w2.astype(BF16)
    w["wpg"], w["wple"] = w_pg.astype(BF16), w_ple.astype(BF16)
    return w


def _rope_slabs(pos):
    inv = 1.0 / (ROPE_THETA ** (jnp.arange(0, QK_ROPE, 2, dtype=F32) / QK_ROPE))
    ang = pos.astype(F32)[:, None] * inv[None, :]
    cos, sin = jnp.cos(ang), jnp.sin(ang)
    n = pos.shape[0]
    half = QK_ROPE // 2
    z = lambda k: jnp.zeros((n, k), F32)
    pad_hi = LANE - QK_NOPE - QK_ROPE
    rc = jnp.concatenate([jnp.ones((n, QK_NOPE), F32), cos, cos, z(pad_hi)], axis=1)
    rs1 = jnp.concatenate([z(QK_NOPE), -sin, z(half), z(pad_hi)], axis=1)
    rs2 = jnp.concatenate([z(QK_NOPE), z(half), sin, z(pad_hi)], axis=1)
    return rc, rs1, rs2


def _params(sem):
    return pltpu.CompilerParams(dimension_semantics=sem, vmem_limit_bytes=VMEM_LIMIT)


def _mla_pre(x, w, rope, tm):
    b, s, d = x.shape
    hw = N_HEADS * LANE
    tok = lambda width: pl.BlockSpec((1, tm, width), lambda i, j: (i, j, 0))
    rope_spec = pl.BlockSpec((tm, LANE), lambda i, j: (j, 0))
    consts = [w["gmix"], w["wa"], w["gcq"], w["gckv"], w["gkr"], w["wuq"], w["wuk"], w["wuv"], w["gq"], w["gk"]]
    return pl.pallas_call(
        _pre_kernel,
        grid=(b, s // tm),
        in_specs=[tok(d)] + [_const(c.shape) for c in consts] + [rope_spec] * 3,
        out_specs=[tok(hw), tok(hw), tok(hw), tok(KV_LORA), tok(QK_ROPE)],
        out_shape=[jax.ShapeDtypeStruct((b, s, hw), BF16)] * 3
        + [jax.ShapeDtypeStruct((b, s, KV_LORA), F32), jax.ShapeDtypeStruct((b, s, QK_ROPE), F32)],
        compiler_params=_params(("parallel", "parallel")),
        name="mla_pre",
    )(x, *consts, *rope)


def _kv_past(past_lat, past_kpe, w, tm):
    b, t, _ = past_lat.shape
    hw = N_HEADS * LANE
    kpe_slab = jnp.pad(past_kpe, ((0, 0), (0, 0), (QK_NOPE, LANE - QK_NOPE - QK_ROPE)))
    tok = lambda width: pl.BlockSpec((1, tm, width), lambda i, j: (i, j, 0))
    consts = [w["wuk"], w["wuv"], w["gk"]]
    return pl.pallas_call(
        _kvpast_kernel,
        grid=(b, t // tm),
        in_specs=[tok(KV_LORA), tok(LANE)] + [_const(c.shape) for c in consts],
        out_specs=[tok(hw), tok(hw)],
        out_shape=[jax.ShapeDtypeStruct((b, t, hw), BF16)] * 2,
        compiler_params=_params(("parallel", "parallel")),
        name="kv_past",
    )(past_lat, kpe_slab, *consts)


def _attn_prompt(q, k, v, tq):
    b, s, _ = q.shape
    pair = 2 * LANE
    return pl.pallas_call(
        functools.partial(_attn_prompt_kernel, tq=tq),
        grid=(b, N_HEADS // 2, s // tq),
        in_specs=[pl.BlockSpec((1, tq, pair), lambda i, h, j: (i, j, h)),
                  pl.BlockSpec((1, s, pair), lambda i, h, j: (i, 0, h)),
                  pl.BlockSpec((1, s, pair), lambda i, h, j: (i, 0, h))],
        out_specs=pl.BlockSpec((1, tq, LANE), lambda i, h, j: (i, j, h)),
        out_shape=jax.ShapeDtypeStruct((b, s, N_HEADS * V_DIM), BF16),
        scratch_shapes=[pltpu.VMEM((tq, tq), F32)] * 4 + [pltpu.VMEM((tq, tq), BF16)] * 4,
        compiler_params=_params(("parallel", "parallel", "arbitrary")),
        name="attn_prompt",
    )(q, k, v)


def _attn_sample(q, kp, vp, kn, vn):
    b, s, _ = q.shape
    t = kp.shape[1]
    pair = 2 * LANE
    blk = lambda rows: pl.BlockSpec((1, rows, pair), lambda i, h: (i, 0, h))
    return pl.pallas_call(
        _attn_sample_kernel,
        grid=(b, N_HEADS // 2),
        in_specs=[blk(s), blk(t), blk(t), blk(s), blk(s)],
        out_specs=pl.BlockSpec((1, s, LANE), lambda i, h: (i, 0, h)),
        out_shape=jax.ShapeDtypeStruct((b, s, N_HEADS * V_DIM), BF16),
        compiler_params=_params(("parallel", "parallel")),
        name="attn_sample",
    )(q, kp, vp, kn, vn)


def _post(x, attn, conv_init, cnt0, x1_all, row_off, n_all, w, tm):
    b, s, d = x.shape
    nchunk = d // LANE
    n_keys = cnt0.shape[0]
    tiles_per_b = s // tm
    off = row_off // tm
    tok = lambda width: pl.BlockSpec((1, tm, width), lambda i, j: (i, j, 0))
    per_b = lambda rows: pl.BlockSpec((1, rows, CONV_DIM), lambda i, j: (i, 0, 0))
    tri = (jnp.arange(tm)[:, None] < jnp.arange(tm)[None, :]).astype(BF16)
    consts = [cnt0, w["gmix"], w["wb"], w["convw"], w["woa"], w["woc"], w["wo"], w["gffn"], w["wrt"], w["brt"], tri]
    in_specs = ([tok(d), tok(N_HEADS * V_DIM), per_b(8)] + [_const(c.shape) for c in consts]
                + [pl.BlockSpec(memory_space=pl.ANY)])
    args = [x, attn, conv_init] + consts + [x1_all]
    return pl.pallas_call(
        _post_kernel,
        grid=(b, tiles_per_b),
        in_specs=in_specs,
        out_specs=[pl.BlockSpec((tm * nchunk, LANE), lambda i, j: (off + i * tiles_per_b + j, 0)),
                   pl.BlockSpec((1, 8, tm), lambda i, j: (i * tiles_per_b + j, 0, 0)),
                   per_b(2), _const((n_keys, 1))],
        out_shape=[jax.ShapeDtypeStruct((n_all * nchunk, LANE), F32),
                   jax.ShapeDtypeStruct((b * tiles_per_b, 8, tm), F32),
                   jax.ShapeDtypeStruct((b, 2, CONV_DIM), F32), jax.ShapeDtypeStruct((n_keys, 1), F32)],
        scratch_shapes=[pltpu.VMEM((8, CONV_DIM), F32)],
        input_output_aliases={len(args) - 1: 0},
        compiler_params=_params(("arbitrary", "arbitrary")),
        name="post",
    )(*args)


def _route_tables(key, rank, counts, n):
    n_keys = counts.shape[0]
    padded = (counts + MOE_ROWS - 1) // MOE_ROWS * MOE_ROWS
    pend = jnp.cumsum(padded)
    pstart = pend - padded
    ids = jnp.arange(n_keys, dtype=jnp.int32)
    dest = rank + jnp.sum(jnp.where(key[:, None] == ids[None, :], pstart[None, :], 0), axis=1)
    nblk = n // MOE_ROWS + N_PAIR_BUCKETS
    blk_start = jnp.arange(nblk, dtype=jnp.int32) * MOE_ROWS
    blk_hot = (blk_start[:, None] >= pstart[None, :]) & (blk_start[:, None] < pend[None, :])
    blk_key = jnp.sum(jnp.where(blk_hot, ids[None, :], 0), axis=1)
    blk_cnt = jnp.sum(jnp.where(blk_hot, jnp.minimum(counts[None, :] - (blk_start[:, None] - pstart[None, :]),
                                                      MOE_ROWS), 0), axis=1)
    any_hot = jnp.any(blk_hot, axis=1)
    blk_key = jnp.where(any_hot, blk_key, n_keys - 1)
    blk_lo = blk_key // EXPERTS_PER_GROUP
    blk_hi = blk_lo // EXPERTS_PER_GROUP * EXPERTS_PER_GROUP + blk_key % EXPERTS_PER_GROUP
    return blk_lo, blk_hi, blk_cnt.astype(jnp.int32), dest.astype(jnp.int32)


def _moe(x1_all, key, rank, counts, w):
    rows, _ = x1_all.shape
    d = w["gffn"].shape[1]
    nchunk = d // LANE
    n = rows // nchunk
    blk_lo, blk_hi, blk_cnt, dest = _route_tables(key, rank, counts, n)
    nblk = blk_lo.shape[0]
    w13_spec = lambda ref_idx: pl.BlockSpec((1, d, 2 * EXPERT_HIDDEN),
                                            lambda i, lo, hi, cnt, dst: ((lo, hi)[ref_idx][i], 0, 0))
    w2_spec = lambda ref_idx: pl.BlockSpec((1, EXPERT_HIDDEN, d),
                                           lambda i, lo, hi, cnt, dst: ((lo, hi)[ref_idx][i], 0, 0))
    buf = pltpu.VMEM((2, MOE_ROWS * nchunk, LANE), F32)
    grid_spec = pltpu.PrefetchScalarGridSpec(
        num_scalar_prefetch=4,
        grid=(nblk,),
        in_specs=[pl.BlockSpec(memory_space=pl.ANY),
                  pl.BlockSpec((1, d), lambda i, *_: (0, 0)),
                  pl.BlockSpec(w["wr"].shape, lambda i, *_: (0, 0)),
                  pl.BlockSpec(w["br"].shape, lambda i, *_: (0, 0)),
                  w13_spec(0), w13_spec(1), w2_spec(0), w2_spec(1)],
        out_specs=pl.BlockSpec(memory_space=pl.ANY),
        scratch_shapes=[pltpu.SMEM((nblk * MOE_ROWS,), jnp.int32), buf, buf,
                        pltpu.SemaphoreType.DMA((2,)), pltpu.SemaphoreType.DMA((2,))],
    )
    return pl.pallas_call(
        _moe_kernel,
        grid_spec=grid_spec,
        out_shape=jax.ShapeDtypeStruct((rows, LANE), F32),
        compiler_params=_params(("arbitrary",)),
        name="moe",
    )(blk_lo, blk_hi, blk_cnt, dest, x1_all, w["gffn"], w["wr"], w["br"], w["w13"], w["w13"], w["w2"], w["w2"])


def _ple(x2_all, row_off, p, w, tm):
    n, pd = p.shape
    d = w["gple"].shape[1]
    nchunk = d // LANE
    off = row_off // tm
    consts = [w["gple"], w["wpg"], w["wple"]]
    return pl.pallas_call(
        _ple_kernel,
        grid=(n // tm,),
        in_specs=[pl.BlockSpec((tm * nchunk, LANE), lambda i: (off + i, 0)), pl.BlockSpec((tm, pd), lambda i: (i, 0))]
        + [_const(c.shape) for c in consts],
        out_specs=pl.BlockSpec((tm, d), lambda i: (i, 0)),
        out_shape=jax.ShapeDtypeStruct((n, d), F32),
        compiler_params=_params(("parallel",)),
        name="ple",
    )(x2_all, p, *consts)


def _mixer(x, past_lat, past_kpe, past_conv, w):
    b, s, _ = x.shape
    tm = min(512, s)
    past_len = 0 if past_lat is None else past_lat.shape[1]
    rope = _rope_slabs(past_len + jnp.arange(s))
    q, k, v, lat_new, kpe_new = _mla_pre(x, w, rope, tm)
    if past_lat is None:
        attn = _attn_prompt(q, k, v, min(512, s))
        conv_init = jnp.zeros((b, 8, CONV_DIM), F32)
    else:
        kp, vp = _kv_past(past_lat, past_kpe, w, min(512, past_len))
        attn = _attn_sample(q, kp, vp, k, v)
        conv_init = jnp.pad(past_conv, ((0, 0), (8 - past_conv.shape[1], 0), (0, 0)))
    return attn, conv_init, lat_new, kpe_new, tm


def _layer(xp, xs, pp, ps, past_lat, past_kpe, past_conv, w):
    bp, sp, d = xp.shape
    bs, ss, _ = xs.shape
    n_p, n_s = bp * sp, bs * ss
    n_all = n_p + n_s
    n_keys = N_EXPERTS * EXPERTS_PER_GROUP
    attn_p, cinit_p, lat_p, kpe_p, tm_p = _mixer(xp, None, None, None, w)
    attn_s, cinit_s, lat_s, kpe_s, tm_s = _mixer(xs, past_lat, past_kpe, past_conv, w)
    x1_all = jnp.zeros((n_all * (d // LANE), LANE), F32)
    x1_all, route_p, conv_p, cnt = _post(xp, attn_p, cinit_p, jnp.zeros((n_keys, 1), F32), x1_all, 0, n_all, w, tm_p)
    x1_all, route_s, conv_s, cnt = _post(xs, attn_s, cinit_s, cnt, x1_all, n_p, n_all, w, tm_s)
    key = jnp.concatenate([route_p[:, 0].reshape(-1), route_s[:, 0].reshape(-1)]).astype(jnp.int32)
    rank = jnp.concatenate([route_p[:, 1].reshape(-1), route_s[:, 1].reshape(-1)]).astype(jnp.int32)
    x2_all = _moe(x1_all, key, rank, cnt[:, 0].astype(jnp.int32), w)
    yp = _ple(x2_all, 0, pp.reshape(n_p, -1), w, min(512, n_p)).reshape(bp, sp, d)
    ys = _ple(x2_all, n_p, ps.reshape(n_s, -1), w, min(512, n_s)).reshape(bs, ss, d)
    return yp, ys, (lat_p, kpe_p, conv_p, lat_s, kpe_s, conv_s)


def kernel(x_prompt, x_sample, cache_kv_latent, cache_k_rope, state_conv, p_prompt, p_sample,
           g_mix, w_in, g_cq, w_uq, g_qn, g_qr, g_ckv, w_ukv, g_kn, g_kr, w_oa,
           conv_w, w_oc, w_o, g_ffn, w_rg, b_rg, w_re, b_re, w1, w3, w2, g_ple, w_pg, w_ple):
    depth = g_mix.shape[0]
    xp, xs = x_prompt, x_sample
    outs = [[] for _ in range(6)]
    for i in range(depth):
        w = _prep_weights(g_mix[i], w_in[i], g_cq[i], w_uq[i], g_qn[i], g_qr[i], g_ckv[i], w_ukv[i],
                          g_kn[i], g_kr[i], w_oa[i], conv_w[i], w_oc[i], w_o[i], g_ffn[i], w_rg[i], b_rg[i],
                          w_re[i], b_re[i], w1[i], w3[i], w2[i], g_ple[i], w_pg[i], w_ple[i])
        xp, xs, new = _layer(xp, xs, p_prompt[i], p_sample[i], cache_kv_latent[i], cache_k_rope[i], state_conv[i], w)
        for o, a in zip(outs, new):
            o.append(a)
    return (xp, xs) + tuple(jnp.stack(o, axis=0) for o in outs)
```

```python
import functools
import math

import jax
import jax.numpy as jnp
from jax import lax
from jax.experimental import pallas as pl
from jax.experimental.pallas import tpu as pltpu

F32 = jnp.float32
BF16 = jnp.bfloat16

LANE = 128
CHUNK = 64
N_HEADS = 8
QK_NOPE = 64
QK_ROPE = 32
V_DIM = 64
Q_LORA = 256
KV_LORA = 256
CONV_DIM = 512
N_GROUPS = 4
EXPERTS_PER_GROUP = 8
N_EXPERTS = N_GROUPS * EXPERTS_PER_GROUP
EXPERT_HIDDEN = 256
ROPE_THETA = 10000.0
EPS = 1e-6
ATTN_SCALE = (QK_NOPE + QK_ROPE) ** -0.5
LOG2E = math.log2(math.e)
NEG_BIG = -1e30
MOE_ROWS = 128
N_PAIR_BUCKETS = N_GROUPS * (EXPERTS_PER_GROUP * (EXPERTS_PER_GROUP - 1) // 2)
VMEM_LIMIT = 56 * 1024 * 1024


def _rms_scale(x, n):
    return lax.rsqrt(jnp.sum(x * x, axis=-1, keepdims=True) * (1.0 / n) + EPS)


def _lane_iota(shape):
    return lax.broadcasted_iota(jnp.int32, shape, len(shape) - 1)


def _rope(t, rc, rs1, rs2):
    return t * rc + pltpu.roll(t, LANE - QK_ROPE // 2, 1) * rs1 + pltpu.roll(t, QK_ROPE // 2, 1) * rs2


def _pre_latents(x_ref, gmix_ref, wa_ref, gcq_ref, gckv_ref, gkr_ref, rope_refs, lat_ref, kpe_ref):
    x = x_ref[0]
    h = x * _rms_scale(x, x.shape[-1]) * gmix_ref[...]
    z = jnp.dot(h.astype(BF16), wa_ref[...], preferred_element_type=F32)
    cq = z[:, :Q_LORA]
    ckv = z[:, Q_LORA:Q_LORA + KV_LORA]
    kr = z[:, Q_LORA + KV_LORA:]
    cqn = cq * _rms_scale(cq, Q_LORA) * gcq_ref[...]
    lat = ckv * _rms_scale(ckv, KV_LORA) * gckv_ref[...]
    lat_ref[0] = lat
    krn = kr * _rms_scale(kr, QK_ROPE) * gkr_ref[...]
    kpe = _rope(krn, *(r[...] for r in rope_refs))
    kpe_ref[0] = kpe[:, QK_NOPE:QK_NOPE + QK_ROPE]
    return cqn, lat, kpe


def _store_keys(k_ref, kf, gk, kpe):
    for hd in range(N_HEADS):
        sl = slice(hd * LANE, (hd + 1) * LANE)
        ks = kf[:, sl]
        k_ref[0, :, sl] = (ks * _rms_scale(ks, QK_NOPE) * gk + kpe).astype(BF16)


def _pre_kernel(x_ref, gmix_ref, wa_ref, gcq_ref, gckv_ref, gkr_ref, wuq_ref, wuk_ref, wuv_ref,
                gq_ref, gk_ref, rc_ref, rs1_ref, rs2_ref,
                q_ref, k_ref, v_ref, lat_ref, kpe_ref):
    rope_refs = (rc_ref, rs1_ref, rs2_ref)
    cqn, lat, kpe = _pre_latents(x_ref, gmix_ref, wa_ref, gcq_ref, gckv_ref, gkr_ref, rope_refs, lat_ref, kpe_ref)
    rc, rs1, rs2 = (r[...] for r in rope_refs)
    is_nope = _lane_iota(kpe.shape) < QK_NOPE
    latb = lat.astype(BF16)
    qf = jnp.dot(cqn.astype(BF16), wuq_ref[...], preferred_element_type=F32)
    vf = jnp.dot(latb, wuv_ref[...], preferred_element_type=F32)
    _store_keys(k_ref, jnp.dot(latb, wuk_ref[...], preferred_element_type=F32), gk_ref[...], kpe)
    gq = gq_ref[...]
    ones_hi = jnp.where(is_nope, 0.0, 1.0)
    for hd in range(N_HEADS):
        sl = slice(hd * LANE, (hd + 1) * LANE)
        qs = qf[:, sl]
        sq = qs * qs
        ss_all = jnp.sum(sq, axis=-1, keepdims=True)
        ss_n = jnp.sum(jnp.where(is_nope, sq, 0.0), axis=-1, keepdims=True)
        r = jnp.where(is_nope, lax.rsqrt(ss_n * (1.0 / QK_NOPE) + EPS),
                      lax.rsqrt((ss_all - ss_n) * (1.0 / QK_ROPE) + EPS))
        q_ref[0, :, sl] = _rope(qs * r * gq, rc, rs1, rs2).astype(BF16)
        v_ref[0, :, sl] = (vf[:, sl] + ones_hi).astype(BF16)


def _pre_kernel_t(x_ref, gmix_ref, wa_ref, gcq_ref, gckv_ref, gkr_ref, wuqt_ref, wuk_ref, wuvt_ref,
                  gqt_ref, gk_ref, rc_ref, rs1_ref, rs2_ref, cos_ref, sin_ref,
                  qt_ref, k_ref, vt_ref, lat_ref, kpe_ref):
    rope_refs = (rc_ref, rs1_ref, rs2_ref)
    cqn, lat, kpe = _pre_latents(x_ref, gmix_ref, wa_ref, gcq_ref, gckv_ref, gkr_ref, rope_refs, lat_ref, kpe_ref)
    latb = lat.astype(BF16)
    _store_keys(k_ref, jnp.dot(latb, wuk_ref[...], preferred_element_type=F32), gk_ref[...], kpe)
    tm = cqn.shape[0]
    qft = jnp.dot(wuqt_ref[...], cqn.T.astype(BF16), preferred_element_type=F32)
    vft = jnp.dot(wuvt_ref[...], lat.T.astype(BF16), preferred_element_type=F32)
    gq, cos, sin = gqt_ref[...], cos_ref[...], sin_ref[...]
    half = QK_ROPE // 2
    ones_lo = jnp.where(lax.broadcasted_iota(jnp.int32, (LANE, tm), 0) < V_DIM, 0.0, 1.0)
    pad = jnp.zeros((LANE - QK_NOPE - QK_ROPE, tm), F32)
    for hd in range(N_HEADS):
        rows = slice(hd * LANE, (hd + 1) * LANE)
        qs = qft[rows]
        sq = qs * qs
        r_n = lax.rsqrt(jnp.sum(sq[:QK_NOPE], axis=0, keepdims=True) * (1.0 / QK_NOPE) + EPS)
        r_p = lax.rsqrt(jnp.sum(sq[QK_NOPE:QK_NOPE + QK_ROPE], axis=0, keepdims=True) * (1.0 / QK_ROPE) + EPS)
        nope = qs[:QK_NOPE] * r_n * gq[:QK_NOPE]
        x1 = qs[QK_NOPE:QK_NOPE + half] * r_p * gq[QK_NOPE:QK_NOPE + half]
        x2 = qs[QK_NOPE + half:QK_NOPE + QK_ROPE] * r_p * gq[QK_NOPE + half:QK_NOPE + QK_ROPE]
        slab = jnp.concatenate([nope, x1 * cos - x2 * sin, x1 * sin + x2 * cos, pad], axis=0)
        qt_ref[0, 0, rows, :] = slab.astype(BF16)
        vt_ref[0, 0, rows, :] = (vft[rows] + ones_lo).astype(BF16)


def _kvpast_kernel(lat_ref, kpe_ref, wuk_ref, wuv_ref, gk_ref, k_ref, v_ref):
    latb = lat_ref[0].astype(BF16)
    kf = jnp.dot(latb, wuk_ref[...], preferred_element_type=F32)
    vf = jnp.dot(latb, wuv_ref[...], preferred_element_type=F32)
    kpe = kpe_ref[0]
    lane = _lane_iota(kpe.shape)
    ones_hi = jnp.where(lane < QK_NOPE, 0.0, 1.0)
    gk = gk_ref[...]
    for hd in range(N_HEADS):
        sl = slice(hd * LANE, (hd + 1) * LANE)
        ks = kf[:, sl]
        k_ref[0, :, sl] = (ks * _rms_scale(ks, QK_NOPE) * gk + kpe).astype(BF16)
        v_ref[0, :, sl] = (vf[:, sl] + ones_hi).astype(BF16)


def _softmax_step(q, k, v, m, acc, mask):
    s = lax.dot_general(q, k, (((1,), (1,)), ((), ())), preferred_element_type=F32)
    if mask is not None:
        s = jnp.where(mask, s, NEG_BIG)
    m_new = jnp.maximum(m, jnp.max(s, axis=-1, keepdims=True))
    alpha = jnp.exp2(m - m_new)
    p = jnp.exp2(s - m_new)
    acc = acc * alpha + jnp.dot(p.astype(BF16), v, preferred_element_type=F32)
    return m_new, acc


def _finish_pair(accs):
    outs = [a / pltpu.roll(a, V_DIM, 1) for a in accs]
    lane = _lane_iota(outs[0].shape)
    return jnp.where(lane < V_DIM, outs[0], pltpu.roll(outs[1], V_DIM, 1))


def _attn_prompt_kernel(qt_ref, k_ref, vt_ref, o_ref, s00, s01, s10, s11, p00, p01, p10, p11, *, tq):
    s_refs = ((s00, s01), (s10, s11))
    p_refs = ((p00, p01), (p10, p11))
    qi = pl.program_id(2)
    key_pos = lax.broadcasted_iota(jnp.int32, (tq, tq), 0)
    query_pos = lax.broadcasted_iota(jnp.int32, (tq, tq), 1)
    diag_mask = (key_pos // CHUNK) <= (query_pos // CHUNK)
    heads = [slice(hh * LANE, (hh + 1) * LANE) for hh in range(2)]

    def scores(i, slot, hh):
        start = pl.multiple_of(i * tq, tq)
        s_refs[slot][hh][...] = jnp.dot(k_ref[0, pl.ds(start, tq), heads[hh]], qt_ref[0, 0, heads[hh], :],
                                        preferred_element_type=F32)

    def softmax(slot, hh, m, mask):
        s = s_refs[slot][hh][...]
        if mask is not None:
            s = jnp.where(mask, s, NEG_BIG)
        m_new = jnp.maximum(m, jnp.max(s, axis=0, keepdims=True))
        p_refs[slot][hh][...] = jnp.exp2(s - m_new).astype(BF16)
        return m_new, jnp.exp2(m - m_new)

    def accumulate(i, slot, hh, alpha, acc):
        pv = jnp.dot(vt_ref[0, jnp.maximum(i, 0), heads[hh], :], p_refs[slot][hh][...],
                     preferred_element_type=F32)
        return acc * alpha + pv

    def iteration(i, slot, carry):
        out = []
        for hh, (m, alpha, acc) in enumerate(carry):
            acc = accumulate(i - 1, 1 - slot, hh, alpha, acc)
            m, alpha = softmax(slot, hh, m, None)
            scores(i + 1, 1 - slot, hh)
            out.append((m, alpha, acc))
        return tuple(out)

    def last(slot, carry):
        accs = []
        for hh, (m, alpha, acc) in enumerate(carry):
            acc = accumulate(qi - 1, 1 - slot, hh, alpha, acc)
            m, alpha = softmax(slot, hh, m, diag_mask)
            accs.append(accumulate(qi, slot, hh, alpha, acc))
        return tuple(accs)

    for hh in range(2):
        scores(0, 0, hh)
        p_refs[1][hh][...] = jnp.zeros((tq, tq), BF16)

    init = tuple((jnp.full((1, tq), NEG_BIG, F32), jnp.ones((1, tq), F32), jnp.zeros((LANE, tq), F32))
                 for _ in heads)
    carry = lax.fori_loop(0, qi // 2, lambda j, c: iteration(2 * j + 1, 1, iteration(2 * j, 0, c)), init)
    accs = lax.cond(qi % 2 == 1, lambda c: last(1, iteration(qi - 1, 0, c)), lambda c: last(0, c), carry)
    out_t = jnp.concatenate([a[:V_DIM] / a[V_DIM:V_DIM + 1] for a in accs], axis=0)
    o_ref[0] = out_t.T.astype(o_ref.dtype)


def _attn_sample_kernel(q_ref, kp_ref, vp_ref, kn_ref, vn_ref, o_ref):
    accs = []
    for hh in range(2):
        sl = slice(hh * LANE, (hh + 1) * LANE)
        q = q_ref[0, :, sl]
        m0 = jnp.full((q.shape[0], 1), NEG_BIG, F32)
        acc0 = jnp.zeros((q.shape[0], LANE), F32)
        m, acc = _softmax_step(q, kp_ref[0, :, sl], vp_ref[0, :, sl], m0, acc0, None)
        m, acc = _softmax_step(q, kn_ref[0, :, sl], vn_ref[0, :, sl], m, acc, None)
        accs.append(acc)
    o_ref[0] = _finish_pair(accs).astype(o_ref.dtype)


def _post_kernel(x_ref, attn_ref, cinit_ref, cnt0_ref, gmix_ref, wb_ref, convw_ref, woa_ref, woc_ref, wo_ref,
                 gffn_ref, wrt_ref, brt_ref, tri_ref, x1_all_ref,
                 x1_ref, route_ref, cnew_ref, cnt_ref, carry_ref):
    del x1_all_ref
    si = pl.program_id(1)
    tm = x_ref.shape[1]

    @pl.when(si == 0)
    def _():
        carry_ref[...] = cinit_ref[0]

    @pl.when((si == 0) & (pl.program_id(0) == 0))
    def _():
        cnt_ref[...] = cnt0_ref[...]

    x = x_ref[0]
    h = x * _rms_scale(x, x.shape[-1]) * gmix_ref[...]
    z = jnp.dot(h.astype(BF16), wb_ref[...], preferred_element_type=F32)
    conv_b = z[:, :CONV_DIM]
    u = z[:, CONV_DIM:2 * CONV_DIM] * z[:, 2 * CONV_DIM:3 * CONV_DIM]
    d = x.shape[-1]
    gate_a = z[:, 3 * CONV_DIM:3 * CONV_DIM + d]
    gate_c = z[:, 3 * CONV_DIM + d:]

    carry = carry_ref[...]
    c1 = carry[7:8, :]
    c2 = carry[6:7, :]
    row = lax.broadcasted_iota(jnp.int32, u.shape, 0)
    u_m1 = jnp.where(row == 0, c1, pltpu.roll(u, 1, 0))
    u_m2 = jnp.where(row == 0, c2, jnp.where(row == 1, c1, pltpu.roll(u, 2, 0)))
    cw = convw_ref[...]
    cv = cw[0:1, :] * u_m2 + cw[1:2, :] * u_m1 + cw[2:3, :] * u
    carry_ref[...] = u[tm - 8:, :]
    cnew_ref[0] = u[tm - 2:, :]

    y_c = jnp.dot((conv_b * cv).astype(BF16), woc_ref[...], preferred_element_type=F32)
    y_a = jnp.dot(attn_ref[0], woa_ref[...], preferred_element_type=F32)
    mrg = jax.nn.sigmoid(gate_a) * y_a + jax.nn.sigmoid(gate_c) * y_c
    x1 = x + jnp.dot(mrg.astype(BF16), wo_ref[...], preferred_element_type=F32)
    nchunk = d // LANE
    for c in range(nchunk):
        x1_ref[pl.ds(c, tm, stride=nchunk), :] = x1[:, c * LANE:(c + 1) * LANE]

    h2 = x1 * _rms_scale(x1, d) * gffn_ref[...]
    h2_hi = h2.astype(BF16)
    h2_lo = (h2 - h2_hi.astype(F32)).astype(BF16)
    nt = (((1,), (1,)), ((), ()))
    n_rows = brt_ref.shape[0]
    lt2 = lax.dot_general(wrt_ref[...], h2_hi, nt, preferred_element_type=F32)
    lt = (lt2[:n_rows] + lt2[n_rows:]
          + lax.dot_general(wrt_ref[:n_rows, :], h2_lo, nt, preferred_element_type=F32) + brt_ref[...])
    sub = lax.broadcasted_iota(jnp.int32, (EXPERTS_PER_GROUP, tm), 0).astype(F32)
    none = float(EXPERTS_PER_GROUP)

    def first_argmax(v):
        vmax = jnp.max(v, axis=0, keepdims=True)
        return jnp.min(jnp.where(v == vmax, sub, none), axis=0, keepdims=True)

    g_idx = first_argmax(lt[:EXPERTS_PER_GROUP])
    el = lt[EXPERTS_PER_GROUP * N_GROUPS:]
    for g in range(N_GROUPS - 2, -1, -1):
        el = jnp.where(g_idx == g, lt[EXPERTS_PER_GROUP * (g + 1):EXPERTS_PER_GROUP * (g + 2)], el)
    i1 = first_argmax(el)
    i2 = first_argmax(jnp.where(sub == i1, NEG_BIG, el))
    key = (g_idx * (EXPERTS_PER_GROUP * EXPERTS_PER_GROUP) + jnp.minimum(i1, i2) * EXPERTS_PER_GROUP
           + jnp.maximum(i1, i2))

    n_keys = cnt_ref.shape[0]
    keys = lax.broadcasted_iota(jnp.int32, (n_keys, tm), 0).astype(F32)
    onehot = jnp.where(keys == key, 1.0, 0.0)
    before = jnp.dot(onehot.astype(BF16), tri_ref[...], preferred_element_type=F32)
    rank = jnp.sum(onehot * (before + cnt_ref[...]), axis=0, keepdims=True)
    cnt_ref[...] += jnp.sum(onehot, axis=1, keepdims=True)
    row8 = lax.broadcasted_iota(jnp.int32, (8, tm), 0)
    route_ref[0] = jnp.where(row8 == 0, key, jnp.where(row8 == 1, rank, 0.0))


def _rows_loop(n, fn, unroll=8):
    def group(j, c):
        for u in range(unroll):
            fn(j * unroll + u)
        return c
    lax.fori_loop(0, n // unroll, group, 0)

    def single(i, c):
        fn(i)
        return c
    lax.fori_loop(n // unroll * unroll, n, single, 0)


def _moe_kernel(lo_ref, hi_ref, cnt_ref, dest_ref,
                x1_hbm, gffn_ref, wr_ref, br_ref, w13lo_ref, w13hi_ref, w2lo_ref, w2hi_ref,
                x2_hbm, tok_ref, xbuf, obuf, gsem, ssem):
    nb = pl.program_id(0)
    nblk = pl.num_programs(0)
    nchunk = xbuf.shape[1] // MOE_ROWS
    slot = nb % 2
    cnt = cnt_ref[nb]

    def start_gathers(blk, sl):
        base = blk * MOE_ROWS

        def one(i):
            src = x1_hbm.at[pl.ds(tok_ref[base + i] * nchunk, nchunk)]
            pltpu.make_async_copy(src, xbuf.at[sl, pl.ds(i * nchunk, nchunk)], gsem.at[sl]).start()
        _rows_loop(cnt_ref[blk], one)

    def start_scatters(blk, sl):
        base = blk * MOE_ROWS

        def one(i):
            dst = x2_hbm.at[pl.ds(tok_ref[base + i] * nchunk, nchunk)]
            pltpu.make_async_copy(obuf.at[sl, pl.ds(i * nchunk, nchunk)], dst, ssem.at[sl]).start()
        _rows_loop(cnt_ref[blk], one)

    def wait_gathers(blk, sl):
        rows = cnt_ref[blk] * nchunk
        pltpu.make_async_copy(x1_hbm.at[pl.ds(0, rows)], xbuf.at[sl, pl.ds(0, rows)], gsem.at[sl]).wait()

    def wait_scatters(blk, sl):
        rows = cnt_ref[blk] * nchunk
        pltpu.make_async_copy(obuf.at[sl, pl.ds(0, rows)], x2_hbm.at[pl.ds(0, rows)], ssem.at[sl]).wait()

    @pl.when(nb == 0)
    def _():
        def fill(i, c):
            tok_ref[i] = 0
            return c
        lax.fori_loop(0, tok_ref.shape[0], fill, 0, unroll=8)

        def invert(i, c):
            tok_ref[dest_ref[i]] = i
            return c
        lax.fori_loop(0, dest_ref.shape[0], invert, 0, unroll=8)
        xbuf[...] = jnp.zeros_like(xbuf)

        @pl.when(cnt > 0)
        def _():
            start_gathers(0, 0)

    @pl.when(nb + 1 < nblk)
    def _():
        @pl.when(cnt_ref[nb + 1] > 0)
        def _():
            start_gathers(nb + 1, 1 - slot)

    @pl.when(nb >= 2)
    def _():
        @pl.when(cnt_ref[nb - 2] > 0)
        def _():
            wait_scatters(nb - 2, slot)

    @pl.when(cnt > 0)
    def _():
        wait_gathers(nb, slot)
        xg = jnp.concatenate([xbuf[slot, pl.ds(c, MOE_ROWS, stride=nchunk), :] for c in range(nchunk)], axis=1)
        h = (xg * _rms_scale(xg, xg.shape[-1]) * gffn_ref[...]).astype(BF16)

        lo, hi = lo_ref[nb], hi_ref[nb]
        logit = jnp.dot(h, wr_ref[...], preferred_element_type=F32) + br_ref[...]
        lane = _lane_iota(logit.shape)
        pick = lambda j: jnp.sum(jnp.where(lane == j, logit, 0.0), axis=-1, keepdims=True)
        is_g = lane < N_GROUPS
        gmax = jnp.max(jnp.where(is_g, logit, NEG_BIG), axis=-1, keepdims=True)
        g_den = jnp.sum(jnp.where(is_g, jnp.exp(logit - gmax), 0.0), axis=-1, keepdims=True)
        g_p = jnp.exp(pick(lo // EXPERTS_PER_GROUP) - gmax) / g_den
        l_lo, l_hi = pick(N_GROUPS + lo), pick(N_GROUPS + hi)
        gates = (g_p * jax.nn.sigmoid(l_lo - l_hi), g_p * jax.nn.sigmoid(l_hi - l_lo))

        y = xg
        for w13_ref, w2_ref, gate in ((w13lo_ref, w2lo_ref, gates[0]), (w13hi_ref, w2hi_ref, gates[1])):
            ab = jnp.dot(h, w13_ref[0], preferred_element_type=F32)
            hid = jax.nn.silu(ab[:, :EXPERT_HIDDEN]) * ab[:, EXPERT_HIDDEN:]
            y = y + gate * jnp.dot(hid.astype(BF16), w2_ref[0], preferred_element_type=F32)
        for c in range(nchunk):
            obuf[slot, pl.ds(c, MOE_ROWS, stride=nchunk), :] = y[:, c * LANE:(c + 1) * LANE]
        start_scatters(nb, slot)

    @pl.when(nb == nblk - 1)
    def _():
        @pl.when(nb >= 1)
        def _():
            @pl.when(cnt_ref[nb - 1] > 0)
            def _():
                wait_scatters(nb - 1, 1 - slot)

        @pl.when(cnt > 0)
        def _():
            wait_scatters(nb, slot)


def _ple_kernel(x_ref, p_ref, gple_ref, wpg_ref, wple_ref, o_ref):
    tm = o_ref.shape[0]
    nchunk = o_ref.shape[1] // LANE
    x = jnp.concatenate([x_ref[pl.ds(c, tm, stride=nchunk), :] for c in range(nchunk)], axis=1)
    hp = (x * _rms_scale(x, x.shape[-1]) * gple_ref[...]).astype(BF16)
    gate = jax.nn.sigmoid(jnp.dot(hp, wpg_ref[...], preferred_element_type=F32))
    emb = jnp.dot(p_ref[...].astype(BF16), wple_ref[...], preferred_element_type=F32)
    o_ref[...] = x + gate * emb


def _const(shape):
    nd = len(shape)
    return pl.BlockSpec(shape, lambda *_: (0,) * nd)


def _head_slab_cols(w, width, offset=0):
    k = w.shape[0]
    w = w.reshape(k, N_HEADS, width)
    w = jnp.pad(w, ((0, 0), (0, 0), (offset, LANE - width - offset)))
    return w.reshape(k, N_HEADS * LANE)


def _prep_weights(g_mix, w_in, g_cq, w_uq, g_qn, g_qr, g_ckv, w_ukv, g_kn, g_kr, w_oa,
                  conv_w, w_oc, w_o, g_ffn, w_rg, b_rg, w_re, b_re, w1, w3, w2, g_ple, w_pg, w_ple):
    d = w_in.shape[0]
    n_mla = Q_LORA + KV_LORA
    kr_cols = jnp.pad(w_in[:, n_mla:n_mla + QK_ROPE], ((0, 0), (QK_NOPE, LANE - QK_NOPE - QK_ROPE)))
    w = {}
    w["wa"] = jnp.concatenate([w_in[:, :n_mla], kr_cols], axis=1).astype(BF16)
    w["wb"] = w_in[:, n_mla + QK_ROPE:].astype(BF16)
    w["wuq"] = _head_slab_cols(w_uq, QK_NOPE + QK_ROPE).astype(BF16)
    ukv = w_ukv.reshape(KV_LORA, N_HEADS, QK_NOPE + V_DIM)
    w["wuk"] = _head_slab_cols(ukv[:, :, :QK_NOPE].reshape(KV_LORA, -1), QK_NOPE).astype(BF16)
    w["wuv"] = _head_slab_cols(ukv[:, :, QK_NOPE:].reshape(KV_LORA, -1), V_DIM).astype(BF16)
    pad_hi = LANE - QK_NOPE - QK_ROPE
    w["gq"] = (jnp.pad(jnp.concatenate([g_qn, g_qr]), (0, pad_hi)) * (ATTN_SCALE * LOG2E))[None]
    w["gk"] = jnp.pad(g_kn, (0, LANE - QK_NOPE))[None]
    w["gkr"] = jnp.pad(g_kr, (QK_NOPE, pad_hi))[None]
    w["gmix"], w["gcq"], w["gckv"] = g_mix[None], g_cq[None], g_ckv[None]
    w["gffn"], w["gple"] = g_ffn[None], g_ple[None]
    w["convw"] = jnp.pad(conv_w, ((0, 8 - conv_w.shape[0]), (0, 0)))
    w["woa"], w["woc"], w["wo"] = w_oa.astype(BF16), w_oc.astype(BF16), w_o.astype(BF16)
    n_r = N_GROUPS + N_EXPERTS
    w["wr"] = jnp.pad(jnp.concatenate([w_rg, w_re], axis=1), ((0, 0), (0, LANE - n_r))).astype(BF16)
    w["br"] = jnp.pad(jnp.concatenate([b_rg, b_re]), (0, LANE - n_r))[None]
    pad_g = EXPERTS_PER_GROUP - N_GROUPS
    wrt = jnp.concatenate([jnp.pad(w_rg.T, ((0, pad_g), (0, 0))), w_re.T], axis=0)
    wrt_hi = wrt.astype(BF16)
    w["wrt"] = jnp.concatenate([wrt_hi, (wrt - wrt_hi.astype(F32)).astype(BF16)], axis=0)
    w["brt"] = jnp.concatenate([b_rg, jnp.full((pad_g,), NEG_BIG, F32), b_re])[:, None]
    w["w13"] = jnp.concatenate([w1, w3], axis=2).astype(BF16)
    w["w2"] = w2.astype(BF16)
    w["wpg"], w["wple"] = w_pg.astype(BF16), w_ple.astype(BF16)
    return w


def _rope_slabs(pos):
    inv = 1.0 / (ROPE_THETA ** (jnp.arange(0, QK_ROPE, 2, dtype=F32) / QK_ROPE))
    ang = pos.astype(F32)[:, None] * inv[None, :]
    cos, sin = jnp.cos(ang), jnp.sin(ang)
    n = pos.shape[0]
    half = QK_ROPE // 2
    z = lambda k: jnp.zeros((n, k), F32)
    pad_hi = LANE - QK_NOPE - QK_ROPE
    rc = jnp.concatenate([jnp.ones((n, QK_NOPE), F32), cos, cos, z(pad_hi)], axis=1)
    rs1 = jnp.concatenate([z(QK_NOPE), -sin, z(half), z(pad_hi)], axis=1)
    rs2 = jnp.concatenate([z(QK_NOPE), z(half), sin, z(pad_hi)], axis=1)
    return (rc, rs1, rs2), cos.T, sin.T


def _params(sem):
    return pltpu.CompilerParams(dimension_semantics=sem, vmem_limit_bytes=VMEM_LIMIT)


def _mla_pre(x, w, rope, tm):
    b, s, d = x.shape
    hw = N_HEADS * LANE
    tok = lambda width: pl.BlockSpec((1, tm, width), lambda i, j: (i, j, 0))
    rope_spec = pl.BlockSpec((tm, LANE), lambda i, j: (j, 0))
    consts = [w["gmix"], w["wa"], w["gcq"], w["gckv"], w["gkr"], w["wuq"], w["wuk"], w["wuv"], w["gq"], w["gk"]]
    return pl.pallas_call(
        _pre_kernel,
        grid=(b, s // tm),
        in_specs=[tok(d)] + [_const(c.shape) for c in consts] + [rope_spec] * 3,
        out_specs=[tok(hw), tok(hw), tok(hw), tok(KV_LORA), tok(QK_ROPE)],
        out_shape=[jax.ShapeDtypeStruct((b, s, hw), BF16)] * 3
        + [jax.ShapeDtypeStruct((b, s, KV_LORA), F32), jax.ShapeDtypeStruct((b, s, QK_ROPE), F32)],
        compiler_params=_params(("parallel", "parallel")),
        name="mla_pre",
    )(x, *consts, *rope)


def _mla_pre_t(x, w, rope, cos_t, sin_t, tm):
    b, s, d = x.shape
    hw = N_HEADS * LANE
    tok = lambda width: pl.BlockSpec((1, tm, width), lambda i, j: (i, j, 0))
    tiled = pl.BlockSpec((1, 1, hw, tm), lambda i, j: (i, j, 0, 0))
    rope_spec = pl.BlockSpec((tm, LANE), lambda i, j: (j, 0))
    rope_t_spec = pl.BlockSpec((QK_ROPE // 2, tm), lambda i, j: (0, j))
    gqt = jnp.broadcast_to(w["gq"].T, (LANE, tm))
    consts = [w["gmix"], w["wa"], w["gcq"], w["gckv"], w["gkr"], w["wuq"].T, w["wuk"], w["wuv"].T, gqt, w["gk"]]
    return pl.pallas_call(
        _pre_kernel_t,
        grid=(b, s // tm),
        in_specs=[tok(d)] + [_const(c.shape) for c in consts] + [rope_spec] * 3 + [rope_t_spec] * 2,
        out_specs=[tiled, tok(hw), tiled, tok(KV_LORA), tok(QK_ROPE)],
        out_shape=[jax.ShapeDtypeStruct((b, s // tm, hw, tm), BF16), jax.ShapeDtypeStruct((b, s, hw), BF16),
                   jax.ShapeDtypeStruct((b, s // tm, hw, tm), BF16),
                   jax.ShapeDtypeStruct((b, s, KV_LORA), F32), jax.ShapeDtypeStruct((b, s, QK_ROPE), F32)],
        compiler_params=_params(("parallel", "parallel")),
        name="mla_pre_t",
    )(x, *consts, *rope, cos_t, sin_t)


def _kv_past(past_lat, past_kpe, w, tm):
    b, t, _ = past_lat.shape
    hw = N_HEADS * LANE
    kpe_slab = jnp.pad(past_kpe, ((0, 0), (0, 0), (QK_NOPE, LANE - QK_NOPE - QK_ROPE)))
    tok = lambda width: pl.BlockSpec((1, tm, width), lambda i, j: (i, j, 0))
    consts = [w["wuk"], w["wuv"], w["gk"]]
    return pl.pallas_call(
        _kvpast_kernel,
        grid=(b, t // tm),
        in_specs=[tok(KV_LORA), tok(LANE)] + [_const(c.shape) for c in consts],
        out_specs=[tok(hw), tok(hw)],
        out_shape=[jax.ShapeDtypeStruct((b, t, hw), BF16)] * 2,
        compiler_params=_params(("parallel", "parallel")),
        name="kv_past",
    )(past_lat, kpe_slab, *consts)


def _attn_prompt(qt, k, vt):
    b, n_tiles, _, tq = qt.shape
    s = n_tiles * tq
    pair = 2 * LANE
    return pl.pallas_call(
        functools.partial(_attn_prompt_kernel, tq=tq),
        grid=(b, N_HEADS // 2, n_tiles),
        in_specs=[pl.BlockSpec((1, 1, pair, tq), lambda i, h, j: (i, j, h, 0)),
                  pl.BlockSpec((1, s, pair), lambda i, h, j: (i, 0, h)),
                  pl.BlockSpec((1, n_tiles, pair, tq), lambda i, h, j: (i, 0, h, 0))],
        out_specs=pl.BlockSpec((1, tq, LANE), lambda i, h, j: (i, j, h)),
        out_shape=jax.ShapeDtypeStruct((b, s, N_HEADS * V_DIM), BF16),
        scratch_shapes=[pltpu.VMEM((tq, tq), F32)] * 4 + [pltpu.VMEM((tq, tq), BF16)] * 4,
        compiler_params=_params(("parallel", "parallel", "arbitrary")),
        name="attn_prompt",
    )(qt, k, vt)


def _attn_sample(q, kp, vp, kn, vn):
    b, s, _ = q.shape
    t = kp.shape[1]
    pair = 2 * LANE
    blk = lambda rows: pl.BlockSpec((1, rows, pair), lambda i, h: (i, 0, h))
    return pl.pallas_call(
        _attn_sample_kernel,
        grid=(b, N_HEADS // 2),
        in_specs=[blk(s), blk(t), blk(t), blk(s), blk(s)],
        out_specs=pl.BlockSpec((1, s, LANE), lambda i, h: (i, 0, h)),
        out_shape=jax.ShapeDtypeStruct((b, s, N_HEADS * V_DIM), BF16),
        compiler_params=_params(("parallel", "parallel")),
        name="attn_sample",
    )(q, kp, vp, kn, vn)


def _post(x, attn, conv_init, cnt0, x1_all, row_off, n_all, w, tm):
    b, s, d = x.shape
    nchunk = d // LANE
    n_keys = cnt0.shape[0]
    tiles_per_b = s // tm
    off = row_off // tm
    tok = lambda width: pl.BlockSpec((1, tm, width), lambda i, j: (i, j, 0))
    per_b = lambda rows: pl.BlockSpec((1, rows, CONV_DIM), lambda i, j: (i, 0, 0))
    tri = (jnp.arange(tm)[:, None] < jnp.arange(tm)[None, :]).astype(BF16)
    consts = [cnt0, w["gmix"], w["wb"], w["convw"], w["woa"], w["woc"], w["wo"], w["gffn"], w["wrt"], w["brt"], tri]
    in_specs = ([tok(d), tok(N_HEADS * V_DIM), per_b(8)] + [_const(c.shape) for c in consts]
                + [pl.BlockSpec(memory_space=pl.ANY)])
    args = [x, attn, conv_init] + consts + [x1_all]
    return pl.pallas_call(
        _post_kernel,
        grid=(b, tiles_per_b),
        in_specs=in_specs,
        out_specs=[pl.BlockSpec((tm * nchunk, LANE), lambda i, j: (off + i * tiles_per_b + j, 0)),
                   pl.BlockSpec((1, 8, tm), lambda i, j: (i * tiles_per_b + j, 0, 0)),
                   per_b(2), _const((n_keys, 1))],
        out_shape=[jax.ShapeDtypeStruct((n_all * nchunk, LANE), F32),
                   jax.ShapeDtypeStruct((b * tiles_per_b, 8, tm), F32),
                   jax.ShapeDtypeStruct((b, 2, CONV_DIM), F32), jax.ShapeDtypeStruct((n_keys, 1), F32)],
        scratch_shapes=[pltpu.VMEM((8, CONV_DIM), F32)],
        input_output_aliases={len(args) - 1: 0},
        compiler_params=_params(("arbitrary", "arbitrary")),
        name="post",
    )(*args)


def _route_tables(key, rank, counts, n):
    n_keys = counts.shape[0]
    padded = (counts + MOE_ROWS - 1) // MOE_ROWS * MOE_ROWS
    pend = jnp.cumsum(padded)
    pstart = pend - padded
    ids = jnp.arange(n_keys, dtype=jnp.int32)
    dest = rank + jnp.sum(jnp.where(key[:, None] == ids[None, :], pstart[None, :], 0), axis=1)
    nblk = n // MOE_ROWS + N_PAIR_BUCKETS
    blk_start = jnp.arange(nblk, dtype=jnp.int32) * MOE_ROWS
    blk_hot = (blk_start[:, None] >= pstart[None, :]) & (blk_start[:, None] < pend[None, :])
    blk_key = jnp.sum(jnp.where(blk_hot, ids[None, :], 0), axis=1)
    blk_cnt = jnp.sum(jnp.where(blk_hot, jnp.minimum(counts[None, :] - (blk_start[:, None] - pstart[None, :]),
                                                      MOE_ROWS), 0), axis=1)
    any_hot = jnp.any(blk_hot, axis=1)
    blk_key = jnp.where(any_hot, blk_key, n_keys - 1)
    blk_lo = blk_key // EXPERTS_PER_GROUP
    blk_hi = blk_lo // EXPERTS_PER_GROUP * EXPERTS_PER_GROUP + blk_key % EXPERTS_PER_GROUP
    return blk_lo, blk_hi, blk_cnt.astype(jnp.int32), dest.astype(jnp.int32)


def _moe(x1_all, key, rank, counts, w):
    rows, _ = x1_all.shape
    d = w["gffn"].shape[1]
    nchunk = d // LANE
    n = rows // nchunk
    blk_lo, blk_hi, blk_cnt, dest = _route_tables(key, rank, counts, n)
    nblk = blk_lo.shape[0]
    w13_spec = lambda ref_idx: pl.BlockSpec((1, d, 2 * EXPERT_HIDDEN),
                                            lambda i, lo, hi, cnt, dst: ((lo, hi)[ref_idx][i], 0, 0))
    w2_spec = lambda ref_idx: pl.BlockSpec((1, EXPERT_HIDDEN, d),
                                           lambda i, lo, hi, cnt, dst: ((lo, hi)[ref_idx][i], 0, 0))
    buf = pltpu.VMEM((2, MOE_ROWS * nchunk, LANE), F32)
    grid_spec = pltpu.PrefetchScalarGridSpec(
        num_scalar_prefetch=4,
        grid=(nblk,),
        in_specs=[pl.BlockSpec(memory_space=pl.ANY),
                  pl.BlockSpec((1, d), lambda i, *_: (0, 0)),
                  pl.BlockSpec(w["wr"].shape, lambda i, *_: (0, 0)),
                  pl.BlockSpec(w["br"].shape, lambda i, *_: (0, 0)),
                  w13_spec(0), w13_spec(1), w2_spec(0), w2_spec(1)],
        out_specs=pl.BlockSpec(memory_space=pl.ANY),
        scratch_shapes=[pltpu.SMEM((nblk * MOE_ROWS,), jnp.int32), buf, buf,
                        pltpu.SemaphoreType.DMA((2,)), pltpu.SemaphoreType.DMA((2,))],
    )
    return pl.pallas_call(
        _moe_kernel,
        grid_spec=grid_spec,
        out_shape=jax.ShapeDtypeStruct((rows, LANE), F32),
        compiler_params=_params(("arbitrary",)),
        name="moe",
    )(blk_lo, blk_hi, blk_cnt, dest, x1_all, w["gffn"], w["wr"], w["br"], w["w13"], w["w13"], w["w2"], w["w2"])


def _ple(x2_all, row_off, p, w, tm):
    n, pd = p.shape
    d = w["gple"].shape[1]
    nchunk = d // LANE
    off = row_off // tm
    consts = [w["gple"], w["wpg"], w["wple"]]
    return pl.pallas_call(
        _ple_kernel,
        grid=(n // tm,),
        in_specs=[pl.BlockSpec((tm * nchunk, LANE), lambda i: (off + i, 0)), pl.BlockSpec((tm, pd), lambda i: (i, 0))]
        + [_const(c.shape) for c in consts],
        out_specs=pl.BlockSpec((tm, d), lambda i: (i, 0)),
        out_shape=jax.ShapeDtypeStruct((n, d), F32),
        compiler_params=_params(("parallel",)),
        name="ple",
    )(x2_all, p, *consts)


def _mixer(x, past_lat, past_kpe, past_conv, w):
    b, s, _ = x.shape
    tm = min(512, s)
    past_len = 0 if past_lat is None else past_lat.shape[1]
    rope, cos_t, sin_t = _rope_slabs(past_len + jnp.arange(s))
    if past_lat is None:
        qt, k, vt, lat_new, kpe_new = _mla_pre_t(x, w, rope, cos_t, sin_t, tm)
        attn = _attn_prompt(qt, k, vt)
        conv_init = jnp.zeros((b, 8, CONV_DIM), F32)
    else:
        q, k, v, lat_new, kpe_new = _mla_pre(x, w, rope, tm)
        kp, vp = _kv_past(past_lat, past_kpe, w, min(512, past_len))
        attn = _attn_sample(q, kp, vp, k, v)
        conv_init = jnp.pad(past_conv, ((0, 0), (8 - past_conv.shape[1], 0), (0, 0)))
    return attn, conv_init, lat_new, kpe_new, tm


def _layer(xp, xs, pp, ps, past_lat, past_kpe, past_conv, w):
    bp, sp, d = xp.shape
    bs, ss, _ = xs.shape
    n_p, n_s = bp * sp, bs * ss
    n_all = n_p + n_s
    n_keys = N_EXPERTS * EXPERTS_PER_GROUP
    attn_p, cinit_p, lat_p, kpe_p, tm_p = _mixer(xp, None, None, None, w)
    attn_s, cinit_s, lat_s, kpe_s, tm_s = _mixer(xs, past_lat, past_kpe, past_conv, w)
    x1_all = jnp.zeros((n_all * (d // LANE), LANE), F32)
    x1_all, route_p, conv_p, cnt = _post(xp, attn_p, cinit_p, jnp.zeros((n_keys, 1), F32), x1_all, 0, n_all, w, tm_p)
    x1_all, route_s, conv_s, cnt = _post(xs, attn_s, cinit_s, cnt, x1_all, n_p, n_all, w, tm_s)
    key = jnp.concatenate([route_p[:, 0].reshape(-1), route_s[:, 0].reshape(-1)]).astype(jnp.int32)
    rank = jnp.concatenate([route_p[:, 1].reshape(-1), route_s[:, 1].reshape(-1)]).astype(jnp.int32)
    x2_all = _moe(x1_all, key, rank, cnt[:, 0].astype(jnp.int32), w)
    yp = _ple(x2_all, 0, pp.reshape(n_p, -1), w, min(512, n_p)).reshape(bp, sp, d)
    ys = _ple(x2_all, n_p, ps.reshape(n_s, -1), w, min(512, n_s)).reshape(bs, ss, d)
    return yp, ys, (lat_p, kpe_p, conv_p, lat_s, kpe_s, conv_s)


def kernel(x_prompt, x_sample, cache_kv_latent, cache_k_rope, state_conv, p_prompt, p_sample,
           g_mix, w_in, g_cq, w_uq, g_qn, g_qr, g_ckv, w_ukv, g_kn, g_kr, w_oa,
           conv_w, w_oc, w_o, g_ffn, w_rg, b_rg, w_re, b_re, w1, w3, w2, g_ple, w_pg, w_ple):
    depth = g_mix.shape[0]
    xp, xs = x_prompt, x_sample
    outs = [[] for _ in range(6)]
    for i in range(depth):
        w = _prep_weights(g_mix[i], w_in[i], g_cq[i], w_uq[i], g_qn[i], g_qr[i], g_ckv[i], w_ukv[i],
                          g_kn[i], g_kr[i], w_oa[i], conv_w[i], w_oc[i], w_o[i], g_ffn[i], w_rg[i], b_rg[i],
                          w_re[i], b_re[i], w1[i], w3[i], w2[i], g_ple[i], w_pg[i], w_ple[i])
        xp, xs, new = _layer(xp, xs, p_prompt[i], p_sample[i], cache_kv_latent[i], cache_k_rope[i], state_conv[i], w)
        for o, a in zip(outs, new):
            o.append(a)
    return (xp, xs) + tuple(jnp.stack(o, axis=0) for o in outs)
```

```python
import functools
import math

import jax
import jax.numpy as jnp
from jax import lax
from jax.experimental import pallas as pl
from jax.experimental.pallas import tpu as pltpu

F32 = jnp.float32
BF16 = jnp.bfloat16

LANE = 128
CHUNK = 64
N_HEADS = 8
QK_NOPE = 64
QK_ROPE = 32
V_DIM = 64
Q_LORA = 256
KV_LORA = 256
CONV_DIM = 512
N_GROUPS = 4
EXPERTS_PER_GROUP = 8
N_EXPERTS = N_GROUPS * EXPERTS_PER_GROUP
EXPERT_HIDDEN = 256
ROPE_THETA = 10000.0
EPS = 1e-6
ATTN_SCALE = (QK_NOPE + QK_ROPE) ** -0.5
LOG2E = math.log2(math.e)
NEG_BIG = -1e30
MOE_ROWS = 128
N_PAIR_BUCKETS = N_GROUPS * (EXPERTS_PER_GROUP * (EXPERTS_PER_GROUP - 1) // 2)
VMEM_LIMIT = 56 * 1024 * 1024


def _rms_scale(x, n):
    return lax.rsqrt(jnp.sum(x * x, axis=-1, keepdims=True) * (1.0 / n) + EPS)


def _lane_iota(shape):
    return lax.broadcasted_iota(jnp.int32, shape, len(shape) - 1)


def _rope(t, rc, rs1, rs2):
    return t * rc + pltpu.roll(t, LANE - QK_ROPE // 2, 1) * rs1 + pltpu.roll(t, QK_ROPE // 2, 1) * rs2


def _pre_latents(x_ref, gmix_ref, wa_ref, gcq_ref, gckv_ref, gkr_ref, rope_refs, lat_ref, kpe_ref):
    x = x_ref[0]
    h = x * _rms_scale(x, x.shape[-1]) * gmix_ref[...]
    z = jnp.dot(h.astype(BF16), wa_ref[...], preferred_element_type=F32)
    cq = z[:, :Q_LORA]
    ckv = z[:, Q_LORA:Q_LORA + KV_LORA]
    kr = z[:, Q_LORA + KV_LORA:]
    cqn = cq * _rms_scale(cq, Q_LORA) * gcq_ref[...]
    lat = ckv * _rms_scale(ckv, KV_LORA) * gckv_ref[...]
    lat_ref[0] = lat
    krn = kr * _rms_scale(kr, QK_ROPE) * gkr_ref[...]
    kpe = _rope(krn, *(r[...] for r in rope_refs))
    kpe_ref[0] = kpe[:, QK_NOPE:QK_NOPE + QK_ROPE]
    return cqn, lat, kpe


def _store_keys(k_ref, kf, gk, kpe):
    for hd in range(N_HEADS):
        sl = slice(hd * LANE, (hd + 1) * LANE)
        ks = kf[:, sl]
        k_ref[0, :, sl] = (ks * _rms_scale(ks, QK_NOPE) * gk + kpe).astype(BF16)


def _pre_kernel(x_ref, gmix_ref, wa_ref, gcq_ref, gckv_ref, gkr_ref, wuq_ref, wuk_ref, wuv_ref,
                gq_ref, gk_ref, rc_ref, rs1_ref, rs2_ref,
                q_ref, k_ref, v_ref, lat_ref, kpe_ref):
    rope_refs = (rc_ref, rs1_ref, rs2_ref)
    cqn, lat, kpe = _pre_latents(x_ref, gmix_ref, wa_ref, gcq_ref, gckv_ref, gkr_ref, rope_refs, lat_ref, kpe_ref)
    rc, rs1, rs2 = (r[...] for r in rope_refs)
    is_nope = _lane_iota(kpe.shape) < QK_NOPE
    latb = lat.astype(BF16)
    qf = jnp.dot(cqn.astype(BF16), wuq_ref[...], preferred_element_type=F32)
    vf = jnp.dot(latb, wuv_ref[...], preferred_element_type=F32)
    _store_keys(k_ref, jnp.dot(latb, wuk_ref[...], preferred_element_type=F32), gk_ref[...], kpe)
    gq = gq_ref[...]
    ones_hi = jnp.where(is_nope, 0.0, 1.0)
    for hd in range(N_HEADS):
        sl = slice(hd * LANE, (hd + 1) * LANE)
        qs = qf[:, sl]
        sq = qs * qs
        ss_all = jnp.sum(sq, axis=-1, keepdims=True)
        ss_n = jnp.sum(jnp.where(is_nope, sq, 0.0), axis=-1, keepdims=True)
        r = jnp.where(is_nope, lax.rsqrt(ss_n * (1.0 / QK_NOPE) + EPS),
                      lax.rsqrt((ss_all - ss_n) * (1.0 / QK_ROPE) + EPS))
        q_ref[0, :, sl] = _rope(qs * r * gq, rc, rs1, rs2).astype(BF16)
        v_ref[0, :, sl] = (vf[:, sl] + ones_hi).astype(BF16)


def _pre_kernel_t(x_ref, gmix_ref, wa_ref, gcq_ref, gckv_ref, gkr_ref, wuqt_ref, wuk_ref, wuvt_ref,
                  gqt_ref, gk_ref, rc_ref, rs1_ref, rs2_ref, cos_ref, sin_ref,
                  qt_ref, k_ref, vt_ref, lat_ref, kpe_ref):
    rope_refs = (rc_ref, rs1_ref, rs2_ref)
    cqn, lat, kpe = _pre_latents(x_ref, gmix_ref, wa_ref, gcq_ref, gckv_ref, gkr_ref, rope_refs, lat_ref, kpe_ref)
    latb = lat.astype(BF16)
    _store_keys(k_ref, jnp.dot(latb, wuk_ref[...], preferred_element_type=F32), gk_ref[...], kpe)
    tm = cqn.shape[0]
    qft = jnp.dot(wuqt_ref[...], cqn.T.astype(BF16), preferred_element_type=F32)
    vft = jnp.dot(wuvt_ref[...], lat.T.astype(BF16), preferred_element_type=F32)
    gq, cos, sin = gqt_ref[...], cos_ref[...], sin_ref[...]
    half = QK_ROPE // 2
    ones_lo = jnp.where(lax.broadcasted_iota(jnp.int32, (LANE, tm), 0) < V_DIM, 0.0, 1.0)
    pad = jnp.zeros((LANE - QK_NOPE - QK_ROPE, tm), F32)
    for hd in range(N_HEADS):
        rows = slice(hd * LANE, (hd + 1) * LANE)
        qs = qft[rows]
        sq = qs * qs
        r_n = lax.rsqrt(jnp.sum(sq[:QK_NOPE], axis=0, keepdims=True) * (1.0 / QK_NOPE) + EPS)
        r_p = lax.rsqrt(jnp.sum(sq[QK_NOPE:QK_NOPE + QK_ROPE], axis=0, keepdims=True) * (1.0 / QK_ROPE) + EPS)
        nope = qs[:QK_NOPE] * r_n * gq[:QK_NOPE]
        x1 = qs[QK_NOPE:QK_NOPE + half] * r_p * gq[QK_NOPE:QK_NOPE + half]
        x2 = qs[QK_NOPE + half:QK_NOPE + QK_ROPE] * r_p * gq[QK_NOPE + half:QK_NOPE + QK_ROPE]
        slab = jnp.concatenate([nope, x1 * cos - x2 * sin, x1 * sin + x2 * cos, pad], axis=0)
        qt_ref[0, 0, rows, :] = slab.astype(BF16)
        vt_ref[0, 0, rows, :] = (vft[rows] + ones_lo).astype(BF16)


def _kvpast_kernel(lat_ref, kpe_ref, wuk_ref, wuv_ref, gk_ref, k_ref, v_ref):
    latb = lat_ref[0].astype(BF16)
    kf = jnp.dot(latb, wuk_ref[...], preferred_element_type=F32)
    vf = jnp.dot(latb, wuv_ref[...], preferred_element_type=F32)
    kpe = kpe_ref[0]
    lane = _lane_iota(kpe.shape)
    ones_hi = jnp.where(lane < QK_NOPE, 0.0, 1.0)
    gk = gk_ref[...]
    for hd in range(N_HEADS):
        sl = slice(hd * LANE, (hd + 1) * LANE)
        ks = kf[:, sl]
        k_ref[0, :, sl] = (ks * _rms_scale(ks, QK_NOPE) * gk + kpe).astype(BF16)
        v_ref[0, :, sl] = (vf[:, sl] + ones_hi).astype(BF16)


def _softmax_step(q, k, v, m, acc, mask):
    s = lax.dot_general(q, k, (((1,), (1,)), ((), ())), preferred_element_type=F32)
    if mask is not None:
        s = jnp.where(mask, s, NEG_BIG)
    m_new = jnp.maximum(m, jnp.max(s, axis=-1, keepdims=True))
    alpha = jnp.exp2(m - m_new)
    p = jnp.exp2(s - m_new)
    acc = acc * alpha + jnp.dot(p.astype(BF16), v, preferred_element_type=F32)
    return m_new, acc


def _finish_pair(accs):
    outs = [a / pltpu.roll(a, V_DIM, 1) for a in accs]
    lane = _lane_iota(outs[0].shape)
    return jnp.where(lane < V_DIM, outs[0], pltpu.roll(outs[1], V_DIM, 1))


def _attn_prompt_kernel(qt_ref, k_ref, vt_ref, o_ref, s00, s01, s10, s11, p00, p01, p10, p11, *, tq):
    s_refs = ((s00, s01), (s10, s11))
    p_refs = ((p00, p01), (p10, p11))
    qi = pl.program_id(2)
    key_pos = lax.broadcasted_iota(jnp.int32, (tq, tq), 0)
    query_pos = lax.broadcasted_iota(jnp.int32, (tq, tq), 1)
    diag_mask = (key_pos // CHUNK) <= (query_pos // CHUNK)
    heads = [slice(hh * LANE, (hh + 1) * LANE) for hh in range(2)]

    def scores(i, slot, hh, mask):
        start = pl.multiple_of(i * tq, tq)
        s = jnp.dot(k_ref[0, pl.ds(start, tq), heads[hh]], qt_ref[0, 0, heads[hh], :], preferred_element_type=F32)
        if mask is not None:
            s = jnp.where(mask, s, NEG_BIG)
        s_refs[slot][hh][...] = s
        return jnp.max(s, axis=0, keepdims=True)

    def softmax(slot, hh, m, tile_max):
        m_new = jnp.maximum(m, tile_max)
        p_refs[slot][hh][...] = jnp.exp2(s_refs[slot][hh][...] - m_new).astype(BF16)
        return m_new, jnp.exp2(m - m_new)

    def accumulate(i, slot, hh, alpha, acc):
        pv = jnp.dot(vt_ref[0, jnp.maximum(i, 0), heads[hh], :], p_refs[slot][hh][...],
                     preferred_element_type=F32)
        return acc * alpha + pv

    def iteration(i, slot, carry, next_mask=None):
        stats = [softmax(slot, hh, m, tile_max) for hh, (m, tile_max, _, _) in enumerate(carry)]
        next_max = [scores(i + 1, 1 - slot, hh, next_mask) for hh in range(2)]
        accs = [accumulate(i - 1, 1 - slot, hh, alpha, acc) for hh, (_, _, alpha, acc) in enumerate(carry)]
        return tuple((m, tmax, alpha, acc) for (m, alpha), tmax, acc in zip(stats, next_max, accs))

    def last(slot, carry):
        accs = []
        for hh, (m, tile_max, alpha, acc) in enumerate(carry):
            acc = accumulate(qi - 1, 1 - slot, hh, alpha, acc)
            m, alpha = softmax(slot, hh, m, tile_max)
            accs.append(accumulate(qi, slot, hh, alpha, acc))
        return tuple(accs)

    first_mask = diag_mask | (qi > 0)
    init = []
    for hh in range(2):
        p_refs[1][hh][...] = jnp.zeros((tq, tq), BF16)
        init.append((jnp.full((1, tq), NEG_BIG, F32), scores(0, 0, hh, first_mask), jnp.ones((1, tq), F32),
                     jnp.zeros((LANE, tq), F32)))
    carry = lax.fori_loop(0, (qi - 1) // 2, lambda j, c: iteration(2 * j + 1, 1, iteration(2 * j, 0, c)),
                          tuple(init))
    tails = [lambda c: last(0, c),
             lambda c: last(1, iteration(qi - 1, 0, c, diag_mask)),
             lambda c: last(0, iteration(qi - 1, 1, iteration(qi - 2, 0, c), diag_mask))]
    accs = lax.switch(jnp.where(qi == 0, 0, 2 - qi % 2), tails, carry)
    out_t = jnp.concatenate([a[:V_DIM] / a[V_DIM:V_DIM + 1] for a in accs], axis=0)
    o_ref[0] = out_t.T.astype(o_ref.dtype)


def _attn_sample_kernel(q_ref, kp_ref, vp_ref, kn_ref, vn_ref, o_ref):
    nt = (((1,), (1,)), ((), ()))
    slabs = [slice(hd * LANE, (hd + 1) * LANE) for hd in range(N_HEADS)]
    s_past = [lax.dot_general(q_ref[0, :, sl], kp_ref[0, :, sl], nt, preferred_element_type=F32) for sl in slabs]
    s_new = [lax.dot_general(q_ref[0, :, sl], kn_ref[0, :, sl], nt, preferred_element_type=F32) for sl in slabs]
    accs = []
    for sl, sp, sn in zip(slabs, s_past, s_new):
        m = jnp.maximum(jnp.max(sp, axis=-1, keepdims=True), jnp.max(sn, axis=-1, keepdims=True))
        accs.append(jnp.dot(jnp.exp2(sp - m).astype(BF16), vp_ref[0, :, sl], preferred_element_type=F32)
                    + jnp.dot(jnp.exp2(sn - m).astype(BF16), vn_ref[0, :, sl], preferred_element_type=F32))
    for pair in range(N_HEADS // 2):
        o_ref[0, :, pair * LANE:(pair + 1) * LANE] = _finish_pair(accs[2 * pair:2 * pair + 2]).astype(o_ref.dtype)


def _post_kernel(x_ref, attn_ref, cinit_ref, cnt0_ref, gmix_ref, wb_ref, convw_ref, woa_ref, woc_ref, wo_ref,
                 gffn_ref, wrt_ref, brt_ref, tri_ref, x1_all_ref,
                 x1_ref, route_ref, cnew_ref, cnt_ref, carry_ref):
    del x1_all_ref
    si = pl.program_id(1)
    tm = x_ref.shape[1]

    @pl.when(si == 0)
    def _():
        carry_ref[...] = cinit_ref[0]

    @pl.when((si == 0) & (pl.program_id(0) == 0))
    def _():
        cnt_ref[...] = cnt0_ref[...]

    x = x_ref[0]
    h = x * _rms_scale(x, x.shape[-1]) * gmix_ref[...]
    z = jnp.dot(h.astype(BF16), wb_ref[...], preferred_element_type=F32)
    conv_b = z[:, :CONV_DIM]
    u = z[:, CONV_DIM:2 * CONV_DIM] * z[:, 2 * CONV_DIM:3 * CONV_DIM]
    d = x.shape[-1]
    gate_a = z[:, 3 * CONV_DIM:3 * CONV_DIM + d]
    gate_c = z[:, 3 * CONV_DIM + d:]

    carry = carry_ref[...]
    c1 = carry[7:8, :]
    c2 = carry[6:7, :]
    row = lax.broadcasted_iota(jnp.int32, u.shape, 0)
    u_m1 = jnp.where(row == 0, c1, pltpu.roll(u, 1, 0))
    u_m2 = jnp.where(row == 0, c2, jnp.where(row == 1, c1, pltpu.roll(u, 2, 0)))
    cw = convw_ref[...]
    cv = cw[0:1, :] * u_m2 + cw[1:2, :] * u_m1 + cw[2:3, :] * u
    carry_ref[...] = u[tm - 8:, :]
    cnew_ref[0] = u[tm - 2:, :]

    y_a = jnp.dot(attn_ref[0], woa_ref[...], preferred_element_type=F32)
    y_c = jnp.dot((conv_b * cv).astype(BF16), woc_ref[...], preferred_element_type=F32)
    mrg = jax.nn.sigmoid(gate_a) * y_a + jax.nn.sigmoid(gate_c) * y_c
    x1 = x + jnp.dot(mrg.astype(BF16), wo_ref[...], preferred_element_type=F32)
    nchunk = d // LANE
    for c in range(nchunk):
        x1_ref[pl.ds(c, tm, stride=nchunk), :] = x1[:, c * LANE:(c + 1) * LANE]

    h2 = x1 * _rms_scale(x1, d) * gffn_ref[...]
    h2_hi = h2.astype(BF16)
    h2_lo = (h2 - h2_hi.astype(F32)).astype(BF16)
    nt = (((1,), (1,)), ((), ()))
    n_rows = brt_ref.shape[0]
    lt2 = lax.dot_general(wrt_ref[...], h2_hi, nt, preferred_element_type=F32)
    lt = (lt2[:n_rows] + lt2[n_rows:]
          + lax.dot_general(wrt_ref[:n_rows, :], h2_lo, nt, preferred_element_type=F32) + brt_ref[...])
    sub = lax.broadcasted_iota(jnp.int32, (EXPERTS_PER_GROUP, tm), 0).astype(F32)
    none = float(EXPERTS_PER_GROUP)

    def first_argmax(v):
        vmax = jnp.max(v, axis=0, keepdims=True)
        return jnp.min(jnp.where(v == vmax, sub, none), axis=0, keepdims=True)

    g_idx = first_argmax(lt[:EXPERTS_PER_GROUP])
    el = lt[EXPERTS_PER_GROUP * N_GROUPS:]
    for g in range(N_GROUPS - 2, -1, -1):
        el = jnp.where(g_idx == g, lt[EXPERTS_PER_GROUP * (g + 1):EXPERTS_PER_GROUP * (g + 2)], el)
    i1 = first_argmax(el)
    i2 = first_argmax(jnp.where(sub == i1, NEG_BIG, el))
    key = (g_idx * (EXPERTS_PER_GROUP * EXPERTS_PER_GROUP) + jnp.minimum(i1, i2) * EXPERTS_PER_GROUP
           + jnp.maximum(i1, i2))

    n_keys = cnt_ref.shape[0]
    keys = lax.broadcasted_iota(jnp.int32, (n_keys, tm), 0).astype(F32)
    onehot = jnp.where(keys == key, 1.0, 0.0)
    before = jnp.dot(onehot.astype(BF16), tri_ref[...], preferred_element_type=F32)
    rank = jnp.sum(onehot * (before + cnt_ref[...]), axis=0, keepdims=True)
    cnt_ref[...] += jnp.sum(onehot, axis=1, keepdims=True)
    row8 = lax.broadcasted_iota(jnp.int32, (8, tm), 0)
    route_ref[0] = jnp.where(row8 == 0, key, jnp.where(row8 == 1, rank, 0.0))


def _rows_loop(n, fn, unroll=8):
    def group(j, c):
        for u in range(unroll):
            fn(j * unroll + u)
        return c
    lax.fori_loop(0, n // unroll, group, 0)

    def single(i, c):
        fn(i)
        return c
    lax.fori_loop(n // unroll * unroll, n, single, 0)


def _moe_kernel(lo_ref, hi_ref, cnt_ref, dest_ref,
                x1_hbm, gffn_ref, wr_ref, br_ref, w13lo_ref, w13hi_ref, w2lo_ref, w2hi_ref,
                x2_hbm, tok_ref, xbuf, obuf, gsem, ssem):
    nb = pl.program_id(0)
    nblk = pl.num_programs(0)
    nchunk = xbuf.shape[1] // MOE_ROWS
    slot = nb % 2
    cnt = cnt_ref[nb]

    def start_gathers(blk, sl):
        base = blk * MOE_ROWS

        def one(i):
            src = x1_hbm.at[pl.ds(tok_ref[base + i] * nchunk, nchunk)]
            pltpu.make_async_copy(src, xbuf.at[sl, pl.ds(i * nchunk, nchunk)], gsem.at[sl]).start()
        _rows_loop(cnt_ref[blk], one)

    def start_scatters(blk, sl):
        base = blk * MOE_ROWS

        def one(i):
            dst = x2_hbm.at[pl.ds(tok_ref[base + i] * nchunk, nchunk)]
            pltpu.make_async_copy(obuf.at[sl, pl.ds(i * nchunk, nchunk)], dst, ssem.at[sl]).start()
        _rows_loop(cnt_ref[blk], one)

    def wait_gathers(blk, sl):
        rows = cnt_ref[blk] * nchunk
        pltpu.make_async_copy(x1_hbm.at[pl.ds(0, rows)], xbuf.at[sl, pl.ds(0, rows)], gsem.at[sl]).wait()

    def wait_scatters(blk, sl):
        rows = cnt_ref[blk] * nchunk
        pltpu.make_async_copy(obuf.at[sl, pl.ds(0, rows)], x2_hbm.at[pl.ds(0, rows)], ssem.at[sl]).wait()

    @pl.when(nb == 0)
    def _():
        def fill(i, c):
            tok_ref[i] = 0
            return c
        lax.fori_loop(0, tok_ref.shape[0], fill, 0, unroll=8)

        def invert(i, c):
            tok_ref[dest_ref[i]] = i
            return c
        lax.fori_loop(0, dest_ref.shape[0], invert, 0, unroll=8)
        xbuf[...] = jnp.zeros_like(xbuf)

        @pl.when(cnt > 0)
        def _():
            start_gathers(0, 0)

    @pl.when(nb + 1 < nblk)
    def _():
        @pl.when(cnt_ref[nb + 1] > 0)
        def _():
            start_gathers(nb + 1, 1 - slot)

    @pl.when(nb >= 2)
    def _():
        @pl.when(cnt_ref[nb - 2] > 0)
        def _():
            wait_scatters(nb - 2, slot)

    @pl.when(cnt > 0)
    def _():
        wait_gathers(nb, slot)
        xg = jnp.concatenate([xbuf[slot, pl.ds(c, MOE_ROWS, stride=nchunk), :] for c in range(nchunk)], axis=1)
        h = (xg * _rms_scale(xg, xg.shape[-1]) * gffn_ref[...]).astype(BF16)

        lo, hi = lo_ref[nb], hi_ref[nb]
        logit = jnp.dot(h, wr_ref[...], preferred_element_type=F32) + br_ref[...]
        lane = _lane_iota(logit.shape)
        pick = lambda j: jnp.sum(jnp.where(lane == j, logit, 0.0), axis=-1, keepdims=True)
        is_g = lane < N_GROUPS
        gmax = jnp.max(jnp.where(is_g, logit, NEG_BIG), axis=-1, keepdims=True)
        g_den = jnp.sum(jnp.where(is_g, jnp.exp(logit - gmax), 0.0), axis=-1, keepdims=True)
        g_p = jnp.exp(pick(lo // EXPERTS_PER_GROUP) - gmax) / g_den
        l_lo, l_hi = pick(N_GROUPS + lo), pick(N_GROUPS + hi)
        gates = (g_p * jax.nn.sigmoid(l_lo - l_hi), g_p * jax.nn.sigmoid(l_hi - l_lo))

        abs_ = [jnp.dot(h, w13_ref[0], preferred_element_type=F32) for w13_ref in (w13lo_ref, w13hi_ref)]
        y = xg
        for ab, w2_ref, gate in zip(abs_, (w2lo_ref, w2hi_ref), gates):
            hid = jax.nn.silu(ab[:, :EXPERT_HIDDEN]) * ab[:, EXPERT_HIDDEN:]
            y = y + gate * jnp.dot(hid.astype(BF16), w2_ref[0], preferred_element_type=F32)
        for c in range(nchunk):
            obuf[slot, pl.ds(c, MOE_ROWS, stride=nchunk), :] = y[:, c * LANE:(c + 1) * LANE]
        start_scatters(nb, slot)

    @pl.when(nb == nblk - 1)
    def _():
        @pl.when(nb >= 1)
        def _():
            @pl.when(cnt_ref[nb - 1] > 0)
            def _():
                wait_scatters(nb - 1, 1 - slot)

        @pl.when(cnt > 0)
        def _():
            wait_scatters(nb, slot)


def _ple_kernel(x_ref, p_ref, gple_ref, wpg_ref, wple_ref, o_ref):
    tm = o_ref.shape[0]
    nchunk = o_ref.shape[1] // LANE
    emb = jnp.dot(p_ref[...].astype(BF16), wple_ref[...], preferred_element_type=F32)
    x = jnp.concatenate([x_ref[pl.ds(c, tm, stride=nchunk), :] for c in range(nchunk)], axis=1)
    hp = (x * _rms_scale(x, x.shape[-1]) * gple_ref[...]).astype(BF16)
    gate = jax.nn.sigmoid(jnp.dot(hp, wpg_ref[...], preferred_element_type=F32))
    o_ref[...] = x + gate * emb


def _const(shape):
    nd = len(shape)
    return pl.BlockSpec(shape, lambda *_: (0,) * nd)


def _head_slab_cols(w, width, offset=0):
    k = w.shape[0]
    w = w.reshape(k, N_HEADS, width)
    w = jnp.pad(w, ((0, 0), (0, 0), (offset, LANE - width - offset)))
    return w.reshape(k, N_HEADS * LANE)


def _prep_weights(g_mix, w_in, g_cq, w_uq, g_qn, g_qr, g_ckv, w_ukv, g_kn, g_kr, w_oa,
                  conv_w, w_oc, w_o, g_ffn, w_rg, b_rg, w_re, b_re, w1, w3, w2, g_ple, w_pg, w_ple):
    d = w_in.shape[0]
    n_mla = Q_LORA + KV_LORA
    kr_cols = jnp.pad(w_in[:, n_mla:n_mla + QK_ROPE], ((0, 0), (QK_NOPE, LANE - QK_NOPE - QK_ROPE)))
    w = {}
    w["wa"] = jnp.concatenate([w_in[:, :n_mla], kr_cols], axis=1).astype(BF16)
    w["wb"] = w_in[:, n_mla + QK_ROPE:].astype(BF16)
    w["wuq"] = _head_slab_cols(w_uq, QK_NOPE + QK_ROPE).astype(BF16)
    ukv = w_ukv.reshape(KV_LORA, N_HEADS, QK_NOPE + V_DIM)
    w["wuk"] = _head_slab_cols(ukv[:, :, :QK_NOPE].reshape(KV_LORA, -1), QK_NOPE).astype(BF16)
    w["wuv"] = _head_slab_cols(ukv[:, :, QK_NOPE:].reshape(KV_LORA, -1), V_DIM).astype(BF16)
    pad_hi = LANE - QK_NOPE - QK_ROPE
    w["gq"] = (jnp.pad(jnp.concatenate([g_qn, g_qr]), (0, pad_hi)) * (ATTN_SCALE * LOG2E))[None]
    w["gk"] = jnp.pad(g_kn, (0, LANE - QK_NOPE))[None]
    w["gkr"] = jnp.pad(g_kr, (QK_NOPE, pad_hi))[None]
    w["gmix"], w["gcq"], w["gckv"] = g_mix[None], g_cq[None], g_ckv[None]
    w["gffn"], w["gple"] = g_ffn[None], g_ple[None]
    w["convw"] = jnp.pad(conv_w, ((0, 8 - conv_w.shape[0]), (0, 0)))
    w["woa"], w["woc"], w["wo"] = w_oa.astype(BF16), w_oc.astype(BF16), w_o.astype(BF16)
    n_r = N_GROUPS + N_EXPERTS
    w["wr"] = jnp.pad(jnp.concatenate([w_rg, w_re], axis=1), ((0, 0), (0, LANE - n_r))).astype(BF16)
    w["br"] = jnp.pad(jnp.concatenate([b_rg, b_re]), (0, LANE - n_r))[None]
    pad_g = EXPERTS_PER_GROUP - N_GROUPS
    wrt = jnp.concatenate([jnp.pad(w_rg.T, ((0, pad_g), (0, 0))), w_re.T], axis=0)
    wrt_hi = wrt.astype(BF16)
    w["wrt"] = jnp.concatenate([wrt_hi, (wrt - wrt_hi.astype(F32)).astype(BF16)], axis=0)
    w["brt"] = jnp.concatenate([b_rg, jnp.full((pad_g,), NEG_BIG, F32), b_re])[:, None]
    w["w13"] = jnp.concatenate([w1, w3], axis=2).astype(BF16)
    w["w2"] = w2.astype(BF16)
    w["wpg"], w["wple"] = w_pg.astype(BF16), w_ple.astype(BF16)
    return w


def _rope_slabs(pos):
    inv = 1.0 / (ROPE_THETA ** (jnp.arange(0, QK_ROPE, 2, dtype=F32) / QK_ROPE))
    ang = pos.astype(F32)[:, None] * inv[None, :]
    cos, sin = jnp.cos(ang), jnp.sin(ang)
    n = pos.shape[0]
    half = QK_ROPE // 2
    z = lambda k: jnp.zeros((n, k), F32)
    pad_hi = LANE - QK_NOPE - QK_ROPE
    rc = jnp.concatenate([jnp.ones((n, QK_NOPE), F32), cos, cos, z(pad_hi)], axis=1)
    rs1 = jnp.concatenate([z(QK_NOPE), -sin, z(half), z(pad_hi)], axis=1)
    rs2 = jnp.concatenate([z(QK_NOPE), z(half), sin, z(pad_hi)], axis=1)
    return (rc, rs1, rs2), cos.T, sin.T


def _params(sem):
    return pltpu.CompilerParams(dimension_semantics=sem, vmem_limit_bytes=VMEM_LIMIT)


def _mla_pre(x, w, rope, tm):
    b, s, d = x.shape
    hw = N_HEADS * LANE
    tok = lambda width: pl.BlockSpec((1, tm, width), lambda i, j: (i, j, 0))
    rope_spec = pl.BlockSpec((tm, LANE), lambda i, j: (j, 0))
    consts = [w["gmix"], w["wa"], w["gcq"], w["gckv"], w["gkr"], w["wuq"], w["wuk"], w["wuv"], w["gq"], w["gk"]]
    return pl.pallas_call(
        _pre_kernel,
        grid=(b, s // tm),
        in_specs=[tok(d)] + [_const(c.shape) for c in consts] + [rope_spec] * 3,
        out_specs=[tok(hw), tok(hw), tok(hw), tok(KV_LORA), tok(QK_ROPE)],
        out_shape=[jax.ShapeDtypeStruct((b, s, hw), BF16)] * 3
        + [jax.ShapeDtypeStruct((b, s, KV_LORA), F32), jax.ShapeDtypeStruct((b, s, QK_ROPE), F32)],
        compiler_params=_params(("parallel", "parallel")),
        name="mla_pre",
    )(x, *consts, *rope)


def _mla_pre_t(x, w, rope, cos_t, sin_t, tm):
    b, s, d = x.shape
    hw = N_HEADS * LANE
    tok = lambda width: pl.BlockSpec((1, tm, width), lambda i, j: (i, j, 0))
    tiled = pl.BlockSpec((1, 1, hw, tm), lambda i, j: (i, j, 0, 0))
    rope_spec = pl.BlockSpec((tm, LANE), lambda i, j: (j, 0))
    rope_t_spec = pl.BlockSpec((QK_ROPE // 2, tm), lambda i, j: (0, j))
    gqt = jnp.broadcast_to(w["gq"].T, (LANE, tm))
    consts = [w["gmix"], w["wa"], w["gcq"], w["gckv"], w["gkr"], w["wuq"].T, w["wuk"], w["wuv"].T, gqt, w["gk"]]
    return pl.pallas_call(
        _pre_kernel_t,
        grid=(b, s // tm),
        in_specs=[tok(d)] + [_const(c.shape) for c in consts] + [rope_spec] * 3 + [rope_t_spec] * 2,
        out_specs=[tiled, tok(hw), tiled, tok(KV_LORA), tok(QK_ROPE)],
        out_shape=[jax.ShapeDtypeStruct((b, s // tm, hw, tm), BF16), jax.ShapeDtypeStruct((b, s, hw), BF16),
                   jax.ShapeDtypeStruct((b, s // tm, hw, tm), BF16),
                   jax.ShapeDtypeStruct((b, s, KV_LORA), F32), jax.ShapeDtypeStruct((b, s, QK_ROPE), F32)],
        compiler_params=_params(("parallel", "parallel")),
        name="mla_pre_t",
    )(x, *consts, *rope, cos_t, sin_t)


def _kv_past(past_lat, past_kpe, w, tm):
    b, t, _ = past_lat.shape
    hw = N_HEADS * LANE
    kpe_slab = jnp.pad(past_kpe, ((0, 0), (0, 0), (QK_NOPE, LANE - QK_NOPE - QK_ROPE)))
    tok = lambda width: pl.BlockSpec((1, tm, width), lambda i, j: (i, j, 0))
    consts = [w["wuk"], w["wuv"], w["gk"]]
    return pl.pallas_call(
        _kvpast_kernel,
        grid=(b, t // tm),
        in_specs=[tok(KV_LORA), tok(LANE)] + [_const(c.shape) for c in consts],
        out_specs=[tok(hw), tok(hw)],
        out_shape=[jax.ShapeDtypeStruct((b, t, hw), BF16)] * 2,
        compiler_params=_params(("parallel", "parallel")),
        name="kv_past",
    )(past_lat, kpe_slab, *consts)


def _attn_prompt(qt, k, vt):
    b, n_tiles, _, tq = qt.shape
    s = n_tiles * tq
    pair = 2 * LANE
    return pl.pallas_call(
        functools.partial(_attn_prompt_kernel, tq=tq),
        grid=(b, N_HEADS // 2, n_tiles),
        in_specs=[pl.BlockSpec((1, 1, pair, tq), lambda i, h, j: (i, j, h, 0)),
                  pl.BlockSpec((1, s, pair), lambda i, h, j: (i, 0, h)),
                  pl.BlockSpec((1, n_tiles, pair, tq), lambda i, h, j: (i, 0, h, 0))],
        out_specs=pl.BlockSpec((1, tq, LANE), lambda i, h, j: (i, j, h)),
        out_shape=jax.ShapeDtypeStruct((b, s, N_HEADS * V_DIM), BF16),
        scratch_shapes=[pltpu.VMEM((tq, tq), F32)] * 4 + [pltpu.VMEM((tq, tq), BF16)] * 4,
        compiler_params=_params(("parallel", "parallel", "arbitrary")),
        name="attn_prompt",
    )(qt, k, vt)


def _attn_sample(q, kp, vp, kn, vn):
    b, s, _ = q.shape
    t = kp.shape[1]
    hw = N_HEADS * LANE
    blk = lambda rows: pl.BlockSpec((1, rows, hw), lambda i: (i, 0, 0))
    return pl.pallas_call(
        _attn_sample_kernel,
        grid=(b,),
        in_specs=[blk(s), blk(t), blk(t), blk(s), blk(s)],
        out_specs=pl.BlockSpec((1, s, N_HEADS * V_DIM), lambda i: (i, 0, 0)),
        out_shape=jax.ShapeDtypeStruct((b, s, N_HEADS * V_DIM), BF16),
        compiler_params=_params(("parallel",)),
        name="attn_sample",
    )(q, kp, vp, kn, vn)


def _post(x, attn, conv_init, cnt0, x1_all, row_off, n_all, w, tm):
    b, s, d = x.shape
    nchunk = d // LANE
    n_keys = cnt0.shape[0]
    tiles_per_b = s // tm
    off = row_off // tm
    tok = lambda width: pl.BlockSpec((1, tm, width), lambda i, j: (i, j, 0))
    per_b = lambda rows: pl.BlockSpec((1, rows, CONV_DIM), lambda i, j: (i, 0, 0))
    tri = (jnp.arange(tm)[:, None] < jnp.arange(tm)[None, :]).astype(BF16)
    consts = [cnt0, w["gmix"], w["wb"], w["convw"], w["woa"], w["woc"], w["wo"], w["gffn"], w["wrt"], w["brt"], tri]
    in_specs = ([tok(d), tok(N_HEADS * V_DIM), per_b(8)] + [_const(c.shape) for c in consts]
                + [pl.BlockSpec(memory_space=pl.ANY)])
    args = [x, attn, conv_init] + consts + [x1_all]
    return pl.pallas_call(
        _post_kernel,
        grid=(b, tiles_per_b),
        in_specs=in_specs,
        out_specs=[pl.BlockSpec((tm * nchunk, LANE), lambda i, j: (off + i * tiles_per_b + j, 0)),
                   pl.BlockSpec((1, 8, tm), lambda i, j: (i * tiles_per_b + j, 0, 0)),
                   per_b(2), _const((n_keys, 1))],
        out_shape=[jax.ShapeDtypeStruct((n_all * nchunk, LANE), F32),
                   jax.ShapeDtypeStruct((b * tiles_per_b, 8, tm), F32),
                   jax.ShapeDtypeStruct((b, 2, CONV_DIM), F32), jax.ShapeDtypeStruct((n_keys, 1), F32)],
        scratch_shapes=[pltpu.VMEM((8, CONV_DIM), F32)],
        input_output_aliases={len(args) - 1: 0},
        compiler_params=_params(("arbitrary", "arbitrary")),
        name="post",
    )(*args)


def _route_tables(key, rank, counts, n):
    n_keys = counts.shape[0]
    padded = (counts + MOE_ROWS - 1) // MOE_ROWS * MOE_ROWS
    pend = jnp.cumsum(padded)
    pstart = pend - padded
    ids = jnp.arange(n_keys, dtype=jnp.int32)
    dest = rank + jnp.sum(jnp.where(key[:, None] == ids[None, :], pstart[None, :], 0), axis=1)
    nblk = n // MOE_ROWS + N_PAIR_BUCKETS
    blk_start = jnp.arange(nblk, dtype=jnp.int32) * MOE_ROWS
    blk_hot = (blk_start[:, None] >= pstart[None, :]) & (blk_start[:, None] < pend[None, :])
    blk_key = jnp.sum(jnp.where(blk_hot, ids[None, :], 0), axis=1)
    blk_cnt = jnp.sum(jnp.where(blk_hot, jnp.minimum(counts[None, :] - (blk_start[:, None] - pstart[None, :]),
                                                      MOE_ROWS), 0), axis=1)
    any_hot = jnp.any(blk_hot, axis=1)
    blk_key = jnp.where(any_hot, blk_key, n_keys - 1)
    blk_lo = blk_key // EXPERTS_PER_GROUP
    blk_hi = blk_lo // EXPERTS_PER_GROUP * EXPERTS_PER_GROUP + blk_key % EXPERTS_PER_GROUP
    return blk_lo, blk_hi, blk_cnt.astype(jnp.int32), dest.astype(jnp.int32)


def _moe(x1_all, key, rank, counts, w):
    rows, _ = x1_all.shape
    d = w["gffn"].shape[1]
    nchunk = d // LANE
    n = rows // nchunk
    blk_lo, blk_hi, blk_cnt, dest = _route_tables(key, rank, counts, n)
    nblk = blk_lo.shape[0]
    w13_spec = lambda ref_idx: pl.BlockSpec((1, d, 2 * EXPERT_HIDDEN),
                                            lambda i, lo, hi, cnt, dst: ((lo, hi)[ref_idx][i], 0, 0))
    w2_spec = lambda ref_idx: pl.BlockSpec((1, EXPERT_HIDDEN, d),
                                           lambda i, lo, hi, cnt, dst: ((lo, hi)[ref_idx][i], 0, 0))
    buf = pltpu.VMEM((2, MOE_ROWS * nchunk, LANE), F32)
    grid_spec = pltpu.PrefetchScalarGridSpec(
        num_scalar_prefetch=4,
        grid=(nblk,),
        in_specs=[pl.BlockSpec(memory_space=pl.ANY),
                  pl.BlockSpec((1, d), lambda i, *_: (0, 0)),
                  pl.BlockSpec(w["wr"].shape, lambda i, *_: (0, 0)),
                  pl.BlockSpec(w["br"].shape, lambda i, *_: (0, 0)),
                  w13_spec(0), w13_spec(1), w2_spec(0), w2_spec(1)],
        out_specs=pl.BlockSpec(memory_space=pl.ANY),
        scratch_shapes=[pltpu.SMEM((nblk * MOE_ROWS,), jnp.int32), buf, buf,
                        pltpu.SemaphoreType.DMA((2,)), pltpu.SemaphoreType.DMA((2,))],
    )
    return pl.pallas_call(
        _moe_kernel,
        grid_spec=grid_spec,
        out_shape=jax.ShapeDtypeStruct((rows, LANE), F32),
        compiler_params=_params(("arbitrary",)),
        name="moe",
    )(blk_lo, blk_hi, blk_cnt, dest, x1_all, w["gffn"], w["wr"], w["br"], w["w13"], w["w13"], w["w2"], w["w2"])


def _ple(x2_all, row_off, p, w, tm):
    n, pd = p.shape
    d = w["gple"].shape[1]
    nchunk = d // LANE
    off = row_off // tm
    consts = [w["gple"], w["wpg"], w["wple"]]
    return pl.pallas_call(
        _ple_kernel,
        grid=(n // tm,),
        in_specs=[pl.BlockSpec((tm * nchunk, LANE), lambda i: (off + i, 0)), pl.BlockSpec((tm, pd), lambda i: (i, 0))]
        + [_const(c.shape) for c in consts],
        out_specs=pl.BlockSpec((tm, d), lambda i: (i, 0)),
        out_shape=jax.ShapeDtypeStruct((n, d), F32),
        compiler_params=_params(("parallel",)),
        name="ple",
    )(x2_all, p, *consts)


def _mixer(x, past_lat, past_kpe, past_conv, w):
    b, s, _ = x.shape
    tm = min(512, s)
    past_len = 0 if past_lat is None else past_lat.shape[1]
    rope, cos_t, sin_t = _rope_slabs(past_len + jnp.arange(s))
    if past_lat is None:
        qt, k, vt, lat_new, kpe_new = _mla_pre_t(x, w, rope, cos_t, sin_t, tm)
        attn = _attn_prompt(qt, k, vt)
        conv_init = jnp.zeros((b, 8, CONV_DIM), F32)
    else:
        q, k, v, lat_new, kpe_new = _mla_pre(x, w, rope, tm)
        kp, vp = _kv_past(past_lat, past_kpe, w, min(512, past_len))
        attn = _attn_sample(q, kp, vp, k, v)
        conv_init = jnp.pad(past_conv, ((0, 0), (8 - past_conv.shape[1], 0), (0, 0)))
    return attn, conv_init, lat_new, kpe_new, tm


def _layer(xp, xs, pp, ps, past_lat, past_kpe, past_conv, w):
    bp, sp, d = xp.shape
    bs, ss, _ = xs.shape
    n_p, n_s = bp * sp, bs * ss
    n_all = n_p + n_s
    n_keys = N_EXPERTS * EXPERTS_PER_GROUP
    attn_p, cinit_p, lat_p, kpe_p, tm_p = _mixer(xp, None, None, None, w)
    attn_s, cinit_s, lat_s, kpe_s, tm_s = _mixer(xs, past_lat, past_kpe, past_conv, w)
    x1_all = jnp.zeros((n_all * (d // LANE), LANE), F32)
    x1_all, route_p, conv_p, cnt = _post(xp, attn_p, cinit_p, jnp.zeros((n_keys, 1), F32), x1_all, 0, n_all, w, tm_p)
    x1_all, route_s, conv_s, cnt = _post(xs, attn_s, cinit_s, cnt, x1_all, n_p, n_all, w, tm_s)
    key = jnp.concatenate([route_p[:, 0].reshape(-1), route_s[:, 0].reshape(-1)]).astype(jnp.int32)
    rank = jnp.concatenate([route_p[:, 1].reshape(-1), route_s[:, 1].reshape(-1)]).astype(jnp.int32)
    x2_all = _moe(x1_all, key, rank, cnt[:, 0].astype(jnp.int32), w)
    yp = _ple(x2_all, 0, pp.reshape(n_p, -1), w, min(512, n_p)).reshape(bp, sp, d)
    ys = _ple(x2_all, n_p, ps.reshape(n_s, -1), w, min(512, n_s)).reshape(bs, ss, d)
    return yp, ys, (lat_p, kpe_p, conv_p, lat_s, kpe_s, conv_s)


def kernel(x_prompt, x_sample, cache_kv_latent, cache_k_rope, state_conv, p_prompt, p_sample,
           g_mix, w_in, g_cq, w_uq, g_qn, g_qr, g_ckv, w_ukv, g_kn, g_kr, w_oa,
           conv_w, w_oc, w_o, g_ffn, w_rg, b_rg, w_re, b_re, w1, w3, w2, g_ple, w_pg, w_ple):
    depth = g_mix.shape[0]
    xp, xs = x_prompt, x_sample
    outs = [[] for _ in range(6)]
    for i in range(depth):
        w = _prep_weights(g_mix[i], w_in[i], g_cq[i], w_uq[i], g_qn[i], g_qr[i], g_ckv[i], w_ukv[i],
                          g_kn[i], g_kr[i], w_oa[i], conv_w[i], w_oc[i], w_o[i], g_ffn[i], w_rg[i], b_rg[i],
                          w_re[i], b_re[i], w1[i], w3[i], w2[i], g_ple[i], w_pg[i], w_ple[i])
        xp, xs, new = _layer(xp, xs, p_prompt[i], p_sample[i], cache_kv_latent[i], cache_k_rope[i], state_conv[i], w)
        for o, a in zip(outs, new):
            o.append(a)
    return (xp, xs) + tuple(jnp.stack(o, axis=0) for o in outs)
```

```python
import functools
import math

import jax
import jax.numpy as jnp
from jax import lax
from jax.experimental import pallas as pl
from jax.experimental.pallas import tpu as pltpu

F32 = jnp.float32
BF16 = jnp.bfloat16

LANE = 128
CHUNK = 64
N_HEADS = 8
QK_NOPE = 64
QK_ROPE = 32
V_DIM = 64
Q_LORA = 256
KV_LORA = 256
CONV_DIM = 512
N_GROUPS = 4
EXPERTS_PER_GROUP = 8
N_EXPERTS = N_GROUPS * EXPERTS_PER_GROUP
EXPERT_HIDDEN = 256
ROPE_THETA = 10000.0
EPS = 1e-6
ATTN_SCALE = (QK_NOPE + QK_ROPE) ** -0.5
LOG2E = math.log2(math.e)
NEG_BIG = -1e30
MOE_ROWS = 128
TRASH_ROWS = 2 * MOE_ROWS
N_PAIR_BUCKETS = N_GROUPS * (EXPERTS_PER_GROUP * (EXPERTS_PER_GROUP - 1) // 2)
VMEM_LIMIT = 56 * 1024 * 1024


def _rms_scale(x, n):
    return lax.rsqrt(jnp.sum(x * x, axis=-1, keepdims=True) * (1.0 / n) + EPS)


def _lane_iota(shape):
    return lax.broadcasted_iota(jnp.int32, shape, len(shape) - 1)


def _rope(t, rc, rs1, rs2):
    return t * rc + pltpu.roll(t, LANE - QK_ROPE // 2, 1) * rs1 + pltpu.roll(t, QK_ROPE // 2, 1) * rs2


def _pre_latents(x_ref, gmix_ref, wa_ref, gcq_ref, gckv_ref, gkr_ref, rope_refs, lat_ref, kpe_ref):
    x = x_ref[0]
    h = x * _rms_scale(x, x.shape[-1]) * gmix_ref[...]
    z = jnp.dot(h.astype(BF16), wa_ref[...], preferred_element_type=F32)
    cq = z[:, :Q_LORA]
    ckv = z[:, Q_LORA:Q_LORA + KV_LORA]
    kr = z[:, Q_LORA + KV_LORA:]
    cqn = cq * _rms_scale(cq, Q_LORA) * gcq_ref[...]
    lat = ckv * _rms_scale(ckv, KV_LORA) * gckv_ref[...]
    lat_ref[0] = lat
    krn = kr * _rms_scale(kr, QK_ROPE) * gkr_ref[...]
    kpe = _rope(krn, *(r[...] for r in rope_refs))
    kpe_ref[0] = kpe[:, QK_NOPE:QK_NOPE + QK_ROPE]
    return cqn, lat, kpe


def _store_keys(k_ref, kf, gk, kpe):
    for hd in range(N_HEADS):
        sl = slice(hd * LANE, (hd + 1) * LANE)
        ks = kf[:, sl]
        k_ref[0, :, sl] = (ks * _rms_scale(ks, QK_NOPE) * gk + kpe).astype(BF16)


def _pre_kernel(x_ref, gmix_ref, wa_ref, gcq_ref, gckv_ref, gkr_ref, wuq_ref, wuk_ref, wuv_ref,
                gq_ref, gk_ref, rc_ref, rs1_ref, rs2_ref,
                q_ref, k_ref, v_ref, lat_ref, kpe_ref):
    rope_refs = (rc_ref, rs1_ref, rs2_ref)
    cqn, lat, kpe = _pre_latents(x_ref, gmix_ref, wa_ref, gcq_ref, gckv_ref, gkr_ref, rope_refs, lat_ref, kpe_ref)
    rc, rs1, rs2 = (r[...] for r in rope_refs)
    is_nope = _lane_iota(kpe.shape) < QK_NOPE
    latb = lat.astype(BF16)
    qf = jnp.dot(cqn.astype(BF16), wuq_ref[...], preferred_element_type=F32)
    vf = jnp.dot(latb, wuv_ref[...], preferred_element_type=F32)
    _store_keys(k_ref, jnp.dot(latb, wuk_ref[...], preferred_element_type=F32), gk_ref[...], kpe)
    gq = gq_ref[...]
    ones_hi = jnp.where(is_nope, 0.0, 1.0)
    for hd in range(N_HEADS):
        sl = slice(hd * LANE, (hd + 1) * LANE)
        qs = qf[:, sl]
        sq = qs * qs
        ss_all = jnp.sum(sq, axis=-1, keepdims=True)
        ss_n = jnp.sum(jnp.where(is_nope, sq, 0.0), axis=-1, keepdims=True)
        r = jnp.where(is_nope, lax.rsqrt(ss_n * (1.0 / QK_NOPE) + EPS),
                      lax.rsqrt((ss_all - ss_n) * (1.0 / QK_ROPE) + EPS))
        q_ref[0, :, sl] = _rope(qs * r * gq, rc, rs1, rs2).astype(BF16)
        v_ref[0, :, sl] = (vf[:, sl] + ones_hi).astype(BF16)


def _pre_kernel_t(x_ref, gmix_ref, wa_ref, gcq_ref, gckv_ref, gkr_ref, wuqt_ref, wuk_ref, wuvt_ref,
                  gqt_ref, gk_ref, rc_ref, rs1_ref, rs2_ref, cos_ref, sin_ref,
                  qt_ref, k_ref, vt_ref, lat_ref, kpe_ref):
    rope_refs = (rc_ref, rs1_ref, rs2_ref)
    cqn, lat, kpe = _pre_latents(x_ref, gmix_ref, wa_ref, gcq_ref, gckv_ref, gkr_ref, rope_refs, lat_ref, kpe_ref)
    latb = lat.astype(BF16)
    _store_keys(k_ref, jnp.dot(latb, wuk_ref[...], preferred_element_type=F32), gk_ref[...], kpe)
    tm = cqn.shape[0]
    qft = jnp.dot(wuqt_ref[...], cqn.T.astype(BF16), preferred_element_type=F32)
    vft = jnp.dot(wuvt_ref[...], lat.T.astype(BF16), preferred_element_type=F32)
    gq, cos, sin = gqt_ref[...], cos_ref[...], sin_ref[...]
    half = QK_ROPE // 2
    ones_lo = jnp.where(lax.broadcasted_iota(jnp.int32, (LANE, tm), 0) < V_DIM, 0.0, 1.0)
    pad = jnp.zeros((LANE - QK_NOPE - QK_ROPE, tm), F32)
    for hd in range(N_HEADS):
        rows = slice(hd * LANE, (hd + 1) * LANE)
        qs = qft[rows]
        sq = qs * qs
        r_n = lax.rsqrt(jnp.sum(sq[:QK_NOPE], axis=0, keepdims=True) * (1.0 / QK_NOPE) + EPS)
        r_p = lax.rsqrt(jnp.sum(sq[QK_NOPE:QK_NOPE + QK_ROPE], axis=0, keepdims=True) * (1.0 / QK_ROPE) + EPS)
        nope = qs[:QK_NOPE] * r_n * gq[:QK_NOPE]
        x1 = qs[QK_NOPE:QK_NOPE + half] * r_p * gq[QK_NOPE:QK_NOPE + half]
        x2 = qs[QK_NOPE + half:QK_NOPE + QK_ROPE] * r_p * gq[QK_NOPE + half:QK_NOPE + QK_ROPE]
        slab = jnp.concatenate([nope, x1 * cos - x2 * sin, x1 * sin + x2 * cos, pad], axis=0)
        qt_ref[0, 0, rows, :] = slab.astype(BF16)
        vt_ref[0, 0, rows, :] = (vft[rows] + ones_lo).astype(BF16)


def _kvpast_kernel(lat_ref, kpe_ref, wuk_ref, wuv_ref, gk_ref, k_ref, v_ref):
    latb = lat_ref[0].astype(BF16)
    kf = jnp.dot(latb, wuk_ref[...], preferred_element_type=F32)
    vf = jnp.dot(latb, wuv_ref[...], preferred_element_type=F32)
    kpe = kpe_ref[0]
    lane = _lane_iota(kpe.shape)
    ones_hi = jnp.where(lane < QK_NOPE, 0.0, 1.0)
    gk = gk_ref[...]
    for hd in range(N_HEADS):
        sl = slice(hd * LANE, (hd + 1) * LANE)
        ks = kf[:, sl]
        k_ref[0, :, sl] = (ks * _rms_scale(ks, QK_NOPE) * gk + kpe).astype(BF16)
        v_ref[0, :, sl] = (vf[:, sl] + ones_hi).astype(BF16)


def _finish_pair(accs):
    outs = [a / pltpu.roll(a, V_DIM, 1) for a in accs]
    lane = _lane_iota(outs[0].shape)
    return jnp.where(lane < V_DIM, outs[0], pltpu.roll(outs[1], V_DIM, 1))


def _attn_prompt_kernel(qt_ref, k_ref, vt_ref, o_ref, s00, s01, s10, s11, p00, p01, p10, p11, *, tq):
    s_refs = ((s00, s01), (s10, s11))
    p_refs = ((p00, p01), (p10, p11))
    qi = pl.program_id(2)
    key_pos = lax.broadcasted_iota(jnp.int32, (tq, tq), 0)
    query_pos = lax.broadcasted_iota(jnp.int32, (tq, tq), 1)
    diag_mask = (key_pos // CHUNK) <= (query_pos // CHUNK)
    heads = [slice(hh * LANE, (hh + 1) * LANE) for hh in range(2)]

    def scores(i, slot, hh, mask):
        start = pl.multiple_of(i * tq, tq)
        s = jnp.dot(k_ref[0, pl.ds(start, tq), heads[hh]], qt_ref[0, 0, heads[hh], :], preferred_element_type=F32)
        if mask is not None:
            s = jnp.where(mask, s, NEG_BIG)
        s_refs[slot][hh][...] = s
        return jnp.max(s, axis=0, keepdims=True)

    def softmax(slot, hh, m, tile_max):
        m_new = jnp.maximum(m, tile_max)
        p_refs[slot][hh][...] = jnp.exp2(s_refs[slot][hh][...] - m_new).astype(BF16)
        return m_new, jnp.exp2(m - m_new)

    def accumulate(i, slot, hh, alpha, acc):
        pv = jnp.dot(vt_ref[0, jnp.maximum(i, 0), heads[hh], :], p_refs[slot][hh][...],
                     preferred_element_type=F32)
        return acc * alpha + pv

    def iteration(i, slot, carry, next_mask=None):
        stats = [softmax(slot, hh, m, tile_max) for hh, (m, tile_max, _, _) in enumerate(carry)]
        next_max = [scores(i + 1, 1 - slot, hh, next_mask) for hh in range(2)]
        accs = [accumulate(i - 1, 1 - slot, hh, alpha, acc) for hh, (_, _, alpha, acc) in enumerate(carry)]
        return tuple((m, tmax, alpha, acc) for (m, alpha), tmax, acc in zip(stats, next_max, accs))

    def last(slot, carry):
        accs = []
        for hh, (m, tile_max, alpha, acc) in enumerate(carry):
            acc = accumulate(qi - 1, 1 - slot, hh, alpha, acc)
            m, alpha = softmax(slot, hh, m, tile_max)
            accs.append(accumulate(qi, slot, hh, alpha, acc))
        return tuple(accs)

    first_mask = diag_mask | (qi > 0)
    init = []
    for hh in range(2):
        p_refs[1][hh][...] = jnp.zeros((tq, tq), BF16)
        init.append((jnp.full((1, tq), NEG_BIG, F32), scores(0, 0, hh, first_mask), jnp.ones((1, tq), F32),
                     jnp.zeros((LANE, tq), F32)))
    carry = lax.fori_loop(0, (qi - 1) // 2, lambda j, c: iteration(2 * j + 1, 1, iteration(2 * j, 0, c)),
                          tuple(init))
    tails = [lambda c: last(0, c),
             lambda c: last(1, iteration(qi - 1, 0, c, diag_mask)),
             lambda c: last(0, iteration(qi - 1, 1, iteration(qi - 2, 0, c), diag_mask))]
    accs = lax.switch(jnp.where(qi == 0, 0, 2 - qi % 2), tails, carry)
    out_t = jnp.concatenate([a[:V_DIM] / a[V_DIM:V_DIM + 1] for a in accs], axis=0)
    o_ref[0] = out_t.T.astype(o_ref.dtype)


def _attn_sample_kernel(q_ref, kp_ref, vp_ref, kn_ref, vn_ref, o_ref):
    nt = (((1,), (1,)), ((), ()))
    slabs = [slice(hd * LANE, (hd + 1) * LANE) for hd in range(N_HEADS)]
    s_past = [lax.dot_general(q_ref[0, :, sl], kp_ref[0, :, sl], nt, preferred_element_type=F32) for sl in slabs]
    s_new = [lax.dot_general(q_ref[0, :, sl], kn_ref[0, :, sl], nt, preferred_element_type=F32) for sl in slabs]
    accs = []
    for sl, sp, sn in zip(slabs, s_past, s_new):
        m = jnp.maximum(jnp.max(sp, axis=-1, keepdims=True), jnp.max(sn, axis=-1, keepdims=True))
        accs.append(jnp.dot(jnp.exp2(sp - m).astype(BF16), vp_ref[0, :, sl], preferred_element_type=F32)
                    + jnp.dot(jnp.exp2(sn - m).astype(BF16), vn_ref[0, :, sl], preferred_element_type=F32))
    for pair in range(N_HEADS // 2):
        o_ref[0, :, pair * LANE:(pair + 1) * LANE] = _finish_pair(accs[2 * pair:2 * pair + 2]).astype(o_ref.dtype)


def _post_kernel(x_ref, attn_ref, cinit_ref, cnt0_ref, gmix_ref, wb_ref, convw_ref, woa_ref, woc_ref, wo_ref,
                 gffn_ref, wrt_ref, brt_ref, tri_ref, x1_all_ref,
                 x1_ref, route_ref, cnew_ref, cnt_ref, carry_ref):
    del x1_all_ref
    si = pl.program_id(1)
    tm = x_ref.shape[1]

    @pl.when(si == 0)
    def _():
        carry_ref[...] = cinit_ref[0]

    @pl.when((si == 0) & (pl.program_id(0) == 0))
    def _():
        cnt_ref[...] = cnt0_ref[...]

    x = x_ref[0]
    h = x * _rms_scale(x, x.shape[-1]) * gmix_ref[...]
    z = jnp.dot(h.astype(BF16), wb_ref[...], preferred_element_type=F32)
    conv_b = z[:, :CONV_DIM]
    u = z[:, CONV_DIM:2 * CONV_DIM] * z[:, 2 * CONV_DIM:3 * CONV_DIM]
    d = x.shape[-1]
    gate_a = z[:, 3 * CONV_DIM:3 * CONV_DIM + d]
    gate_c = z[:, 3 * CONV_DIM + d:]

    carry = carry_ref[...]
    c1 = carry[7:8, :]
    c2 = carry[6:7, :]
    row = lax.broadcasted_iota(jnp.int32, u.shape, 0)
    u_m1 = jnp.where(row == 0, c1, pltpu.roll(u, 1, 0))
    u_m2 = jnp.where(row == 0, c2, jnp.where(row == 1, c1, pltpu.roll(u, 2, 0)))
    cw = convw_ref[...]
    cv = cw[0:1, :] * u_m2 + cw[1:2, :] * u_m1 + cw[2:3, :] * u
    carry_ref[...] = u[tm - 8:, :]
    cnew_ref[0] = u[tm - 2:, :]

    y_a = jnp.dot(attn_ref[0], woa_ref[...], preferred_element_type=F32)
    y_c = jnp.dot((conv_b * cv).astype(BF16), woc_ref[...], preferred_element_type=F32)
    mrg = jax.nn.sigmoid(gate_a) * y_a + jax.nn.sigmoid(gate_c) * y_c
    x1 = x + jnp.dot(mrg.astype(BF16), wo_ref[...], preferred_element_type=F32)
    nchunk = d // LANE
    for c in range(nchunk):
        x1_ref[pl.ds(c, tm, stride=nchunk), :] = x1[:, c * LANE:(c + 1) * LANE]

    h2 = x1 * _rms_scale(x1, d) * gffn_ref[...]
    h2_hi = h2.astype(BF16)
    h2_lo = (h2 - h2_hi.astype(F32)).astype(BF16)
    nt = (((1,), (1,)), ((), ()))
    n_rows = brt_ref.shape[0]
    lt2 = lax.dot_general(wrt_ref[...], h2_hi, nt, preferred_element_type=F32)
    lt = (lt2[:n_rows] + lt2[n_rows:]
          + lax.dot_general(wrt_ref[:n_rows, :], h2_lo, nt, preferred_element_type=F32) + brt_ref[...])
    sub = lax.broadcasted_iota(jnp.int32, (EXPERTS_PER_GROUP, tm), 0).astype(F32)
    none = float(EXPERTS_PER_GROUP)

    def first_argmax(v):
        vmax = jnp.max(v, axis=0, keepdims=True)
        return jnp.min(jnp.where(v == vmax, sub, none), axis=0, keepdims=True)

    g_idx = first_argmax(lt[:EXPERTS_PER_GROUP])
    el = lt[EXPERTS_PER_GROUP * N_GROUPS:]
    for g in range(N_GROUPS - 2, -1, -1):
        el = jnp.where(g_idx == g, lt[EXPERTS_PER_GROUP * (g + 1):EXPERTS_PER_GROUP * (g + 2)], el)
    i1 = first_argmax(el)
    i2 = first_argmax(jnp.where(sub == i1, NEG_BIG, el))
    key = (g_idx * (EXPERTS_PER_GROUP * EXPERTS_PER_GROUP) + jnp.minimum(i1, i2) * EXPERTS_PER_GROUP
           + jnp.maximum(i1, i2))

    n_keys = cnt_ref.shape[0]
    keys = lax.broadcasted_iota(jnp.int32, (n_keys, tm), 0).astype(F32)
    onehot = jnp.where(keys == key, 1.0, 0.0)
    before = jnp.dot(onehot.astype(BF16), tri_ref[...], preferred_element_type=F32)
    rank = jnp.sum(onehot * (before + cnt_ref[...]), axis=0, keepdims=True)
    cnt_ref[...] += jnp.sum(onehot, axis=1, keepdims=True)
    row8 = lax.broadcasted_iota(jnp.int32, (8, tm), 0)
    route_ref[0] = jnp.where(row8 == 0, key, jnp.where(row8 == 1, rank, 0.0))


def _moe_kernel(lo_ref, hi_ref, cnt_ref, dest_ref,
                x1_hbm, gffn_ref, wr_ref, br_ref, w13lo_ref, w13hi_ref, w2lo_ref, w2hi_ref,
                x2_hbm, tok_ref, xbuf, obuf, gsem, ssem):
    nb = pl.program_id(0)
    nchunk = xbuf.shape[1] // MOE_ROWS
    n_tok = dest_ref.shape[0]
    slot = nb % 2
    real = lambda blk: cnt_ref[blk] > 0

    def start_gathers(blk, sl):
        for i in range(MOE_ROWS):
            src = x1_hbm.at[pl.ds(tok_ref[blk * MOE_ROWS + i] * nchunk, nchunk)]
            pltpu.make_async_copy(src, xbuf.at[sl, pl.ds(i * nchunk, nchunk)], gsem.at[sl]).start()

    def start_scatters(blk, sl):
        for i in range(MOE_ROWS):
            dst = x2_hbm.at[pl.ds(tok_ref[blk * MOE_ROWS + i] * nchunk, nchunk)]
            pltpu.make_async_copy(obuf.at[sl, pl.ds(i * nchunk, nchunk)], dst, ssem.at[sl]).start()

    def wait_gathers(sl):
        pltpu.make_async_copy(x1_hbm.at[pl.ds(0, MOE_ROWS * nchunk)], xbuf.at[sl], gsem.at[sl]).wait()

    def wait_scatters(sl):
        pltpu.make_async_copy(obuf.at[sl], x2_hbm.at[pl.ds(0, MOE_ROWS * nchunk)], ssem.at[sl]).wait()

    @pl.when(nb == 0)
    def _():
        def fill(blk, c):
            first = n_tok + blk % 2 * MOE_ROWS
            for i in range(MOE_ROWS):
                tok_ref[blk * MOE_ROWS + i] = first + i
            return c
        lax.fori_loop(0, pl.num_programs(0), fill, 0)

        def invert(i, c):
            tok_ref[dest_ref[i]] = i
            return c
        lax.fori_loop(0, n_tok, invert, 0, unroll=8)

        obuf[...] = jnp.zeros_like(obuf)
        for sl in range(2):
            first = (n_tok + sl * MOE_ROWS) * nchunk
            init = pltpu.make_async_copy(obuf.at[sl], x2_hbm.at[pl.ds(first, MOE_ROWS * nchunk)], ssem.at[sl])
            init.start()
            init.wait()

        @pl.when(real(0))
        def _():
            start_gathers(0, 0)

    @pl.when(nb >= 2)
    def _():
        @pl.when(real(nb - 1))
        def _():
            wait_scatters(slot)

    def compute(with_prev):
        wait_gathers(slot)
        if with_prev:
            start_scatters(nb - 1, 1 - slot)
        xg = jnp.concatenate([xbuf[slot, pl.ds(c, MOE_ROWS, stride=nchunk), :] for c in range(nchunk)], axis=1)
        start_gathers(nb + 1, 1 - slot)
        h = (xg * _rms_scale(xg, xg.shape[-1]) * gffn_ref[...]).astype(BF16)

        lo, hi = lo_ref[nb], hi_ref[nb]
        logit = jnp.dot(h, wr_ref[...], preferred_element_type=F32) + br_ref[...]
        lane = _lane_iota(logit.shape)
        pick = lambda j: jnp.sum(jnp.where(lane == j, logit, 0.0), axis=-1, keepdims=True)
        is_g = lane < N_GROUPS
        gmax = jnp.max(jnp.where(is_g, logit, NEG_BIG), axis=-1, keepdims=True)
        g_den = jnp.sum(jnp.where(is_g, jnp.exp(logit - gmax), 0.0), axis=-1, keepdims=True)
        g_p = jnp.exp(pick(lo // EXPERTS_PER_GROUP) - gmax) / g_den
        l_lo, l_hi = pick(N_GROUPS + lo), pick(N_GROUPS + hi)
        gates = (g_p * jax.nn.sigmoid(l_lo - l_hi), g_p * jax.nn.sigmoid(l_hi - l_lo))

        abs_ = [jnp.dot(h, w13_ref[0], preferred_element_type=F32) for w13_ref in (w13lo_ref, w13hi_ref)]
        y = xg
        for ab, w2_ref, gate in zip(abs_, (w2lo_ref, w2hi_ref), gates):
            hid = jax.nn.silu(ab[:, :EXPERT_HIDDEN]) * ab[:, EXPERT_HIDDEN:]
            y = y + gate * jnp.dot(hid.astype(BF16), w2_ref[0], preferred_element_type=F32)
        for c in range(nchunk):
            obuf[slot, pl.ds(c, MOE_ROWS, stride=nchunk), :] = y[:, c * LANE:(c + 1) * LANE]

    @pl.when((nb == 0) & real(0))
    def _():
        compute(with_prev=False)

    @pl.when((nb > 0) & real(nb))
    def _():
        compute(with_prev=True)

    @pl.when((nb > 0) & jnp.logical_not(real(nb)))
    def _():
        @pl.when(real(nb - 1))
        def _():
            wait_gathers(slot)
            start_scatters(nb - 1, 1 - slot)
            wait_scatters(1 - slot)


def _ple_kernel(x_ref, p_ref, gple_ref, wpg_ref, wple_ref, o_ref):
    tm = o_ref.shape[0]
    nchunk = o_ref.shape[1] // LANE
    emb = jnp.dot(p_ref[...].astype(BF16), wple_ref[...], preferred_element_type=F32)
    x = jnp.concatenate([x_ref[pl.ds(c, tm, stride=nchunk), :] for c in range(nchunk)], axis=1)
    hp = (x * _rms_scale(x, x.shape[-1]) * gple_ref[...]).astype(BF16)
    gate = jax.nn.sigmoid(jnp.dot(hp, wpg_ref[...], preferred_element_type=F32))
    o_ref[...] = x + gate * emb


def _const(shape):
    nd = len(shape)
    return pl.BlockSpec(shape, lambda *_: (0,) * nd)


def _head_slab_cols(w, width, offset=0):
    k = w.shape[0]
    w = w.reshape(k, N_HEADS, width)
    w = jnp.pad(w, ((0, 0), (0, 0), (offset, LANE - width - offset)))
    return w.reshape(k, N_HEADS * LANE)


def _prep_weights(g_mix, w_in, g_cq, w_uq, g_qn, g_qr, g_ckv, w_ukv, g_kn, g_kr, w_oa,
                  conv_w, w_oc, w_o, g_ffn, w_rg, b_rg, w_re, b_re, w1, w3, w2, g_ple, w_pg, w_ple):
    d = w_in.shape[0]
    n_mla = Q_LORA + KV_LORA
    kr_cols = jnp.pad(w_in[:, n_mla:n_mla + QK_ROPE], ((0, 0), (QK_NOPE, LANE - QK_NOPE - QK_ROPE)))
    w = {}
    w["wa"] = jnp.concatenate([w_in[:, :n_mla], kr_cols], axis=1).astype(BF16)
    w["wb"] = w_in[:, n_mla + QK_ROPE:].astype(BF16)
    w["wuq"] = _head_slab_cols(w_uq, QK_NOPE + QK_ROPE).astype(BF16)
    ukv = w_ukv.reshape(KV_LORA, N_HEADS, QK_NOPE + V_DIM)
    w["wuk"] = _head_slab_cols(ukv[:, :, :QK_NOPE].reshape(KV_LORA, -1), QK_NOPE).astype(BF16)
    w["wuv"] = _head_slab_cols(ukv[:, :, QK_NOPE:].reshape(KV_LORA, -1), V_DIM).astype(BF16)
    pad_hi = LANE - QK_NOPE - QK_ROPE
    w["gq"] = (jnp.pad(jnp.concatenate([g_qn, g_qr]), (0, pad_hi)) * (ATTN_SCALE * LOG2E))[None]
    w["gk"] = jnp.pad(g_kn, (0, LANE - QK_NOPE))[None]
    w["gkr"] = jnp.pad(g_kr, (QK_NOPE, pad_hi))[None]
    w["gmix"], w["gcq"], w["gckv"] = g_mix[None], g_cq[None], g_ckv[None]
    w["gffn"], w["gple"] = g_ffn[None], g_ple[None]
    w["convw"] = jnp.pad(conv_w, ((0, 8 - conv_w.shape[0]), (0, 0)))
    w["woa"], w["woc"], w["wo"] = w_oa.astype(BF16), w_oc.astype(BF16), w_o.astype(BF16)
    n_r = N_GROUPS + N_EXPERTS
    w["wr"] = jnp.pad(jnp.concatenate([w_rg, w_re], axis=1), ((0, 0), (0, LANE - n_r))).astype(BF16)
    w["br"] = jnp.pad(jnp.concatenate([b_rg, b_re]), (0, LANE - n_r))[None]
    pad_g = EXPERTS_PER_GROUP - N_GROUPS
    wrt = jnp.concatenate([jnp.pad(w_rg.T, ((0, pad_g), (0, 0))), w_re.T], axis=0)
    wrt_hi = wrt.astype(BF16)
    w["wrt"] = jnp.concatenate([wrt_hi, (wrt - wrt_hi.astype(F32)).astype(BF16)], axis=0)
    w["brt"] = jnp.concatenate([b_rg, jnp.full((pad_g,), NEG_BIG, F32), b_re])[:, None]
    w["w13"] = jnp.concatenate([w1, w3], axis=2).astype(BF16)
    w["w2"] = w2.astype(BF16)
    w["wpg"], w["wple"] = w_pg.astype(BF16), w_ple.astype(BF16)
    return w


def _rope_slabs(pos):
    inv = 1.0 / (ROPE_THETA ** (jnp.arange(0, QK_ROPE, 2, dtype=F32) / QK_ROPE))
    ang = pos.astype(F32)[:, None] * inv[None, :]
    cos, sin = jnp.cos(ang), jnp.sin(ang)
    n = pos.shape[0]
    half = QK_ROPE // 2
    z = lambda k: jnp.zeros((n, k), F32)
    pad_hi = LANE - QK_NOPE - QK_ROPE
    rc = jnp.concatenate([jnp.ones((n, QK_NOPE), F32), cos, cos, z(pad_hi)], axis=1)
    rs1 = jnp.concatenate([z(QK_NOPE), -sin, z(half), z(pad_hi)], axis=1)
    rs2 = jnp.concatenate([z(QK_NOPE), z(half), sin, z(pad_hi)], axis=1)
    return (rc, rs1, rs2), cos.T, sin.T


def _params(sem):
    return pltpu.CompilerParams(dimension_semantics=sem, vmem_limit_bytes=VMEM_LIMIT)


def _mla_pre(x, w, rope, tm):
    b, s, d = x.shape
    hw = N_HEADS * LANE
    tok = lambda width: pl.BlockSpec((1, tm, width), lambda i, j: (i, j, 0))
    rope_spec = pl.BlockSpec((tm, LANE), lambda i, j: (j, 0))
    consts = [w["gmix"], w["wa"], w["gcq"], w["gckv"], w["gkr"], w["wuq"], w["wuk"], w["wuv"], w["gq"], w["gk"]]
    return pl.pallas_call(
        _pre_kernel,
        grid=(b, s // tm),
        in_specs=[tok(d)] + [_const(c.shape) for c in consts] + [rope_spec] * 3,
        out_specs=[tok(hw), tok(hw), tok(hw), tok(KV_LORA), tok(QK_ROPE)],
        out_shape=[jax.ShapeDtypeStruct((b, s, hw), BF16)] * 3
        + [jax.ShapeDtypeStruct((b, s, KV_LORA), F32), jax.ShapeDtypeStruct((b, s, QK_ROPE), F32)],
        compiler_params=_params(("parallel", "parallel")),
        name="mla_pre",
    )(x, *consts, *rope)


def _mla_pre_t(x, w, rope, cos_t, sin_t, tm):
    b, s, d = x.shape
    hw = N_HEADS * LANE
    tok = lambda width: pl.BlockSpec((1, tm, width), lambda i, j: (i, j, 0))
    tiled = pl.BlockSpec((1, 1, hw, tm), lambda i, j: (i, j, 0, 0))
    rope_spec = pl.BlockSpec((tm, LANE), lambda i, j: (j, 0))
    rope_t_spec = pl.BlockSpec((QK_ROPE // 2, tm), lambda i, j: (0, j))
    gqt = jnp.broadcast_to(w["gq"].T, (LANE, tm))
    consts = [w["gmix"], w["wa"], w["gcq"], w["gckv"], w["gkr"], w["wuq"].T, w["wuk"], w["wuv"].T, gqt, w["gk"]]
    return pl.pallas_call(
        _pre_kernel_t,
        grid=(b, s // tm),
        in_specs=[tok(d)] + [_const(c.shape) for c in consts] + [rope_spec] * 3 + [rope_t_spec] * 2,
        out_specs=[tiled, tok(hw), tiled, tok(KV_LORA), tok(QK_ROPE)],
        out_shape=[jax.ShapeDtypeStruct((b, s // tm, hw, tm), BF16), jax.ShapeDtypeStruct((b, s, hw), BF16),
                   jax.ShapeDtypeStruct((b, s // tm, hw, tm), BF16),
                   jax.ShapeDtypeStruct((b, s, KV_LORA), F32), jax.ShapeDtypeStruct((b, s, QK_ROPE), F32)],
        compiler_params=_params(("parallel", "parallel")),
        name="mla_pre_t",
    )(x, *consts, *rope, cos_t, sin_t)


def _kv_past(past_lat, past_kpe, w, tm):
    b, t, _ = past_lat.shape
    hw = N_HEADS * LANE
    kpe_slab = jnp.pad(past_kpe, ((0, 0), (0, 0), (QK_NOPE, LANE - QK_NOPE - QK_ROPE)))
    tok = lambda width: pl.BlockSpec((1, tm, width), lambda i, j: (i, j, 0))
    consts = [w["wuk"], w["wuv"], w["gk"]]
    return pl.pallas_call(
        _kvpast_kernel,
        grid=(b, t // tm),
        in_specs=[tok(KV_LORA), tok(LANE)] + [_const(c.shape) for c in consts],
        out_specs=[tok(hw), tok(hw)],
        out_shape=[jax.ShapeDtypeStruct((b, t, hw), BF16)] * 2,
        compiler_params=_params(("parallel", "parallel")),
        name="kv_past",
    )(past_lat, kpe_slab, *consts)


def _attn_prompt(qt, k, vt):
    b, n_tiles, _, tq = qt.shape
    s = n_tiles * tq
    pair = 2 * LANE
    return pl.pallas_call(
        functools.partial(_attn_prompt_kernel, tq=tq),
        grid=(b, N_HEADS // 2, n_tiles),
        in_specs=[pl.BlockSpec((1, 1, pair, tq), lambda i, h, j: (i, j, h, 0)),
                  pl.BlockSpec((1, s, pair), lambda i, h, j: (i, 0, h)),
                  pl.BlockSpec((1, n_tiles, pair, tq), lambda i, h, j: (i, 0, h, 0))],
        out_specs=pl.BlockSpec((1, tq, LANE), lambda i, h, j: (i, j, h)),
        out_shape=jax.ShapeDtypeStruct((b, s, N_HEADS * V_DIM), BF16),
        scratch_shapes=[pltpu.VMEM((tq, tq), F32)] * 4 + [pltpu.VMEM((tq, tq), BF16)] * 4,
        compiler_params=_params(("parallel", "parallel", "arbitrary")),
        name="attn_prompt",
    )(qt, k, vt)


def _attn_sample(q, kp, vp, kn, vn):
    b, s, _ = q.shape
    t = kp.shape[1]
    hw = N_HEADS * LANE
    blk = lambda rows: pl.BlockSpec((1, rows, hw), lambda i: (i, 0, 0))
    return pl.pallas_call(
        _attn_sample_kernel,
        grid=(b,),
        in_specs=[blk(s), blk(t), blk(t), blk(s), blk(s)],
        out_specs=pl.BlockSpec((1, s, N_HEADS * V_DIM), lambda i: (i, 0, 0)),
        out_shape=jax.ShapeDtypeStruct((b, s, N_HEADS * V_DIM), BF16),
        compiler_params=_params(("parallel",)),
        name="attn_sample",
    )(q, kp, vp, kn, vn)


def _post(x, attn, conv_init, cnt0, x1_all, row_off, n_all, w, tm):
    b, s, d = x.shape
    nchunk = d // LANE
    n_keys = cnt0.shape[0]
    tiles_per_b = s // tm
    off = row_off // tm
    tok = lambda width: pl.BlockSpec((1, tm, width), lambda i, j: (i, j, 0))
    per_b = lambda rows: pl.BlockSpec((1, rows, CONV_DIM), lambda i, j: (i, 0, 0))
    tri = (jnp.arange(tm)[:, None] < jnp.arange(tm)[None, :]).astype(BF16)
    consts = [cnt0, w["gmix"], w["wb"], w["convw"], w["woa"], w["woc"], w["wo"], w["gffn"], w["wrt"], w["brt"], tri]
    in_specs = ([tok(d), tok(N_HEADS * V_DIM), per_b(8)] + [_const(c.shape) for c in consts]
                + [pl.BlockSpec(memory_space=pl.ANY)])
    args = [x, attn, conv_init] + consts + [x1_all]
    return pl.pallas_call(
        _post_kernel,
        grid=(b, tiles_per_b),
        in_specs=in_specs,
        out_specs=[pl.BlockSpec((tm * nchunk, LANE), lambda i, j: (off + i * tiles_per_b + j, 0)),
                   pl.BlockSpec((1, 8, tm), lambda i, j: (i * tiles_per_b + j, 0, 0)),
                   per_b(2), _const((n_keys, 1))],
        out_shape=[jax.ShapeDtypeStruct((n_all * nchunk, LANE), F32),
                   jax.ShapeDtypeStruct((b * tiles_per_b, 8, tm), F32),
                   jax.ShapeDtypeStruct((b, 2, CONV_DIM), F32), jax.ShapeDtypeStruct((n_keys, 1), F32)],
        scratch_shapes=[pltpu.VMEM((8, CONV_DIM), F32)],
        input_output_aliases={len(args) - 1: 0},
        compiler_params=_params(("arbitrary", "arbitrary")),
        name="post",
    )(*args)


def _route_tables(key, rank, counts, n):
    n_keys = counts.shape[0]
    padded = (counts + MOE_ROWS - 1) // MOE_ROWS * MOE_ROWS
    pend = jnp.cumsum(padded)
    pstart = pend - padded
    ids = jnp.arange(n_keys, dtype=jnp.int32)
    dest = rank + jnp.sum(jnp.where(key[:, None] == ids[None, :], pstart[None, :], 0), axis=1)
    nblk = n // MOE_ROWS + N_PAIR_BUCKETS + 1
    blk_start = jnp.arange(nblk, dtype=jnp.int32) * MOE_ROWS
    blk_hot = (blk_start[:, None] >= pstart[None, :]) & (blk_start[:, None] < pend[None, :])
    blk_key = jnp.sum(jnp.where(blk_hot, ids[None, :], 0), axis=1)
    blk_cnt = jnp.sum(jnp.where(blk_hot, jnp.minimum(counts[None, :] - (blk_start[:, None] - pstart[None, :]),
                                                      MOE_ROWS), 0), axis=1)
    any_hot = jnp.any(blk_hot, axis=1)
    blk_key = jnp.where(any_hot, blk_key, n_keys - 1)
    blk_lo = blk_key // EXPERTS_PER_GROUP
    blk_hi = blk_lo // EXPERTS_PER_GROUP * EXPERTS_PER_GROUP + blk_key % EXPERTS_PER_GROUP
    return blk_lo, blk_hi, blk_cnt.astype(jnp.int32), dest.astype(jnp.int32)


def _moe(x1_all, key, rank, counts, w):
    rows, _ = x1_all.shape
    d = w["gffn"].shape[1]
    nchunk = d // LANE
    n = rows // nchunk - TRASH_ROWS
    blk_lo, blk_hi, blk_cnt, dest = _route_tables(key, rank, counts, n)
    nblk = blk_lo.shape[0]
    w13_spec = lambda ref_idx: pl.BlockSpec((1, d, 2 * EXPERT_HIDDEN),
                                            lambda i, lo, hi, cnt, dst: ((lo, hi)[ref_idx][i], 0, 0))
    w2_spec = lambda ref_idx: pl.BlockSpec((1, EXPERT_HIDDEN, d),
                                           lambda i, lo, hi, cnt, dst: ((lo, hi)[ref_idx][i], 0, 0))
    buf = pltpu.VMEM((2, MOE_ROWS * nchunk, LANE), F32)
    grid_spec = pltpu.PrefetchScalarGridSpec(
        num_scalar_prefetch=4,
        grid=(nblk,),
        in_specs=[pl.BlockSpec(memory_space=pl.ANY),
                  pl.BlockSpec((1, d), lambda i, *_: (0, 0)),
                  pl.BlockSpec(w["wr"].shape, lambda i, *_: (0, 0)),
                  pl.BlockSpec(w["br"].shape, lambda i, *_: (0, 0)),
                  w13_spec(0), w13_spec(1), w2_spec(0), w2_spec(1)],
        out_specs=pl.BlockSpec(memory_space=pl.ANY),
        scratch_shapes=[pltpu.SMEM((nblk * MOE_ROWS,), jnp.int32), buf, buf,
                        pltpu.SemaphoreType.DMA((2,)), pltpu.SemaphoreType.DMA((2,))],
    )
    return pl.pallas_call(
        _moe_kernel,
        grid_spec=grid_spec,
        out_shape=jax.ShapeDtypeStruct((rows, LANE), F32),
        compiler_params=_params(("arbitrary",)),
        name="moe",
    )(blk_lo, blk_hi, blk_cnt, dest, x1_all, w["gffn"], w["wr"], w["br"], w["w13"], w["w13"], w["w2"], w["w2"])


def _ple(x2_all, row_off, p, w, tm):
    n, pd = p.shape
    d = w["gple"].shape[1]
    nchunk = d // LANE
    off = row_off // tm
    consts = [w["gple"], w["wpg"], w["wple"]]
    return pl.pallas_call(
        _ple_kernel,
        grid=(n // tm,),
        in_specs=[pl.BlockSpec((tm * nchunk, LANE), lambda i: (off + i, 0)), pl.BlockSpec((tm, pd), lambda i: (i, 0))]
        + [_const(c.shape) for c in consts],
        out_specs=pl.BlockSpec((tm, d), lambda i: (i, 0)),
        out_shape=jax.ShapeDtypeStruct((n, d), F32),
        compiler_params=_params(("parallel",)),
        name="ple",
    )(x2_all, p, *consts)


def _mixer(x, past_lat, past_kpe, past_conv, w):
    b, s, _ = x.shape
    tm = min(512, s)
    past_len = 0 if past_lat is None else past_lat.shape[1]
    rope, cos_t, sin_t = _rope_slabs(past_len + jnp.arange(s))
    if past_lat is None:
        qt, k, vt, lat_new, kpe_new = _mla_pre_t(x, w, rope, cos_t, sin_t, tm)
        attn = _attn_prompt(qt, k, vt)
        conv_init = jnp.zeros((b, 8, CONV_DIM), F32)
    else:
        q, k, v, lat_new, kpe_new = _mla_pre(x, w, rope, tm)
        kp, vp = _kv_past(past_lat, past_kpe, w, min(512, past_len))
        attn = _attn_sample(q, kp, vp, k, v)
        conv_init = jnp.pad(past_conv, ((0, 0), (8 - past_conv.shape[1], 0), (0, 0)))
    return attn, conv_init, lat_new, kpe_new, tm


def _layer(xp, xs, pp, ps, past_lat, past_kpe, past_conv, w):
    bp, sp, d = xp.shape
    bs, ss, _ = xs.shape
    n_p, n_s = bp * sp, bs * ss
    n_all = n_p + n_s
    n_keys = N_EXPERTS * EXPERTS_PER_GROUP
    attn_p, cinit_p, lat_p, kpe_p, tm_p = _mixer(xp, None, None, None, w)
    attn_s, cinit_s, lat_s, kpe_s, tm_s = _mixer(xs, past_lat, past_kpe, past_conv, w)
    n_rows = n_all + TRASH_ROWS
    x1_all = jnp.zeros((n_rows * (d // LANE), LANE), F32)
    x1_all, route_p, conv_p, cnt = _post(xp, attn_p, cinit_p, jnp.zeros((n_keys, 1), F32), x1_all, 0, n_rows, w, tm_p)
    x1_all, route_s, conv_s, cnt = _post(xs, attn_s, cinit_s, cnt, x1_all, n_p, n_rows, w, tm_s)
    key = jnp.concatenate([route_p[:, 0].reshape(-1), route_s[:, 0].reshape(-1)]).astype(jnp.int32)
    rank = jnp.concatenate([route_p[:, 1].reshape(-1), route_s[:, 1].reshape(-1)]).astype(jnp.int32)
    x2_all = _moe(x1_all, key, rank, cnt[:, 0].astype(jnp.int32), w)
    yp = _ple(x2_all, 0, pp.reshape(n_p, -1), w, min(512, n_p)).reshape(bp, sp, d)
    ys = _ple(x2_all, n_p, ps.reshape(n_s, -1), w, min(512, n_s)).reshape(bs, ss, d)
    return yp, ys, (lat_p, kpe_p, conv_p, lat_s, kpe_s, conv_s)


def kernel(x_prompt, x_sample, cache_kv_latent, cache_k_rope, state_conv, p_prompt, p_sample,
           g_mix, w_in, g_cq, w_uq, g_qn, g_qr, g_ckv, w_ukv, g_kn, g_kr, w_oa,
           conv_w, w_oc, w_o, g_ffn, w_rg, b_rg, w_re, b_re, w1, w3, w2, g_ple, w_pg, w_ple):
    depth = g_mix.shape[0]
    xp, xs = x_prompt, x_sample
    outs = [[] for _ in range(6)]
    for i in range(depth):
        w = _prep_weights(g_mix[i], w_in[i], g_cq[i], w_uq[i], g_qn[i], g_qr[i], g_ckv[i], w_ukv[i],
                          g_kn[i], g_kr[i], w_oa[i], conv_w[i], w_oc[i], w_o[i], g_ffn[i], w_rg[i], b_rg[i],
                          w_re[i], b_re[i], w1[i], w3[i], w2[i], g_ple[i], w_pg[i], w_ple[i])
        xp, xs, new = _layer(xp, xs, p_prompt[i], p_sample[i], cache_kv_latent[i], cache_k_rope[i], state_conv[i], w)
        for o, a in zip(outs, new):
            o.append(a)
    return (xp, xs) + tuple(jnp.stack(o, axis=0) for o in outs)
```

```python
import functools
import math

import jax
import jax.numpy as jnp
from jax import lax
from jax.experimental import pallas as pl
from jax.experimental.pallas import tpu as pltpu

F32 = jnp.float32
BF16 = jnp.bfloat16

LANE = 128
CHUNK = 64
N_HEADS = 8
QK_NOPE = 64
QK_ROPE = 32
V_DIM = 64
Q_LORA = 256
KV_LORA = 256
CONV_DIM = 512
N_GROUPS = 4
EXPERTS_PER_GROUP = 8
N_EXPERTS = N_GROUPS * EXPERTS_PER_GROUP
EXPERT_HIDDEN = 256
ROPE_THETA = 10000.0
EPS = 1e-6
ATTN_SCALE = (QK_NOPE + QK_ROPE) ** -0.5
LOG2E = math.log2(math.e)
NEG_BIG = -1e30
MOE_ROWS = 128
N_PAIR_BUCKETS = N_GROUPS * (EXPERTS_PER_GROUP * (EXPERTS_PER_GROUP - 1) // 2)
VMEM_LIMIT = 56 * 1024 * 1024


def _rms_scale(x, n):
    return lax.rsqrt(jnp.sum(x * x, axis=-1, keepdims=True) * (1.0 / n) + EPS)


def _lane_iota(shape):
    return lax.broadcasted_iota(jnp.int32, shape, len(shape) - 1)


def _rope(t, rc, rs1, rs2):
    return t * rc + pltpu.roll(t, LANE - QK_ROPE // 2, 1) * rs1 + pltpu.roll(t, QK_ROPE // 2, 1) * rs2


def _pre_latents(x_ref, gmix_ref, wa_ref, gcq_ref, gckv_ref, gkr_ref, rope_refs, lat_ref, kpe_ref):
    x = x_ref[0]
    h = x * _rms_scale(x, x.shape[-1]) * gmix_ref[...]
    z = jnp.dot(h.astype(BF16), wa_ref[...], preferred_element_type=F32)
    cq = z[:, :Q_LORA]
    ckv = z[:, Q_LORA:Q_LORA + KV_LORA]
    kr = z[:, Q_LORA + KV_LORA:]
    cqn = cq * _rms_scale(cq, Q_LORA) * gcq_ref[...]
    lat = ckv * _rms_scale(ckv, KV_LORA) * gckv_ref[...]
    lat_ref[0] = lat
    krn = kr * _rms_scale(kr, QK_ROPE) * gkr_ref[...]
    kpe = _rope(krn, *(r[...] for r in rope_refs))
    kpe_ref[0] = kpe[:, QK_NOPE:QK_NOPE + QK_ROPE]
    return cqn, lat, kpe


def _store_keys(k_ref, kf, gk, kpe):
    for hd in range(N_HEADS):
        sl = slice(hd * LANE, (hd + 1) * LANE)
        ks = kf[:, sl]
        k_ref[0, :, sl] = (ks * _rms_scale(ks, QK_NOPE) * gk + kpe).astype(BF16)


def _pre_kernel(x_ref, gmix_ref, wa_ref, gcq_ref, gckv_ref, gkr_ref, wuq_ref, wuk_ref, wuv_ref,
                gq_ref, gk_ref, rc_ref, rs1_ref, rs2_ref,
                q_ref, k_ref, v_ref, lat_ref, kpe_ref):
    rope_refs = (rc_ref, rs1_ref, rs2_ref)
    cqn, lat, kpe = _pre_latents(x_ref, gmix_ref, wa_ref, gcq_ref, gckv_ref, gkr_ref, rope_refs, lat_ref, kpe_ref)
    rc, rs1, rs2 = (r[...] for r in rope_refs)
    is_nope = _lane_iota(kpe.shape) < QK_NOPE
    latb = lat.astype(BF16)
    qf = jnp.dot(cqn.astype(BF16), wuq_ref[...], preferred_element_type=F32)
    vf = jnp.dot(latb, wuv_ref[...], preferred_element_type=F32)
    _store_keys(k_ref, jnp.dot(latb, wuk_ref[...], preferred_element_type=F32), gk_ref[...], kpe)
    gq = gq_ref[...]
    ones_hi = jnp.where(is_nope, 0.0, 1.0)
    for hd in range(N_HEADS):
        sl = slice(hd * LANE, (hd + 1) * LANE)
        qs = qf[:, sl]
        sq = qs * qs
        ss_all = jnp.sum(sq, axis=-1, keepdims=True)
        ss_n = jnp.sum(jnp.where(is_nope, sq, 0.0), axis=-1, keepdims=True)
        r = jnp.where(is_nope, lax.rsqrt(ss_n * (1.0 / QK_NOPE) + EPS),
                      lax.rsqrt((ss_all - ss_n) * (1.0 / QK_ROPE) + EPS))
        q_ref[0, :, sl] = _rope(qs * r * gq, rc, rs1, rs2).astype(BF16)
        v_ref[0, :, sl] = (vf[:, sl] + ones_hi).astype(BF16)


def _pre_kernel_t(x_ref, gmix_ref, wa_ref, gcq_ref, gckv_ref, gkr_ref, wuqt_ref, wuk_ref, wuvt_ref,
                  gqt_ref, gk_ref, rc_ref, rs1_ref, rs2_ref, cos_ref, sin_ref,
                  qt_ref, k_ref, vt_ref, lat_ref, kpe_ref):
    rope_refs = (rc_ref, rs1_ref, rs2_ref)
    cqn, lat, kpe = _pre_latents(x_ref, gmix_ref, wa_ref, gcq_ref, gckv_ref, gkr_ref, rope_refs, lat_ref, kpe_ref)
    latb = lat.astype(BF16)
    _store_keys(k_ref, jnp.dot(latb, wuk_ref[...], preferred_element_type=F32), gk_ref[...], kpe)
    tm = cqn.shape[0]
    qft = jnp.dot(wuqt_ref[...], cqn.T.astype(BF16), preferred_element_type=F32)
    vft = jnp.dot(wuvt_ref[...], lat.T.astype(BF16), preferred_element_type=F32)
    gq, cos, sin = gqt_ref[...], cos_ref[...], sin_ref[...]
    half = QK_ROPE // 2
    ones_lo = jnp.where(lax.broadcasted_iota(jnp.int32, (LANE, tm), 0) < V_DIM, 0.0, 1.0)
    pad = jnp.zeros((LANE - QK_NOPE - QK_ROPE, tm), F32)
    for hd in range(N_HEADS):
        rows = slice(hd * LANE, (hd + 1) * LANE)
        qs = qft[rows]
        sq = qs * qs
        r_n = lax.rsqrt(jnp.sum(sq[:QK_NOPE], axis=0, keepdims=True) * (1.0 / QK_NOPE) + EPS)
        r_p = lax.rsqrt(jnp.sum(sq[QK_NOPE:QK_NOPE + QK_ROPE], axis=0, keepdims=True) * (1.0 / QK_ROPE) + EPS)
        nope = qs[:QK_NOPE] * r_n * gq[:QK_NOPE]
        x1 = qs[QK_NOPE:QK_NOPE + half] * r_p * gq[QK_NOPE:QK_NOPE + half]
        x2 = qs[QK_NOPE + half:QK_NOPE + QK_ROPE] * r_p * gq[QK_NOPE + half:QK_NOPE + QK_ROPE]
        slab = jnp.concatenate([nope, x1 * cos - x2 * sin, x1 * sin + x2 * cos, pad], axis=0)
        qt_ref[0, 0, rows, :] = slab.astype(BF16)
        vt_ref[0, 0, rows, :] = (vft[rows] + ones_lo).astype(BF16)


def _kvpast_kernel(lat_ref, kpe_ref, wuk_ref, wuv_ref, gk_ref, k_ref, v_ref):
    latb = lat_ref[0].astype(BF16)
    kf = jnp.dot(latb, wuk_ref[...], preferred_element_type=F32)
    vf = jnp.dot(latb, wuv_ref[...], preferred_element_type=F32)
    kpe = kpe_ref[0]
    lane = _lane_iota(kpe.shape)
    ones_hi = jnp.where(lane < QK_NOPE, 0.0, 1.0)
    gk = gk_ref[...]
    for hd in range(N_HEADS):
        sl = slice(hd * LANE, (hd + 1) * LANE)
        ks = kf[:, sl]
        k_ref[0, :, sl] = (ks * _rms_scale(ks, QK_NOPE) * gk + kpe).astype(BF16)
        v_ref[0, :, sl] = (vf[:, sl] + ones_hi).astype(BF16)


def _finish_pair(accs):
    outs = [a / pltpu.roll(a, V_DIM, 1) for a in accs]
    lane = _lane_iota(outs[0].shape)
    return jnp.where(lane < V_DIM, outs[0], pltpu.roll(outs[1], V_DIM, 1))


def _attn_prompt_kernel(qt_ref, k_ref, vt_ref, o_ref, s00, s01, s10, s11, p00, p01, p10, p11, *, tq):
    s_refs = ((s00, s01), (s10, s11))
    p_refs = ((p00, p01), (p10, p11))
    qi = pl.program_id(2)
    key_pos = lax.broadcasted_iota(jnp.int32, (tq, tq), 0)
    query_pos = lax.broadcasted_iota(jnp.int32, (tq, tq), 1)
    diag_mask = (key_pos // CHUNK) <= (query_pos // CHUNK)
    heads = [slice(hh * LANE, (hh + 1) * LANE) for hh in range(2)]

    def scores(i, slot, hh, mask):
        start = pl.multiple_of(i * tq, tq)
        s = jnp.dot(k_ref[0, pl.ds(start, tq), heads[hh]], qt_ref[0, 0, heads[hh], :], preferred_element_type=F32)
        if mask is not None:
            s = jnp.where(mask, s, NEG_BIG)
        s_refs[slot][hh][...] = s
        return jnp.max(s, axis=0, keepdims=True)

    def softmax(slot, hh, m, tile_max):
        m_new = jnp.maximum(m, tile_max)
        p_refs[slot][hh][...] = jnp.exp2(s_refs[slot][hh][...] - m_new).astype(BF16)
        return m_new, jnp.exp2(m - m_new)

    def accumulate(i, slot, hh, alpha, acc):
        pv = jnp.dot(vt_ref[0, jnp.maximum(i, 0), heads[hh], :], p_refs[slot][hh][...],
                     preferred_element_type=F32)
        return acc * alpha + pv

    def iteration(i, slot, carry, next_mask=None):
        stats = [softmax(slot, hh, m, tile_max) for hh, (m, tile_max, _, _) in enumerate(carry)]
        next_max = [scores(i + 1, 1 - slot, hh, next_mask) for hh in range(2)]
        accs = [accumulate(i - 1, 1 - slot, hh, alpha, acc) for hh, (_, _, alpha, acc) in enumerate(carry)]
        return tuple((m, tmax, alpha, acc) for (m, alpha), tmax, acc in zip(stats, next_max, accs))

    def last(slot, carry):
        accs = []
        for hh, (m, tile_max, alpha, acc) in enumerate(carry):
            acc = accumulate(qi - 1, 1 - slot, hh, alpha, acc)
            m, alpha = softmax(slot, hh, m, tile_max)
            accs.append(accumulate(qi, slot, hh, alpha, acc))
        return tuple(accs)

    first_mask = diag_mask | (qi > 0)
    init = []
    for hh in range(2):
        p_refs[1][hh][...] = jnp.zeros((tq, tq), BF16)
        init.append((jnp.full((1, tq), NEG_BIG, F32), scores(0, 0, hh, first_mask), jnp.ones((1, tq), F32),
                     jnp.zeros((LANE, tq), F32)))
    carry = lax.fori_loop(0, (qi - 1) // 2, lambda j, c: iteration(2 * j + 1, 1, iteration(2 * j, 0, c)),
                          tuple(init))
    tails = [lambda c: last(0, c),
             lambda c: last(1, iteration(qi - 1, 0, c, diag_mask)),
             lambda c: last(0, iteration(qi - 1, 1, iteration(qi - 2, 0, c), diag_mask))]
    accs = lax.switch(jnp.where(qi == 0, 0, 2 - qi % 2), tails, carry)
    out_t = jnp.concatenate([a[:V_DIM] / a[V_DIM:V_DIM + 1] for a in accs], axis=0)
    o_ref[0] = out_t.T.astype(o_ref.dtype)


def _attn_sample_kernel(q_ref, kp_ref, vp_ref, kn_ref, vn_ref, o_ref):
    nt = (((1,), (1,)), ((), ()))
    slabs = [slice(hd * LANE, (hd + 1) * LANE) for hd in range(N_HEADS)]
    s_past = [lax.dot_general(q_ref[0, :, sl], kp_ref[0, :, sl], nt, preferred_element_type=F32) for sl in slabs]
    s_new = [lax.dot_general(q_ref[0, :, sl], kn_ref[0, :, sl], nt, preferred_element_type=F32) for sl in slabs]
    accs = []
    for sl, sp, sn in zip(slabs, s_past, s_new):
        m = jnp.maximum(jnp.max(sp, axis=-1, keepdims=True), jnp.max(sn, axis=-1, keepdims=True))
        accs.append(jnp.dot(jnp.exp2(sp - m).astype(BF16), vp_ref[0, :, sl], preferred_element_type=F32)
                    + jnp.dot(jnp.exp2(sn - m).astype(BF16), vn_ref[0, :, sl], preferred_element_type=F32))
    for pair in range(N_HEADS // 2):
        o_ref[0, :, pair * LANE:(pair + 1) * LANE] = _finish_pair(accs[2 * pair:2 * pair + 2]).astype(o_ref.dtype)


def _post_kernel(x_ref, attn_ref, cinit_ref, cnt0_ref, gmix_ref, wb_ref, convw_ref, woa_ref, woc_ref, wo_ref,
                 gffn_ref, wrt_ref, brt_ref, tri_ref, x1_all_ref,
                 x1_ref, route_ref, cnew_ref, cnt_ref, carry_ref, *, seq):
    del x1_all_ref
    si = pl.program_id(1)
    tm = x_ref.shape[1]

    if seq is None:
        @pl.when(si == 0)
        def _():
            carry_ref[...] = cinit_ref[0]

    @pl.when((si == 0) & (pl.program_id(0) == 0))
    def _():
        cnt_ref[...] = cnt0_ref[...]

    x = x_ref[0]
    h = x * _rms_scale(x, x.shape[-1]) * gmix_ref[...]
    z = jnp.dot(h.astype(BF16), wb_ref[...], preferred_element_type=F32)
    conv_b = z[:, :CONV_DIM]
    u = z[:, CONV_DIM:2 * CONV_DIM] * z[:, 2 * CONV_DIM:3 * CONV_DIM]
    d = x.shape[-1]
    gate_a = z[:, 3 * CONV_DIM:3 * CONV_DIM + d]
    gate_c = z[:, 3 * CONV_DIM + d:]

    row = lax.broadcasted_iota(jnp.int32, u.shape, 0)
    if seq is None:
        carry = carry_ref[...]
        c1 = carry[7:8, :]
        c2 = carry[6:7, :]
        u_m1 = jnp.where(row == 0, c1, pltpu.roll(u, 1, 0))
        u_m2 = jnp.where(row == 0, c2, jnp.where(row == 1, c1, pltpu.roll(u, 2, 0)))
        carry_ref[...] = u[tm - 8:, :]
        cnew_ref[0] = u[tm - 2:, :]
    else:
        pos = row % seq
        u_m1 = jnp.where(pos == 0, cinit_ref[0, :tm, :], pltpu.roll(u, 1, 0))
        u_m2 = jnp.where(pos <= 1, cinit_ref[0, tm:, :], pltpu.roll(u, 2, 0))
        cnew_ref[0] = u.reshape(tm // seq, seq, u.shape[-1])[:, seq - 2:, :]
    cw = convw_ref[...]
    cv = cw[0:1, :] * u_m2 + cw[1:2, :] * u_m1 + cw[2:3, :] * u

    y_a = jnp.dot(attn_ref[0], woa_ref[...], preferred_element_type=F32)
    y_c = jnp.dot((conv_b * cv).astype(BF16), woc_ref[...], preferred_element_type=F32)
    mrg = jax.nn.sigmoid(gate_a) * y_a + jax.nn.sigmoid(gate_c) * y_c
    x1 = x + jnp.dot(mrg.astype(BF16), wo_ref[...], preferred_element_type=F32)
    nchunk = d // LANE
    for c in range(nchunk):
        x1_ref[pl.ds(c, tm, stride=nchunk), :] = x1[:, c * LANE:(c + 1) * LANE]

    h2 = x1 * _rms_scale(x1, d) * gffn_ref[...]
    h2_hi = h2.astype(BF16)
    h2_lo = (h2 - h2_hi.astype(F32)).astype(BF16)
    nt = (((1,), (1,)), ((), ()))
    n_rows = brt_ref.shape[0]
    lt2 = lax.dot_general(wrt_ref[...], h2_hi, nt, preferred_element_type=F32)
    lt = (lt2[:n_rows] + lt2[n_rows:]
          + lax.dot_general(wrt_ref[:n_rows, :], h2_lo, nt, preferred_element_type=F32) + brt_ref[...])
    sub = lax.broadcasted_iota(jnp.int32, (EXPERTS_PER_GROUP, tm), 0).astype(F32)
    none = float(EXPERTS_PER_GROUP)

    def first_argmax(v):
        vmax = jnp.max(v, axis=0, keepdims=True)
        return jnp.min(jnp.where(v == vmax, sub, none), axis=0, keepdims=True)

    g_idx = first_argmax(lt[:EXPERTS_PER_GROUP])
    el = lt[EXPERTS_PER_GROUP * N_GROUPS:]
    for g in range(N_GROUPS - 2, -1, -1):
        el = jnp.where(g_idx == g, lt[EXPERTS_PER_GROUP * (g + 1):EXPERTS_PER_GROUP * (g + 2)], el)
    i1 = first_argmax(el)
    i2 = first_argmax(jnp.where(sub == i1, NEG_BIG, el))
    key = (g_idx * (EXPERTS_PER_GROUP * EXPERTS_PER_GROUP) + jnp.minimum(i1, i2) * EXPERTS_PER_GROUP
           + jnp.maximum(i1, i2))

    n_keys = cnt_ref.shape[0]
    keys = lax.broadcasted_iota(jnp.int32, (n_keys, tm), 0).astype(F32)
    onehot = jnp.where(keys == key, 1.0, 0.0)
    before = jnp.dot(onehot.astype(BF16), tri_ref[...], preferred_element_type=F32)
    rank = jnp.sum(onehot * (before + cnt_ref[...]), axis=0, keepdims=True)
    cnt_ref[...] += jnp.sum(onehot, axis=1, keepdims=True)
    row8 = lax.broadcasted_iota(jnp.int32, (8, tm), 0)
    route_ref[0] = jnp.where(row8 == 0, key, jnp.where(row8 == 1, rank, 0.0))


def _rows_loop(n, fn, unroll=8):
    def group(j, c):
        for u in range(unroll):
            fn(j * unroll + u)
        return c
    lax.fori_loop(0, n // unroll, group, 0)

    def single(i, c):
        fn(i)
        return c
    lax.fori_loop(n // unroll * unroll, n, single, 0)


def _moe_kernel(lo_ref, hi_ref, cnt_ref, dest_ref,
                x1_hbm, gffn_ref, wr_ref, br_ref, w13lo_ref, w13hi_ref, w2lo_ref, w2hi_ref,
                x2_hbm, tok_ref, xbuf, obuf, gsem, ssem):
    nb = pl.program_id(0)
    nblk = pl.num_programs(0)
    nchunk = xbuf.shape[1] // MOE_ROWS
    slot = nb % 2
    cnt = cnt_ref[nb]

    def start_gathers(blk, sl):
        base = blk * MOE_ROWS

        def one(i):
            src = x1_hbm.at[pl.ds(tok_ref[base + i] * nchunk, nchunk)]
            pltpu.make_async_copy(src, xbuf.at[sl, pl.ds(i * nchunk, nchunk)], gsem.at[sl]).start()
        _rows_loop(cnt_ref[blk], one)

    def start_scatters(blk, sl):
        base = blk * MOE_ROWS

        def one(i):
            dst = x2_hbm.at[pl.ds(tok_ref[base + i] * nchunk, nchunk)]
            pltpu.make_async_copy(obuf.at[sl, pl.ds(i * nchunk, nchunk)], dst, ssem.at[sl]).start()
        _rows_loop(cnt_ref[blk], one)

    def wait_gathers(blk, sl):
        rows = cnt_ref[blk] * nchunk
        pltpu.make_async_copy(x1_hbm.at[pl.ds(0, rows)], xbuf.at[sl, pl.ds(0, rows)], gsem.at[sl]).wait()

    def wait_scatters(blk, sl):
        rows = cnt_ref[blk] * nchunk
        pltpu.make_async_copy(obuf.at[sl, pl.ds(0, rows)], x2_hbm.at[pl.ds(0, rows)], ssem.at[sl]).wait()

    @pl.when(nb == 0)
    def _():
        def fill(i, c):
            tok_ref[i] = 0
            return c
        lax.fori_loop(0, tok_ref.shape[0], fill, 0, unroll=8)

        def invert(i, c):
            tok_ref[dest_ref[i]] = i
            return c
        lax.fori_loop(0, dest_ref.shape[0], invert, 0, unroll=8)
        xbuf[...] = jnp.zeros_like(xbuf)

        @pl.when(cnt > 0)
        def _():
            start_gathers(0, 0)

    @pl.when(nb + 1 < nblk)
    def _():
        @pl.when(cnt_ref[nb + 1] > 0)
        def _():
            start_gathers(nb + 1, 1 - slot)

    @pl.when(nb >= 2)
    def _():
        @pl.when(cnt_ref[nb - 2] > 0)
        def _():
            wait_scatters(nb - 2, slot)

    @pl.when(cnt > 0)
    def _():
        wait_gathers(nb, slot)
        xg = jnp.concatenate([xbuf[slot, pl.ds(c, MOE_ROWS, stride=nchunk), :] for c in range(nchunk)], axis=1)
        h = (xg * _rms_scale(xg, xg.shape[-1]) * gffn_ref[...]).astype(BF16)

        lo, hi = lo_ref[nb], hi_ref[nb]
        logit = jnp.dot(h, wr_ref[...], preferred_element_type=F32) + br_ref[...]
        lane = _lane_iota(logit.shape)
        pick = lambda j: jnp.sum(jnp.where(lane == j, logit, 0.0), axis=-1, keepdims=True)
        is_g = lane < N_GROUPS
        gmax = jnp.max(jnp.where(is_g, logit, NEG_BIG), axis=-1, keepdims=True)
        g_den = jnp.sum(jnp.where(is_g, jnp.exp(logit - gmax), 0.0), axis=-1, keepdims=True)
        g_p = jnp.exp(pick(lo // EXPERTS_PER_GROUP) - gmax) / g_den
        l_lo, l_hi = pick(N_GROUPS + lo), pick(N_GROUPS + hi)
        gates = (g_p * jax.nn.sigmoid(l_lo - l_hi), g_p * jax.nn.sigmoid(l_hi - l_lo))

        abs_ = [jnp.dot(h, w13_ref[0], preferred_element_type=F32) for w13_ref in (w13lo_ref, w13hi_ref)]
        y = xg
        for ab, w2_ref, gate in zip(abs_, (w2lo_ref, w2hi_ref), gates):
            hid = jax.nn.silu(ab[:, :EXPERT_HIDDEN]) * ab[:, EXPERT_HIDDEN:]
            y = y + gate * jnp.dot(hid.astype(BF16), w2_ref[0], preferred_element_type=F32)
        for c in range(nchunk):
            obuf[slot, pl.ds(c, MOE_ROWS, stride=nchunk), :] = y[:, c * LANE:(c + 1) * LANE]
        start_scatters(nb, slot)

    @pl.when(nb == nblk - 1)
    def _():
        @pl.when(nb >= 1)
        def _():
            @pl.when(cnt_ref[nb - 1] > 0)
            def _():
                wait_scatters(nb - 1, 1 - slot)

        @pl.when(cnt > 0)
        def _():
            wait_scatters(nb, slot)


def _ple_kernel(x_ref, p_ref, gple_ref, wpg_ref, wple_ref, o_ref):
    tm = o_ref.shape[0]
    nchunk = o_ref.shape[1] // LANE
    emb = jnp.dot(p_ref[...].astype(BF16), wple_ref[...], preferred_element_type=F32)
    x = jnp.concatenate([x_ref[pl.ds(c, tm, stride=nchunk), :] for c in range(nchunk)], axis=1)
    hp = (x * _rms_scale(x, x.shape[-1]) * gple_ref[...]).astype(BF16)
    gate = jax.nn.sigmoid(jnp.dot(hp, wpg_ref[...], preferred_element_type=F32))
    o_ref[...] = x + gate * emb


def _const(shape):
    nd = len(shape)
    return pl.BlockSpec(shape, lambda *_: (0,) * nd)


def _head_slab_cols(w, width, offset=0):
    k = w.shape[0]
    w = w.reshape(k, N_HEADS, width)
    w = jnp.pad(w, ((0, 0), (0, 0), (offset, LANE - width - offset)))
    return w.reshape(k, N_HEADS * LANE)


def _prep_weights(g_mix, w_in, g_cq, w_uq, g_qn, g_qr, g_ckv, w_ukv, g_kn, g_kr, w_oa,
                  conv_w, w_oc, w_o, g_ffn, w_rg, b_rg, w_re, b_re, w1, w3, w2, g_ple, w_pg, w_ple):
    d = w_in.shape[0]
    n_mla = Q_LORA + KV_LORA
    kr_cols = jnp.pad(w_in[:, n_mla:n_mla + QK_ROPE], ((0, 0), (QK_NOPE, LANE - QK_NOPE - QK_ROPE)))
    w = {}
    w["wa"] = jnp.concatenate([w_in[:, :n_mla], kr_cols], axis=1).astype(BF16)
    w["wb"] = w_in[:, n_mla + QK_ROPE:].astype(BF16)
    w["wuq"] = _head_slab_cols(w_uq, QK_NOPE + QK_ROPE).astype(BF16)
    ukv = w_ukv.reshape(KV_LORA, N_HEADS, QK_NOPE + V_DIM)
    w["wuk"] = _head_slab_cols(ukv[:, :, :QK_NOPE].reshape(KV_LORA, -1), QK_NOPE).astype(BF16)
    w["wuv"] = _head_slab_cols(ukv[:, :, QK_NOPE:].reshape(KV_LORA, -1), V_DIM).astype(BF16)
    pad_hi = LANE - QK_NOPE - QK_ROPE
    w["gq"] = (jnp.pad(jnp.concatenate([g_qn, g_qr]), (0, pad_hi)) * (ATTN_SCALE * LOG2E))[None]
    w["gk"] = jnp.pad(g_kn, (0, LANE - QK_NOPE))[None]
    w["gkr"] = jnp.pad(g_kr, (QK_NOPE, pad_hi))[None]
    w["gmix"], w["gcq"], w["gckv"] = g_mix[None], g_cq[None], g_ckv[None]
    w["gffn"], w["gple"] = g_ffn[None], g_ple[None]
    w["convw"] = jnp.pad(conv_w, ((0, 8 - conv_w.shape[0]), (0, 0)))
    w["woa"], w["woc"], w["wo"] = w_oa.astype(BF16), w_oc.astype(BF16), w_o.astype(BF16)
    n_r = N_GROUPS + N_EXPERTS
    w["wr"] = jnp.pad(jnp.concatenate([w_rg, w_re], axis=1), ((0, 0), (0, LANE - n_r))).astype(BF16)
    w["br"] = jnp.pad(jnp.concatenate([b_rg, b_re]), (0, LANE - n_r))[None]
    pad_g = EXPERTS_PER_GROUP - N_GROUPS
    wrt = jnp.concatenate([jnp.pad(w_rg.T, ((0, pad_g), (0, 0))), w_re.T], axis=0)
    wrt_hi = wrt.astype(BF16)
    w["wrt"] = jnp.concatenate([wrt_hi, (wrt - wrt_hi.astype(F32)).astype(BF16)], axis=0)
    w["brt"] = jnp.concatenate([b_rg, jnp.full((pad_g,), NEG_BIG, F32), b_re])[:, None]
    w["w13"] = jnp.concatenate([w1, w3], axis=2).astype(BF16)
    w["w2"] = w2.astype(BF16)
    w["wpg"], w["wple"] = w_pg.astype(BF16), w_ple.astype(BF16)
    return w


def _rope_slabs(pos):
    inv = 1.0 / (ROPE_THETA ** (jnp.arange(0, QK_ROPE, 2, dtype=F32) / QK_ROPE))
    ang = pos.astype(F32)[:, None] * inv[None, :]
    cos, sin = jnp.cos(ang), jnp.sin(ang)
    n = pos.shape[0]
    half = QK_ROPE // 2
    z = lambda k: jnp.zeros((n, k), F32)
    pad_hi = LANE - QK_NOPE - QK_ROPE
    rc = jnp.concatenate([jnp.ones((n, QK_NOPE), F32), cos, cos, z(pad_hi)], axis=1)
    rs1 = jnp.concatenate([z(QK_NOPE), -sin, z(half), z(pad_hi)], axis=1)
    rs2 = jnp.concatenate([z(QK_NOPE), z(half), sin, z(pad_hi)], axis=1)
    return (rc, rs1, rs2), cos.T, sin.T


def _params(sem):
    return pltpu.CompilerParams(dimension_semantics=sem, vmem_limit_bytes=VMEM_LIMIT)


def _mla_pre(x, w, rope, tm):
    b, s, d = x.shape
    hw = N_HEADS * LANE
    tok = lambda width: pl.BlockSpec((1, tm, width), lambda i, j: (i, j, 0))
    rope_spec = pl.BlockSpec((tm, LANE), lambda i, j: (j, 0))
    consts = [w["gmix"], w["wa"], w["gcq"], w["gckv"], w["gkr"], w["wuq"], w["wuk"], w["wuv"], w["gq"], w["gk"]]
    return pl.pallas_call(
        _pre_kernel,
        grid=(b, s // tm),
        in_specs=[tok(d)] + [_const(c.shape) for c in consts] + [rope_spec] * 3,
        out_specs=[tok(hw), tok(hw), tok(hw), tok(KV_LORA), tok(QK_ROPE)],
        out_shape=[jax.ShapeDtypeStruct((b, s, hw), BF16)] * 3
        + [jax.ShapeDtypeStruct((b, s, KV_LORA), F32), jax.ShapeDtypeStruct((b, s, QK_ROPE), F32)],
        compiler_params=_params(("parallel", "parallel")),
        name="mla_pre",
    )(x, *consts, *rope)


def _mla_pre_t(x, w, rope, cos_t, sin_t, tm):
    b, s, d = x.shape
    hw = N_HEADS * LANE
    tok = lambda width: pl.BlockSpec((1, tm, width), lambda i, j: (i, j, 0))
    tiled = pl.BlockSpec((1, 1, hw, tm), lambda i, j: (i, j, 0, 0))
    rope_spec = pl.BlockSpec((tm, LANE), lambda i, j: (j, 0))
    rope_t_spec = pl.BlockSpec((QK_ROPE // 2, tm), lambda i, j: (0, j))
    gqt = jnp.broadcast_to(w["gq"].T, (LANE, tm))
    consts = [w["gmix"], w["wa"], w["gcq"], w["gckv"], w["gkr"], w["wuq"].T, w["wuk"], w["wuv"].T, gqt, w["gk"]]
    return pl.pallas_call(
        _pre_kernel_t,
        grid=(b, s // tm),
        in_specs=[tok(d)] + [_const(c.shape) for c in consts] + [rope_spec] * 3 + [rope_t_spec] * 2,
        out_specs=[tiled, tok(hw), tiled, tok(KV_LORA), tok(QK_ROPE)],
        out_shape=[jax.ShapeDtypeStruct((b, s // tm, hw, tm), BF16), jax.ShapeDtypeStruct((b, s, hw), BF16),
                   jax.ShapeDtypeStruct((b, s // tm, hw, tm), BF16),
                   jax.ShapeDtypeStruct((b, s, KV_LORA), F32), jax.ShapeDtypeStruct((b, s, QK_ROPE), F32)],
        compiler_params=_params(("parallel", "parallel")),
        name="mla_pre_t",
    )(x, *consts, *rope, cos_t, sin_t)


def _kv_past(past_lat, past_kpe, w, tm):
    b, t, _ = past_lat.shape
    hw = N_HEADS * LANE
    kpe_slab = jnp.pad(past_kpe, ((0, 0), (0, 0), (QK_NOPE, LANE - QK_NOPE - QK_ROPE)))
    tok = lambda width: pl.BlockSpec((1, tm, width), lambda i, j: (i, j, 0))
    consts = [w["wuk"], w["wuv"], w["gk"]]
    return pl.pallas_call(
        _kvpast_kernel,
        grid=(b, t // tm),
        in_specs=[tok(KV_LORA), tok(LANE)] + [_const(c.shape) for c in consts],
        out_specs=[tok(hw), tok(hw)],
        out_shape=[jax.ShapeDtypeStruct((b, t, hw), BF16)] * 2,
        compiler_params=_params(("parallel", "parallel")),
        name="kv_past",
    )(past_lat, kpe_slab, *consts)


def _attn_prompt(qt, k, vt):
    b, n_tiles, _, tq = qt.shape
    s = n_tiles * tq
    pair = 2 * LANE
    return pl.pallas_call(
        functools.partial(_attn_prompt_kernel, tq=tq),
        grid=(b, N_HEADS // 2, n_tiles),
        in_specs=[pl.BlockSpec((1, 1, pair, tq), lambda i, h, j: (i, j, h, 0)),
                  pl.BlockSpec((1, s, pair), lambda i, h, j: (i, 0, h)),
                  pl.BlockSpec((1, n_tiles, pair, tq), lambda i, h, j: (i, 0, h, 0))],
        out_specs=pl.BlockSpec((1, tq, LANE), lambda i, h, j: (i, j, h)),
        out_shape=jax.ShapeDtypeStruct((b, s, N_HEADS * V_DIM), BF16),
        scratch_shapes=[pltpu.VMEM((tq, tq), F32)] * 4 + [pltpu.VMEM((tq, tq), BF16)] * 4,
        compiler_params=_params(("parallel", "parallel", "arbitrary")),
        name="attn_prompt",
    )(qt, k, vt)


def _attn_sample(q, kp, vp, kn, vn):
    b, s, _ = q.shape
    t = kp.shape[1]
    hw = N_HEADS * LANE
    blk = lambda rows: pl.BlockSpec((1, rows, hw), lambda i: (i, 0, 0))
    return pl.pallas_call(
        _attn_sample_kernel,
        grid=(b,),
        in_specs=[blk(s), blk(t), blk(t), blk(s), blk(s)],
        out_specs=pl.BlockSpec((1, s, N_HEADS * V_DIM), lambda i: (i, 0, 0)),
        out_shape=jax.ShapeDtypeStruct((b, s, N_HEADS * V_DIM), BF16),
        compiler_params=_params(("parallel",)),
        name="attn_sample",
    )(q, kp, vp, kn, vn)


def _post(x, attn, conv_init, cnt0, x1_all, row_off, n_all, w, tm, seq=None):
    b, s, d = x.shape
    nchunk = d // LANE
    n_keys = cnt0.shape[0]
    tiles_per_b = s // tm
    off = row_off // tm
    tok = lambda width: pl.BlockSpec((1, tm, width), lambda i, j: (i, j, 0))
    if seq is None:
        cin_spec = pl.BlockSpec((1, 8, CONV_DIM), lambda i, j: (i, 0, 0))
        cnew_spec = pl.BlockSpec((1, 2, CONV_DIM), lambda i, j: (i, 0, 0))
        cnew_shape = (b, 2, CONV_DIM)
        conv_rows = 8
    else:
        cin_spec = pl.BlockSpec((1, 2 * tm, CONV_DIM), lambda i, j: (j, 0, 0))
        cnew_spec = pl.BlockSpec((1, tm // seq, 2, CONV_DIM), lambda i, j: (j, 0, 0, 0))
        cnew_shape = (tiles_per_b, tm // seq, 2, CONV_DIM)
        conv_rows = 8
    tri = (jnp.arange(tm)[:, None] < jnp.arange(tm)[None, :]).astype(BF16)
    consts = [cnt0, w["gmix"], w["wb"], w["convw"], w["woa"], w["woc"], w["wo"], w["gffn"], w["wrt"], w["brt"], tri]
    in_specs = ([tok(d), tok(N_HEADS * V_DIM), cin_spec] + [_const(c.shape) for c in consts]
                + [pl.BlockSpec(memory_space=pl.ANY)])
    args = [x, attn, conv_init] + consts + [x1_all]
    return pl.pallas_call(
        functools.partial(_post_kernel, seq=seq),
        grid=(b, tiles_per_b),
        in_specs=in_specs,
        out_specs=[pl.BlockSpec((tm * nchunk, LANE), lambda i, j: (off + i * tiles_per_b + j, 0)),
                   pl.BlockSpec((1, 8, tm), lambda i, j: (i * tiles_per_b + j, 0, 0)),
                   cnew_spec, _const((n_keys, 1))],
        out_shape=[jax.ShapeDtypeStruct((n_all * nchunk, LANE), F32),
                   jax.ShapeDtypeStruct((b * tiles_per_b, 8, tm), F32),
                   jax.ShapeDtypeStruct(cnew_shape, F32), jax.ShapeDtypeStruct((n_keys, 1), F32)],
        scratch_shapes=[pltpu.VMEM((conv_rows, CONV_DIM), F32)],
        input_output_aliases={len(args) - 1: 0},
        compiler_params=_params(("arbitrary", "arbitrary")),
        name="post",
    )(*args)


def _route_tables(key, rank, counts, n):
    n_keys = counts.shape[0]
    padded = (counts + MOE_ROWS - 1) // MOE_ROWS * MOE_ROWS
    pend = jnp.cumsum(padded)
    pstart = pend - padded
    ids = jnp.arange(n_keys, dtype=jnp.int32)
    dest = rank + jnp.sum(jnp.where(key[:, None] == ids[None, :], pstart[None, :], 0), axis=1)
    nblk = n // MOE_ROWS + N_PAIR_BUCKETS
    blk_start = jnp.arange(nblk, dtype=jnp.int32) * MOE_ROWS
    blk_hot = (blk_start[:, None] >= pstart[None, :]) & (blk_start[:, None] < pend[None, :])
    blk_key = jnp.sum(jnp.where(blk_hot, ids[None, :], 0), axis=1)
    blk_cnt = jnp.sum(jnp.where(blk_hot, jnp.minimum(counts[None, :] - (blk_start[:, None] - pstart[None, :]),
                                                      MOE_ROWS), 0), axis=1)
    any_hot = jnp.any(blk_hot, axis=1)
    blk_key = jnp.where(any_hot, blk_key, n_keys - 1)
    blk_lo = blk_key // EXPERTS_PER_GROUP
    blk_hi = blk_lo // EXPERTS_PER_GROUP * EXPERTS_PER_GROUP + blk_key % EXPERTS_PER_GROUP
    return blk_lo, blk_hi, blk_cnt.astype(jnp.int32), dest.astype(jnp.int32)


def _moe(x1_all, key, rank, counts, w):
    rows, _ = x1_all.shape
    d = w["gffn"].shape[1]
    nchunk = d // LANE
    n = rows // nchunk
    blk_lo, blk_hi, blk_cnt, dest = _route_tables(key, rank, counts, n)
    nblk = blk_lo.shape[0]
    w13_spec = lambda ref_idx: pl.BlockSpec((1, d, 2 * EXPERT_HIDDEN),
                                            lambda i, lo, hi, cnt, dst: ((lo, hi)[ref_idx][i], 0, 0))
    w2_spec = lambda ref_idx: pl.BlockSpec((1, EXPERT_HIDDEN, d),
                                           lambda i, lo, hi, cnt, dst: ((lo, hi)[ref_idx][i], 0, 0))
    buf = pltpu.VMEM((2, MOE_ROWS * nchunk, LANE), F32)
    grid_spec = pltpu.PrefetchScalarGridSpec(
        num_scalar_prefetch=4,
        grid=(nblk,),
        in_specs=[pl.BlockSpec(memory_space=pl.ANY),
                  pl.BlockSpec((1, d), lambda i, *_: (0, 0)),
                  pl.BlockSpec(w["wr"].shape, lambda i, *_: (0, 0)),
                  pl.BlockSpec(w["br"].shape, lambda i, *_: (0, 0)),
                  w13_spec(0), w13_spec(1), w2_spec(0), w2_spec(1)],
        out_specs=pl.BlockSpec(memory_space=pl.ANY),
        scratch_shapes=[pltpu.SMEM((nblk * MOE_ROWS,), jnp.int32), buf, buf,
                        pltpu.SemaphoreType.DMA((2,)), pltpu.SemaphoreType.DMA((2,))],
    )
    return pl.pallas_call(
        _moe_kernel,
        grid_spec=grid_spec,
        out_shape=jax.ShapeDtypeStruct((rows, LANE), F32),
        compiler_params=_params(("arbitrary",)),
        name="moe",
    )(blk_lo, blk_hi, blk_cnt, dest, x1_all, w["gffn"], w["wr"], w["br"], w["w13"], w["w13"], w["w2"], w["w2"])


def _ple(x2_all, row_off, p, w, tm):
    n, pd = p.shape
    d = w["gple"].shape[1]
    nchunk = d // LANE
    off = row_off // tm
    consts = [w["gple"], w["wpg"], w["wple"]]
    return pl.pallas_call(
        _ple_kernel,
        grid=(n // tm,),
        in_specs=[pl.BlockSpec((tm * nchunk, LANE), lambda i: (off + i, 0)), pl.BlockSpec((tm, pd), lambda i: (i, 0))]
        + [_const(c.shape) for c in consts],
        out_specs=pl.BlockSpec((tm, d), lambda i: (i, 0)),
        out_shape=jax.ShapeDtypeStruct((n, d), F32),
        compiler_params=_params(("parallel",)),
        name="ple",
    )(x2_all, p, *consts)


def _layer(xp, xs, pp, ps, past_lat, past_kpe, past_conv, w):
    bp, sp, d = xp.shape
    bs, ss, _ = xs.shape
    n_p, n_s = bp * sp, bs * ss
    n_all = n_p + n_s
    n_keys = N_EXPERTS * EXPERTS_PER_GROUP
    past_len = past_lat.shape[1]
    tm_p, tm_s = min(512, sp), min(512, n_s)

    rope, cos_t, sin_t = _rope_slabs(jnp.arange(sp))
    qt, k, vt, lat_p, kpe_p = _mla_pre_t(xp, w, rope, cos_t, sin_t, tm_p)
    attn_p = _attn_prompt(qt, k, vt)
    cinit_p = jnp.zeros((bp, 8, CONV_DIM), F32)

    rope, _, _ = _rope_slabs(past_len + jnp.arange(n_s) % ss)
    xs_rows = xs.reshape(1, n_s, d)
    q, k, v, lat_s, kpe_s = _mla_pre(xs_rows, w, rope, tm_s)
    by_seq = lambda a: a.reshape(bs, ss, a.shape[-1])
    kp, vp = _kv_past(past_lat, past_kpe, w, min(512, past_len))
    attn_s = _attn_sample(by_seq(q), kp, vp, by_seq(k), by_seq(v)).reshape(1, n_s, -1)
    lat_s, kpe_s = by_seq(lat_s), by_seq(kpe_s)
    in_tiles = lambda a: a.reshape(n_s // tm_s, tm_s, CONV_DIM)
    n_hist = past_conv.shape[1]
    cinit_s = jnp.concatenate(
        [in_tiles(jnp.pad(past_conv[:, n_hist - 1:], ((0, 0), (0, ss - 1), (0, 0)))),
         in_tiles(jnp.pad(past_conv, ((0, 0), (0, ss - n_hist), (0, 0))))], axis=1)

    x1_all = jnp.zeros((n_all * (d // LANE), LANE), F32)
    x1_all, route_p, conv_p, cnt = _post(xp, attn_p, cinit_p, jnp.zeros((n_keys, 1), F32), x1_all, 0, n_all, w, tm_p)
    x1_all, route_s, conv_s, cnt = _post(xs_rows, attn_s, cinit_s, cnt, x1_all, n_p, n_all, w, tm_s, seq=ss)
    conv_s = conv_s.reshape(bs, n_hist, CONV_DIM)
    key = jnp.concatenate([route_p[:, 0].reshape(-1), route_s[:, 0].reshape(-1)]).astype(jnp.int32)
    rank = jnp.concatenate([route_p[:, 1].reshape(-1), route_s[:, 1].reshape(-1)]).astype(jnp.int32)
    x2_all = _moe(x1_all, key, rank, cnt[:, 0].astype(jnp.int32), w)
    yp = _ple(x2_all, 0, pp.reshape(n_p, -1), w, min(512, n_p)).reshape(bp, sp, d)
    ys = _ple(x2_all, n_p, ps.reshape(n_s, -1), w, min(512, n_s)).reshape(bs, ss, d)
    return yp, ys, (lat_p, kpe_p, conv_p, lat_s, kpe_s, conv_s)


def kernel(x_prompt, x_sample, cache_kv_latent, cache_k_rope, state_conv, p_prompt, p_sample,
           g_mix, w_in, g_cq, w_uq, g_qn, g_qr, g_ckv, w_ukv, g_kn, g_kr, w_oa,
           conv_w, w_oc, w_o, g_ffn, w_rg, b_rg, w_re, b_re, w1, w3, w2, g_ple, w_pg, w_ple):
    depth = g_mix.shape[0]
    xp, xs = x_prompt, x_sample
    outs = [[] for _ in range(6)]
    for i in range(depth):
        w = _prep_weights(g_mix[i], w_in[i], g_cq[i], w_uq[i], g_qn[i], g_qr[i], g_ckv[i], w_ukv[i],
                          g_kn[i], g_kr[i], w_oa[i], conv_w[i], w_oc[i], w_o[i], g_ffn[i], w_rg[i], b_rg[i],
                          w_re[i], b_re[i], w1[i], w3[i], w2[i], g_ple[i], w_pg[i], w_ple[i])
        xp, xs, new = _layer(xp, xs, p_prompt[i], p_sample[i], cache_kv_latent[i], cache_k_rope[i], state_conv[i], w)
        for o, a in zip(outs, new):
            o.append(a)
    return (xp, xs) + tuple(jnp.stack(o, axis=0) for o in outs)
```

```python
import functools
import math

import jax
import jax.numpy as jnp
from jax import lax
from jax.experimental import pallas as pl
from jax.experimental.pallas import tpu as pltpu

F32 = jnp.float32
BF16 = jnp.bfloat16

LANE = 128
CHUNK = 64
N_HEADS = 8
QK_NOPE = 64
QK_ROPE = 32
V_DIM = 64
Q_LORA = 256
KV_LORA = 256
CONV_DIM = 512
N_GROUPS = 4
EXPERTS_PER_GROUP = 8
N_EXPERTS = N_GROUPS * EXPERTS_PER_GROUP
EXPERT_HIDDEN = 256
ROPE_THETA = 10000.0
EPS = 1e-6
ATTN_SCALE = (QK_NOPE + QK_ROPE) ** -0.5
LOG2E = math.log2(math.e)
NEG_BIG = -1e30
MOE_ROWS = 128
ATTN_HEADS_PER_STEP = 4
N_PAIR_BUCKETS = N_GROUPS * (EXPERTS_PER_GROUP * (EXPERTS_PER_GROUP - 1) // 2)
VMEM_LIMIT = 56 * 1024 * 1024


def _rms_scale(x, n):
    return lax.rsqrt(jnp.sum(x * x, axis=-1, keepdims=True) * (1.0 / n) + EPS)


def _lane_iota(shape):
    return lax.broadcasted_iota(jnp.int32, shape, len(shape) - 1)


def _rope(t, rc, rs1, rs2):
    return t * rc + pltpu.roll(t, LANE - QK_ROPE // 2, 1) * rs1 + pltpu.roll(t, QK_ROPE // 2, 1) * rs2


def _pre_latents(x_ref, gmix_ref, wa_ref, gcq_ref, gckv_ref, gkr_ref, rope_refs, lat_ref, kpe_ref):
    x = x_ref[0]
    h = x * _rms_scale(x, x.shape[-1]) * gmix_ref[...]
    z = jnp.dot(h.astype(BF16), wa_ref[...], preferred_element_type=F32)
    cq = z[:, :Q_LORA]
    ckv = z[:, Q_LORA:Q_LORA + KV_LORA]
    kr = z[:, Q_LORA + KV_LORA:]
    cqn = cq * _rms_scale(cq, Q_LORA) * gcq_ref[...]
    lat = ckv * _rms_scale(ckv, KV_LORA) * gckv_ref[...]
    lat_ref[0] = lat
    krn = kr * _rms_scale(kr, QK_ROPE) * gkr_ref[...]
    kpe = _rope(krn, *(r[...] for r in rope_refs))
    kpe_ref[0] = kpe[:, QK_NOPE:QK_NOPE + QK_ROPE]
    return cqn, lat, kpe


def _store_keys(k_ref, kf, gk, kpe):
    for hd in range(N_HEADS):
        sl = slice(hd * LANE, (hd + 1) * LANE)
        ks = kf[:, sl]
        k_ref[0, :, sl] = (ks * _rms_scale(ks, QK_NOPE) * gk + kpe).astype(BF16)


def _pre_kernel(x_ref, gmix_ref, wa_ref, gcq_ref, gckv_ref, gkr_ref, wuq_ref, wuk_ref, wuv_ref,
                gq_ref, gk_ref, rc_ref, rs1_ref, rs2_ref,
                q_ref, k_ref, v_ref, lat_ref, kpe_ref):
    rope_refs = (rc_ref, rs1_ref, rs2_ref)
    cqn, lat, kpe = _pre_latents(x_ref, gmix_ref, wa_ref, gcq_ref, gckv_ref, gkr_ref, rope_refs, lat_ref, kpe_ref)
    rc, rs1, rs2 = (r[...] for r in rope_refs)
    is_nope = _lane_iota(kpe.shape) < QK_NOPE
    latb = lat.astype(BF16)
    qf = jnp.dot(cqn.astype(BF16), wuq_ref[...], preferred_element_type=F32)
    vf = jnp.dot(latb, wuv_ref[...], preferred_element_type=F32)
    _store_keys(k_ref, jnp.dot(latb, wuk_ref[...], preferred_element_type=F32), gk_ref[...], kpe)
    gq = gq_ref[...]
    ones_hi = jnp.where(is_nope, 0.0, 1.0)
    for hd in range(N_HEADS):
        sl = slice(hd * LANE, (hd + 1) * LANE)
        qs = qf[:, sl]
        sq = qs * qs
        ss_all = jnp.sum(sq, axis=-1, keepdims=True)
        ss_n = jnp.sum(jnp.where(is_nope, sq, 0.0), axis=-1, keepdims=True)
        r = jnp.where(is_nope, lax.rsqrt(ss_n * (1.0 / QK_NOPE) + EPS),
                      lax.rsqrt((ss_all - ss_n) * (1.0 / QK_ROPE) + EPS))
        q_ref[0, :, sl] = _rope(qs * r * gq, rc, rs1, rs2).astype(BF16)
        v_ref[0, :, sl] = (vf[:, sl] + ones_hi).astype(BF16)


def _pre_kernel_t(x_ref, gmix_ref, wa_ref, gcq_ref, gckv_ref, gkr_ref, wuqt_ref, wuk_ref, wuvt_ref,
                  gqt_ref, gk_ref, rc_ref, rs1_ref, rs2_ref, cos_ref, sin_ref,
                  qt_ref, k_ref, vt_ref, lat_ref, kpe_ref):
    rope_refs = (rc_ref, rs1_ref, rs2_ref)
    cqn, lat, kpe = _pre_latents(x_ref, gmix_ref, wa_ref, gcq_ref, gckv_ref, gkr_ref, rope_refs, lat_ref, kpe_ref)
    latb = lat.astype(BF16)
    _store_keys(k_ref, jnp.dot(latb, wuk_ref[...], preferred_element_type=F32), gk_ref[...], kpe)
    tm = cqn.shape[0]
    qft = jnp.dot(wuqt_ref[...], cqn.T.astype(BF16), preferred_element_type=F32)
    vft = jnp.dot(wuvt_ref[...], lat.T.astype(BF16), preferred_element_type=F32)
    gq, cos, sin = gqt_ref[...], cos_ref[...], sin_ref[...]
    half = QK_ROPE // 2
    ones_lo = jnp.where(lax.broadcasted_iota(jnp.int32, (LANE, tm), 0) < V_DIM, 0.0, 1.0)
    pad = jnp.zeros((LANE - QK_NOPE - QK_ROPE, tm), F32)
    for hd in range(N_HEADS):
        rows = slice(hd * LANE, (hd + 1) * LANE)
        qs = qft[rows]
        sq = qs * qs
        r_n = lax.rsqrt(jnp.sum(sq[:QK_NOPE], axis=0, keepdims=True) * (1.0 / QK_NOPE) + EPS)
        r_p = lax.rsqrt(jnp.sum(sq[QK_NOPE:QK_NOPE + QK_ROPE], axis=0, keepdims=True) * (1.0 / QK_ROPE) + EPS)
        nope = qs[:QK_NOPE] * r_n * gq[:QK_NOPE]
        x1 = qs[QK_NOPE:QK_NOPE + half] * r_p * gq[QK_NOPE:QK_NOPE + half]
        x2 = qs[QK_NOPE + half:QK_NOPE + QK_ROPE] * r_p * gq[QK_NOPE + half:QK_NOPE + QK_ROPE]
        slab = jnp.concatenate([nope, x1 * cos - x2 * sin, x1 * sin + x2 * cos, pad], axis=0)
        qt_ref[0, 0, rows, :] = slab.astype(BF16)
        vt_ref[0, 0, rows, :] = (vft[rows] + ones_lo).astype(BF16)


def _kvpast_kernel(lat_ref, kpe_ref, wuk_ref, wuv_ref, gk_ref, k_ref, v_ref):
    latb = lat_ref[0].astype(BF16)
    kf = jnp.dot(latb, wuk_ref[...], preferred_element_type=F32)
    vf = jnp.dot(latb, wuv_ref[...], preferred_element_type=F32)
    kpe = kpe_ref[0]
    lane = _lane_iota(kpe.shape)
    ones_hi = jnp.where(lane < QK_NOPE, 0.0, 1.0)
    gk = gk_ref[...]
    for hd in range(N_HEADS):
        sl = slice(hd * LANE, (hd + 1) * LANE)
        ks = kf[:, sl]
        k_ref[0, :, sl] = (ks * _rms_scale(ks, QK_NOPE) * gk + kpe).astype(BF16)
        v_ref[0, :, sl] = (vf[:, sl] + ones_hi).astype(BF16)


def _finish_pair(accs):
    outs = [a / pltpu.roll(a, V_DIM, 1) for a in accs]
    lane = _lane_iota(outs[0].shape)
    return jnp.where(lane < V_DIM, outs[0], pltpu.roll(outs[1], V_DIM, 1))


def _attn_prompt_kernel(qt_ref, k_ref, vt_ref, o_ref, *scratch, tq, nh):
    s_refs = (scratch[:nh], scratch[nh:2 * nh])
    p_refs = (scratch[2 * nh:3 * nh], scratch[3 * nh:])
    qi = pl.program_id(2)
    key_pos = lax.broadcasted_iota(jnp.int32, (tq, tq), 0)
    query_pos = lax.broadcasted_iota(jnp.int32, (tq, tq), 1)
    diag_mask = (key_pos // CHUNK) <= (query_pos // CHUNK)
    heads = [slice(hh * LANE, (hh + 1) * LANE) for hh in range(nh)]

    def scores(i, slot, hh, mask):
        start = pl.multiple_of(i * tq, tq)
        s = jnp.dot(k_ref[0, pl.ds(start, tq), heads[hh]], qt_ref[0, 0, heads[hh], :], preferred_element_type=F32)
        if mask is not None:
            s = jnp.where(mask, s, NEG_BIG)
        s_refs[slot][hh][...] = s
        return jnp.max(s, axis=0, keepdims=True)

    def softmax(slot, hh, m, tile_max):
        m_new = jnp.maximum(m, tile_max)
        p_refs[slot][hh][...] = jnp.exp2(s_refs[slot][hh][...] - m_new).astype(BF16)
        return m_new, jnp.exp2(m - m_new)

    def accumulate(i, slot, hh, alpha, acc):
        pv = jnp.dot(vt_ref[0, jnp.maximum(i, 0), heads[hh], :], p_refs[slot][hh][...],
                     preferred_element_type=F32)
        return acc * alpha + pv

    def iteration(i, slot, carry, next_mask=None):
        stats = [softmax(slot, hh, m, tile_max) for hh, (m, tile_max, _, _) in enumerate(carry)]
        next_max = [scores(i + 1, 1 - slot, hh, next_mask) for hh in range(nh)]
        accs = [accumulate(i - 1, 1 - slot, hh, alpha, acc) for hh, (_, _, alpha, acc) in enumerate(carry)]
        return tuple((m, tmax, alpha, acc) for (m, alpha), tmax, acc in zip(stats, next_max, accs))

    def last(slot, carry):
        accs = []
        for hh, (m, tile_max, alpha, acc) in enumerate(carry):
            acc = accumulate(qi - 1, 1 - slot, hh, alpha, acc)
            m, alpha = softmax(slot, hh, m, tile_max)
            accs.append(accumulate(qi, slot, hh, alpha, acc))
        return tuple(accs)

    first_mask = diag_mask | (qi > 0)
    init = []
    for hh in range(nh):
        p_refs[1][hh][...] = jnp.zeros((tq, tq), BF16)
        init.append((jnp.full((1, tq), NEG_BIG, F32), scores(0, 0, hh, first_mask), jnp.ones((1, tq), F32),
                     jnp.zeros((LANE, tq), F32)))
    carry = lax.fori_loop(0, (qi - 1) // 2, lambda j, c: iteration(2 * j + 1, 1, iteration(2 * j, 0, c)),
                          tuple(init))
    tails = [lambda c: last(0, c),
             lambda c: last(1, iteration(qi - 1, 0, c, diag_mask)),
             lambda c: last(0, iteration(qi - 1, 1, iteration(qi - 2, 0, c), diag_mask))]
    accs = lax.switch(jnp.where(qi == 0, 0, 2 - qi % 2), tails, carry)
    out_t = jnp.concatenate([a[:V_DIM] / a[V_DIM:V_DIM + 1] for a in accs], axis=0)
    o_ref[0] = out_t.T.astype(o_ref.dtype)


def _attn_sample_kernel(q_ref, kp_ref, vp_ref, kn_ref, vn_ref, o_ref):
    nt = (((1,), (1,)), ((), ()))
    slabs = [slice(hd * LANE, (hd + 1) * LANE) for hd in range(N_HEADS)]
    s_past = [lax.dot_general(q_ref[0, :, sl], kp_ref[0, :, sl], nt, preferred_element_type=F32) for sl in slabs]
    s_new = [lax.dot_general(q_ref[0, :, sl], kn_ref[0, :, sl], nt, preferred_element_type=F32) for sl in slabs]
    accs = []
    for sl, sp, sn in zip(slabs, s_past, s_new):
        m = jnp.maximum(jnp.max(sp, axis=-1, keepdims=True), jnp.max(sn, axis=-1, keepdims=True))
        accs.append(jnp.dot(jnp.exp2(sp - m).astype(BF16), vp_ref[0, :, sl], preferred_element_type=F32)
                    + jnp.dot(jnp.exp2(sn - m).astype(BF16), vn_ref[0, :, sl], preferred_element_type=F32))
    for pair in range(N_HEADS // 2):
        o_ref[0, :, pair * LANE:(pair + 1) * LANE] = _finish_pair(accs[2 * pair:2 * pair + 2]).astype(o_ref.dtype)


def _post_kernel(x_ref, attn_ref, cinit_ref, cnt0_ref, gmix_ref, wb_ref, convw_ref, woa_ref, woc_ref, wo_ref,
                 gffn_ref, wrt_ref, brt_ref, tri_ref, x1_all_ref,
                 x1_ref, route_ref, cnew_ref, cnt_ref, carry_ref, *, seq):
    del x1_all_ref
    si = pl.program_id(1)
    tm = x_ref.shape[1]

    if seq is None:
        @pl.when(si == 0)
        def _():
            carry_ref[...] = cinit_ref[0]

    @pl.when((si == 0) & (pl.program_id(0) == 0))
    def _():
        cnt_ref[...] = cnt0_ref[...]

    x = x_ref[0]
    h = x * _rms_scale(x, x.shape[-1]) * gmix_ref[...]
    z = jnp.dot(h.astype(BF16), wb_ref[...], preferred_element_type=F32)
    conv_b = z[:, :CONV_DIM]
    u = z[:, CONV_DIM:2 * CONV_DIM] * z[:, 2 * CONV_DIM:3 * CONV_DIM]
    d = x.shape[-1]
    gate_a = z[:, 3 * CONV_DIM:3 * CONV_DIM + d]
    gate_c = z[:, 3 * CONV_DIM + d:]

    row = lax.broadcasted_iota(jnp.int32, u.shape, 0)
    if seq is None:
        carry = carry_ref[...]
        c1 = carry[7:8, :]
        c2 = carry[6:7, :]
        u_m1 = jnp.where(row == 0, c1, pltpu.roll(u, 1, 0))
        u_m2 = jnp.where(row == 0, c2, jnp.where(row == 1, c1, pltpu.roll(u, 2, 0)))
        carry_ref[...] = u[tm - 8:, :]
        cnew_ref[0] = u[tm - 2:, :]
    else:
        pos = row % seq
        u_m1 = jnp.where(pos == 0, cinit_ref[0, :tm, :], pltpu.roll(u, 1, 0))
        u_m2 = jnp.where(pos <= 1, cinit_ref[0, tm:, :], pltpu.roll(u, 2, 0))
        cnew_ref[0] = u.reshape(tm // seq, seq, u.shape[-1])[:, seq - 2:, :]
    cw = convw_ref[...]
    cv = cw[0:1, :] * u_m2 + cw[1:2, :] * u_m1 + cw[2:3, :] * u

    y_a = jnp.dot(attn_ref[0], woa_ref[...], preferred_element_type=F32)
    y_c = jnp.dot((conv_b * cv).astype(BF16), woc_ref[...], preferred_element_type=F32)
    mrg = jax.nn.sigmoid(gate_a) * y_a + jax.nn.sigmoid(gate_c) * y_c
    x1 = x + jnp.dot(mrg.astype(BF16), wo_ref[...], preferred_element_type=F32)
    nchunk = d // LANE
    for c in range(nchunk):
        x1_ref[pl.ds(c, tm, stride=nchunk), :] = x1[:, c * LANE:(c + 1) * LANE]

    h2 = x1 * _rms_scale(x1, d) * gffn_ref[...]
    h2_hi = h2.astype(BF16)
    h2_lo = (h2 - h2_hi.astype(F32)).astype(BF16)
    nt = (((1,), (1,)), ((), ()))
    n_rows = brt_ref.shape[0]
    lt2 = lax.dot_general(wrt_ref[...], h2_hi, nt, preferred_element_type=F32)
    lt = (lt2[:n_rows] + lt2[n_rows:]
          + lax.dot_general(wrt_ref[:n_rows, :], h2_lo, nt, preferred_element_type=F32) + brt_ref[...])
    sub = lax.broadcasted_iota(jnp.int32, (EXPERTS_PER_GROUP, tm), 0).astype(F32)
    none = float(EXPERTS_PER_GROUP)

    def first_argmax(v):
        vmax = jnp.max(v, axis=0, keepdims=True)
        return jnp.min(jnp.where(v == vmax, sub, none), axis=0, keepdims=True)

    g_idx = first_argmax(lt[:EXPERTS_PER_GROUP])
    el = lt[EXPERTS_PER_GROUP * N_GROUPS:]
    for g in range(N_GROUPS - 2, -1, -1):
        el = jnp.where(g_idx == g, lt[EXPERTS_PER_GROUP * (g + 1):EXPERTS_PER_GROUP * (g + 2)], el)
    i1 = first_argmax(el)
    i2 = first_argmax(jnp.where(sub == i1, NEG_BIG, el))
    key = (g_idx * (EXPERTS_PER_GROUP * EXPERTS_PER_GROUP) + jnp.minimum(i1, i2) * EXPERTS_PER_GROUP
           + jnp.maximum(i1, i2))

    n_keys = cnt_ref.shape[0]
    keys = lax.broadcasted_iota(jnp.int32, (n_keys, tm), 0).astype(F32)
    onehot = jnp.where(keys == key, 1.0, 0.0)
    before = jnp.dot(onehot.astype(BF16), tri_ref[...], preferred_element_type=F32)
    rank = jnp.sum(onehot * (before + cnt_ref[...]), axis=0, keepdims=True)
    cnt_ref[...] += jnp.sum(onehot, axis=1, keepdims=True)
    row8 = lax.broadcasted_iota(jnp.int32, (8, tm), 0)
    route_ref[0] = jnp.where(row8 == 0, key, jnp.where(row8 == 1, rank, 0.0))


def _rows_loop(n, fn, unroll=8):
    def group(j, c):
        for u in range(unroll):
            fn(j * unroll + u)
        return c
    lax.fori_loop(0, n // unroll, group, 0)

    def single(i, c):
        fn(i)
        return c
    lax.fori_loop(n // unroll * unroll, n, single, 0)


def _moe_kernel(lo_ref, hi_ref, cnt_ref, dest_ref,
                x1_hbm, gffn_ref, wr_ref, br_ref, w13lo_ref, w13hi_ref, w2lo_ref, w2hi_ref,
                x2_hbm, tok_ref, xbuf, obuf, gsem, ssem):
    nb = pl.program_id(0)
    nblk = pl.num_programs(0)
    nchunk = xbuf.shape[1] // MOE_ROWS
    slot = nb % 2
    cnt = cnt_ref[nb]

    def start_gathers(blk, sl):
        base = blk * MOE_ROWS

        def one(i):
            src = x1_hbm.at[pl.ds(tok_ref[base + i] * nchunk, nchunk)]
            pltpu.make_async_copy(src, xbuf.at[sl, pl.ds(i * nchunk, nchunk)], gsem.at[sl]).start()
        _rows_loop(cnt_ref[blk], one)

    def start_scatters(blk, sl):
        base = blk * MOE_ROWS

        def one(i):
            dst = x2_hbm.at[pl.ds(tok_ref[base + i] * nchunk, nchunk)]
            pltpu.make_async_copy(obuf.at[sl, pl.ds(i * nchunk, nchunk)], dst, ssem.at[sl]).start()
        _rows_loop(cnt_ref[blk], one)

    def wait_gathers(blk, sl):
        rows = cnt_ref[blk] * nchunk
        pltpu.make_async_copy(x1_hbm.at[pl.ds(0, rows)], xbuf.at[sl, pl.ds(0, rows)], gsem.at[sl]).wait()

    def wait_scatters(blk, sl):
        rows = cnt_ref[blk] * nchunk
        pltpu.make_async_copy(obuf.at[sl, pl.ds(0, rows)], x2_hbm.at[pl.ds(0, rows)], ssem.at[sl]).wait()

    @pl.when(nb == 0)
    def _():
        def fill(i, c):
            tok_ref[i] = 0
            return c
        lax.fori_loop(0, tok_ref.shape[0], fill, 0, unroll=8)

        def invert(i, c):
            tok_ref[dest_ref[i]] = i
            return c
        lax.fori_loop(0, dest_ref.shape[0], invert, 0, unroll=8)
        xbuf[...] = jnp.zeros_like(xbuf)

        @pl.when(cnt > 0)
        def _():
            start_gathers(0, 0)

    @pl.when(nb + 1 < nblk)
    def _():
        @pl.when(cnt_ref[nb + 1] > 0)
        def _():
            start_gathers(nb + 1, 1 - slot)

    @pl.when(nb >= 2)
    def _():
        @pl.when(cnt_ref[nb - 2] > 0)
        def _():
            wait_scatters(nb - 2, slot)

    @pl.when(cnt > 0)
    def _():
        wait_gathers(nb, slot)
        xg = jnp.concatenate([xbuf[slot, pl.ds(c, MOE_ROWS, stride=nchunk), :] for c in range(nchunk)], axis=1)
        h = (xg * _rms_scale(xg, xg.shape[-1]) * gffn_ref[...]).astype(BF16)

        lo, hi = lo_ref[nb], hi_ref[nb]
        logit = jnp.dot(h, wr_ref[...], preferred_element_type=F32) + br_ref[...]
        lane = _lane_iota(logit.shape)
        pick = lambda j: jnp.sum(jnp.where(lane == j, logit, 0.0), axis=-1, keepdims=True)
        is_g = lane < N_GROUPS
        gmax = jnp.max(jnp.where(is_g, logit, NEG_BIG), axis=-1, keepdims=True)
        g_den = jnp.sum(jnp.where(is_g, jnp.exp(logit - gmax), 0.0), axis=-1, keepdims=True)
        g_p = jnp.exp(pick(lo // EXPERTS_PER_GROUP) - gmax) / g_den
        l_lo, l_hi = pick(N_GROUPS + lo), pick(N_GROUPS + hi)
        gates = (g_p * jax.nn.sigmoid(l_lo - l_hi), g_p * jax.nn.sigmoid(l_hi - l_lo))

        abs_ = [jnp.dot(h, w13_ref[0], preferred_element_type=F32) for w13_ref in (w13lo_ref, w13hi_ref)]
        y = xg
        for ab, w2_ref, gate in zip(abs_, (w2lo_ref, w2hi_ref), gates):
            hid = jax.nn.silu(ab[:, :EXPERT_HIDDEN]) * ab[:, EXPERT_HIDDEN:]
            y = y + gate * jnp.dot(hid.astype(BF16), w2_ref[0], preferred_element_type=F32)
        for c in range(nchunk):
            obuf[slot, pl.ds(c, MOE_ROWS, stride=nchunk), :] = y[:, c * LANE:(c + 1) * LANE]
        start_scatters(nb, slot)

    @pl.when(nb == nblk - 1)
    def _():
        @pl.when(nb >= 1)
        def _():
            @pl.when(cnt_ref[nb - 1] > 0)
            def _():
                wait_scatters(nb - 1, 1 - slot)

        @pl.when(cnt > 0)
        def _():
            wait_scatters(nb, slot)


def _ple_kernel(x_ref, p_ref, gple_ref, wpg_ref, wple_ref, o_ref):
    tm = o_ref.shape[0]
    nchunk = o_ref.shape[1] // LANE
    emb = jnp.dot(p_ref[...].astype(BF16), wple_ref[...], preferred_element_type=F32)
    x = jnp.concatenate([x_ref[pl.ds(c, tm, stride=nchunk), :] for c in range(nchunk)], axis=1)
    hp = (x * _rms_scale(x, x.shape[-1]) * gple_ref[...]).astype(BF16)
    gate = jax.nn.sigmoid(jnp.dot(hp, wpg_ref[...], preferred_element_type=F32))
    o_ref[...] = x + gate * emb


def _const(shape):
    nd = len(shape)
    return pl.BlockSpec(shape, lambda *_: (0,) * nd)


def _head_slab_cols(w, width, offset=0):
    k = w.shape[0]
    w = w.reshape(k, N_HEADS, width)
    w = jnp.pad(w, ((0, 0), (0, 0), (offset, LANE - width - offset)))
    return w.reshape(k, N_HEADS * LANE)


def _prep_weights(g_mix, w_in, g_cq, w_uq, g_qn, g_qr, g_ckv, w_ukv, g_kn, g_kr, w_oa,
                  conv_w, w_oc, w_o, g_ffn, w_rg, b_rg, w_re, b_re, w1, w3, w2, g_ple, w_pg, w_ple):
    d = w_in.shape[0]
    n_mla = Q_LORA + KV_LORA
    kr_cols = jnp.pad(w_in[:, n_mla:n_mla + QK_ROPE], ((0, 0), (QK_NOPE, LANE - QK_NOPE - QK_ROPE)))
    w = {}
    w["wa"] = jnp.concatenate([w_in[:, :n_mla], kr_cols], axis=1).astype(BF16)
    w["wb"] = w_in[:, n_mla + QK_ROPE:].astype(BF16)
    w["wuq"] = _head_slab_cols(w_uq, QK_NOPE + QK_ROPE).astype(BF16)
    ukv = w_ukv.reshape(KV_LORA, N_HEADS, QK_NOPE + V_DIM)
    w["wuk"] = _head_slab_cols(ukv[:, :, :QK_NOPE].reshape(KV_LORA, -1), QK_NOPE).astype(BF16)
    w["wuv"] = _head_slab_cols(ukv[:, :, QK_NOPE:].reshape(KV_LORA, -1), V_DIM).astype(BF16)
    pad_hi = LANE - QK_NOPE - QK_ROPE
    w["gq"] = (jnp.pad(jnp.concatenate([g_qn, g_qr]), (0, pad_hi)) * (ATTN_SCALE * LOG2E))[None]
    w["gk"] = jnp.pad(g_kn, (0, LANE - QK_NOPE))[None]
    w["gkr"] = jnp.pad(g_kr, (QK_NOPE, pad_hi))[None]
    w["gmix"], w["gcq"], w["gckv"] = g_mix[None], g_cq[None], g_ckv[None]
    w["gffn"], w["gple"] = g_ffn[None], g_ple[None]
    w["convw"] = jnp.pad(conv_w, ((0, 8 - conv_w.shape[0]), (0, 0)))
    w["woa"], w["woc"], w["wo"] = w_oa.astype(BF16), w_oc.astype(BF16), w_o.astype(BF16)
    n_r = N_GROUPS + N_EXPERTS
    w["wr"] = jnp.pad(jnp.concatenate([w_rg, w_re], axis=1), ((0, 0), (0, LANE - n_r))).astype(BF16)
    w["br"] = jnp.pad(jnp.concatenate([b_rg, b_re]), (0, LANE - n_r))[None]
    pad_g = EXPERTS_PER_GROUP - N_GROUPS
    wrt = jnp.concatenate([jnp.pad(w_rg.T, ((0, pad_g), (0, 0))), w_re.T], axis=0)
    wrt_hi = wrt.astype(BF16)
    w["wrt"] = jnp.concatenate([wrt_hi, (wrt - wrt_hi.astype(F32)).astype(BF16)], axis=0)
    w["brt"] = jnp.concatenate([b_rg, jnp.full((pad_g,), NEG_BIG, F32), b_re])[:, None]
    w["w13"] = jnp.concatenate([w1, w3], axis=2).astype(BF16)
    w["w2"] = w2.astype(BF16)
    w["wpg"], w["wple"] = w_pg.astype(BF16), w_ple.astype(BF16)
    return w


def _rope_slabs(pos):
    inv = 1.0 / (ROPE_THETA ** (jnp.arange(0, QK_ROPE, 2, dtype=F32) / QK_ROPE))
    ang = pos.astype(F32)[:, None] * inv[None, :]
    cos, sin = jnp.cos(ang), jnp.sin(ang)
    n = pos.shape[0]
    half = QK_ROPE // 2
    z = lambda k: jnp.zeros((n, k), F32)
    pad_hi = LANE - QK_NOPE - QK_ROPE
    rc = jnp.concatenate([jnp.ones((n, QK_NOPE), F32), cos, cos, z(pad_hi)], axis=1)
    rs1 = jnp.concatenate([z(QK_NOPE), -sin, z(half), z(pad_hi)], axis=1)
    rs2 = jnp.concatenate([z(QK_NOPE), z(half), sin, z(pad_hi)], axis=1)
    return (rc, rs1, rs2), cos.T, sin.T


def _params(sem):
    return pltpu.CompilerParams(dimension_semantics=sem, vmem_limit_bytes=VMEM_LIMIT)


def _mla_pre(x, w, rope, tm):
    b, s, d = x.shape
    hw = N_HEADS * LANE
    tok = lambda width: pl.BlockSpec((1, tm, width), lambda i, j: (i, j, 0))
    rope_spec = pl.BlockSpec((tm, LANE), lambda i, j: (j, 0))
    consts = [w["gmix"], w["wa"], w["gcq"], w["gckv"], w["gkr"], w["wuq"], w["wuk"], w["wuv"], w["gq"], w["gk"]]
    return pl.pallas_call(
        _pre_kernel,
        grid=(b, s // tm),
        in_specs=[tok(d)] + [_const(c.shape) for c in consts] + [rope_spec] * 3,
        out_specs=[tok(hw), tok(hw), tok(hw), tok(KV_LORA), tok(QK_ROPE)],
        out_shape=[jax.ShapeDtypeStruct((b, s, hw), BF16)] * 3
        + [jax.ShapeDtypeStruct((b, s, KV_LORA), F32), jax.ShapeDtypeStruct((b, s, QK_ROPE), F32)],
        compiler_params=_params(("parallel", "parallel")),
        name="mla_pre",
    )(x, *consts, *rope)


def _mla_pre_t(x, w, rope, cos_t, sin_t, tm):
    b, s, d = x.shape
    hw = N_HEADS * LANE
    tok = lambda width: pl.BlockSpec((1, tm, width), lambda i, j: (i, j, 0))
    tiled = pl.BlockSpec((1, 1, hw, tm), lambda i, j: (i, j, 0, 0))
    rope_spec = pl.BlockSpec((tm, LANE), lambda i, j: (j, 0))
    rope_t_spec = pl.BlockSpec((QK_ROPE // 2, tm), lambda i, j: (0, j))
    gqt = jnp.broadcast_to(w["gq"].T, (LANE, tm))
    consts = [w["gmix"], w["wa"], w["gcq"], w["gckv"], w["gkr"], w["wuq"].T, w["wuk"], w["wuv"].T, gqt, w["gk"]]
    return pl.pallas_call(
        _pre_kernel_t,
        grid=(b, s // tm),
        in_specs=[tok(d)] + [_const(c.shape) for c in consts] + [rope_spec] * 3 + [rope_t_spec] * 2,
        out_specs=[tiled, tok(hw), tiled, tok(KV_LORA), tok(QK_ROPE)],
        out_shape=[jax.ShapeDtypeStruct((b, s // tm, hw, tm), BF16), jax.ShapeDtypeStruct((b, s, hw), BF16),
                   jax.ShapeDtypeStruct((b, s // tm, hw, tm), BF16),
                   jax.ShapeDtypeStruct((b, s, KV_LORA), F32), jax.ShapeDtypeStruct((b, s, QK_ROPE), F32)],
        compiler_params=_params(("parallel", "parallel")),
        name="mla_pre_t",
    )(x, *consts, *rope, cos_t, sin_t)


def _kv_past(past_lat, past_kpe, w, tm):
    b, t, _ = past_lat.shape
    hw = N_HEADS * LANE
    kpe_slab = jnp.pad(past_kpe, ((0, 0), (0, 0), (QK_NOPE, LANE - QK_NOPE - QK_ROPE)))
    tok = lambda width: pl.BlockSpec((1, tm, width), lambda i, j: (i, j, 0))
    consts = [w["wuk"], w["wuv"], w["gk"]]
    return pl.pallas_call(
        _kvpast_kernel,
        grid=(b, t // tm),
        in_specs=[tok(KV_LORA), tok(LANE)] + [_const(c.shape) for c in consts],
        out_specs=[tok(hw), tok(hw)],
        out_shape=[jax.ShapeDtypeStruct((b, t, hw), BF16)] * 2,
        compiler_params=_params(("parallel", "parallel")),
        name="kv_past",
    )(past_lat, kpe_slab, *consts)


def _attn_prompt(qt, k, vt):
    b, n_tiles, _, tq = qt.shape
    s = n_tiles * tq
    nh = ATTN_HEADS_PER_STEP
    width = nh * LANE
    return pl.pallas_call(
        functools.partial(_attn_prompt_kernel, tq=tq, nh=nh),
        grid=(b, N_HEADS // nh, n_tiles),
        in_specs=[pl.BlockSpec((1, 1, width, tq), lambda i, h, j: (i, j, h, 0)),
                  pl.BlockSpec((1, s, width), lambda i, h, j: (i, 0, h)),
                  pl.BlockSpec((1, n_tiles, width, tq), lambda i, h, j: (i, 0, h, 0))],
        out_specs=pl.BlockSpec((1, tq, nh * V_DIM), lambda i, h, j: (i, j, h)),
        out_shape=jax.ShapeDtypeStruct((b, s, N_HEADS * V_DIM), BF16),
        scratch_shapes=[pltpu.VMEM((tq, tq), F32)] * (2 * nh) + [pltpu.VMEM((tq, tq), BF16)] * (2 * nh),
        compiler_params=_params(("parallel", "parallel", "arbitrary")),
        name="attn_prompt",
    )(qt, k, vt)


def _attn_sample(q, kp, vp, kn, vn):
    b, s, _ = q.shape
    t = kp.shape[1]
    hw = N_HEADS * LANE
    blk = lambda rows: pl.BlockSpec((1, rows, hw), lambda i: (i, 0, 0))
    return pl.pallas_call(
        _attn_sample_kernel,
        grid=(b,),
        in_specs=[blk(s), blk(t), blk(t), blk(s), blk(s)],
        out_specs=pl.BlockSpec((1, s, N_HEADS * V_DIM), lambda i: (i, 0, 0)),
        out_shape=jax.ShapeDtypeStruct((b, s, N_HEADS * V_DIM), BF16),
        compiler_params=_params(("parallel",)),
        name="attn_sample",
    )(q, kp, vp, kn, vn)


def _post(x, attn, conv_init, cnt0, x1_all, row_off, n_all, w, tm, seq=None):
    b, s, d = x.shape
    nchunk = d // LANE
    n_keys = cnt0.shape[0]
    tiles_per_b = s // tm
    off = row_off // tm
    tok = lambda width: pl.BlockSpec((1, tm, width), lambda i, j: (i, j, 0))
    if seq is None:
        cin_spec = pl.BlockSpec((1, 8, CONV_DIM), lambda i, j: (i, 0, 0))
        cnew_spec = pl.BlockSpec((1, 2, CONV_DIM), lambda i, j: (i, 0, 0))
        cnew_shape = (b, 2, CONV_DIM)
        conv_rows = 8
    else:
        cin_spec = pl.BlockSpec((1, 2 * tm, CONV_DIM), lambda i, j: (j, 0, 0))
        cnew_spec = pl.BlockSpec((1, tm // seq, 2, CONV_DIM), lambda i, j: (j, 0, 0, 0))
        cnew_shape = (tiles_per_b, tm // seq, 2, CONV_DIM)
        conv_rows = 8
    tri = (jnp.arange(tm)[:, None] < jnp.arange(tm)[None, :]).astype(BF16)
    consts = [cnt0, w["gmix"], w["wb"], w["convw"], w["woa"], w["woc"], w["wo"], w["gffn"], w["wrt"], w["brt"], tri]
    in_specs = ([tok(d), tok(N_HEADS * V_DIM), cin_spec] + [_const(c.shape) for c in consts]
                + [pl.BlockSpec(memory_space=pl.ANY)])
    args = [x, attn, conv_init] + consts + [x1_all]
    return pl.pallas_call(
        functools.partial(_post_kernel, seq=seq),
        grid=(b, tiles_per_b),
        in_specs=in_specs,
        out_specs=[pl.BlockSpec((tm * nchunk, LANE), lambda i, j: (off + i * tiles_per_b + j, 0)),
                   pl.BlockSpec((1, 8, tm), lambda i, j: (i * tiles_per_b + j, 0, 0)),
                   cnew_spec, _const((n_keys, 1))],
        out_shape=[jax.ShapeDtypeStruct((n_all * nchunk, LANE), F32),
                   jax.ShapeDtypeStruct((b * tiles_per_b, 8, tm), F32),
                   jax.ShapeDtypeStruct(cnew_shape, F32), jax.ShapeDtypeStruct((n_keys, 1), F32)],
        scratch_shapes=[pltpu.VMEM((conv_rows, CONV_DIM), F32)],
        input_output_aliases={len(args) - 1: 0},
        compiler_params=_params(("arbitrary", "arbitrary")),
        name="post",
    )(*args)


def _route_tables(key, rank, counts, n):
    n_keys = counts.shape[0]
    padded = (counts + MOE_ROWS - 1) // MOE_ROWS * MOE_ROWS
    pend = jnp.cumsum(padded)
    pstart = pend - padded
    ids = jnp.arange(n_keys, dtype=jnp.int32)
    dest = rank + jnp.sum(jnp.where(key[:, None] == ids[None, :], pstart[None, :], 0), axis=1)
    nblk = n // MOE_ROWS + N_PAIR_BUCKETS
    blk_start = jnp.arange(nblk, dtype=jnp.int32) * MOE_ROWS
    blk_hot = (blk_start[:, None] >= pstart[None, :]) & (blk_start[:, None] < pend[None, :])
    blk_key = jnp.sum(jnp.where(blk_hot, ids[None, :], 0), axis=1)
    blk_cnt = jnp.sum(jnp.where(blk_hot, jnp.minimum(counts[None, :] - (blk_start[:, None] - pstart[None, :]),
                                                      MOE_ROWS), 0), axis=1)
    any_hot = jnp.any(blk_hot, axis=1)
    blk_key = jnp.where(any_hot, blk_key, n_keys - 1)
    blk_lo = blk_key // EXPERTS_PER_GROUP
    blk_hi = blk_lo // EXPERTS_PER_GROUP * EXPERTS_PER_GROUP + blk_key % EXPERTS_PER_GROUP
    return blk_lo, blk_hi, blk_cnt.astype(jnp.int32), dest.astype(jnp.int32)


def _moe(x1_all, key, rank, counts, w):
    rows, _ = x1_all.shape
    d = w["gffn"].shape[1]
    nchunk = d // LANE
    n = rows // nchunk
    blk_lo, blk_hi, blk_cnt, dest = _route_tables(key, rank, counts, n)
    nblk = blk_lo.shape[0]
    w13_spec = lambda ref_idx: pl.BlockSpec((1, d, 2 * EXPERT_HIDDEN),
                                            lambda i, lo, hi, cnt, dst: ((lo, hi)[ref_idx][i], 0, 0))
    w2_spec = lambda ref_idx: pl.BlockSpec((1, EXPERT_HIDDEN, d),
                                           lambda i, lo, hi, cnt, dst: ((lo, hi)[ref_idx][i], 0, 0))
    buf = pltpu.VMEM((2, MOE_ROWS * nchunk, LANE), F32)
    grid_spec = pltpu.PrefetchScalarGridSpec(
        num_scalar_prefetch=4,
        grid=(nblk,),
        in_specs=[pl.BlockSpec(memory_space=pl.ANY),
                  pl.BlockSpec((1, d), lambda i, *_: (0, 0)),
                  pl.BlockSpec(w["wr"].shape, lambda i, *_: (0, 0)),
                  pl.BlockSpec(w["br"].shape, lambda i, *_: (0, 0)),
                  w13_spec(0), w13_spec(1), w2_spec(0), w2_spec(1)],
        out_specs=pl.BlockSpec(memory_space=pl.ANY),
        scratch_shapes=[pltpu.SMEM((nblk * MOE_ROWS,), jnp.int32), buf, buf,
                        pltpu.SemaphoreType.DMA((2,)), pltpu.SemaphoreType.DMA((2,))],
    )
    return pl.pallas_call(
        _moe_kernel,
        grid_spec=grid_spec,
        out_shape=jax.ShapeDtypeStruct((rows, LANE), F32),
        compiler_params=_params(("arbitrary",)),
        name="moe",
    )(blk_lo, blk_hi, blk_cnt, dest, x1_all, w["gffn"], w["wr"], w["br"], w["w13"], w["w13"], w["w2"], w["w2"])


def _ple(x2_all, row_off, p, w, tm):
    n, pd = p.shape
    d = w["gple"].shape[1]
    nchunk = d // LANE
    off = row_off // tm
    consts = [w["gple"], w["wpg"], w["wple"]]
    return pl.pallas_call(
        _ple_kernel,
        grid=(n // tm,),
        in_specs=[pl.BlockSpec((tm * nchunk, LANE), lambda i: (off + i, 0)), pl.BlockSpec((tm, pd), lambda i: (i, 0))]
        + [_const(c.shape) for c in consts],
        out_specs=pl.BlockSpec((tm, d), lambda i: (i, 0)),
        out_shape=jax.ShapeDtypeStruct((n, d), F32),
        compiler_params=_params(("parallel",)),
        name="ple",
    )(x2_all, p, *consts)


def _layer(xp, xs, pp, ps, past_lat, past_kpe, past_conv, w):
    bp, sp, d = xp.shape
    bs, ss, _ = xs.shape
    n_p, n_s = bp * sp, bs * ss
    n_all = n_p + n_s
    n_keys = N_EXPERTS * EXPERTS_PER_GROUP
    past_len = past_lat.shape[1]
    tm_p, tm_s = min(512, sp), min(512, n_s)

    rope, cos_t, sin_t = _rope_slabs(jnp.arange(sp))
    qt, k, vt, lat_p, kpe_p = _mla_pre_t(xp, w, rope, cos_t, sin_t, tm_p)
    attn_p = _attn_prompt(qt, k, vt)
    cinit_p = jnp.zeros((bp, 8, CONV_DIM), F32)

    rope, _, _ = _rope_slabs(past_len + jnp.arange(n_s) % ss)
    xs_rows = xs.reshape(1, n_s, d)
    q, k, v, lat_s, kpe_s = _mla_pre(xs_rows, w, rope, tm_s)
    by_seq = lambda a: a.reshape(bs, ss, a.shape[-1])
    kp, vp = _kv_past(past_lat, past_kpe, w, min(512, past_len))
    attn_s = _attn_sample(by_seq(q), kp, vp, by_seq(k), by_seq(v)).reshape(1, n_s, -1)
    lat_s, kpe_s = by_seq(lat_s), by_seq(kpe_s)
    in_tiles = lambda a: a.reshape(n_s // tm_s, tm_s, CONV_DIM)
    n_hist = past_conv.shape[1]
    cinit_s = jnp.concatenate(
        [in_tiles(jnp.pad(past_conv[:, n_hist - 1:], ((0, 0), (0, ss - 1), (0, 0)))),
         in_tiles(jnp.pad(past_conv, ((0, 0), (0, ss - n_hist), (0, 0))))], axis=1)

    x1_all = jnp.zeros((n_all * (d // LANE), LANE), F32)
    x1_all, route_p, conv_p, cnt = _post(xp, attn_p, cinit_p, jnp.zeros((n_keys, 1), F32), x1_all, 0, n_all, w, tm_p)
    x1_all, route_s, conv_s, cnt = _post(xs_rows, attn_s, cinit_s, cnt, x1_all, n_p, n_all, w, tm_s, seq=ss)
    conv_s = conv_s.reshape(bs, n_hist, CONV_DIM)
    key = jnp.concatenate([route_p[:, 0].reshape(-1), route_s[:, 0].reshape(-1)]).astype(jnp.int32)
    rank = jnp.concatenate([route_p[:, 1].reshape(-1), route_s[:, 1].reshape(-1)]).astype(jnp.int32)
    x2_all = _moe(x1_all, key, rank, cnt[:, 0].astype(jnp.int32), w)
    yp = _ple(x2_all, 0, pp.reshape(n_p, -1), w, min(512, n_p)).reshape(bp, sp, d)
    ys = _ple(x2_all, n_p, ps.reshape(n_s, -1), w, min(512, n_s)).reshape(bs, ss, d)
    return yp, ys, (lat_p, kpe_p, conv_p, lat_s, kpe_s, conv_s)


def kernel(x_prompt, x_sample, cache_kv_latent, cache_k_rope, state_conv, p_prompt, p_sample,
           g_mix, w_in, g_cq, w_uq, g_qn, g_qr, g_ckv, w_ukv, g_kn, g_kr, w_oa,
           conv_w, w_oc, w_o, g_ffn, w_rg, b_rg, w_re, b_re, w1, w3, w2, g_ple, w_pg, w_ple):
    depth = g_mix.shape[0]
    xp, xs = x_prompt, x_sample
    outs = [[] for _ in range(6)]
    for i in range(depth):
        w = _prep_weights(g_mix[i], w_in[i], g_cq[i], w_uq[i], g_qn[i], g_qr[i], g_ckv[i], w_ukv[i],
                          g_kn[i], g_kr[i], w_oa[i], conv_w[i], w_oc[i], w_o[i], g_ffn[i], w_rg[i], b_rg[i],
                          w_re[i], b_re[i], w1[i], w3[i], w2[i], g_ple[i], w_pg[i], w_ple[i])
        xp, xs, new = _layer(xp, xs, p_prompt[i], p_sample[i], cache_kv_latent[i], cache_k_rope[i], state_conv[i], w)
        for o, a in zip(outs, new):
            o.append(a)
    return (xp, xs) + tuple(jnp.stack(o, axis=0) for o in outs)
```

```python
import functools
import math

import jax
import jax.numpy as jnp
from jax import lax
from jax.experimental import pallas as pl
from jax.experimental.pallas import tpu as pltpu

F32 = jnp.float32
BF16 = jnp.bfloat16

LANE = 128
SUBLANE = 8
TOKEN_TILE = 512
CHUNK = 64
N_HEADS = 8
QK_NOPE = 64
QK_ROPE = 32
V_DIM = 64
Q_LORA = 256
KV_LORA = 256
CONV_DIM = 512
N_GROUPS = 4
EXPERTS_PER_GROUP = 8
N_EXPERTS = N_GROUPS * EXPERTS_PER_GROUP
EXPERT_HIDDEN = 256
ROPE_THETA = 10000.0
EPS = 1e-6
ATTN_SCALE = (QK_NOPE + QK_ROPE) ** -0.5
LOG2E = math.log2(math.e)
NEG_BIG = -1e30
MOE_ROWS = 128
ATTN_HEADS_PER_STEP = 4
PLE_ROW_GROUPS = 2
N_PAIR_BUCKETS = N_GROUPS * (EXPERTS_PER_GROUP * (EXPERTS_PER_GROUP - 1) // 2)
VMEM_LIMIT = 56 * 1024 * 1024


def _rms_scale(x, n):
    return lax.rsqrt(jnp.sum(x * x, axis=-1, keepdims=True) * (1.0 / n) + EPS)


def _lane_iota(shape):
    return lax.broadcasted_iota(jnp.int32, shape, len(shape) - 1)


def _rope(t, rc, rs1, rs2):
    return t * rc + pltpu.roll(t, LANE - QK_ROPE // 2, 1) * rs1 + pltpu.roll(t, QK_ROPE // 2, 1) * rs2


def _pre_latents(x_ref, gmix_ref, wa_ref, gcq_ref, gckv_ref, gkr_ref, rope_refs, lat_ref, kpe_ref):
    x = x_ref[0]
    h = x * _rms_scale(x, x.shape[-1]) * gmix_ref[...]
    z = jnp.dot(h.astype(BF16), wa_ref[...], preferred_element_type=F32)
    cq = z[:, :Q_LORA]
    ckv = z[:, Q_LORA:Q_LORA + KV_LORA]
    kr = z[:, Q_LORA + KV_LORA:]
    cqn = cq * _rms_scale(cq, Q_LORA) * gcq_ref[...]
    lat = ckv * _rms_scale(ckv, KV_LORA) * gckv_ref[...]
    lat_ref[0] = lat
    krn = kr * _rms_scale(kr, QK_ROPE) * gkr_ref[...]
    kpe = _rope(krn, *(r[...] for r in rope_refs))
    kpe_ref[0] = kpe[:, QK_NOPE:QK_NOPE + QK_ROPE]
    return cqn, lat, kpe


def _store_keys(k_ref, kf, gk, kpe):
    for hd in range(N_HEADS):
        sl = slice(hd * LANE, (hd + 1) * LANE)
        ks = kf[:, sl]
        k_ref[0, :, sl] = (ks * _rms_scale(ks, QK_NOPE) * gk + kpe).astype(BF16)


def _pre_kernel(x_ref, gmix_ref, wa_ref, gcq_ref, gckv_ref, gkr_ref, wuq_ref, wuk_ref, wuv_ref,
                gq_ref, gk_ref, rc_ref, rs1_ref, rs2_ref,
                q_ref, k_ref, v_ref, lat_ref, kpe_ref):
    rope_refs = (rc_ref, rs1_ref, rs2_ref)
    cqn, lat, kpe = _pre_latents(x_ref, gmix_ref, wa_ref, gcq_ref, gckv_ref, gkr_ref, rope_refs, lat_ref, kpe_ref)
    rc, rs1, rs2 = (r[...] for r in rope_refs)
    is_nope = _lane_iota(kpe.shape) < QK_NOPE
    latb = lat.astype(BF16)
    qf = jnp.dot(cqn.astype(BF16), wuq_ref[...], preferred_element_type=F32)
    vf = jnp.dot(latb, wuv_ref[...], preferred_element_type=F32)
    _store_keys(k_ref, jnp.dot(latb, wuk_ref[...], preferred_element_type=F32), gk_ref[...], kpe)
    gq = gq_ref[...]
    ones_hi = jnp.where(is_nope, 0.0, 1.0)
    for hd in range(N_HEADS):
        sl = slice(hd * LANE, (hd + 1) * LANE)
        qs = qf[:, sl]
        sq = qs * qs
        ss_all = jnp.sum(sq, axis=-1, keepdims=True)
        ss_n = jnp.sum(jnp.where(is_nope, sq, 0.0), axis=-1, keepdims=True)
        r = jnp.where(is_nope, lax.rsqrt(ss_n * (1.0 / QK_NOPE) + EPS),
                      lax.rsqrt((ss_all - ss_n) * (1.0 / QK_ROPE) + EPS))
        q_ref[0, :, sl] = _rope(qs * r * gq, rc, rs1, rs2).astype(BF16)
        v_ref[0, :, sl] = (vf[:, sl] + ones_hi).astype(BF16)


def _pre_kernel_t(x_ref, gmix_ref, wa_ref, gcq_ref, gckv_ref, gkr_ref, wuqt_ref, wuk_ref, wuvt_ref,
                  gqt_ref, gk_ref, rc_ref, rs1_ref, rs2_ref, cos_ref, sin_ref,
                  qt_ref, k_ref, vt_ref, lat_ref, kpe_ref):
    rope_refs = (rc_ref, rs1_ref, rs2_ref)
    cqn, lat, kpe = _pre_latents(x_ref, gmix_ref, wa_ref, gcq_ref, gckv_ref, gkr_ref, rope_refs, lat_ref, kpe_ref)
    latb = lat.astype(BF16)
    _store_keys(k_ref, jnp.dot(latb, wuk_ref[...], preferred_element_type=F32), gk_ref[...], kpe)
    tm = cqn.shape[0]
    qft = jnp.dot(wuqt_ref[...], cqn.T.astype(BF16), preferred_element_type=F32)
    vft = jnp.dot(wuvt_ref[...], lat.T.astype(BF16), preferred_element_type=F32)
    gq, cos, sin = gqt_ref[...], cos_ref[...], sin_ref[...]
    half = QK_ROPE // 2
    ones_lo = jnp.where(lax.broadcasted_iota(jnp.int32, (LANE, tm), 0) < V_DIM, 0.0, 1.0)
    pad = jnp.zeros((LANE - QK_NOPE - QK_ROPE, tm), F32)
    for hd in range(N_HEADS):
        rows = slice(hd * LANE, (hd + 1) * LANE)
        qs = qft[rows]
        sq = qs * qs
        r_n = lax.rsqrt(jnp.sum(sq[:QK_NOPE], axis=0, keepdims=True) * (1.0 / QK_NOPE) + EPS)
        r_p = lax.rsqrt(jnp.sum(sq[QK_NOPE:QK_NOPE + QK_ROPE], axis=0, keepdims=True) * (1.0 / QK_ROPE) + EPS)
        nope = qs[:QK_NOPE] * r_n * gq[:QK_NOPE]
        x1 = qs[QK_NOPE:QK_NOPE + half] * r_p * gq[QK_NOPE:QK_NOPE + half]
        x2 = qs[QK_NOPE + half:QK_NOPE + QK_ROPE] * r_p * gq[QK_NOPE + half:QK_NOPE + QK_ROPE]
        slab = jnp.concatenate([nope, x1 * cos - x2 * sin, x1 * sin + x2 * cos, pad], axis=0)
        qt_ref[0, 0, rows, :] = slab.astype(BF16)
        vt_ref[0, 0, rows, :] = (vft[rows] + ones_lo).astype(BF16)


def _kvpast_kernel(lat_ref, kpe_ref, wuk_ref, wuv_ref, gk_ref, k_ref, v_ref):
    latb = lat_ref[0].astype(BF16)
    kf = jnp.dot(latb, wuk_ref[...], preferred_element_type=F32)
    vf = jnp.dot(latb, wuv_ref[...], preferred_element_type=F32)
    kpe = kpe_ref[0]
    lane = _lane_iota(kpe.shape)
    ones_hi = jnp.where(lane < QK_NOPE, 0.0, 1.0)
    gk = gk_ref[...]
    for hd in range(N_HEADS):
        sl = slice(hd * LANE, (hd + 1) * LANE)
        ks = kf[:, sl]
        k_ref[0, :, sl] = (ks * _rms_scale(ks, QK_NOPE) * gk + kpe).astype(BF16)
        v_ref[0, :, sl] = (vf[:, sl] + ones_hi).astype(BF16)


def _finish_pair(accs):
    outs = [a / pltpu.roll(a, V_DIM, 1) for a in accs]
    lane = _lane_iota(outs[0].shape)
    return jnp.where(lane < V_DIM, outs[0], pltpu.roll(outs[1], V_DIM, 1))


def _attn_prompt_kernel(qt_ref, k_ref, vt_ref, o_ref, *scratch, tq, nh):
    s_refs = (scratch[:nh], scratch[nh:2 * nh])
    p_refs = (scratch[2 * nh:3 * nh], scratch[3 * nh:])
    qi = pl.program_id(2)
    key_pos = lax.broadcasted_iota(jnp.int32, (tq, tq), 0)
    query_pos = lax.broadcasted_iota(jnp.int32, (tq, tq), 1)
    diag_mask = (key_pos // CHUNK) <= (query_pos // CHUNK)
    heads = [slice(hh * LANE, (hh + 1) * LANE) for hh in range(nh)]

    def scores(i, slot, hh, mask):
        start = pl.multiple_of(i * tq, tq)
        s = jnp.dot(k_ref[0, pl.ds(start, tq), heads[hh]], qt_ref[0, 0, heads[hh], :], preferred_element_type=F32)
        if mask is not None:
            s = jnp.where(mask, s, NEG_BIG)
        s_refs[slot][hh][...] = s
        return jnp.max(s, axis=0, keepdims=True)

    def softmax(slot, hh, m, tile_max):
        m_new = jnp.maximum(m, tile_max)
        p_refs[slot][hh][...] = jnp.exp2(s_refs[slot][hh][...] - m_new).astype(BF16)
        return m_new, jnp.exp2(m - m_new)

    def accumulate(i, slot, hh, alpha, acc):
        pv = jnp.dot(vt_ref[0, jnp.maximum(i, 0), heads[hh], :], p_refs[slot][hh][...],
                     preferred_element_type=F32)
        return acc * alpha + pv

    def iteration(i, slot, carry, next_mask=None):
        stats = [softmax(slot, hh, m, tile_max) for hh, (m, tile_max, _, _) in enumerate(carry)]
        next_max = [scores(i + 1, 1 - slot, hh, next_mask) for hh in range(nh)]
        accs = [accumulate(i - 1, 1 - slot, hh, alpha, acc) for hh, (_, _, alpha, acc) in enumerate(carry)]
        return tuple((m, tmax, alpha, acc) for (m, alpha), tmax, acc in zip(stats, next_max, accs))

    def last(slot, carry):
        accs = []
        for hh, (m, tile_max, alpha, acc) in enumerate(carry):
            acc = accumulate(qi - 1, 1 - slot, hh, alpha, acc)
            m, alpha = softmax(slot, hh, m, tile_max)
            accs.append(accumulate(qi, slot, hh, alpha, acc))
        return tuple(accs)

    first_mask = diag_mask | (qi > 0)
    init = []
    for hh in range(nh):
        p_refs[1][hh][...] = jnp.zeros((tq, tq), BF16)
        init.append((jnp.full((1, tq), NEG_BIG, F32), scores(0, 0, hh, first_mask), jnp.ones((1, tq), F32),
                     jnp.zeros((LANE, tq), F32)))
    carry = lax.fori_loop(0, (qi - 1) // 2, lambda j, c: iteration(2 * j + 1, 1, iteration(2 * j, 0, c)),
                          tuple(init))
    tails = [lambda c: last(0, c),
             lambda c: last(1, iteration(qi - 1, 0, c, diag_mask)),
             lambda c: last(0, iteration(qi - 1, 1, iteration(qi - 2, 0, c), diag_mask))]
    accs = lax.switch(jnp.where(qi == 0, 0, 2 - qi % 2), tails, carry)
    out_t = jnp.concatenate([a[:V_DIM] / a[V_DIM:V_DIM + 1] for a in accs], axis=0)
    o_ref[0] = out_t.T.astype(o_ref.dtype)


def _attn_sample_kernel(q_ref, kp_ref, vp_ref, kn_ref, vn_ref, o_ref):
    nt = (((1,), (1,)), ((), ()))
    slabs = [slice(hd * LANE, (hd + 1) * LANE) for hd in range(N_HEADS)]
    s_past = [lax.dot_general(q_ref[0, :, sl], kp_ref[0, :, sl], nt, preferred_element_type=F32) for sl in slabs]
    s_new = [lax.dot_general(q_ref[0, :, sl], kn_ref[0, :, sl], nt, preferred_element_type=F32) for sl in slabs]
    accs = []
    for sl, sp, sn in zip(slabs, s_past, s_new):
        m = jnp.maximum(jnp.max(sp, axis=-1, keepdims=True), jnp.max(sn, axis=-1, keepdims=True))
        accs.append(jnp.dot(jnp.exp2(sp - m).astype(BF16), vp_ref[0, :, sl], preferred_element_type=F32)
                    + jnp.dot(jnp.exp2(sn - m).astype(BF16), vn_ref[0, :, sl], preferred_element_type=F32))
    for pair in range(N_HEADS // 2):
        o_ref[0, :, pair * LANE:(pair + 1) * LANE] = _finish_pair(accs[2 * pair:2 * pair + 2]).astype(o_ref.dtype)


def _post_kernel(x_ref, attn_ref, cinit_ref, cnt0_ref, gmix_ref, wb_ref, convw_ref, woa_ref, woc_ref, wo_ref,
                 gffn_ref, wrt_ref, brt_ref, tri_ref, x1_all_ref,
                 x1_ref, route_ref, cnew_ref, cnt_ref, carry_ref, *, seq):
    del x1_all_ref
    si = pl.program_id(1)
    tm = x_ref.shape[1]

    if seq is None:
        @pl.when(si == 0)
        def _():
            carry_ref[...] = cinit_ref[0]

    @pl.when((si == 0) & (pl.program_id(0) == 0))
    def _():
        cnt_ref[...] = cnt0_ref[...]

    d = x_ref.shape[-1]
    hm = tm // 2
    groups = (0, hm)
    xs, zs = [], []
    for r0 in groups:
        x = x_ref[0, r0:r0 + hm, :]
        h = x * _rms_scale(x, d) * gmix_ref[...]
        xs.append(x)
        zs.append(jnp.dot(h.astype(BF16), wb_ref[...], preferred_element_type=F32))

    cw = convw_ref[...]
    row = lax.broadcasted_iota(jnp.int32, (hm, CONV_DIM), 0)
    if seq is None:
        c1, c2 = carry_ref[SUBLANE - 1:SUBLANE, :], carry_ref[SUBLANE - 2:SUBLANE - 1, :]
    merged = []
    for r0, z in zip(groups, zs):
        conv_b = z[:, :CONV_DIM]
        u = z[:, CONV_DIM:2 * CONV_DIM] * z[:, 2 * CONV_DIM:3 * CONV_DIM]
        if seq is None:
            u_m1 = jnp.where(row == 0, c1, pltpu.roll(u, 1, 0))
            u_m2 = jnp.where(row == 0, c2, jnp.where(row == 1, c1, pltpu.roll(u, 2, 0)))
            c1, c2 = u[hm - 1:, :], u[hm - 2:hm - 1, :]
            if r0 + hm == tm:
                carry_ref[...] = u[hm - SUBLANE:, :]
                cnew_ref[0] = u[hm - 2:, :]
        else:
            pos = row % seq
            u_m1 = jnp.where(pos == 0, cinit_ref[0, r0:r0 + hm, :], pltpu.roll(u, 1, 0))
            u_m2 = jnp.where(pos <= 1, cinit_ref[0, tm + r0:tm + r0 + hm, :], pltpu.roll(u, 2, 0))
            cnew_ref[0, r0 // seq:(r0 + hm) // seq] = u.reshape(hm // seq, seq, CONV_DIM)[:, seq - 2:, :]
        cv = cw[0:1, :] * u_m2 + cw[1:2, :] * u_m1 + cw[2:3, :] * u
        y_a = jnp.dot(attn_ref[0, r0:r0 + hm, :], woa_ref[...], preferred_element_type=F32)
        y_c = jnp.dot((conv_b * cv).astype(BF16), woc_ref[...], preferred_element_type=F32)
        gate_a = z[:, 3 * CONV_DIM:3 * CONV_DIM + d]
        gate_c = z[:, 3 * CONV_DIM + d:]
        merged.append((jax.nn.sigmoid(gate_a) * y_a + jax.nn.sigmoid(gate_c) * y_c).astype(BF16))
    x1s = [x + jnp.dot(mrg, wo_ref[...], preferred_element_type=F32) for x, mrg in zip(xs, merged)]
    nchunk = d // LANE
    for r0, x1 in zip(groups, x1s):
        for c in range(nchunk):
            x1_ref[pl.ds(r0 * nchunk + c, hm, stride=nchunk), :] = x1[:, c * LANE:(c + 1) * LANE]

    nt = (((1,), (1,)), ((), ()))
    n_rows = brt_ref.shape[0]
    logits = []
    for x1 in x1s:
        h2 = x1 * _rms_scale(x1, d) * gffn_ref[...]
        h2_hi = h2.astype(BF16)
        h2_lo = (h2 - h2_hi.astype(F32)).astype(BF16)
        lt2 = lax.dot_general(wrt_ref[...], h2_hi, nt, preferred_element_type=F32)
        logits.append(lt2[:n_rows] + lt2[n_rows:]
                      + lax.dot_general(wrt_ref[:n_rows, :], h2_lo, nt, preferred_element_type=F32) + brt_ref[...])
    sub = lax.broadcasted_iota(jnp.int32, (EXPERTS_PER_GROUP, hm), 0).astype(F32)
    none = float(EXPERTS_PER_GROUP)

    def first_argmax(v):
        vmax = jnp.max(v, axis=0, keepdims=True)
        return jnp.min(jnp.where(v == vmax, sub, none), axis=0, keepdims=True)

    n_keys = cnt_ref.shape[0]
    keys = lax.broadcasted_iota(jnp.int32, (n_keys, hm), 0).astype(F32)
    row8 = lax.broadcasted_iota(jnp.int32, (8, hm), 0)
    seen = cnt_ref[...]
    for r0, lt in zip(groups, logits):
        g_idx = first_argmax(lt[:EXPERTS_PER_GROUP])
        el = lt[EXPERTS_PER_GROUP * N_GROUPS:]
        for g in range(N_GROUPS - 2, -1, -1):
            el = jnp.where(g_idx == g, lt[EXPERTS_PER_GROUP * (g + 1):EXPERTS_PER_GROUP * (g + 2)], el)
        i1 = first_argmax(el)
        i2 = first_argmax(jnp.where(sub == i1, NEG_BIG, el))
        key = (g_idx * (EXPERTS_PER_GROUP * EXPERTS_PER_GROUP) + jnp.minimum(i1, i2) * EXPERTS_PER_GROUP
               + jnp.maximum(i1, i2))
        onehot = jnp.where(keys == key, 1.0, 0.0)
        before = jnp.dot(onehot.astype(BF16), tri_ref[...], preferred_element_type=F32)
        rank = jnp.sum(onehot * (before + seen), axis=0, keepdims=True)
        seen = seen + jnp.sum(onehot, axis=1, keepdims=True)
        route_ref[0, :, r0:r0 + hm] = jnp.where(row8 == 0, key, jnp.where(row8 == 1, rank, 0.0))
    cnt_ref[...] = seen


def _rows_loop(n, fn, unroll=8):
    def group(j, c):
        for u in range(unroll):
            fn(j * unroll + u)
        return c
    lax.fori_loop(0, n // unroll, group, 0)

    def single(i, c):
        fn(i)
        return c
    lax.fori_loop(n // unroll * unroll, n, single, 0)


def _moe_kernel(lo_ref, hi_ref, cnt_ref, dest_ref,
                x1_hbm, gffn_ref, wr_ref, br_ref, w13lo_ref, w13hi_ref, w2lo_ref, w2hi_ref,
                x2_hbm, tok_ref, xbuf, obuf, gsem, ssem):
    nb = pl.program_id(0)
    nblk = pl.num_programs(0)
    nchunk = xbuf.shape[1] // MOE_ROWS
    slot = nb % 2
    cnt = cnt_ref[nb]

    def start_gathers(blk, sl):
        base = blk * MOE_ROWS

        def one(i):
            src = x1_hbm.at[pl.ds(tok_ref[base + i] * nchunk, nchunk)]
            pltpu.make_async_copy(src, xbuf.at[sl, pl.ds(i * nchunk, nchunk)], gsem.at[sl]).start()
        _rows_loop(cnt_ref[blk], one)

    def start_scatters(blk, sl):
        base = blk * MOE_ROWS

        def one(i):
            dst = x2_hbm.at[pl.ds(tok_ref[base + i] * nchunk, nchunk)]
            pltpu.make_async_copy(obuf.at[sl, pl.ds(i * nchunk, nchunk)], dst, ssem.at[sl]).start()
        _rows_loop(cnt_ref[blk], one)

    def wait_gathers(blk, sl):
        rows = cnt_ref[blk] * nchunk
        pltpu.make_async_copy(x1_hbm.at[pl.ds(0, rows)], xbuf.at[sl, pl.ds(0, rows)], gsem.at[sl]).wait()

    def wait_scatters(blk, sl):
        rows = cnt_ref[blk] * nchunk
        pltpu.make_async_copy(obuf.at[sl, pl.ds(0, rows)], x2_hbm.at[pl.ds(0, rows)], ssem.at[sl]).wait()

    @pl.when(nb == 0)
    def _():
        def invert(i, c):
            tok_ref[dest_ref[i]] = i
            return c
        lax.fori_loop(0, dest_ref.shape[0], invert, 0, unroll=16)
        xbuf[...] = jnp.zeros_like(xbuf)

        @pl.when(cnt > 0)
        def _():
            start_gathers(0, 0)

    @pl.when(nb + 1 < nblk)
    def _():
        @pl.when(cnt_ref[nb + 1] > 0)
        def _():
            start_gathers(nb + 1, 1 - slot)

    @pl.when(nb >= 2)
    def _():
        @pl.when(cnt_ref[nb - 2] > 0)
        def _():
            wait_scatters(nb - 2, slot)

    @pl.when(cnt > 0)
    def _():
        wait_gathers(nb, slot)
        xg = jnp.concatenate([xbuf[slot, pl.ds(c, MOE_ROWS, stride=nchunk), :] for c in range(nchunk)], axis=1)
        h = (xg * _rms_scale(xg, xg.shape[-1]) * gffn_ref[...]).astype(BF16)

        lo, hi = lo_ref[nb], hi_ref[nb]
        logit = jnp.dot(h, wr_ref[...], preferred_element_type=F32) + br_ref[...]
        lane = _lane_iota(logit.shape)
        pick = lambda j: jnp.sum(jnp.where(lane == j, logit, 0.0), axis=-1, keepdims=True)
        is_g = lane < N_GROUPS
        gmax = jnp.max(jnp.where(is_g, logit, NEG_BIG), axis=-1, keepdims=True)
        g_den = jnp.sum(jnp.where(is_g, jnp.exp(logit - gmax), 0.0), axis=-1, keepdims=True)
        g_p = jnp.exp(pick(lo // EXPERTS_PER_GROUP) - gmax) / g_den
        l_lo, l_hi = pick(N_GROUPS + lo), pick(N_GROUPS + hi)
        gates = (g_p * jax.nn.sigmoid(l_lo - l_hi), g_p * jax.nn.sigmoid(l_hi - l_lo))

        abs_ = [jnp.dot(h, w13_ref[0], preferred_element_type=F32) for w13_ref in (w13lo_ref, w13hi_ref)]
        y = xg
        for ab, w2_ref, gate in zip(abs_, (w2lo_ref, w2hi_ref), gates):
            hid = jax.nn.silu(ab[:, :EXPERT_HIDDEN]) * ab[:, EXPERT_HIDDEN:]
            y = y + gate * jnp.dot(hid.astype(BF16), w2_ref[0], preferred_element_type=F32)
        for c in range(nchunk):
            obuf[slot, pl.ds(c, MOE_ROWS, stride=nchunk), :] = y[:, c * LANE:(c + 1) * LANE]
        start_scatters(nb, slot)

    @pl.when(nb == nblk - 1)
    def _():
        @pl.when(nb >= 1)
        def _():
            @pl.when(cnt_ref[nb - 1] > 0)
            def _():
                wait_scatters(nb - 1, 1 - slot)

        @pl.when(cnt > 0)
        def _():
            wait_scatters(nb, slot)


def _ple_kernel(x_ref, p_ref, gple_ref, wpg_ref, wple_ref, o_ref):
    tm = o_ref.shape[0]
    nchunk = o_ref.shape[1] // LANE
    emb = jnp.dot(p_ref[...].astype(BF16), wple_ref[...], preferred_element_type=F32)
    rows = tm // PLE_ROW_GROUPS
    for r0 in range(0, tm, rows):
        x = jnp.concatenate([x_ref[pl.ds(r0 * nchunk + c, rows, stride=nchunk), :] for c in range(nchunk)], axis=1)
        hp = (x * _rms_scale(x, x.shape[-1]) * gple_ref[...]).astype(BF16)
        gate = jax.nn.sigmoid(jnp.dot(hp, wpg_ref[...], preferred_element_type=F32))
        o_ref[r0:r0 + rows, :] = x + gate * emb[r0:r0 + rows]


def _const(shape):
    nd = len(shape)
    return pl.BlockSpec(shape, lambda *_: (0,) * nd)


def _head_slab_cols(w, width, offset=0):
    k = w.shape[0]
    w = w.reshape(k, N_HEADS, width)
    w = jnp.pad(w, ((0, 0), (0, 0), (offset, LANE - width - offset)))
    return w.reshape(k, N_HEADS * LANE)


def _prep_weights(g_mix, w_in, g_cq, w_uq, g_qn, g_qr, g_ckv, w_ukv, g_kn, g_kr, w_oa,
                  conv_w, w_oc, w_o, g_ffn, w_rg, b_rg, w_re, b_re, w1, w3, w2, g_ple, w_pg, w_ple):
    d = w_in.shape[0]
    n_mla = Q_LORA + KV_LORA
    kr_cols = jnp.pad(w_in[:, n_mla:n_mla + QK_ROPE], ((0, 0), (QK_NOPE, LANE - QK_NOPE - QK_ROPE)))
    w = {}
    w["wa"] = jnp.concatenate([w_in[:, :n_mla], kr_cols], axis=1).astype(BF16)
    w["wb"] = w_in[:, n_mla + QK_ROPE:].astype(BF16)
    w["wuq"] = _head_slab_cols(w_uq, QK_NOPE + QK_ROPE).astype(BF16)
    ukv = w_ukv.reshape(KV_LORA, N_HEADS, QK_NOPE + V_DIM)
    w["wuk"] = _head_slab_cols(ukv[:, :, :QK_NOPE].reshape(KV_LORA, -1), QK_NOPE).astype(BF16)
    w["wuv"] = _head_slab_cols(ukv[:, :, QK_NOPE:].reshape(KV_LORA, -1), V_DIM).astype(BF16)
    pad_hi = LANE - QK_NOPE - QK_ROPE
    w["gq"] = (jnp.pad(jnp.concatenate([g_qn, g_qr]), (0, pad_hi)) * (ATTN_SCALE * LOG2E))[None]
    w["gk"] = jnp.pad(g_kn, (0, LANE - QK_NOPE))[None]
    w["gkr"] = jnp.pad(g_kr, (QK_NOPE, pad_hi))[None]
    w["gmix"], w["gcq"], w["gckv"] = g_mix[None], g_cq[None], g_ckv[None]
    w["gffn"], w["gple"] = g_ffn[None], g_ple[None]
    w["convw"] = jnp.pad(conv_w, ((0, 8 - conv_w.shape[0]), (0, 0)))
    w["woa"], w["woc"], w["wo"] = w_oa.astype(BF16), w_oc.astype(BF16), w_o.astype(BF16)
    n_r = N_GROUPS + N_EXPERTS
    w["wr"] = jnp.pad(jnp.concatenate([w_rg, w_re], axis=1), ((0, 0), (0, LANE - n_r))).astype(BF16)
    w["br"] = jnp.pad(jnp.concatenate([b_rg, b_re]), (0, LANE - n_r))[None]
    pad_g = EXPERTS_PER_GROUP - N_GROUPS
    wrt = jnp.concatenate([jnp.pad(w_rg.T, ((0, pad_g), (0, 0))), w_re.T], axis=0)
    wrt_hi = wrt.astype(BF16)
    w["wrt"] = jnp.concatenate([wrt_hi, (wrt - wrt_hi.astype(F32)).astype(BF16)], axis=0)
    w["brt"] = jnp.concatenate([b_rg, jnp.full((pad_g,), NEG_BIG, F32), b_re])[:, None]
    w["w13"] = jnp.concatenate([w1, w3], axis=2).astype(BF16)
    w["w2"] = w2.astype(BF16)
    w["wpg"], w["wple"] = w_pg.astype(BF16), w_ple.astype(BF16)
    return w


def _rope_slabs(pos):
    inv = 1.0 / (ROPE_THETA ** (jnp.arange(0, QK_ROPE, 2, dtype=F32) / QK_ROPE))
    ang = pos.astype(F32)[:, None] * inv[None, :]
    cos, sin = jnp.cos(ang), jnp.sin(ang)
    n = pos.shape[0]
    half = QK_ROPE // 2
    z = lambda k: jnp.zeros((n, k), F32)
    pad_hi = LANE - QK_NOPE - QK_ROPE
    rc = jnp.concatenate([jnp.ones((n, QK_NOPE), F32), cos, cos, z(pad_hi)], axis=1)
    rs1 = jnp.concatenate([z(QK_NOPE), -sin, z(half), z(pad_hi)], axis=1)
    rs2 = jnp.concatenate([z(QK_NOPE), z(half), sin, z(pad_hi)], axis=1)
    return (rc, rs1, rs2), cos.T, sin.T


def _params(sem):
    return pltpu.CompilerParams(dimension_semantics=sem, vmem_limit_bytes=VMEM_LIMIT)


def _mla_pre(x, w, rope, tm):
    b, s, d = x.shape
    hw = N_HEADS * LANE
    tok = lambda width: pl.BlockSpec((1, tm, width), lambda i, j: (i, j, 0))
    rope_spec = pl.BlockSpec((tm, LANE), lambda i, j: (j, 0))
    consts = [w["gmix"], w["wa"], w["gcq"], w["gckv"], w["gkr"], w["wuq"], w["wuk"], w["wuv"], w["gq"], w["gk"]]
    return pl.pallas_call(
        _pre_kernel,
        grid=(b, s // tm),
        in_specs=[tok(d)] + [_const(c.shape) for c in consts] + [rope_spec] * 3,
        out_specs=[tok(hw), tok(hw), tok(hw), tok(KV_LORA), tok(QK_ROPE)],
        out_shape=[jax.ShapeDtypeStruct((b, s, hw), BF16)] * 3
        + [jax.ShapeDtypeStruct((b, s, KV_LORA), F32), jax.ShapeDtypeStruct((b, s, QK_ROPE), F32)],
        compiler_params=_params(("parallel", "parallel")),
        name="mla_pre",
    )(x, *consts, *rope)


def _mla_pre_t(x, w, rope, cos_t, sin_t, tm):
    b, s, d = x.shape
    hw = N_HEADS * LANE
    tok = lambda width: pl.BlockSpec((1, tm, width), lambda i, j: (i, j, 0))
    tiled = pl.BlockSpec((1, 1, hw, tm), lambda i, j: (i, j, 0, 0))
    rope_spec = pl.BlockSpec((tm, LANE), lambda i, j: (j, 0))
    rope_t_spec = pl.BlockSpec((QK_ROPE // 2, tm), lambda i, j: (0, j))
    gqt = jnp.broadcast_to(w["gq"].T, (LANE, tm))
    consts = [w["gmix"], w["wa"], w["gcq"], w["gckv"], w["gkr"], w["wuq"].T, w["wuk"], w["wuv"].T, gqt, w["gk"]]
    return pl.pallas_call(
        _pre_kernel_t,
        grid=(b, s // tm),
        in_specs=[tok(d)] + [_const(c.shape) for c in consts] + [rope_spec] * 3 + [rope_t_spec] * 2,
        out_specs=[tiled, tok(hw), tiled, tok(KV_LORA), tok(QK_ROPE)],
        out_shape=[jax.ShapeDtypeStruct((b, s // tm, hw, tm), BF16), jax.ShapeDtypeStruct((b, s, hw), BF16),
                   jax.ShapeDtypeStruct((b, s // tm, hw, tm), BF16),
                   jax.ShapeDtypeStruct((b, s, KV_LORA), F32), jax.ShapeDtypeStruct((b, s, QK_ROPE), F32)],
        compiler_params=_params(("parallel", "parallel")),
        name="mla_pre_t",
    )(x, *consts, *rope, cos_t, sin_t)


def _kv_past(past_lat, past_kpe, w, tm):
    b, t, _ = past_lat.shape
    hw = N_HEADS * LANE
    kpe_slab = jnp.pad(past_kpe, ((0, 0), (0, 0), (QK_NOPE, LANE - QK_NOPE - QK_ROPE)))
    tok = lambda width: pl.BlockSpec((1, tm, width), lambda i, j: (i, j, 0))
    consts = [w["wuk"], w["wuv"], w["gk"]]
    return pl.pallas_call(
        _kvpast_kernel,
        grid=(b, t // tm),
        in_specs=[tok(KV_LORA), tok(LANE)] + [_const(c.shape) for c in consts],
        out_specs=[tok(hw), tok(hw)],
        out_shape=[jax.ShapeDtypeStruct((b, t, hw), BF16)] * 2,
        compiler_params=_params(("parallel", "parallel")),
        name="kv_past",
    )(past_lat, kpe_slab, *consts)


def _attn_prompt(qt, k, vt):
    b, n_tiles, _, tq = qt.shape
    s = n_tiles * tq
    nh = ATTN_HEADS_PER_STEP
    width = nh * LANE
    return pl.pallas_call(
        functools.partial(_attn_prompt_kernel, tq=tq, nh=nh),
        grid=(b, N_HEADS // nh, n_tiles),
        in_specs=[pl.BlockSpec((1, 1, width, tq), lambda i, h, j: (i, j, h, 0)),
                  pl.BlockSpec((1, s, width), lambda i, h, j: (i, 0, h)),
                  pl.BlockSpec((1, n_tiles, width, tq), lambda i, h, j: (i, 0, h, 0))],
        out_specs=pl.BlockSpec((1, tq, nh * V_DIM), lambda i, h, j: (i, j, h)),
        out_shape=jax.ShapeDtypeStruct((b, s, N_HEADS * V_DIM), BF16),
        scratch_shapes=[pltpu.VMEM((tq, tq), F32)] * (2 * nh) + [pltpu.VMEM((tq, tq), BF16)] * (2 * nh),
        compiler_params=_params(("parallel", "parallel", "arbitrary")),
        name="attn_prompt",
    )(qt, k, vt)


def _attn_sample(q, kp, vp, kn, vn):
    b, s, _ = q.shape
    t = kp.shape[1]
    hw = N_HEADS * LANE
    blk = lambda rows: pl.BlockSpec((1, rows, hw), lambda i: (i, 0, 0))
    return pl.pallas_call(
        _attn_sample_kernel,
        grid=(b,),
        in_specs=[blk(s), blk(t), blk(t), blk(s), blk(s)],
        out_specs=pl.BlockSpec((1, s, N_HEADS * V_DIM), lambda i: (i, 0, 0)),
        out_shape=jax.ShapeDtypeStruct((b, s, N_HEADS * V_DIM), BF16),
        compiler_params=_params(("parallel",)),
        name="attn_sample",
    )(q, kp, vp, kn, vn)


def _post(x, attn, conv_init, cnt0, x1_all, row_off, n_all, w, tm, seq=None):
    b, s, d = x.shape
    nchunk = d // LANE
    n_keys = cnt0.shape[0]
    tiles_per_b = s // tm
    off = row_off // tm
    tok = lambda width: pl.BlockSpec((1, tm, width), lambda i, j: (i, j, 0))
    if seq is None:
        cin_spec = pl.BlockSpec((1, SUBLANE, CONV_DIM), lambda i, j: (i, 0, 0))
        cnew_spec = pl.BlockSpec((1, 2, CONV_DIM), lambda i, j: (i, 0, 0))
        cnew_shape = (b, 2, CONV_DIM)
    else:
        cin_spec = pl.BlockSpec((1, 2 * tm, CONV_DIM), lambda i, j: (j, 0, 0))
        cnew_spec = pl.BlockSpec((1, tm // seq, 2, CONV_DIM), lambda i, j: (j, 0, 0, 0))
        cnew_shape = (tiles_per_b, tm // seq, 2, CONV_DIM)
    hm = tm // 2
    tri = (jnp.arange(hm)[:, None] < jnp.arange(hm)[None, :]).astype(BF16)
    consts = [cnt0, w["gmix"], w["wb"], w["convw"], w["woa"], w["woc"], w["wo"], w["gffn"], w["wrt"], w["brt"], tri]
    in_specs = ([tok(d), tok(N_HEADS * V_DIM), cin_spec] + [_const(c.shape) for c in consts]
                + [pl.BlockSpec(memory_space=pl.ANY)])
    args = [x, attn, conv_init] + consts + [x1_all]
    return pl.pallas_call(
        functools.partial(_post_kernel, seq=seq),
        grid=(b, tiles_per_b),
        in_specs=in_specs,
        out_specs=[pl.BlockSpec((tm * nchunk, LANE), lambda i, j: (off + i * tiles_per_b + j, 0)),
                   pl.BlockSpec((1, 8, tm), lambda i, j: (i * tiles_per_b + j, 0, 0)),
                   cnew_spec, _const((n_keys, 1))],
        out_shape=[jax.ShapeDtypeStruct((n_all * nchunk, LANE), F32),
                   jax.ShapeDtypeStruct((b * tiles_per_b, 8, tm), F32),
                   jax.ShapeDtypeStruct(cnew_shape, F32), jax.ShapeDtypeStruct((n_keys, 1), F32)],
        scratch_shapes=[pltpu.VMEM((SUBLANE, CONV_DIM), F32)],
        input_output_aliases={len(args) - 1: 0},
        compiler_params=_params(("arbitrary", "arbitrary")),
        name="post",
    )(*args)


def _route_tables(key, rank, counts, n):
    n_keys = counts.shape[0]
    padded = (counts + MOE_ROWS - 1) // MOE_ROWS * MOE_ROWS
    pend = jnp.cumsum(padded)
    pstart = pend - padded
    ids = jnp.arange(n_keys, dtype=jnp.int32)
    dest = rank + jnp.sum(jnp.where(key[:, None] == ids[None, :], pstart[None, :], 0), axis=1)
    nblk = n // MOE_ROWS + N_PAIR_BUCKETS
    blk_start = jnp.arange(nblk, dtype=jnp.int32) * MOE_ROWS
    blk_hot = (blk_start[:, None] >= pstart[None, :]) & (blk_start[:, None] < pend[None, :])
    blk_key = jnp.sum(jnp.where(blk_hot, ids[None, :], 0), axis=1)
    blk_cnt = jnp.sum(jnp.where(blk_hot, jnp.minimum(counts[None, :] - (blk_start[:, None] - pstart[None, :]),
                                                      MOE_ROWS), 0), axis=1)
    any_hot = jnp.any(blk_hot, axis=1)
    blk_key = jnp.where(any_hot, blk_key, n_keys - 1)
    blk_lo = blk_key // EXPERTS_PER_GROUP
    blk_hi = blk_lo // EXPERTS_PER_GROUP * EXPERTS_PER_GROUP + blk_key % EXPERTS_PER_GROUP
    return blk_lo, blk_hi, blk_cnt.astype(jnp.int32), dest.astype(jnp.int32)


def _moe(x1_all, key, rank, counts, w):
    rows, _ = x1_all.shape
    d = w["gffn"].shape[1]
    nchunk = d // LANE
    n = rows // nchunk
    blk_lo, blk_hi, blk_cnt, dest = _route_tables(key, rank, counts, n)
    nblk = blk_lo.shape[0]
    w13_spec = lambda ref_idx: pl.BlockSpec((1, d, 2 * EXPERT_HIDDEN),
                                            lambda i, lo, hi, cnt, dst: ((lo, hi)[ref_idx][i], 0, 0))
    w2_spec = lambda ref_idx: pl.BlockSpec((1, EXPERT_HIDDEN, d),
                                           lambda i, lo, hi, cnt, dst: ((lo, hi)[ref_idx][i], 0, 0))
    buf = pltpu.VMEM((2, MOE_ROWS * nchunk, LANE), F32)
    grid_spec = pltpu.PrefetchScalarGridSpec(
        num_scalar_prefetch=4,
        grid=(nblk,),
        in_specs=[pl.BlockSpec(memory_space=pl.ANY),
                  pl.BlockSpec((1, d), lambda i, *_: (0, 0)),
                  pl.BlockSpec(w["wr"].shape, lambda i, *_: (0, 0)),
                  pl.BlockSpec(w["br"].shape, lambda i, *_: (0, 0)),
                  w13_spec(0), w13_spec(1), w2_spec(0), w2_spec(1)],
        out_specs=pl.BlockSpec(memory_space=pl.ANY),
        scratch_shapes=[pltpu.SMEM((nblk * MOE_ROWS,), jnp.int32), buf, buf,
                        pltpu.SemaphoreType.DMA((2,)), pltpu.SemaphoreType.DMA((2,))],
    )
    return pl.pallas_call(
        _moe_kernel,
        grid_spec=grid_spec,
        out_shape=jax.ShapeDtypeStruct((rows, LANE), F32),
        compiler_params=_params(("arbitrary",)),
        name="moe",
    )(blk_lo, blk_hi, blk_cnt, dest, x1_all, w["gffn"], w["wr"], w["br"], w["w13"], w["w13"], w["w2"], w["w2"])


def _ple(x2_all, row_off, p, w, tm):
    n, pd = p.shape
    d = w["gple"].shape[1]
    nchunk = d // LANE
    off = row_off // tm
    consts = [w["gple"], w["wpg"], w["wple"]]
    return pl.pallas_call(
        _ple_kernel,
        grid=(n // tm,),
        in_specs=[pl.BlockSpec((tm * nchunk, LANE), lambda i: (off + i, 0)), pl.BlockSpec((tm, pd), lambda i: (i, 0))]
        + [_const(c.shape) for c in consts],
        out_specs=pl.BlockSpec((tm, d), lambda i: (i, 0)),
        out_shape=jax.ShapeDtypeStruct((n, d), F32),
        compiler_params=_params(("parallel",)),
        name="ple",
    )(x2_all, p, *consts)


def _layer(xp, xs, pp, ps, past_lat, past_kpe, past_conv, w):
    bp, sp, d = xp.shape
    bs, ss, _ = xs.shape
    n_p, n_s = bp * sp, bs * ss
    n_all = n_p + n_s
    n_keys = N_EXPERTS * EXPERTS_PER_GROUP
    past_len = past_lat.shape[1]
    tm_p, tm_s = min(TOKEN_TILE, sp), min(TOKEN_TILE, n_s)

    rope, cos_t, sin_t = _rope_slabs(jnp.arange(sp))
    qt, k, vt, lat_p, kpe_p = _mla_pre_t(xp, w, rope, cos_t, sin_t, tm_p)
    attn_p = _attn_prompt(qt, k, vt)
    cinit_p = jnp.zeros((bp, SUBLANE, CONV_DIM), F32)

    rope, _, _ = _rope_slabs(past_len + jnp.arange(n_s) % ss)
    xs_rows = xs.reshape(1, n_s, d)
    q, k, v, lat_s, kpe_s = _mla_pre(xs_rows, w, rope, tm_s)
    by_seq = lambda a: a.reshape(bs, ss, a.shape[-1])
    kp, vp = _kv_past(past_lat, past_kpe, w, min(TOKEN_TILE, past_len))
    attn_s = _attn_sample(by_seq(q), kp, vp, by_seq(k), by_seq(v)).reshape(1, n_s, -1)
    lat_s, kpe_s = by_seq(lat_s), by_seq(kpe_s)
    in_tiles = lambda a: a.reshape(n_s // tm_s, tm_s, CONV_DIM)
    n_hist = past_conv.shape[1]
    cinit_s = jnp.concatenate(
        [in_tiles(jnp.pad(past_conv[:, n_hist - 1:], ((0, 0), (0, ss - 1), (0, 0)))),
         in_tiles(jnp.pad(past_conv, ((0, 0), (0, ss - n_hist), (0, 0))))], axis=1)

    x1_all = jnp.zeros((n_all * (d // LANE), LANE), F32)
    x1_all, route_p, conv_p, cnt = _post(xp, attn_p, cinit_p, jnp.zeros((n_keys, 1), F32), x1_all, 0, n_all, w, tm_p)
    x1_all, route_s, conv_s, cnt = _post(xs_rows, attn_s, cinit_s, cnt, x1_all, n_p, n_all, w, tm_s, seq=ss)
    conv_s = conv_s.reshape(bs, n_hist, CONV_DIM)
    key = jnp.concatenate([route_p[:, 0].reshape(-1), route_s[:, 0].reshape(-1)]).astype(jnp.int32)
    rank = jnp.concatenate([route_p[:, 1].reshape(-1), route_s[:, 1].reshape(-1)]).astype(jnp.int32)
    x2_all = _moe(x1_all, key, rank, cnt[:, 0].astype(jnp.int32), w)
    yp = _ple(x2_all, 0, pp.reshape(n_p, -1), w, min(TOKEN_TILE, n_p)).reshape(bp, sp, d)
    ys = _ple(x2_all, n_p, ps.reshape(n_s, -1), w, min(TOKEN_TILE, n_s)).reshape(bs, ss, d)
    return yp, ys, (lat_p, kpe_p, conv_p, lat_s, kpe_s, conv_s)


def kernel(x_prompt, x_sample, cache_kv_latent, cache_k_rope, state_conv, p_prompt, p_sample,
           g_mix, w_in, g_cq, w_uq, g_qn, g_qr, g_ckv, w_ukv, g_kn, g_kr, w_oa,
           conv_w, w_oc, w_o, g_ffn, w_rg, b_rg, w_re, b_re, w1, w3, w2, g_ple, w_pg, w_ple):
    depth = g_mix.shape[0]
    xp, xs = x_prompt, x_sample
    outs = [[] for _ in range(6)]
    for i in range(depth):
        w = _prep_weights(g_mix[i], w_in[i], g_cq[i], w_uq[i], g_qn[i], g_qr[i], g_ckv[i], w_ukv[i],
                          g_kn[i], g_kr[i], w_oa[i], conv_w[i], w_oc[i], w_o[i], g_ffn[i], w_rg[i], b_rg[i],
                          w_re[i], b_re[i], w1[i], w3[i], w2[i], g_ple[i], w_pg[i], w_ple[i])
        xp, xs, new = _layer(xp, xs, p_prompt[i], p_sample[i], cache_kv_latent[i], cache_k_rope[i], state_conv[i], w)
        for o, a in zip(outs, new):
            o.append(a)
    return (xp, xs) + tuple(jnp.stack(o, axis=0) for o in outs)
```

```python
import functools
import math

import jax
import jax.numpy as jnp
from jax import lax
from jax.experimental import pallas as pl
from jax.experimental.pallas import tpu as pltpu

F32 = jnp.float32
BF16 = jnp.bfloat16

LANE = 128
SUBLANE = 8
TOKEN_TILE = 512
CHUNK = 64
N_HEADS = 8
QK_NOPE = 64
QK_ROPE = 32
V_DIM = 64
Q_LORA = 256
KV_LORA = 256
CONV_DIM = 512
N_GROUPS = 4
EXPERTS_PER_GROUP = 8
N_EXPERTS = N_GROUPS * EXPERTS_PER_GROUP
EXPERT_HIDDEN = 256
ROPE_THETA = 10000.0
EPS = 1e-6
ATTN_SCALE = (QK_NOPE + QK_ROPE) ** -0.5
LOG2E = math.log2(math.e)
NEG_BIG = -1e30
MOE_ROWS = 128
ATTN_HEADS_PER_STEP = 4
PLE_ROW_GROUPS = 2
N_PAIR_BUCKETS = N_GROUPS * (EXPERTS_PER_GROUP * (EXPERTS_PER_GROUP - 1) // 2)
VMEM_LIMIT = 56 * 1024 * 1024


def _rms_scale(x, n):
    return lax.rsqrt(jnp.sum(x * x, axis=-1, keepdims=True) * (1.0 / n) + EPS)


def _lane_iota(shape):
    return lax.broadcasted_iota(jnp.int32, shape, len(shape) - 1)


def _rope(t, rc, rs1, rs2):
    return t * rc + pltpu.roll(t, LANE - QK_ROPE // 2, 1) * rs1 + pltpu.roll(t, QK_ROPE // 2, 1) * rs2


def _pre_latents(x_ref, gmix_ref, wa_ref, gcq_ref, gckv_ref, gkr_ref, rope_refs, lat_ref, kpe_ref):
    x = x_ref[0]
    h = x * _rms_scale(x, x.shape[-1]) * gmix_ref[...]
    z = jnp.dot(h.astype(BF16), wa_ref[...], preferred_element_type=F32)
    cq = z[:, :Q_LORA]
    ckv = z[:, Q_LORA:Q_LORA + KV_LORA]
    kr = z[:, Q_LORA + KV_LORA:]
    cqn = cq * _rms_scale(cq, Q_LORA) * gcq_ref[...]
    lat = ckv * _rms_scale(ckv, KV_LORA) * gckv_ref[...]
    lat_ref[0] = lat
    krn = kr * _rms_scale(kr, QK_ROPE) * gkr_ref[...]
    kpe = _rope(krn, *(r[...] for r in rope_refs))
    kpe_ref[0] = kpe[:, QK_NOPE:QK_NOPE + QK_ROPE]
    return cqn, lat, kpe


def _store_keys(k_ref, kf, gk, kpe):
    for hd in range(N_HEADS):
        sl = slice(hd * LANE, (hd + 1) * LANE)
        ks = kf[:, sl]
        k_ref[0, :, sl] = (ks * _rms_scale(ks, QK_NOPE) * gk + kpe).astype(BF16)


def _pre_kernel(x_ref, gmix_ref, wa_ref, gcq_ref, gckv_ref, gkr_ref, wuq_ref, wuk_ref, wuv_ref,
                gq_ref, gk_ref, rc_ref, rs1_ref, rs2_ref,
                q_ref, k_ref, v_ref, lat_ref, kpe_ref):
    rope_refs = (rc_ref, rs1_ref, rs2_ref)
    cqn, lat, kpe = _pre_latents(x_ref, gmix_ref, wa_ref, gcq_ref, gckv_ref, gkr_ref, rope_refs, lat_ref, kpe_ref)
    rc, rs1, rs2 = (r[...] for r in rope_refs)
    is_nope = _lane_iota(kpe.shape) < QK_NOPE
    latb = lat.astype(BF16)
    qf = jnp.dot(cqn.astype(BF16), wuq_ref[...], preferred_element_type=F32)
    vf = jnp.dot(latb, wuv_ref[...], preferred_element_type=F32)
    _store_keys(k_ref, jnp.dot(latb, wuk_ref[...], preferred_element_type=F32), gk_ref[...], kpe)
    gq = gq_ref[...]
    ones_hi = jnp.where(is_nope, 0.0, 1.0)
    for hd in range(N_HEADS):
        sl = slice(hd * LANE, (hd + 1) * LANE)
        qs = qf[:, sl]
        sq = qs * qs
        ss_all = jnp.sum(sq, axis=-1, keepdims=True)
        ss_n = jnp.sum(jnp.where(is_nope, sq, 0.0), axis=-1, keepdims=True)
        r = jnp.where(is_nope, lax.rsqrt(ss_n * (1.0 / QK_NOPE) + EPS),
                      lax.rsqrt((ss_all - ss_n) * (1.0 / QK_ROPE) + EPS))
        q_ref[0, :, sl] = _rope(qs * r * gq, rc, rs1, rs2).astype(BF16)
        v_ref[0, :, sl] = (vf[:, sl] + ones_hi).astype(BF16)


def _pre_kernel_t(x_ref, gmix_ref, wa_ref, gcq_ref, gckv_ref, gkr_ref, wuqt_ref, wuk_ref, wuvt_ref,
                  gqt_ref, gk_ref, rc_ref, rs1_ref, rs2_ref, cos_ref, sin_ref,
                  qt_ref, k_ref, vt_ref, lat_ref, kpe_ref):
    rope_refs = (rc_ref, rs1_ref, rs2_ref)
    cqn, lat, kpe = _pre_latents(x_ref, gmix_ref, wa_ref, gcq_ref, gckv_ref, gkr_ref, rope_refs, lat_ref, kpe_ref)
    latb = lat.astype(BF16)
    _store_keys(k_ref, jnp.dot(latb, wuk_ref[...], preferred_element_type=F32), gk_ref[...], kpe)
    tm = cqn.shape[0]
    qft = jnp.dot(wuqt_ref[...], cqn.T.astype(BF16), preferred_element_type=F32)
    vft = jnp.dot(wuvt_ref[...], lat.T.astype(BF16), preferred_element_type=F32)
    gq, cos, sin = gqt_ref[...], cos_ref[...], sin_ref[...]
    half = QK_ROPE // 2
    ones_lo = jnp.where(lax.broadcasted_iota(jnp.int32, (LANE, tm), 0) < V_DIM, 0.0, 1.0)
    pad = jnp.zeros((LANE - QK_NOPE - QK_ROPE, tm), F32)
    for hd in range(N_HEADS):
        rows = slice(hd * LANE, (hd + 1) * LANE)
        qs = qft[rows]
        sq = qs * qs
        r_n = lax.rsqrt(jnp.sum(sq[:QK_NOPE], axis=0, keepdims=True) * (1.0 / QK_NOPE) + EPS)
        r_p = lax.rsqrt(jnp.sum(sq[QK_NOPE:QK_NOPE + QK_ROPE], axis=0, keepdims=True) * (1.0 / QK_ROPE) + EPS)
        nope = qs[:QK_NOPE] * r_n * gq[:QK_NOPE]
        x1 = qs[QK_NOPE:QK_NOPE + half] * r_p * gq[QK_NOPE:QK_NOPE + half]
        x2 = qs[QK_NOPE + half:QK_NOPE + QK_ROPE] * r_p * gq[QK_NOPE + half:QK_NOPE + QK_ROPE]
        slab = jnp.concatenate([nope, x1 * cos - x2 * sin, x1 * sin + x2 * cos, pad], axis=0)
        qt_ref[0, 0, rows, :] = slab.astype(BF16)
        vt_ref[0, 0, rows, :] = (vft[rows] + ones_lo).astype(BF16)


def _kvpast_kernel(lat_ref, kpe_ref, wuk_ref, wuv_ref, gk_ref, k_ref, v_ref):
    hm = lat_ref.shape[1] // 2
    groups = (0, hm)
    lats = [lat_ref[0, r0:r0 + hm, :].astype(BF16) for r0 in groups]
    kfs = [jnp.dot(latb, wuk_ref[...], preferred_element_type=F32) for latb in lats]
    vfs = [jnp.dot(latb, wuv_ref[...], preferred_element_type=F32) for latb in lats]
    ones_hi = jnp.where(_lane_iota((hm, LANE)) < QK_NOPE, 0.0, 1.0)
    gk = gk_ref[...]
    for r0, kf, vf in zip(groups, kfs, vfs):
        kpe = kpe_ref[0, r0:r0 + hm, :]
        for hd in range(N_HEADS):
            sl = slice(hd * LANE, (hd + 1) * LANE)
            ks = kf[:, sl]
            k_ref[0, r0:r0 + hm, sl] = (ks * _rms_scale(ks, QK_NOPE) * gk + kpe).astype(BF16)
            v_ref[0, r0:r0 + hm, sl] = (vf[:, sl] + ones_hi).astype(BF16)


def _finish_pair(accs):
    outs = [a / pltpu.roll(a, V_DIM, 1) for a in accs]
    lane = _lane_iota(outs[0].shape)
    return jnp.where(lane < V_DIM, outs[0], pltpu.roll(outs[1], V_DIM, 1))


def _attn_prompt_kernel(qt_ref, k_ref, vt_ref, o_ref, *scratch, tq, nh):
    s_refs = (scratch[:nh], scratch[nh:2 * nh])
    p_refs = (scratch[2 * nh:3 * nh], scratch[3 * nh:])
    qi = pl.program_id(2)
    key_pos = lax.broadcasted_iota(jnp.int32, (tq, tq), 0)
    query_pos = lax.broadcasted_iota(jnp.int32, (tq, tq), 1)
    diag_mask = (key_pos // CHUNK) <= (query_pos // CHUNK)
    heads = [slice(hh * LANE, (hh + 1) * LANE) for hh in range(nh)]

    def scores(i, slot, hh, mask):
        start = pl.multiple_of(i * tq, tq)
        s = jnp.dot(k_ref[0, pl.ds(start, tq), heads[hh]], qt_ref[0, 0, heads[hh], :], preferred_element_type=F32)
        if mask is not None:
            s = jnp.where(mask, s, NEG_BIG)
        s_refs[slot][hh][...] = s
        return jnp.max(s, axis=0, keepdims=True)

    def softmax(slot, hh, m, tile_max):
        m_new = jnp.maximum(m, tile_max)
        p_refs[slot][hh][...] = jnp.exp2(s_refs[slot][hh][...] - m_new).astype(BF16)
        return m_new, jnp.exp2(m - m_new)

    def accumulate(i, slot, hh, alpha, acc):
        pv = jnp.dot(vt_ref[0, jnp.maximum(i, 0), heads[hh], :], p_refs[slot][hh][...],
                     preferred_element_type=F32)
        return acc * alpha + pv

    def iteration(i, slot, carry, next_mask=None):
        stats = [softmax(slot, hh, m, tile_max) for hh, (m, tile_max, _, _) in enumerate(carry)]
        next_max = [scores(i + 1, 1 - slot, hh, next_mask) for hh in range(nh)]
        accs = [accumulate(i - 1, 1 - slot, hh, alpha, acc) for hh, (_, _, alpha, acc) in enumerate(carry)]
        return tuple((m, tmax, alpha, acc) for (m, alpha), tmax, acc in zip(stats, next_max, accs))

    def last(slot, carry):
        accs = []
        for hh, (m, tile_max, alpha, acc) in enumerate(carry):
            acc = accumulate(qi - 1, 1 - slot, hh, alpha, acc)
            m, alpha = softmax(slot, hh, m, tile_max)
            accs.append(accumulate(qi, slot, hh, alpha, acc))
        return tuple(accs)

    first_mask = diag_mask | (qi > 0)
    init = []
    for hh in range(nh):
        p_refs[1][hh][...] = jnp.zeros((tq, tq), BF16)
        init.append((jnp.full((1, tq), NEG_BIG, F32), scores(0, 0, hh, first_mask), jnp.ones((1, tq), F32),
                     jnp.zeros((LANE, tq), F32)))
    carry = lax.fori_loop(0, (qi - 1) // 2, lambda j, c: iteration(2 * j + 1, 1, iteration(2 * j, 0, c)),
                          tuple(init))
    tails = [lambda c: last(0, c),
             lambda c: last(1, iteration(qi - 1, 0, c, diag_mask)),
             lambda c: last(0, iteration(qi - 1, 1, iteration(qi - 2, 0, c), diag_mask))]
    accs = lax.switch(jnp.where(qi == 0, 0, 2 - qi % 2), tails, carry)
    out_t = jnp.concatenate([a[:V_DIM] / a[V_DIM:V_DIM + 1] for a in accs], axis=0)
    o_ref[0] = out_t.T.astype(o_ref.dtype)


def _attn_sample_kernel(q_ref, kp_ref, vp_ref, kn_ref, vn_ref, o_ref):
    nt = (((1,), (1,)), ((), ()))
    slabs = [slice(hd * LANE, (hd + 1) * LANE) for hd in range(N_HEADS)]
    s_past = [lax.dot_general(q_ref[0, :, sl], kp_ref[0, :, sl], nt, preferred_element_type=F32) for sl in slabs]
    s_new = [lax.dot_general(q_ref[0, :, sl], kn_ref[0, :, sl], nt, preferred_element_type=F32) for sl in slabs]
    accs = []
    for sl, sp, sn in zip(slabs, s_past, s_new):
        m = jnp.maximum(jnp.max(sp, axis=-1, keepdims=True), jnp.max(sn, axis=-1, keepdims=True))
        accs.append(jnp.dot(jnp.exp2(sp - m).astype(BF16), vp_ref[0, :, sl], preferred_element_type=F32)
                    + jnp.dot(jnp.exp2(sn - m).astype(BF16), vn_ref[0, :, sl], preferred_element_type=F32))
    for pair in range(N_HEADS // 2):
        o_ref[0, :, pair * LANE:(pair + 1) * LANE] = _finish_pair(accs[2 * pair:2 * pair + 2]).astype(o_ref.dtype)


def _post_kernel(x_ref, attn_ref, cinit_ref, cnt0_ref, gmix_ref, wb_ref, convw_ref, woa_ref, woc_ref, wo_ref,
                 gffn_ref, wrt_ref, brt_ref, tri_ref, x1_all_ref,
                 x1_ref, route_ref, cnew_ref, cnt_ref, carry_ref, *, seq):
    del x1_all_ref
    si = pl.program_id(1)
    tm = x_ref.shape[1]

    if seq is None:
        @pl.when(si == 0)
        def _():
            carry_ref[...] = cinit_ref[0]

    @pl.when((si == 0) & (pl.program_id(0) == 0))
    def _():
        cnt_ref[...] = cnt0_ref[...]

    d = x_ref.shape[-1]
    hm = tm // 2
    groups = (0, hm)
    xs, zs = [], []
    for r0 in groups:
        x = x_ref[0, r0:r0 + hm, :]
        h = x * _rms_scale(x, d) * gmix_ref[...]
        xs.append(x)
        zs.append(jnp.dot(h.astype(BF16), wb_ref[...], preferred_element_type=F32))

    cw = convw_ref[...]
    row = lax.broadcasted_iota(jnp.int32, (hm, CONV_DIM), 0)
    if seq is None:
        c1, c2 = carry_ref[SUBLANE - 1:SUBLANE, :], carry_ref[SUBLANE - 2:SUBLANE - 1, :]
    merged = []
    for r0, z in zip(groups, zs):
        conv_b = z[:, :CONV_DIM]
        u = z[:, CONV_DIM:2 * CONV_DIM] * z[:, 2 * CONV_DIM:3 * CONV_DIM]
        if seq is None:
            u_m1 = jnp.where(row == 0, c1, pltpu.roll(u, 1, 0))
            u_m2 = jnp.where(row == 0, c2, jnp.where(row == 1, c1, pltpu.roll(u, 2, 0)))
            c1, c2 = u[hm - 1:, :], u[hm - 2:hm - 1, :]
            if r0 + hm == tm:
                carry_ref[...] = u[hm - SUBLANE:, :]
                cnew_ref[0] = u[hm - 2:, :]
        else:
            pos = row % seq
            u_m1 = jnp.where(pos == 0, cinit_ref[0, r0:r0 + hm, :], pltpu.roll(u, 1, 0))
            u_m2 = jnp.where(pos <= 1, cinit_ref[0, tm + r0:tm + r0 + hm, :], pltpu.roll(u, 2, 0))
            cnew_ref[0, r0 // seq:(r0 + hm) // seq] = u.reshape(hm // seq, seq, CONV_DIM)[:, seq - 2:, :]
        cv = cw[0:1, :] * u_m2 + cw[1:2, :] * u_m1 + cw[2:3, :] * u
        y_a = jnp.dot(attn_ref[0, r0:r0 + hm, :], woa_ref[...], preferred_element_type=F32)
        y_c = jnp.dot((conv_b * cv).astype(BF16), woc_ref[...], preferred_element_type=F32)
        gate_a = z[:, 3 * CONV_DIM:3 * CONV_DIM + d]
        gate_c = z[:, 3 * CONV_DIM + d:]
        merged.append((jax.nn.sigmoid(gate_a) * y_a + jax.nn.sigmoid(gate_c) * y_c).astype(BF16))
    x1s = [x + jnp.dot(mrg, wo_ref[...], preferred_element_type=F32) for x, mrg in zip(xs, merged)]
    nchunk = d // LANE
    for r0, x1 in zip(groups, x1s):
        for c in range(nchunk):
            x1_ref[pl.ds(r0 * nchunk + c, hm, stride=nchunk), :] = x1[:, c * LANE:(c + 1) * LANE]

    nt = (((1,), (1,)), ((), ()))
    n_rows = brt_ref.shape[0]
    logits = []
    for x1 in x1s:
        h2 = x1 * _rms_scale(x1, d) * gffn_ref[...]
        h2_hi = h2.astype(BF16)
        h2_lo = (h2 - h2_hi.astype(F32)).astype(BF16)
        lt2 = lax.dot_general(wrt_ref[...], h2_hi, nt, preferred_element_type=F32)
        logits.append(lt2[:n_rows] + lt2[n_rows:]
                      + lax.dot_general(wrt_ref[:n_rows, :], h2_lo, nt, preferred_element_type=F32) + brt_ref[...])
    sub = lax.broadcasted_iota(jnp.int32, (EXPERTS_PER_GROUP, hm), 0).astype(F32)
    none = float(EXPERTS_PER_GROUP)

    def first_argmax(v):
        vmax = jnp.max(v, axis=0, keepdims=True)
        return jnp.min(jnp.where(v == vmax, sub, none), axis=0, keepdims=True)

    n_keys = cnt_ref.shape[0]
    keys = lax.broadcasted_iota(jnp.int32, (n_keys, hm), 0).astype(F32)
    row8 = lax.broadcasted_iota(jnp.int32, (8, hm), 0)
    seen = cnt_ref[...]
    for r0, lt in zip(groups, logits):
        g_idx = first_argmax(lt[:EXPERTS_PER_GROUP])
        el = lt[EXPERTS_PER_GROUP * N_GROUPS:]
        for g in range(N_GROUPS - 2, -1, -1):
            el = jnp.where(g_idx == g, lt[EXPERTS_PER_GROUP * (g + 1):EXPERTS_PER_GROUP * (g + 2)], el)
        i1 = first_argmax(el)
        i2 = first_argmax(jnp.where(sub == i1, NEG_BIG, el))
        key = (g_idx * (EXPERTS_PER_GROUP * EXPERTS_PER_GROUP) + jnp.minimum(i1, i2) * EXPERTS_PER_GROUP
               + jnp.maximum(i1, i2))
        onehot = jnp.where(keys == key, 1.0, 0.0)
        before = jnp.dot(onehot.astype(BF16), tri_ref[...], preferred_element_type=F32)
        rank = jnp.sum(onehot * (before + seen), axis=0, keepdims=True)
        seen = seen + jnp.sum(onehot, axis=1, keepdims=True)
        route_ref[0, :, r0:r0 + hm] = jnp.where(row8 == 0, key, jnp.where(row8 == 1, rank, 0.0))
    cnt_ref[...] = seen


def _rows_loop(n, fn, unroll=8):
    def group(j, c):
        for u in range(unroll):
            fn(j * unroll + u)
        return c
    lax.fori_loop(0, n // unroll, group, 0)

    def single(i, c):
        fn(i)
        return c
    lax.fori_loop(n // unroll * unroll, n, single, 0)


def _moe_kernel(lo_ref, hi_ref, cnt_ref, dest_ref,
                x1_hbm, gffn_ref, wr_ref, br_ref, w1lo_ref, w3lo_ref, w2lo_ref, w1hi_ref, w3hi_ref, w2hi_ref,
                x2_hbm, tok_ref, xbuf, obuf, w13_bf, w2_bf, gsem, ssem):
    nb = pl.program_id(0)
    nblk = pl.num_programs(0)
    nchunk = xbuf.shape[1] // MOE_ROWS
    slot = nb % 2
    cnt = cnt_ref[nb]

    def start_gathers(blk, sl):
        base = blk * MOE_ROWS

        def one(i):
            src = x1_hbm.at[pl.ds(tok_ref[base + i] * nchunk, nchunk)]
            pltpu.make_async_copy(src, xbuf.at[sl, pl.ds(i * nchunk, nchunk)], gsem.at[sl]).start()
        _rows_loop(cnt_ref[blk], one)

    def start_scatters(blk, sl):
        base = blk * MOE_ROWS

        def one(i):
            dst = x2_hbm.at[pl.ds(tok_ref[base + i] * nchunk, nchunk)]
            pltpu.make_async_copy(obuf.at[sl, pl.ds(i * nchunk, nchunk)], dst, ssem.at[sl]).start(priority=1)
        _rows_loop(cnt_ref[blk], one)

    def wait_gathers(blk, sl):
        rows = cnt_ref[blk] * nchunk
        pltpu.make_async_copy(x1_hbm.at[pl.ds(0, rows)], xbuf.at[sl, pl.ds(0, rows)], gsem.at[sl]).wait()

    def wait_scatters(blk, sl):
        rows = cnt_ref[blk] * nchunk
        pltpu.make_async_copy(obuf.at[sl, pl.ds(0, rows)], x2_hbm.at[pl.ds(0, rows)], ssem.at[sl]).wait()

    @pl.when(nb == 0)
    def _():
        def invert(i, c):
            tok_ref[dest_ref[i]] = i
            return c
        lax.fori_loop(0, dest_ref.shape[0], invert, 0, unroll=16)
        xbuf[...] = jnp.zeros_like(xbuf)

        @pl.when(cnt > 0)
        def _():
            start_gathers(0, 0)

    @pl.when(nb + 1 < nblk)
    def _():
        @pl.when(cnt_ref[nb + 1] > 0)
        def _():
            start_gathers(nb + 1, 1 - slot)

    @pl.when(nb >= 2)
    def _():
        @pl.when(cnt_ref[nb - 2] > 0)
        def _():
            wait_scatters(nb - 2, slot)

    prev = jnp.maximum(nb - 1, 0)
    for e, (ids, w1_ref, w3_ref, w2_ref) in enumerate(((lo_ref, w1lo_ref, w3lo_ref, w2lo_ref),
                                                        (hi_ref, w1hi_ref, w3hi_ref, w2hi_ref))):
        @pl.when((cnt > 0) & ((nb == 0) | (ids[nb] != ids[prev])))
        def _(e=e, w1_ref=w1_ref, w3_ref=w3_ref, w2_ref=w2_ref):
            w13_bf[e, :, :EXPERT_HIDDEN] = w1_ref[0].astype(BF16)
            w13_bf[e, :, EXPERT_HIDDEN:] = w3_ref[0].astype(BF16)
            w2_bf[e] = w2_ref[0].astype(BF16)

    @pl.when(cnt > 0)
    def _():
        wait_gathers(nb, slot)
        xg = jnp.concatenate([xbuf[slot, pl.ds(c, MOE_ROWS, stride=nchunk), :] for c in range(nchunk)], axis=1)
        h = (xg * _rms_scale(xg, xg.shape[-1]) * gffn_ref[...]).astype(BF16)

        lo, hi = lo_ref[nb], hi_ref[nb]
        logit = jnp.dot(h, wr_ref[...], preferred_element_type=F32) + br_ref[...]
        lane = _lane_iota(logit.shape)
        pick = lambda j: jnp.sum(jnp.where(lane == j, logit, 0.0), axis=-1, keepdims=True)
        is_g = lane < N_GROUPS
        gmax = jnp.max(jnp.where(is_g, logit, NEG_BIG), axis=-1, keepdims=True)
        g_den = jnp.sum(jnp.where(is_g, jnp.exp(logit - gmax), 0.0), axis=-1, keepdims=True)
        g_p = jnp.exp(pick(lo // EXPERTS_PER_GROUP) - gmax) / g_den
        l_lo, l_hi = pick(N_GROUPS + lo), pick(N_GROUPS + hi)
        gates = (g_p * jax.nn.sigmoid(l_lo - l_hi), g_p * jax.nn.sigmoid(l_hi - l_lo))

        abs_ = [jnp.dot(h, w13_bf[e], preferred_element_type=F32) for e in range(2)]
        y = xg
        for e, (ab, gate) in enumerate(zip(abs_, gates)):
            hid = jax.nn.silu(ab[:, :EXPERT_HIDDEN]) * ab[:, EXPERT_HIDDEN:]
            y = y + gate * jnp.dot(hid.astype(BF16), w2_bf[e], preferred_element_type=F32)
        for c in range(nchunk):
            obuf[slot, pl.ds(c, MOE_ROWS, stride=nchunk), :] = y[:, c * LANE:(c + 1) * LANE]
        start_scatters(nb, slot)

    @pl.when(nb == nblk - 1)
    def _():
        @pl.when(nb >= 1)
        def _():
            @pl.when(cnt_ref[nb - 1] > 0)
            def _():
                wait_scatters(nb - 1, 1 - slot)

        @pl.when(cnt > 0)
        def _():
            wait_scatters(nb, slot)


def _ple_kernel(x_ref, p_ref, gple_ref, wpg_ref, wple_ref, o_ref):
    tm = o_ref.shape[0]
    nchunk = o_ref.shape[1] // LANE
    emb = jnp.dot(p_ref[...].astype(BF16), wple_ref[...], preferred_element_type=F32)
    rows = tm // PLE_ROW_GROUPS
    for r0 in range(0, tm, rows):
        x = jnp.concatenate([x_ref[pl.ds(r0 * nchunk + c, rows, stride=nchunk), :] for c in range(nchunk)], axis=1)
        hp = (x * _rms_scale(x, x.shape[-1]) * gple_ref[...]).astype(BF16)
        gate = jax.nn.sigmoid(jnp.dot(hp, wpg_ref[...], preferred_element_type=F32))
        o_ref[r0:r0 + rows, :] = x + gate * emb[r0:r0 + rows]


def _const(shape):
    nd = len(shape)
    return pl.BlockSpec(shape, lambda *_: (0,) * nd)


def _head_slab_cols(w, width, offset=0):
    k = w.shape[0]
    w = w.reshape(k, N_HEADS, width)
    w = jnp.pad(w, ((0, 0), (0, 0), (offset, LANE - width - offset)))
    return w.reshape(k, N_HEADS * LANE)


def _prep_weights(g_mix, w_in, g_cq, w_uq, g_qn, g_qr, g_ckv, w_ukv, g_kn, g_kr, w_oa,
                  conv_w, w_oc, w_o, g_ffn, w_rg, b_rg, w_re, b_re, w1, w3, w2, g_ple, w_pg, w_ple):
    d = w_in.shape[0]
    n_mla = Q_LORA + KV_LORA
    kr_cols = jnp.pad(w_in[:, n_mla:n_mla + QK_ROPE], ((0, 0), (QK_NOPE, LANE - QK_NOPE - QK_ROPE)))
    w = {}
    w["wa"] = jnp.concatenate([w_in[:, :n_mla], kr_cols], axis=1).astype(BF16)
    w["wb"] = w_in[:, n_mla + QK_ROPE:].astype(BF16)
    w["wuq"] = _head_slab_cols(w_uq, QK_NOPE + QK_ROPE).astype(BF16)
    ukv = w_ukv.reshape(KV_LORA, N_HEADS, QK_NOPE + V_DIM)
    w["wuk"] = _head_slab_cols(ukv[:, :, :QK_NOPE].reshape(KV_LORA, -1), QK_NOPE).astype(BF16)
    w["wuv"] = _head_slab_cols(ukv[:, :, QK_NOPE:].reshape(KV_LORA, -1), V_DIM).astype(BF16)
    pad_hi = LANE - QK_NOPE - QK_ROPE
    w["gq"] = (jnp.pad(jnp.concatenate([g_qn, g_qr]), (0, pad_hi)) * (ATTN_SCALE * LOG2E))[None]
    w["gk"] = jnp.pad(g_kn, (0, LANE - QK_NOPE))[None]
    w["gkr"] = jnp.pad(g_kr, (QK_NOPE, pad_hi))[None]
    w["gmix"], w["gcq"], w["gckv"] = g_mix[None], g_cq[None], g_ckv[None]
    w["gffn"], w["gple"] = g_ffn[None], g_ple[None]
    w["convw"] = jnp.pad(conv_w, ((0, 8 - conv_w.shape[0]), (0, 0)))
    w["woa"], w["woc"], w["wo"] = w_oa.astype(BF16), w_oc.astype(BF16), w_o.astype(BF16)
    n_r = N_GROUPS + N_EXPERTS
    w["wr"] = jnp.pad(jnp.concatenate([w_rg, w_re], axis=1), ((0, 0), (0, LANE - n_r))).astype(BF16)
    w["br"] = jnp.pad(jnp.concatenate([b_rg, b_re]), (0, LANE - n_r))[None]
    pad_g = EXPERTS_PER_GROUP - N_GROUPS
    wrt = jnp.concatenate([jnp.pad(w_rg.T, ((0, pad_g), (0, 0))), w_re.T], axis=0)
    wrt_hi = wrt.astype(BF16)
    w["wrt"] = jnp.concatenate([wrt_hi, (wrt - wrt_hi.astype(F32)).astype(BF16)], axis=0)
    w["brt"] = jnp.concatenate([b_rg, jnp.full((pad_g,), NEG_BIG, F32), b_re])[:, None]
    w["w1"], w["w3"], w["w2"] = w1, w3, w2
    w["wpg"], w["wple"] = w_pg.astype(BF16), w_ple.astype(BF16)
    return w


def _rope_slabs(pos):
    inv = 1.0 / (ROPE_THETA ** (jnp.arange(0, QK_ROPE, 2, dtype=F32) / QK_ROPE))
    ang = pos.astype(F32)[:, None] * inv[None, :]
    cos, sin = jnp.cos(ang), jnp.sin(ang)
    n = pos.shape[0]
    half = QK_ROPE // 2
    z = lambda k: jnp.zeros((n, k), F32)
    pad_hi = LANE - QK_NOPE - QK_ROPE
    rc = jnp.concatenate([jnp.ones((n, QK_NOPE), F32), cos, cos, z(pad_hi)], axis=1)
    rs1 = jnp.concatenate([z(QK_NOPE), -sin, z(half), z(pad_hi)], axis=1)
    rs2 = jnp.concatenate([z(QK_NOPE), z(half), sin, z(pad_hi)], axis=1)
    return (rc, rs1, rs2), cos.T, sin.T


def _params(sem):
    return pltpu.CompilerParams(dimension_semantics=sem, vmem_limit_bytes=VMEM_LIMIT)


def _mla_pre(x, w, rope, tm):
    b, s, d = x.shape
    hw = N_HEADS * LANE
    tok = lambda width: pl.BlockSpec((1, tm, width), lambda i, j: (i, j, 0))
    rope_spec = pl.BlockSpec((tm, LANE), lambda i, j: (j, 0))
    consts = [w["gmix"], w["wa"], w["gcq"], w["gckv"], w["gkr"], w["wuq"], w["wuk"], w["wuv"], w["gq"], w["gk"]]
    return pl.pallas_call(
        _pre_kernel,
        grid=(b, s // tm),
        in_specs=[tok(d)] + [_const(c.shape) for c in consts] + [rope_spec] * 3,
        out_specs=[tok(hw), tok(hw), tok(hw), tok(KV_LORA), tok(QK_ROPE)],
        out_shape=[jax.ShapeDtypeStruct((b, s, hw), BF16)] * 3
        + [jax.ShapeDtypeStruct((b, s, KV_LORA), F32), jax.ShapeDtypeStruct((b, s, QK_ROPE), F32)],
        compiler_params=_params(("parallel", "parallel")),
        name="mla_pre",
    )(x, *consts, *rope)


def _mla_pre_t(x, w, rope, cos_t, sin_t, tm):
    b, s, d = x.shape
    hw = N_HEADS * LANE
    tok = lambda width: pl.BlockSpec((1, tm, width), lambda i, j: (i, j, 0))
    tiled = pl.BlockSpec((1, 1, hw, tm), lambda i, j: (i, j, 0, 0))
    rope_spec = pl.BlockSpec((tm, LANE), lambda i, j: (j, 0))
    rope_t_spec = pl.BlockSpec((QK_ROPE // 2, tm), lambda i, j: (0, j))
    gqt = jnp.broadcast_to(w["gq"].T, (LANE, tm))
    consts = [w["gmix"], w["wa"], w["gcq"], w["gckv"], w["gkr"], w["wuq"].T, w["wuk"], w["wuv"].T, gqt, w["gk"]]
    return pl.pallas_call(
        _pre_kernel_t,
        grid=(b, s // tm),
        in_specs=[tok(d)] + [_const(c.shape) for c in consts] + [rope_spec] * 3 + [rope_t_spec] * 2,
        out_specs=[tiled, tok(hw), tiled, tok(KV_LORA), tok(QK_ROPE)],
        out_shape=[jax.ShapeDtypeStruct((b, s // tm, hw, tm), BF16), jax.ShapeDtypeStruct((b, s, hw), BF16),
                   jax.ShapeDtypeStruct((b, s // tm, hw, tm), BF16),
                   jax.ShapeDtypeStruct((b, s, KV_LORA), F32), jax.ShapeDtypeStruct((b, s, QK_ROPE), F32)],
        compiler_params=_params(("parallel", "parallel")),
        name="mla_pre_t",
    )(x, *consts, *rope, cos_t, sin_t)


def _kv_past(past_lat, past_kpe, w, tm):
    b, t, _ = past_lat.shape
    hw = N_HEADS * LANE
    kpe_slab = jnp.pad(past_kpe, ((0, 0), (0, 0), (QK_NOPE, LANE - QK_NOPE - QK_ROPE)))
    tok = lambda width: pl.BlockSpec((1, tm, width), lambda i, j: (i, j, 0))
    consts = [w["wuk"], w["wuv"], w["gk"]]
    return pl.pallas_call(
        _kvpast_kernel,
        grid=(b, t // tm),
        in_specs=[tok(KV_LORA), tok(LANE)] + [_const(c.shape) for c in consts],
        out_specs=[tok(hw), tok(hw)],
        out_shape=[jax.ShapeDtypeStruct((b, t, hw), BF16)] * 2,
        compiler_params=_params(("parallel", "parallel")),
        name="kv_past",
    )(past_lat, kpe_slab, *consts)


def _attn_prompt(qt, k, vt):
    b, n_tiles, _, tq = qt.shape
    s = n_tiles * tq
    nh = ATTN_HEADS_PER_STEP
    width = nh * LANE
    return pl.pallas_call(
        functools.partial(_attn_prompt_kernel, tq=tq, nh=nh),
        grid=(b, N_HEADS // nh, n_tiles),
        in_specs=[pl.BlockSpec((1, 1, width, tq), lambda i, h, j: (i, j, h, 0)),
                  pl.BlockSpec((1, s, width), lambda i, h, j: (i, 0, h)),
                  pl.BlockSpec((1, n_tiles, width, tq), lambda i, h, j: (i, 0, h, 0))],
        out_specs=pl.BlockSpec((1, tq, nh * V_DIM), lambda i, h, j: (i, j, h)),
        out_shape=jax.ShapeDtypeStruct((b, s, N_HEADS * V_DIM), BF16),
        scratch_shapes=[pltpu.VMEM((tq, tq), F32)] * (2 * nh) + [pltpu.VMEM((tq, tq), BF16)] * (2 * nh),
        compiler_params=_params(("parallel", "parallel", "arbitrary")),
        name="attn_prompt",
    )(qt, k, vt)


def _attn_sample(q, kp, vp, kn, vn):
    b, s, _ = q.shape
    t = kp.shape[1]
    hw = N_HEADS * LANE
    blk = lambda rows: pl.BlockSpec((1, rows, hw), lambda i: (i, 0, 0))
    return pl.pallas_call(
        _attn_sample_kernel,
        grid=(b,),
        in_specs=[blk(s), blk(t), blk(t), blk(s), blk(s)],
        out_specs=pl.BlockSpec((1, s, N_HEADS * V_DIM), lambda i: (i, 0, 0)),
        out_shape=jax.ShapeDtypeStruct((b, s, N_HEADS * V_DIM), BF16),
        compiler_params=_params(("parallel",)),
        name="attn_sample",
    )(q, kp, vp, kn, vn)


def _post(x, attn, conv_init, cnt0, x1_all, row_off, n_all, w, tm, seq=None):
    b, s, d = x.shape
    nchunk = d // LANE
    n_keys = cnt0.shape[0]
    tiles_per_b = s // tm
    off = row_off // tm
    tok = lambda width: pl.BlockSpec((1, tm, width), lambda i, j: (i, j, 0))
    if seq is None:
        cin_spec = pl.BlockSpec((1, SUBLANE, CONV_DIM), lambda i, j: (i, 0, 0))
        cnew_spec = pl.BlockSpec((1, 2, CONV_DIM), lambda i, j: (i, 0, 0))
        cnew_shape = (b, 2, CONV_DIM)
    else:
        cin_spec = pl.BlockSpec((1, 2 * tm, CONV_DIM), lambda i, j: (j, 0, 0))
        cnew_spec = pl.BlockSpec((1, tm // seq, 2, CONV_DIM), lambda i, j: (j, 0, 0, 0))
        cnew_shape = (tiles_per_b, tm // seq, 2, CONV_DIM)
    hm = tm // 2
    tri = (jnp.arange(hm)[:, None] < jnp.arange(hm)[None, :]).astype(BF16)
    consts = [cnt0, w["gmix"], w["wb"], w["convw"], w["woa"], w["woc"], w["wo"], w["gffn"], w["wrt"], w["brt"], tri]
    in_specs = ([tok(d), tok(N_HEADS * V_DIM), cin_spec] + [_const(c.shape) for c in consts]
                + [pl.BlockSpec(memory_space=pl.ANY)])
    args = [x, attn, conv_init] + consts + [x1_all]
    return pl.pallas_call(
        functools.partial(_post_kernel, seq=seq),
        grid=(b, tiles_per_b),
        in_specs=in_specs,
        out_specs=[pl.BlockSpec((tm * nchunk, LANE), lambda i, j: (off + i * tiles_per_b + j, 0)),
                   pl.BlockSpec((1, 8, tm), lambda i, j: (i * tiles_per_b + j, 0, 0)),
                   cnew_spec, _const((n_keys, 1))],
        out_shape=[jax.ShapeDtypeStruct((n_all * nchunk, LANE), F32),
                   jax.ShapeDtypeStruct((b * tiles_per_b, 8, tm), F32),
                   jax.ShapeDtypeStruct(cnew_shape, F32), jax.ShapeDtypeStruct((n_keys, 1), F32)],
        scratch_shapes=[pltpu.VMEM((SUBLANE, CONV_DIM), F32)],
        input_output_aliases={len(args) - 1: 0},
        compiler_params=_params(("arbitrary", "arbitrary")),
        name="post",
    )(*args)


def _route_tables(key, rank, counts, n):
    n_keys = counts.shape[0]
    padded = (counts + MOE_ROWS - 1) // MOE_ROWS * MOE_ROWS
    pend = jnp.cumsum(padded)
    pstart = pend - padded
    ids = jnp.arange(n_keys, dtype=jnp.int32)
    dest = rank + jnp.sum(jnp.where(key[:, None] == ids[None, :], pstart[None, :], 0), axis=1)
    nblk = n // MOE_ROWS + N_PAIR_BUCKETS
    blk_start = jnp.arange(nblk, dtype=jnp.int32) * MOE_ROWS
    blk_hot = (blk_start[:, None] >= pstart[None, :]) & (blk_start[:, None] < pend[None, :])
    blk_key = jnp.sum(jnp.where(blk_hot, ids[None, :], 0), axis=1)
    blk_cnt = jnp.sum(jnp.where(blk_hot, jnp.minimum(counts[None, :] - (blk_start[:, None] - pstart[None, :]),
                                                      MOE_ROWS), 0), axis=1)
    any_hot = jnp.any(blk_hot, axis=1)
    blk_key = jnp.where(any_hot, blk_key, n_keys - 1)
    blk_lo = blk_key // EXPERTS_PER_GROUP
    blk_hi = blk_lo // EXPERTS_PER_GROUP * EXPERTS_PER_GROUP + blk_key % EXPERTS_PER_GROUP
    return blk_lo, blk_hi, blk_cnt.astype(jnp.int32), dest.astype(jnp.int32)


def _moe(x1_all, key, rank, counts, w):
    rows, _ = x1_all.shape
    d = w["gffn"].shape[1]
    nchunk = d // LANE
    n = rows // nchunk
    blk_lo, blk_hi, blk_cnt, dest = _route_tables(key, rank, counts, n)
    nblk = blk_lo.shape[0]
    up_spec = lambda ref_idx: pl.BlockSpec((1, d, EXPERT_HIDDEN),
                                           lambda i, lo, hi, cnt, dst: ((lo, hi)[ref_idx][i], 0, 0))
    down_spec = lambda ref_idx: pl.BlockSpec((1, EXPERT_HIDDEN, d),
                                             lambda i, lo, hi, cnt, dst: ((lo, hi)[ref_idx][i], 0, 0))
    expert_specs = [spec(e) for e in range(2) for spec in (up_spec, up_spec, down_spec)]
    buf = pltpu.VMEM((2, MOE_ROWS * nchunk, LANE), F32)
    grid_spec = pltpu.PrefetchScalarGridSpec(
        num_scalar_prefetch=4,
        grid=(nblk,),
        in_specs=[pl.BlockSpec(memory_space=pl.ANY),
                  pl.BlockSpec((1, d), lambda i, *_: (0, 0)),
                  pl.BlockSpec(w["wr"].shape, lambda i, *_: (0, 0)),
                  pl.BlockSpec(w["br"].shape, lambda i, *_: (0, 0)),
                  ] + expert_specs,
        out_specs=pl.BlockSpec(memory_space=pl.ANY),
        scratch_shapes=[pltpu.SMEM((nblk * MOE_ROWS,), jnp.int32), buf, buf,
                        pltpu.VMEM((2, d, 2 * EXPERT_HIDDEN), BF16), pltpu.VMEM((2, EXPERT_HIDDEN, d), BF16),
                        pltpu.SemaphoreType.DMA((2,)), pltpu.SemaphoreType.DMA((2,))],
    )
    return pl.pallas_call(
        _moe_kernel,
        grid_spec=grid_spec,
        out_shape=jax.ShapeDtypeStruct((rows, LANE), F32),
        compiler_params=_params(("arbitrary",)),
        name="moe",
    )(blk_lo, blk_hi, blk_cnt, dest, x1_all, w["gffn"], w["wr"], w["br"], *([w["w1"], w["w3"], w["w2"]] * 2))


def _ple(x2_all, row_off, p, w, tm):
    n, pd = p.shape
    d = w["gple"].shape[1]
    nchunk = d // LANE
    off = row_off // tm
    consts = [w["gple"], w["wpg"], w["wple"]]
    return pl.pallas_call(
        _ple_kernel,
        grid=(n // tm,),
        in_specs=[pl.BlockSpec((tm * nchunk, LANE), lambda i: (off + i, 0)), pl.BlockSpec((tm, pd), lambda i: (i, 0))]
        + [_const(c.shape) for c in consts],
        out_specs=pl.BlockSpec((tm, d), lambda i: (i, 0)),
        out_shape=jax.ShapeDtypeStruct((n, d), F32),
        compiler_params=_params(("parallel",)),
        name="ple",
    )(x2_all, p, *consts)


def _layer(xp, xs, pp, ps, past_lat, past_kpe, past_conv, w):
    bp, sp, d = xp.shape
    bs, ss, _ = xs.shape
    n_p, n_s = bp * sp, bs * ss
    n_all = n_p + n_s
    n_keys = N_EXPERTS * EXPERTS_PER_GROUP
    past_len = past_lat.shape[1]
    tm_p, tm_s = min(TOKEN_TILE, sp), min(TOKEN_TILE, n_s)

    rope, cos_t, sin_t = _rope_slabs(jnp.arange(sp))
    qt, k, vt, lat_p, kpe_p = _mla_pre_t(xp, w, rope, cos_t, sin_t, tm_p)
    attn_p = _attn_prompt(qt, k, vt)
    cinit_p = jnp.zeros((bp, SUBLANE, CONV_DIM), F32)

    rope, _, _ = _rope_slabs(past_len + jnp.arange(n_s) % ss)
    xs_rows = xs.reshape(1, n_s, d)
    q, k, v, lat_s, kpe_s = _mla_pre(xs_rows, w, rope, tm_s)
    by_seq = lambda a: a.reshape(bs, ss, a.shape[-1])
    kp, vp = _kv_past(past_lat, past_kpe, w, min(TOKEN_TILE, past_len))
    attn_s = _attn_sample(by_seq(q), kp, vp, by_seq(k), by_seq(v)).reshape(1, n_s, -1)
    lat_s, kpe_s = by_seq(lat_s), by_seq(kpe_s)
    in_tiles = lambda a: a.reshape(n_s // tm_s, tm_s, CONV_DIM)
    n_hist = past_conv.shape[1]
    cinit_s = jnp.concatenate(
        [in_tiles(jnp.pad(past_conv[:, n_hist - 1:], ((0, 0), (0, ss - 1), (0, 0)))),
         in_tiles(jnp.pad(past_conv, ((0, 0), (0, ss - n_hist), (0, 0))))], axis=1)

    x1_all = jnp.zeros((n_all * (d // LANE), LANE), F32)
    x1_all, route_p, conv_p, cnt = _post(xp, attn_p, cinit_p, jnp.zeros((n_keys, 1), F32), x1_all, 0, n_all, w, tm_p)
    x1_all, route_s, conv_s, cnt = _post(xs_rows, attn_s, cinit_s, cnt, x1_all, n_p, n_all, w, tm_s, seq=ss)
    conv_s = conv_s.reshape(bs, n_hist, CONV_DIM)
    key = jnp.concatenate([route_p[:, 0].reshape(-1), route_s[:, 0].reshape(-1)]).astype(jnp.int32)
    rank = jnp.concatenate([route_p[:, 1].reshape(-1), route_s[:, 1].reshape(-1)]).astype(jnp.int32)
    x2_all = _moe(x1_all, key, rank, cnt[:, 0].astype(jnp.int32), w)
    yp = _ple(x2_all, 0, pp.reshape(n_p, -1), w, min(TOKEN_TILE, n_p)).reshape(bp, sp, d)
    ys = _ple(x2_all, n_p, ps.reshape(n_s, -1), w, min(TOKEN_TILE, n_s)).reshape(bs, ss, d)
    return yp, ys, (lat_p, kpe_p, conv_p, lat_s, kpe_s, conv_s)


def kernel(x_prompt, x_sample, cache_kv_latent, cache_k_rope, state_conv, p_prompt, p_sample,
           g_mix, w_in, g_cq, w_uq, g_qn, g_qr, g_ckv, w_ukv, g_kn, g_kr, w_oa,
           conv_w, w_oc, w_o, g_ffn, w_rg, b_rg, w_re, b_re, w1, w3, w2, g_ple, w_pg, w_ple):
    depth = g_mix.shape[0]
    xp, xs = x_prompt, x_sample
    outs = [[] for _ in range(6)]
    for i in range(depth):
        w = _prep_weights(g_mix[i], w_in[i], g_cq[i], w_uq[i], g_qn[i], g_qr[i], g_ckv[i], w_ukv[i],
                          g_kn[i], g_kr[i], w_oa[i], conv_w[i], w_oc[i], w_o[i], g_ffn[i], w_rg[i], b_rg[i],
                          w_re[i], b_re[i], w1[i], w3[i], w2[i], g_ple[i], w_pg[i], w_ple[i])
        xp, xs, new = _layer(xp, xs, p_prompt[i], p_sample[i], cache_kv_latent[i], cache_k_rope[i], state_conv[i], w)
        for o, a in zip(outs, new):
            o.append(a)
    return (xp, xs) + tuple(jnp.stack(o, axis=0) for o in outs)
```

```python
import functools
import math

import jax
import jax.numpy as jnp
from jax import lax
from jax.experimental import pallas as pl
from jax.experimental.pallas import tpu as pltpu

F32 = jnp.float32
BF16 = jnp.bfloat16

LANE = 128
SUBLANE = 8
TOKEN_TILE = 512
CHUNK = 64
N_HEADS = 8
QK_NOPE = 64
QK_ROPE = 32
V_DIM = 64
Q_LORA = 256
KV_LORA = 256
CONV_DIM = 512
N_GROUPS = 4
EXPERTS_PER_GROUP = 8
N_EXPERTS = N_GROUPS * EXPERTS_PER_GROUP
EXPERT_HIDDEN = 256
ROPE_THETA = 10000.0
EPS = 1e-6
ATTN_SCALE = (QK_NOPE + QK_ROPE) ** -0.5
LOG2E = math.log2(math.e)
NEG_BIG = -1e30
MOE_ROWS = 128
ATTN_HEADS_PER_STEP = 4
PLE_ROW_GROUPS = 2
N_PAIR_BUCKETS = N_GROUPS * (EXPERTS_PER_GROUP * (EXPERTS_PER_GROUP - 1) // 2)
VMEM_LIMIT = 56 * 1024 * 1024


def _rms_scale(x, n):
    return lax.rsqrt(jnp.sum(x * x, axis=-1, keepdims=True) * (1.0 / n) + EPS)


def _lane_iota(shape):
    return lax.broadcasted_iota(jnp.int32, shape, len(shape) - 1)


def _rope(t, rc, rs1, rs2):
    return t * rc + pltpu.roll(t, LANE - QK_ROPE // 2, 1) * rs1 + pltpu.roll(t, QK_ROPE // 2, 1) * rs2


def _pre_latents(x_ref, gmix_ref, wa_ref, gcq_ref, gckv_ref, gkr_ref, rope_refs, lat_ref, kpe_ref):
    x = x_ref[0]
    h = x * _rms_scale(x, x.shape[-1]) * gmix_ref[...]
    z = jnp.dot(h.astype(BF16), wa_ref[...], preferred_element_type=F32)
    cq = z[:, :Q_LORA]
    ckv = z[:, Q_LORA:Q_LORA + KV_LORA]
    kr = z[:, Q_LORA + KV_LORA:]
    cqn = cq * _rms_scale(cq, Q_LORA) * gcq_ref[...]
    lat = ckv * _rms_scale(ckv, KV_LORA) * gckv_ref[...]
    lat_ref[0] = lat
    krn = kr * _rms_scale(kr, QK_ROPE) * gkr_ref[...]
    kpe = _rope(krn, *(r[...] for r in rope_refs))
    kpe_ref[0] = kpe[:, QK_NOPE:QK_NOPE + QK_ROPE]
    return cqn, lat, kpe


def _store_keys(k_ref, kf, gk, kpe):
    for hd in range(N_HEADS):
        sl = slice(hd * LANE, (hd + 1) * LANE)
        ks = kf[:, sl]
        k_ref[0, :, sl] = (ks * _rms_scale(ks, QK_NOPE) * gk + kpe).astype(BF16)


def _pre_kernel(x_ref, gmix_ref, wa_ref, gcq_ref, gckv_ref, gkr_ref, wuq_ref, wuk_ref, wuv_ref,
                gq_ref, gk_ref, rc_ref, rs1_ref, rs2_ref,
                q_ref, k_ref, v_ref, lat_ref, kpe_ref):
    rope_refs = (rc_ref, rs1_ref, rs2_ref)
    cqn, lat, kpe = _pre_latents(x_ref, gmix_ref, wa_ref, gcq_ref, gckv_ref, gkr_ref, rope_refs, lat_ref, kpe_ref)
    rc, rs1, rs2 = (r[...] for r in rope_refs)
    is_nope = _lane_iota(kpe.shape) < QK_NOPE
    latb = lat.astype(BF16)
    qf = jnp.dot(cqn.astype(BF16), wuq_ref[...], preferred_element_type=F32)
    vf = jnp.dot(latb, wuv_ref[...], preferred_element_type=F32)
    _store_keys(k_ref, jnp.dot(latb, wuk_ref[...], preferred_element_type=F32), gk_ref[...], kpe)
    gq = gq_ref[...]
    ones_hi = jnp.where(is_nope, 0.0, 1.0)
    for hd in range(N_HEADS):
        sl = slice(hd * LANE, (hd + 1) * LANE)
        qs = qf[:, sl]
        sq = qs * qs
        ss_all = jnp.sum(sq, axis=-1, keepdims=True)
        ss_n = jnp.sum(jnp.where(is_nope, sq, 0.0), axis=-1, keepdims=True)
        r = jnp.where(is_nope, lax.rsqrt(ss_n * (1.0 / QK_NOPE) + EPS),
                      lax.rsqrt((ss_all - ss_n) * (1.0 / QK_ROPE) + EPS))
        q_ref[0, :, sl] = _rope(qs * r * gq, rc, rs1, rs2).astype(BF16)
        v_ref[0, :, sl] = (vf[:, sl] + ones_hi).astype(BF16)


def _pre_kernel_t(x_ref, gmix_ref, wa_ref, gcq_ref, gckv_ref, gkr_ref, wuqt_ref, wuk_ref, wuvt_ref,
                  gqt_ref, gk_ref, rc_ref, rs1_ref, rs2_ref, cos_ref, sin_ref,
                  qt_ref, k_ref, vt_ref, lat_ref, kpe_ref):
    rope_refs = (rc_ref, rs1_ref, rs2_ref)
    cqn, lat, kpe = _pre_latents(x_ref, gmix_ref, wa_ref, gcq_ref, gckv_ref, gkr_ref, rope_refs, lat_ref, kpe_ref)
    latb = lat.astype(BF16)
    _store_keys(k_ref, jnp.dot(latb, wuk_ref[...], preferred_element_type=F32), gk_ref[...], kpe)
    tm = cqn.shape[0]
    qft = jnp.dot(wuqt_ref[...], cqn.T.astype(BF16), preferred_element_type=F32)
    vft = jnp.dot(wuvt_ref[...], lat.T.astype(BF16), preferred_element_type=F32)
    gq, cos, sin = gqt_ref[...], cos_ref[...], sin_ref[...]
    half = QK_ROPE // 2
    ones_lo = jnp.where(lax.broadcasted_iota(jnp.int32, (LANE, tm), 0) < V_DIM, 0.0, 1.0)
    pad = jnp.zeros((LANE - QK_NOPE - QK_ROPE, tm), F32)
    for hd in range(N_HEADS):
        rows = slice(hd * LANE, (hd + 1) * LANE)
        qs = qft[rows]
        sq = qs * qs
        r_n = lax.rsqrt(jnp.sum(sq[:QK_NOPE], axis=0, keepdims=True) * (1.0 / QK_NOPE) + EPS)
        r_p = lax.rsqrt(jnp.sum(sq[QK_NOPE:QK_NOPE + QK_ROPE], axis=0, keepdims=True) * (1.0 / QK_ROPE) + EPS)
        nope = qs[:QK_NOPE] * r_n * gq[:QK_NOPE]
        x1 = qs[QK_NOPE:QK_NOPE + half] * r_p * gq[QK_NOPE:QK_NOPE + half]
        x2 = qs[QK_NOPE + half:QK_NOPE + QK_ROPE] * r_p * gq[QK_NOPE + half:QK_NOPE + QK_ROPE]
        slab = jnp.concatenate([nope, x1 * cos - x2 * sin, x1 * sin + x2 * cos, pad], axis=0)
        qt_ref[0, 0, rows, :] = slab.astype(BF16)
        vt_ref[0, 0, rows, :] = (vft[rows] + ones_lo).astype(BF16)


def _kvpast_kernel(lat_ref, kpe_ref, wuk_ref, wuv_ref, gk_ref, k_ref, v_ref):
    latb = lat_ref[0].astype(BF16)
    kf = jnp.dot(latb, wuk_ref[...], preferred_element_type=F32)
    vf = jnp.dot(latb, wuv_ref[...], preferred_element_type=F32)
    kpe = kpe_ref[0]
    lane = _lane_iota(kpe.shape)
    ones_hi = jnp.where(lane < QK_NOPE, 0.0, 1.0)
    gk = gk_ref[...]
    for hd in range(N_HEADS):
        sl = slice(hd * LANE, (hd + 1) * LANE)
        ks = kf[:, sl]
        k_ref[0, :, sl] = (ks * _rms_scale(ks, QK_NOPE) * gk + kpe).astype(BF16)
        v_ref[0, :, sl] = (vf[:, sl] + ones_hi).astype(BF16)


def _finish_pair(accs):
    outs = [a / pltpu.roll(a, V_DIM, 1) for a in accs]
    lane = _lane_iota(outs[0].shape)
    return jnp.where(lane < V_DIM, outs[0], pltpu.roll(outs[1], V_DIM, 1))


def _attn_prompt_kernel(qt_ref, k_ref, vt_ref, o_ref, *scratch, tq, nh):
    s_refs = (scratch[:nh], scratch[nh:2 * nh])
    p_refs = (scratch[2 * nh:3 * nh], scratch[3 * nh:])
    qi = pl.program_id(2)
    key_pos = lax.broadcasted_iota(jnp.int32, (tq, tq), 0)
    query_pos = lax.broadcasted_iota(jnp.int32, (tq, tq), 1)
    diag_mask = (key_pos // CHUNK) <= (query_pos // CHUNK)
    heads = [slice(hh * LANE, (hh + 1) * LANE) for hh in range(nh)]

    def scores(i, slot, hh, mask):
        start = pl.multiple_of(i * tq, tq)
        s = jnp.dot(k_ref[0, pl.ds(start, tq), heads[hh]], qt_ref[0, 0, heads[hh], :], preferred_element_type=F32)
        if mask is not None:
            s = jnp.where(mask, s, NEG_BIG)
        s_refs[slot][hh][...] = s
        return jnp.max(s, axis=0, keepdims=True)

    def softmax(slot, hh, m, tile_max):
        m_new = jnp.maximum(m, tile_max)
        p_refs[slot][hh][...] = jnp.exp2(s_refs[slot][hh][...] - m_new).astype(BF16)
        return m_new, jnp.exp2(m - m_new)

    def accumulate(i, slot, hh, alpha, acc):
        pv = jnp.dot(vt_ref[0, jnp.maximum(i, 0), heads[hh], :], p_refs[slot][hh][...],
                     preferred_element_type=F32)
        return acc * alpha + pv

    def iteration(i, slot, carry, next_mask=None):
        stats = [softmax(slot, hh, m, tile_max) for hh, (m, tile_max, _, _) in enumerate(carry)]
        next_max = [scores(i + 1, 1 - slot, hh, next_mask) for hh in range(nh)]
        accs = [accumulate(i - 1, 1 - slot, hh, alpha, acc) for hh, (_, _, alpha, acc) in enumerate(carry)]
        return tuple((m, tmax, alpha, acc) for (m, alpha), tmax, acc in zip(stats, next_max, accs))

    def last(slot, carry):
        accs = []
        for hh, (m, tile_max, alpha, acc) in enumerate(carry):
            acc = accumulate(qi - 1, 1 - slot, hh, alpha, acc)
            m, alpha = softmax(slot, hh, m, tile_max)
            accs.append(accumulate(qi, slot, hh, alpha, acc))
        return tuple(accs)

    first_mask = diag_mask | (qi > 0)
    init = []
    for hh in range(nh):
        p_refs[1][hh][...] = jnp.zeros((tq, tq), BF16)
        init.append((jnp.full((1, tq), NEG_BIG, F32), scores(0, 0, hh, first_mask), jnp.ones((1, tq), F32),
                     jnp.zeros((LANE, tq), F32)))
    carry = lax.fori_loop(0, (qi - 1) // 2, lambda j, c: iteration(2 * j + 1, 1, iteration(2 * j, 0, c)),
                          tuple(init))
    tails = [lambda c: last(0, c),
             lambda c: last(1, iteration(qi - 1, 0, c, diag_mask)),
             lambda c: last(0, iteration(qi - 1, 1, iteration(qi - 2, 0, c), diag_mask))]
    accs = lax.switch(jnp.where(qi == 0, 0, 2 - qi % 2), tails, carry)
    out_t = jnp.concatenate([a[:V_DIM] / a[V_DIM:V_DIM + 1] for a in accs], axis=0)
    o_ref[0] = out_t.T.astype(o_ref.dtype)


def _attn_sample_kernel(q_ref, kp_ref, vp_ref, kn_ref, vn_ref, o_ref):
    nt = (((1,), (1,)), ((), ()))
    slabs = [slice(hd * LANE, (hd + 1) * LANE) for hd in range(N_HEADS)]
    s_past = [lax.dot_general(q_ref[0, :, sl], kp_ref[0, :, sl], nt, preferred_element_type=F32) for sl in slabs]
    s_new = [lax.dot_general(q_ref[0, :, sl], kn_ref[0, :, sl], nt, preferred_element_type=F32) for sl in slabs]
    accs = []
    for sl, sp, sn in zip(slabs, s_past, s_new):
        m = jnp.maximum(jnp.max(sp, axis=-1, keepdims=True), jnp.max(sn, axis=-1, keepdims=True))
        accs.append(jnp.dot(jnp.exp2(sp - m).astype(BF16), vp_ref[0, :, sl], preferred_element_type=F32)
                    + jnp.dot(jnp.exp2(sn - m).astype(BF16), vn_ref[0, :, sl], preferred_element_type=F32))
    for pair in range(N_HEADS // 2):
        o_ref[0, :, pair * LANE:(pair + 1) * LANE] = _finish_pair(accs[2 * pair:2 * pair + 2]).astype(o_ref.dtype)


def _post_kernel(x_ref, attn_ref, cinit_ref, cnt0_ref, gmix_ref, wb_ref, convw_ref, woa_ref, woc_ref, wo_ref,
                 gffn_ref, wrt_ref, brt_ref, tri_ref, x1_all_ref,
                 x1_ref, route_ref, cnew_ref, cnt_ref, carry_ref, *, seq):
    del x1_all_ref
    si = pl.program_id(1)
    tm = x_ref.shape[1]

    if seq is None:
        @pl.when(si == 0)
        def _():
            carry_ref[...] = cinit_ref[0]

    @pl.when((si == 0) & (pl.program_id(0) == 0))
    def _():
        cnt_ref[...] = cnt0_ref[...]

    d = x_ref.shape[-1]
    hm = tm // 2
    groups = (0, hm)
    xs, zs = [], []
    for r0 in groups:
        x = x_ref[0, r0:r0 + hm, :]
        h = x * _rms_scale(x, d) * gmix_ref[...]
        xs.append(x)
        zs.append(jnp.dot(h.astype(BF16), wb_ref[...], preferred_element_type=F32))

    cw = convw_ref[...]
    row = lax.broadcasted_iota(jnp.int32, (hm, CONV_DIM), 0)
    if seq is None:
        c1, c2 = carry_ref[SUBLANE - 1:SUBLANE, :], carry_ref[SUBLANE - 2:SUBLANE - 1, :]
    merged = []
    for r0, z in zip(groups, zs):
        conv_b = z[:, :CONV_DIM]
        u = z[:, CONV_DIM:2 * CONV_DIM] * z[:, 2 * CONV_DIM:3 * CONV_DIM]
        if seq is None:
            u_m1 = jnp.where(row == 0, c1, pltpu.roll(u, 1, 0))
            u_m2 = jnp.where(row == 0, c2, jnp.where(row == 1, c1, pltpu.roll(u, 2, 0)))
            c1, c2 = u[hm - 1:, :], u[hm - 2:hm - 1, :]
            if r0 + hm == tm:
                carry_ref[...] = u[hm - SUBLANE:, :]
                cnew_ref[0] = u[hm - 2:, :]
        else:
            pos = row % seq
            u_m1 = jnp.where(pos == 0, cinit_ref[0, r0:r0 + hm, :], pltpu.roll(u, 1, 0))
            u_m2 = jnp.where(pos <= 1, cinit_ref[0, tm + r0:tm + r0 + hm, :], pltpu.roll(u, 2, 0))
            cnew_ref[0, r0 // seq:(r0 + hm) // seq] = u.reshape(hm // seq, seq, CONV_DIM)[:, seq - 2:, :]
        cv = cw[0:1, :] * u_m2 + cw[1:2, :] * u_m1 + cw[2:3, :] * u
        y_a = jnp.dot(attn_ref[0, r0:r0 + hm, :], woa_ref[...], preferred_element_type=F32)
        y_c = jnp.dot((conv_b * cv).astype(BF16), woc_ref[...], preferred_element_type=F32)
        gate_a = z[:, 3 * CONV_DIM:3 * CONV_DIM + d]
        gate_c = z[:, 3 * CONV_DIM + d:]
        merged.append((jax.nn.sigmoid(gate_a) * y_a + jax.nn.sigmoid(gate_c) * y_c).astype(BF16))
    x1s = [x + jnp.dot(mrg, wo_ref[...], preferred_element_type=F32) for x, mrg in zip(xs, merged)]
    nchunk = d // LANE
    for r0, x1 in zip(groups, x1s):
        for c in range(nchunk):
            x1_ref[pl.ds(r0 * nchunk + c, hm, stride=nchunk), :] = x1[:, c * LANE:(c + 1) * LANE]

    nt = (((1,), (1,)), ((), ()))
    n_rows = brt_ref.shape[0]
    logits = []
    for x1 in x1s:
        h2 = x1 * _rms_scale(x1, d) * gffn_ref[...]
        h2_hi = h2.astype(BF16)
        h2_lo = (h2 - h2_hi.astype(F32)).astype(BF16)
        lt2 = lax.dot_general(wrt_ref[...], h2_hi, nt, preferred_element_type=F32)
        logits.append(lt2[:n_rows] + lt2[n_rows:]
                      + lax.dot_general(wrt_ref[:n_rows, :], h2_lo, nt, preferred_element_type=F32) + brt_ref[...])
    sub = lax.broadcasted_iota(jnp.int32, (EXPERTS_PER_GROUP, hm), 0).astype(F32)
    none = float(EXPERTS_PER_GROUP)

    def first_argmax(v):
        vmax = jnp.max(v, axis=0, keepdims=True)
        return jnp.min(jnp.where(v == vmax, sub, none), axis=0, keepdims=True)

    n_keys = cnt_ref.shape[0]
    keys = lax.broadcasted_iota(jnp.int32, (n_keys, hm), 0).astype(F32)
    row8 = lax.broadcasted_iota(jnp.int32, (8, hm), 0)
    seen = cnt_ref[...]
    for r0, lt in zip(groups, logits):
        g_idx = first_argmax(lt[:EXPERTS_PER_GROUP])
        el = lt[EXPERTS_PER_GROUP * N_GROUPS:]
        for g in range(N_GROUPS - 2, -1, -1):
            el = jnp.where(g_idx == g, lt[EXPERTS_PER_GROUP * (g + 1):EXPERTS_PER_GROUP * (g + 2)], el)
        i1 = first_argmax(el)
        i2 = first_argmax(jnp.where(sub == i1, NEG_BIG, el))
        key = (g_idx * (EXPERTS_PER_GROUP * EXPERTS_PER_GROUP) + jnp.minimum(i1, i2) * EXPERTS_PER_GROUP
               + jnp.maximum(i1, i2))
        onehot = jnp.where(keys == key, 1.0, 0.0)
        before = jnp.dot(onehot.astype(BF16), tri_ref[...], preferred_element_type=F32)
        rank = jnp.sum(onehot * (before + seen), axis=0, keepdims=True)
        seen = seen + jnp.sum(onehot, axis=1, keepdims=True)
        route_ref[0, :, r0:r0 + hm] = jnp.where(row8 == 0, key, jnp.where(row8 == 1, rank, 0.0))
    cnt_ref[...] = seen


def _rows_loop(n, fn, unroll=8):
    def group(j, c):
        for u in range(unroll):
            fn(j * unroll + u)
        return c
    lax.fori_loop(0, n // unroll, group, 0)

    def single(i, c):
        fn(i)
        return c
    lax.fori_loop(n // unroll * unroll, n, single, 0)


def _moe_kernel(lo_ref, hi_ref, cnt_ref, dest_ref,
                x1_hbm, gffn_ref, wr_ref, br_ref, w13lo_ref, w13hi_ref, w2lo_ref, w2hi_ref,
                x2_hbm, tok_ref, xbuf, obuf, gsem, ssem):
    nb = pl.program_id(0)
    nblk = pl.num_programs(0)
    nchunk = xbuf.shape[1] // MOE_ROWS
    slot = nb % 2
    cnt = cnt_ref[nb]

    def start_gathers(blk, sl):
        base = blk * MOE_ROWS

        def one(i):
            src = x1_hbm.at[pl.ds(tok_ref[base + i] * nchunk, nchunk)]
            pltpu.make_async_copy(src, xbuf.at[sl, pl.ds(i * nchunk, nchunk)], gsem.at[sl]).start()
        _rows_loop(cnt_ref[blk], one)

    def start_scatters(blk, sl):
        base = blk * MOE_ROWS

        def one(i):
            dst = x2_hbm.at[pl.ds(tok_ref[base + i] * nchunk, nchunk)]
            pltpu.make_async_copy(obuf.at[sl, pl.ds(i * nchunk, nchunk)], dst, ssem.at[sl]).start(priority=1)
        _rows_loop(cnt_ref[blk], one)

    def wait_gathers(blk, sl):
        rows = cnt_ref[blk] * nchunk
        pltpu.make_async_copy(x1_hbm.at[pl.ds(0, rows)], xbuf.at[sl, pl.ds(0, rows)], gsem.at[sl]).wait()

    def wait_scatters(blk, sl):
        rows = cnt_ref[blk] * nchunk
        pltpu.make_async_copy(obuf.at[sl, pl.ds(0, rows)], x2_hbm.at[pl.ds(0, rows)], ssem.at[sl]).wait()

    @pl.when(nb == 0)
    def _():
        def invert(i, c):
            tok_ref[dest_ref[i]] = i
            return c
        lax.fori_loop(0, dest_ref.shape[0], invert, 0, unroll=16)
        xbuf[...] = jnp.zeros_like(xbuf)

        @pl.when(cnt > 0)
        def _():
            start_gathers(0, 0)

    @pl.when(nb + 1 < nblk)
    def _():
        @pl.when(cnt_ref[nb + 1] > 0)
        def _():
            start_gathers(nb + 1, 1 - slot)

    @pl.when(nb >= 2)
    def _():
        @pl.when(cnt_ref[nb - 2] > 0)
        def _():
            wait_scatters(nb - 2, slot)

    @pl.when(cnt > 0)
    def _():
        wait_gathers(nb, slot)
        xg = jnp.concatenate([xbuf[slot, pl.ds(c, MOE_ROWS, stride=nchunk), :] for c in range(nchunk)], axis=1)
        h = (xg * _rms_scale(xg, xg.shape[-1]) * gffn_ref[...]).astype(BF16)

        lo, hi = lo_ref[nb], hi_ref[nb]
        logit = jnp.dot(h, wr_ref[...], preferred_element_type=F32) + br_ref[...]
        lane = _lane_iota(logit.shape)
        pick = lambda j: jnp.sum(jnp.where(lane == j, logit, 0.0), axis=-1, keepdims=True)
        is_g = lane < N_GROUPS
        gmax = jnp.max(jnp.where(is_g, logit, NEG_BIG), axis=-1, keepdims=True)
        g_den = jnp.sum(jnp.where(is_g, jnp.exp(logit - gmax), 0.0), axis=-1, keepdims=True)
        g_p = jnp.exp(pick(lo // EXPERTS_PER_GROUP) - gmax) / g_den
        l_lo, l_hi = pick(N_GROUPS + lo), pick(N_GROUPS + hi)
        gates = (g_p * jax.nn.sigmoid(l_lo - l_hi), g_p * jax.nn.sigmoid(l_hi - l_lo))

        abs_ = [jnp.dot(h, w13_ref[0], preferred_element_type=F32) for w13_ref in (w13lo_ref, w13hi_ref)]
        y = xg
        for ab, w2_ref, gate in zip(abs_, (w2lo_ref, w2hi_ref), gates):
            hid = jax.nn.silu(ab[:, :EXPERT_HIDDEN]) * ab[:, EXPERT_HIDDEN:]
            y = y + gate * jnp.dot(hid.astype(BF16), w2_ref[0], preferred_element_type=F32)
        for c in range(nchunk):
            obuf[slot, pl.ds(c, MOE_ROWS, stride=nchunk), :] = y[:, c * LANE:(c + 1) * LANE]
        start_scatters(nb, slot)

    @pl.when(nb == nblk - 1)
    def _():
        @pl.when(nb >= 1)
        def _():
            @pl.when(cnt_ref[nb - 1] > 0)
            def _():
                wait_scatters(nb - 1, 1 - slot)

        @pl.when(cnt > 0)
        def _():
            wait_scatters(nb, slot)


def _ple_kernel(x_ref, p_ref, gple_ref, wpg_ref, wple_ref, o_ref):
    tm = o_ref.shape[0]
    nchunk = o_ref.shape[1] // LANE
    emb = jnp.dot(p_ref[...].astype(BF16), wple_ref[...], preferred_element_type=F32)
    rows = tm // PLE_ROW_GROUPS
    for r0 in range(0, tm, rows):
        x = jnp.concatenate([x_ref[pl.ds(r0 * nchunk + c, rows, stride=nchunk), :] for c in range(nchunk)], axis=1)
        hp = (x * _rms_scale(x, x.shape[-1]) * gple_ref[...]).astype(BF16)
        gate = jax.nn.sigmoid(jnp.dot(hp, wpg_ref[...], preferred_element_type=F32))
        o_ref[r0:r0 + rows, :] = x + gate * emb[r0:r0 + rows]


def _const(shape):
    nd = len(shape)
    return pl.BlockSpec(shape, lambda *_: (0,) * nd)


def _head_slab_cols(w, width, offset=0):
    k = w.shape[0]
    w = w.reshape(k, N_HEADS, width)
    w = jnp.pad(w, ((0, 0), (0, 0), (offset, LANE - width - offset)))
    return w.reshape(k, N_HEADS * LANE)


def _prep_weights(g_mix, w_in, g_cq, w_uq, g_qn, g_qr, g_ckv, w_ukv, g_kn, g_kr, w_oa,
                  conv_w, w_oc, w_o, g_ffn, w_rg, b_rg, w_re, b_re, w1, w3, w2, g_ple, w_pg, w_ple):
    d = w_in.shape[0]
    n_mla = Q_LORA + KV_LORA
    kr_cols = jnp.pad(w_in[:, n_mla:n_mla + QK_ROPE], ((0, 0), (QK_NOPE, LANE - QK_NOPE - QK_ROPE)))
    w = {}
    w["wa"] = jnp.concatenate([w_in[:, :n_mla], kr_cols], axis=1).astype(BF16)
    w["wb"] = w_in[:, n_mla + QK_ROPE:].astype(BF16)
    w["wuq"] = _head_slab_cols(w_uq, QK_NOPE + QK_ROPE).astype(BF16)
    ukv = w_ukv.reshape(KV_LORA, N_HEADS, QK_NOPE + V_DIM)
    w["wuk"] = _head_slab_cols(ukv[:, :, :QK_NOPE].reshape(KV_LORA, -1), QK_NOPE).astype(BF16)
    w["wuv"] = _head_slab_cols(ukv[:, :, QK_NOPE:].reshape(KV_LORA, -1), V_DIM).astype(BF16)
    pad_hi = LANE - QK_NOPE - QK_ROPE
    w["gq"] = (jnp.pad(jnp.concatenate([g_qn, g_qr]), (0, pad_hi)) * (ATTN_SCALE * LOG2E))[None]
    w["gk"] = jnp.pad(g_kn, (0, LANE - QK_NOPE))[None]
    w["gkr"] = jnp.pad(g_kr, (QK_NOPE, pad_hi))[None]
    w["gmix"], w["gcq"], w["gckv"] = g_mix[None], g_cq[None], g_ckv[None]
    w["gffn"], w["gple"] = g_ffn[None], g_ple[None]
    w["convw"] = jnp.pad(conv_w, ((0, 8 - conv_w.shape[0]), (0, 0)))
    w["woa"], w["woc"], w["wo"] = w_oa.astype(BF16), w_oc.astype(BF16), w_o.astype(BF16)
    n_r = N_GROUPS + N_EXPERTS
    w["wr"] = jnp.pad(jnp.concatenate([w_rg, w_re], axis=1), ((0, 0), (0, LANE - n_r))).astype(BF16)
    w["br"] = jnp.pad(jnp.concatenate([b_rg, b_re]), (0, LANE - n_r))[None]
    pad_g = EXPERTS_PER_GROUP - N_GROUPS
    wrt = jnp.concatenate([jnp.pad(w_rg.T, ((0, pad_g), (0, 0))), w_re.T], axis=0)
    wrt_hi = wrt.astype(BF16)
    w["wrt"] = jnp.concatenate([wrt_hi, (wrt - wrt_hi.astype(F32)).astype(BF16)], axis=0)
    w["brt"] = jnp.concatenate([b_rg, jnp.full((pad_g,), NEG_BIG, F32), b_re])[:, None]
    w["w13"] = jnp.concatenate([w1, w3], axis=2).astype(BF16)
    w["w2"] = w2.astype(BF16)
    w["wpg"], w["wple"] = w_pg.astype(BF16), w_ple.astype(BF16)
    return w


def _rope_slabs(pos):
    inv = 1.0 / (ROPE_THETA ** (jnp.arange(0, QK_ROPE, 2, dtype=F32) / QK_ROPE))
    ang = pos.astype(F32)[:, None] * inv[None, :]
    cos, sin = jnp.cos(ang), jnp.sin(ang)
    n = pos.shape[0]
    half = QK_ROPE // 2
    z = lambda k: jnp.zeros((n, k), F32)
    pad_hi = LANE - QK_NOPE - QK_ROPE
    rc = jnp.concatenate([jnp.ones((n, QK_NOPE), F32), cos, cos, z(pad_hi)], axis=1)
    rs1 = jnp.concatenate([z(QK_NOPE), -sin, z(half), z(pad_hi)], axis=1)
    rs2 = jnp.concatenate([z(QK_NOPE), z(half), sin, z(pad_hi)], axis=1)
    return (rc, rs1, rs2), cos.T, sin.T


def _params(sem):
    return pltpu.CompilerParams(dimension_semantics=sem, vmem_limit_bytes=VMEM_LIMIT)


def _mla_pre(x, w, rope, tm):
    b, s, d = x.shape
    hw = N_HEADS * LANE
    tok = lambda width: pl.BlockSpec((1, tm, width), lambda i, j: (i, j, 0))
    rope_spec = pl.BlockSpec((tm, LANE), lambda i, j: (j, 0))
    consts = [w["gmix"], w["wa"], w["gcq"], w["gckv"], w["gkr"], w["wuq"], w["wuk"], w["wuv"], w["gq"], w["gk"]]
    return pl.pallas_call(
        _pre_kernel,
        grid=(b, s // tm),
        in_specs=[tok(d)] + [_const(c.shape) for c in consts] + [rope_spec] * 3,
        out_specs=[tok(hw), tok(hw), tok(hw), tok(KV_LORA), tok(QK_ROPE)],
        out_shape=[jax.ShapeDtypeStruct((b, s, hw), BF16)] * 3
        + [jax.ShapeDtypeStruct((b, s, KV_LORA), F32), jax.ShapeDtypeStruct((b, s, QK_ROPE), F32)],
        compiler_params=_params(("parallel", "parallel")),
        name="mla_pre",
    )(x, *consts, *rope)


def _mla_pre_t(x, w, rope, cos_t, sin_t, tm):
    b, s, d = x.shape
    hw = N_HEADS * LANE
    tok = lambda width: pl.BlockSpec((1, tm, width), lambda i, j: (i, j, 0))
    tiled = pl.BlockSpec((1, 1, hw, tm), lambda i, j: (i, j, 0, 0))
    rope_spec = pl.BlockSpec((tm, LANE), lambda i, j: (j, 0))
    rope_t_spec = pl.BlockSpec((QK_ROPE // 2, tm), lambda i, j: (0, j))
    gqt = jnp.broadcast_to(w["gq"].T, (LANE, tm))
    consts = [w["gmix"], w["wa"], w["gcq"], w["gckv"], w["gkr"], w["wuq"].T, w["wuk"], w["wuv"].T, gqt, w["gk"]]
    return pl.pallas_call(
        _pre_kernel_t,
        grid=(b, s // tm),
        in_specs=[tok(d)] + [_const(c.shape) for c in consts] + [rope_spec] * 3 + [rope_t_spec] * 2,
        out_specs=[tiled, tok(hw), tiled, tok(KV_LORA), tok(QK_ROPE)],
        out_shape=[jax.ShapeDtypeStruct((b, s // tm, hw, tm), BF16), jax.ShapeDtypeStruct((b, s, hw), BF16),
                   jax.ShapeDtypeStruct((b, s // tm, hw, tm), BF16),
                   jax.ShapeDtypeStruct((b, s, KV_LORA), F32), jax.ShapeDtypeStruct((b, s, QK_ROPE), F32)],
        compiler_params=_params(("parallel", "parallel")),
        name="mla_pre_t",
    )(x, *consts, *rope, cos_t, sin_t)


def _kv_past(past_lat, past_kpe, w, tm):
    b, t, _ = past_lat.shape
    hw = N_HEADS * LANE
    kpe_slab = jnp.pad(past_kpe, ((0, 0), (0, 0), (QK_NOPE, LANE - QK_NOPE - QK_ROPE)))
    tok = lambda width: pl.BlockSpec((1, tm, width), lambda i, j: (i, j, 0))
    consts = [w["wuk"], w["wuv"], w["gk"]]
    return pl.pallas_call(
        _kvpast_kernel,
        grid=(b, t // tm),
        in_specs=[tok(KV_LORA), tok(LANE)] + [_const(c.shape) for c in consts],
        out_specs=[tok(hw), tok(hw)],
        out_shape=[jax.ShapeDtypeStruct((b, t, hw), BF16)] * 2,
        compiler_params=_params(("parallel", "parallel")),
        name="kv_past",
    )(past_lat, kpe_slab, *consts)


def _attn_prompt(qt, k, vt):
    b, n_tiles, _, tq = qt.shape
    s = n_tiles * tq
    nh = ATTN_HEADS_PER_STEP
    width = nh * LANE
    return pl.pallas_call(
        functools.partial(_attn_prompt_kernel, tq=tq, nh=nh),
        grid=(b, N_HEADS // nh, n_tiles),
        in_specs=[pl.BlockSpec((1, 1, width, tq), lambda i, h, j: (i, j, h, 0)),
                  pl.BlockSpec((1, s, width), lambda i, h, j: (i, 0, h)),
                  pl.BlockSpec((1, n_tiles, width, tq), lambda i, h, j: (i, 0, h, 0))],
        out_specs=pl.BlockSpec((1, tq, nh * V_DIM), lambda i, h, j: (i, j, h)),
        out_shape=jax.ShapeDtypeStruct((b, s, N_HEADS * V_DIM), BF16),
        scratch_shapes=[pltpu.VMEM((tq, tq), F32)] * (2 * nh) + [pltpu.VMEM((tq, tq), BF16)] * (2 * nh),
        compiler_params=_params(("parallel", "parallel", "arbitrary")),
        name="attn_prompt",
    )(qt, k, vt)


def _attn_sample(q, kp, vp, kn, vn):
    b, s, _ = q.shape
    t = kp.shape[1]
    hw = N_HEADS * LANE
    blk = lambda rows: pl.BlockSpec((1, rows, hw), lambda i: (i, 0, 0))
    return pl.pallas_call(
        _attn_sample_kernel,
        grid=(b,),
        in_specs=[blk(s), blk(t), blk(t), blk(s), blk(s)],
        out_specs=pl.BlockSpec((1, s, N_HEADS * V_DIM), lambda i: (i, 0, 0)),
        out_shape=jax.ShapeDtypeStruct((b, s, N_HEADS * V_DIM), BF16),
        compiler_params=_params(("parallel",)),
        name="attn_sample",
    )(q, kp, vp, kn, vn)


def _post(x, attn, conv_init, cnt0, x1_all, row_off, n_all, w, tm, seq=None):
    b, s, d = x.shape
    nchunk = d // LANE
    n_keys = cnt0.shape[0]
    tiles_per_b = s // tm
    off = row_off // tm
    tok = lambda width: pl.BlockSpec((1, tm, width), lambda i, j: (i, j, 0))
    if seq is None:
        cin_spec = pl.BlockSpec((1, SUBLANE, CONV_DIM), lambda i, j: (i, 0, 0))
        cnew_spec = pl.BlockSpec((1, 2, CONV_DIM), lambda i, j: (i, 0, 0))
        cnew_shape = (b, 2, CONV_DIM)
    else:
        cin_spec = pl.BlockSpec((1, 2 * tm, CONV_DIM), lambda i, j: (j, 0, 0))
        cnew_spec = pl.BlockSpec((1, tm // seq, 2, CONV_DIM), lambda i, j: (j, 0, 0, 0))
        cnew_shape = (tiles_per_b, tm // seq, 2, CONV_DIM)
    hm = tm // 2
    tri = (jnp.arange(hm)[:, None] < jnp.arange(hm)[None, :]).astype(BF16)
    consts = [cnt0, w["gmix"], w["wb"], w["convw"], w["woa"], w["woc"], w["wo"], w["gffn"], w["wrt"], w["brt"], tri]
    in_specs = ([tok(d), tok(N_HEADS * V_DIM), cin_spec] + [_const(c.shape) for c in consts]
                + [pl.BlockSpec(memory_space=pl.ANY)])
    args = [x, attn, conv_init] + consts + [x1_all]
    return pl.pallas_call(
        functools.partial(_post_kernel, seq=seq),
        grid=(b, tiles_per_b),
        in_specs=in_specs,
        out_specs=[pl.BlockSpec((tm * nchunk, LANE), lambda i, j: (off + i * tiles_per_b + j, 0)),
                   pl.BlockSpec((1, 8, tm), lambda i, j: (i * tiles_per_b + j, 0, 0)),
                   cnew_spec, _const((n_keys, 1))],
        out_shape=[jax.ShapeDtypeStruct((n_all * nchunk, LANE), F32),
                   jax.ShapeDtypeStruct((b * tiles_per_b, 8, tm), F32),
                   jax.ShapeDtypeStruct(cnew_shape, F32), jax.ShapeDtypeStruct((n_keys, 1), F32)],
        scratch_shapes=[pltpu.VMEM((SUBLANE, CONV_DIM), F32)],
        input_output_aliases={len(args) - 1: 0},
        compiler_params=_params(("arbitrary", "arbitrary")),
        name="post",
    )(*args)


def _route_tables(key, rank, counts, n):
    n_keys = counts.shape[0]
    padded = (counts + MOE_ROWS - 1) // MOE_ROWS * MOE_ROWS
    pend = jnp.cumsum(padded)
    pstart = pend - padded
    ids = jnp.arange(n_keys, dtype=jnp.int32)
    dest = rank + jnp.sum(jnp.where(key[:, None] == ids[None, :], pstart[None, :], 0), axis=1)
    nblk = n // MOE_ROWS + N_PAIR_BUCKETS
    blk_start = jnp.arange(nblk, dtype=jnp.int32) * MOE_ROWS
    blk_hot = (blk_start[:, None] >= pstart[None, :]) & (blk_start[:, None] < pend[None, :])
    blk_key = jnp.sum(jnp.where(blk_hot, ids[None, :], 0), axis=1)
    blk_cnt = jnp.sum(jnp.where(blk_hot, jnp.minimum(counts[None, :] - (blk_start[:, None] - pstart[None, :]),
                                                      MOE_ROWS), 0), axis=1)
    any_hot = jnp.any(blk_hot, axis=1)
    blk_key = jnp.where(any_hot, blk_key, n_keys - 1)
    blk_lo = blk_key // EXPERTS_PER_GROUP
    blk_hi = blk_lo // EXPERTS_PER_GROUP * EXPERTS_PER_GROUP + blk_key % EXPERTS_PER_GROUP
    return blk_lo, blk_hi, blk_cnt.astype(jnp.int32), dest.astype(jnp.int32)


def _moe(x1_all, key, rank, counts, w):
    rows, _ = x1_all.shape
    d = w["gffn"].shape[1]
    nchunk = d // LANE
    n = rows // nchunk
    blk_lo, blk_hi, blk_cnt, dest = _route_tables(key, rank, counts, n)
    nblk = blk_lo.shape[0]
    w13_spec = lambda ref_idx: pl.BlockSpec((1, d, 2 * EXPERT_HIDDEN),
                                            lambda i, lo, hi, cnt, dst: ((lo, hi)[ref_idx][i], 0, 0))
    w2_spec = lambda ref_idx: pl.BlockSpec((1, EXPERT_HIDDEN, d),
                                           lambda i, lo, hi, cnt, dst: ((lo, hi)[ref_idx][i], 0, 0))
    buf = pltpu.VMEM((2, MOE_ROWS * nchunk, LANE), F32)
    grid_spec = pltpu.PrefetchScalarGridSpec(
        num_scalar_prefetch=4,
        grid=(nblk,),
        in_specs=[pl.BlockSpec(memory_space=pl.ANY),
                  pl.BlockSpec((1, d), lambda i, *_: (0, 0)),
                  pl.BlockSpec(w["wr"].shape, lambda i, *_: (0, 0)),
                  pl.BlockSpec(w["br"].shape, lambda i, *_: (0, 0)),
                  w13_spec(0), w13_spec(1), w2_spec(0), w2_spec(1)],
        out_specs=pl.BlockSpec(memory_space=pl.ANY),
        scratch_shapes=[pltpu.SMEM((nblk * MOE_ROWS,), jnp.int32), buf, buf,
                        pltpu.SemaphoreType.DMA((2,)), pltpu.SemaphoreType.DMA((2,))],
    )
    return pl.pallas_call(
        _moe_kernel,
        grid_spec=grid_spec,
        out_shape=jax.ShapeDtypeStruct((rows, LANE), F32),
        compiler_params=_params(("arbitrary",)),
        name="moe",
    )(blk_lo, blk_hi, blk_cnt, dest, x1_all, w["gffn"], w["wr"], w["br"], w["w13"], w["w13"], w["w2"], w["w2"])


def _ple(x2_all, row_off, p, w, tm):
    n, pd = p.shape
    d = w["gple"].shape[1]
    nchunk = d // LANE
    off = row_off // tm
    consts = [w["gple"], w["wpg"], w["wple"]]
    return pl.pallas_call(
        _ple_kernel,
        grid=(n // tm,),
        in_specs=[pl.BlockSpec((tm * nchunk, LANE), lambda i: (off + i, 0)), pl.BlockSpec((tm, pd), lambda i: (i, 0))]
        + [_const(c.shape) for c in consts],
        out_specs=pl.BlockSpec((tm, d), lambda i: (i, 0)),
        out_shape=jax.ShapeDtypeStruct((n, d), F32),
        compiler_params=_params(("parallel",)),
        name="ple",
    )(x2_all, p, *consts)


def _layer(xp, xs, pp, ps, past_lat, past_kpe, past_conv, w):
    bp, sp, d = xp.shape
    bs, ss, _ = xs.shape
    n_p, n_s = bp * sp, bs * ss
    n_all = n_p + n_s
    n_keys = N_EXPERTS * EXPERTS_PER_GROUP
    past_len = past_lat.shape[1]
    tm_p, tm_s = min(TOKEN_TILE, sp), min(TOKEN_TILE, n_s)

    rope, cos_t, sin_t = _rope_slabs(jnp.arange(sp))
    qt, k, vt, lat_p, kpe_p = _mla_pre_t(xp, w, rope, cos_t, sin_t, tm_p)
    attn_p = _attn_prompt(qt, k, vt)
    cinit_p = jnp.zeros((bp, SUBLANE, CONV_DIM), F32)

    rope, _, _ = _rope_slabs(past_len + jnp.arange(n_s) % ss)
    xs_rows = xs.reshape(1, n_s, d)
    q, k, v, lat_s, kpe_s = _mla_pre(xs_rows, w, rope, tm_s)
    by_seq = lambda a: a.reshape(bs, ss, a.shape[-1])
    kp, vp = _kv_past(past_lat, past_kpe, w, min(TOKEN_TILE, past_len))
    attn_s = _attn_sample(by_seq(q), kp, vp, by_seq(k), by_seq(v)).reshape(1, n_s, -1)
    lat_s, kpe_s = by_seq(lat_s), by_seq(kpe_s)
    in_tiles = lambda a: a.reshape(n_s // tm_s, tm_s, CONV_DIM)
    n_hist = past_conv.shape[1]
    cinit_s = jnp.concatenate(
        [in_tiles(jnp.pad(past_conv[:, n_hist - 1:], ((0, 0), (0, ss - 1), (0, 0)))),
         in_tiles(jnp.pad(past_conv, ((0, 0), (0, ss - n_hist), (0, 0))))], axis=1)

    x1_all = jnp.zeros((n_all * (d // LANE), LANE), F32)
    x1_all, route_p, conv_p, cnt = _post(xp, attn_p, cinit_p, jnp.zeros((n_keys, 1), F32), x1_all, 0, n_all, w, tm_p)
    x1_all, route_s, conv_s, cnt = _post(xs_rows, attn_s, cinit_s, cnt, x1_all, n_p, n_all, w, tm_s, seq=ss)
    conv_s = conv_s.reshape(bs, n_hist, CONV_DIM)
    key = jnp.concatenate([route_p[:, 0].reshape(-1), route_s[:, 0].reshape(-1)]).astype(jnp.int32)
    rank = jnp.concatenate([route_p[:, 1].reshape(-1), route_s[:, 1].reshape(-1)]).astype(jnp.int32)
    x2_all = _moe(x1_all, key, rank, cnt[:, 0].astype(jnp.int32), w)
    yp = _ple(x2_all, 0, pp.reshape(n_p, -1), w, min(TOKEN_TILE, n_p)).reshape(bp, sp, d)
    ys = _ple(x2_all, n_p, ps.reshape(n_s, -1), w, min(TOKEN_TILE, n_s)).reshape(bs, ss, d)
    return yp, ys, (lat_p, kpe_p, conv_p, lat_s, kpe_s, conv_s)


def kernel(x_prompt, x_sample, cache_kv_latent, cache_k_rope, state_conv, p_prompt, p_sample,
           g_mix, w_in, g_cq, w_uq, g_qn, g_qr, g_ckv, w_ukv, g_kn, g_kr, w_oa,
           conv_w, w_oc, w_o, g_ffn, w_rg, b_rg, w_re, b_re, w1, w3, w2, g_ple, w_pg, w_ple):
    depth = g_mix.shape[0]
    xp, xs = x_prompt, x_sample
    outs = [[] for _ in range(6)]
    for i in range(depth):
        w = _prep_weights(g_mix[i], w_in[i], g_cq[i], w_uq[i], g_qn[i], g_qr[i], g_ckv[i], w_ukv[i],
                          g_kn[i], g_kr[i], w_oa[i], conv_w[i], w_oc[i], w_o[i], g_ffn[i], w_rg[i], b_rg[i],
                          w_re[i], b_re[i], w1[i], w3[i], w2[i], g_ple[i], w_pg[i], w_ple[i])
        xp, xs, new = _layer(xp, xs, p_prompt[i], p_sample[i], cache_kv_latent[i], cache_k_rope[i], state_conv[i], w)
        for o, a in zip(outs, new):
            o.append(a)
    return (xp, xs) + tuple(jnp.stack(o, axis=0) for o in outs)
```

```python
import functools
import math

import jax
import jax.numpy as jnp
from jax import lax
from jax.experimental import pallas as pl
from jax.experimental.pallas import tpu as pltpu

F32 = jnp.float32
BF16 = jnp.bfloat16

LANE = 128
SUBLANE = 8
TOKEN_TILE = 512
CHUNK = 64
N_HEADS = 8
QK_NOPE = 64
QK_ROPE = 32
V_DIM = 64
Q_LORA = 256
KV_LORA = 256
CONV_DIM = 512
N_GROUPS = 4
EXPERTS_PER_GROUP = 8
N_EXPERTS = N_GROUPS * EXPERTS_PER_GROUP
EXPERT_HIDDEN = 256
ROPE_THETA = 10000.0
EPS = 1e-6
ATTN_SCALE = (QK_NOPE + QK_ROPE) ** -0.5
LOG2E = math.log2(math.e)
NEG_BIG = -1e30
MOE_ROWS = 128
ATTN_HEADS_PER_STEP = 4
PLE_ROW_GROUPS = 2
N_PAIR_BUCKETS = N_GROUPS * (EXPERTS_PER_GROUP * (EXPERTS_PER_GROUP - 1) // 2)
VMEM_LIMIT = 56 * 1024 * 1024


def _rms_scale(x, n):
    return lax.rsqrt(jnp.sum(x * x, axis=-1, keepdims=True) * (1.0 / n) + EPS)


def _lane_iota(shape):
    return lax.broadcasted_iota(jnp.int32, shape, len(shape) - 1)


def _rope(t, rc, rs1, rs2):
    return t * rc + pltpu.roll(t, LANE - QK_ROPE // 2, 1) * rs1 + pltpu.roll(t, QK_ROPE // 2, 1) * rs2


def _pre_latents(x_ref, gmix_ref, wa_ref, gcq_ref, gckv_ref, gkr_ref, rope_refs, lat_ref, kpe_ref):
    x = x_ref[0]
    h = x * _rms_scale(x, x.shape[-1]) * gmix_ref[...]
    z = jnp.dot(h.astype(BF16), wa_ref[...], preferred_element_type=F32)
    cq = z[:, :Q_LORA]
    ckv = z[:, Q_LORA:Q_LORA + KV_LORA]
    kr = z[:, Q_LORA + KV_LORA:]
    cqn = cq * _rms_scale(cq, Q_LORA) * gcq_ref[...]
    lat = ckv * _rms_scale(ckv, KV_LORA) * gckv_ref[...]
    lat_ref[0] = lat
    krn = kr * _rms_scale(kr, QK_ROPE) * gkr_ref[...]
    kpe = _rope(krn, *(r[...] for r in rope_refs))
    kpe_ref[0] = kpe[:, QK_NOPE:QK_NOPE + QK_ROPE]
    return cqn, lat, kpe


def _store_keys(k_ref, kf, gk, kpe):
    for hd in range(N_HEADS):
        sl = slice(hd * LANE, (hd + 1) * LANE)
        ks = kf[:, sl]
        k_ref[0, :, sl] = (ks * _rms_scale(ks, QK_NOPE) * gk + kpe).astype(BF16)


def _pre_kernel(x_ref, gmix_ref, wa_ref, gcq_ref, gckv_ref, gkr_ref, wuq_ref, wuk_ref, wuv_ref,
                gq_ref, gk_ref, rc_ref, rs1_ref, rs2_ref,
                q_ref, k_ref, v_ref, lat_ref, kpe_ref):
    rope_refs = (rc_ref, rs1_ref, rs2_ref)
    cqn, lat, kpe = _pre_latents(x_ref, gmix_ref, wa_ref, gcq_ref, gckv_ref, gkr_ref, rope_refs, lat_ref, kpe_ref)
    rc, rs1, rs2 = (r[...] for r in rope_refs)
    is_nope = _lane_iota(kpe.shape) < QK_NOPE
    latb = lat.astype(BF16)
    qf = jnp.dot(cqn.astype(BF16), wuq_ref[...], preferred_element_type=F32)
    vf = jnp.dot(latb, wuv_ref[...], preferred_element_type=F32)
    _store_keys(k_ref, jnp.dot(latb, wuk_ref[...], preferred_element_type=F32), gk_ref[...], kpe)
    gq = gq_ref[...]
    ones_hi = jnp.where(is_nope, 0.0, 1.0)
    for hd in range(N_HEADS):
        sl = slice(hd * LANE, (hd + 1) * LANE)
        qs = qf[:, sl]
        sq = qs * qs
        ss_all = jnp.sum(sq, axis=-1, keepdims=True)
        ss_n = jnp.sum(jnp.where(is_nope, sq, 0.0), axis=-1, keepdims=True)
        r = jnp.where(is_nope, lax.rsqrt(ss_n * (1.0 / QK_NOPE) + EPS),
                      lax.rsqrt((ss_all - ss_n) * (1.0 / QK_ROPE) + EPS))
        q_ref[0, :, sl] = _rope(qs * r * gq, rc, rs1, rs2).astype(BF16)
        v_ref[0, :, sl] = (vf[:, sl] + ones_hi).astype(BF16)


def _pre_kernel_t(x_ref, gmix_ref, wa_ref, gcq_ref, gckv_ref, gkr_ref, wuqt_ref, wuk_ref, wuvt_ref,
                  gqt_ref, gk_ref, rc_ref, rs1_ref, rs2_ref, cos_ref, sin_ref,
                  qt_ref, k_ref, vt_ref, lat_ref, kpe_ref):
    rope_refs = (rc_ref, rs1_ref, rs2_ref)
    cqn, lat, kpe = _pre_latents(x_ref, gmix_ref, wa_ref, gcq_ref, gckv_ref, gkr_ref, rope_refs, lat_ref, kpe_ref)
    latb = lat.astype(BF16)
    _store_keys(k_ref, jnp.dot(latb, wuk_ref[...], preferred_element_type=F32), gk_ref[...], kpe)
    tm = cqn.shape[0]
    qft = jnp.dot(wuqt_ref[...], cqn.T.astype(BF16), preferred_element_type=F32)
    vft = jnp.dot(wuvt_ref[...], lat.T.astype(BF16), preferred_element_type=F32)
    gq, cos, sin = gqt_ref[...], cos_ref[...], sin_ref[...]
    half = QK_ROPE // 2
    ones_lo = jnp.where(lax.broadcasted_iota(jnp.int32, (LANE, tm), 0) < V_DIM, 0.0, 1.0)
    pad = jnp.zeros((LANE - QK_NOPE - QK_ROPE, tm), F32)
    for hd in range(N_HEADS):
        rows = slice(hd * LANE, (hd + 1) * LANE)
        qs = qft[rows]
        sq = qs * qs
        r_n = lax.rsqrt(jnp.sum(sq[:QK_NOPE], axis=0, keepdims=True) * (1.0 / QK_NOPE) + EPS)
        r_p = lax.rsqrt(jnp.sum(sq[QK_NOPE:QK_NOPE + QK_ROPE], axis=0, keepdims=True) * (1.0 / QK_ROPE) + EPS)
        nope = qs[:QK_NOPE] * r_n * gq[:QK_NOPE]
        x1 = qs[QK_NOPE:QK_NOPE + half] * r_p * gq[QK_NOPE:QK_NOPE + half]
        x2 = qs[QK_NOPE + half:QK_NOPE + QK_ROPE] * r_p * gq[QK_NOPE + half:QK_NOPE + QK_ROPE]
        slab = jnp.concatenate([nope, x1 * cos - x2 * sin, x1 * sin + x2 * cos, pad], axis=0)
        qt_ref[0, 0, rows, :] = slab.astype(BF16)
        vt_ref[0, 0, rows, :] = (vft[rows] + ones_lo).astype(BF16)


def _kvpast_kernel(lat_ref, kpe_ref, wuk_ref, wuv_ref, gk_ref, k_ref, v_ref):
    latb = lat_ref[0].astype(BF16)
    kf = jnp.dot(latb, wuk_ref[...], preferred_element_type=F32)
    vf = jnp.dot(latb, wuv_ref[...], preferred_element_type=F32)
    kpe = kpe_ref[0]
    lane = _lane_iota(kpe.shape)
    ones_hi = jnp.where(lane < QK_NOPE, 0.0, 1.0)
    gk = gk_ref[...]
    for hd in range(N_HEADS):
        sl = slice(hd * LANE, (hd + 1) * LANE)
        ks = kf[:, sl]
        k_ref[0, :, sl] = (ks * _rms_scale(ks, QK_NOPE) * gk + kpe).astype(BF16)
        v_ref[0, :, sl] = (vf[:, sl] + ones_hi).astype(BF16)


def _finish_pair(accs):
    outs = [a / pltpu.roll(a, V_DIM, 1) for a in accs]
    lane = _lane_iota(outs[0].shape)
    return jnp.where(lane < V_DIM, outs[0], pltpu.roll(outs[1], V_DIM, 1))


def _attn_prompt_kernel(qt_ref, k_ref, vt_ref, o_ref, *scratch, tq, nh):
    s_refs = (scratch[:nh], scratch[nh:2 * nh])
    p_refs = (scratch[2 * nh:3 * nh], scratch[3 * nh:])
    qi = pl.program_id(2)
    key_pos = lax.broadcasted_iota(jnp.int32, (tq, tq), 0)
    query_pos = lax.broadcasted_iota(jnp.int32, (tq, tq), 1)
    diag_mask = (key_pos // CHUNK) <= (query_pos // CHUNK)
    heads = [slice(hh * LANE, (hh + 1) * LANE) for hh in range(nh)]

    def scores(i, slot, hh, mask):
        start = pl.multiple_of(i * tq, tq)
        s = jnp.dot(k_ref[0, pl.ds(start, tq), heads[hh]], qt_ref[0, 0, heads[hh], :], preferred_element_type=F32)
        if mask is not None:
            s = jnp.where(mask, s, NEG_BIG)
        s_refs[slot][hh][...] = s
        return jnp.max(s, axis=0, keepdims=True)

    def softmax(slot, hh, m, tile_max):
        m_new = jnp.maximum(m, tile_max)
        p_refs[slot][hh][...] = jnp.exp2(s_refs[slot][hh][...] - m_new).astype(BF16)
        return m_new, jnp.exp2(m - m_new)

    def accumulate(i, slot, hh, alpha, acc):
        pv = jnp.dot(vt_ref[0, jnp.maximum(i, 0), heads[hh], :], p_refs[slot][hh][...],
                     preferred_element_type=F32)
        return acc * alpha + pv

    def iteration(i, slot, carry, next_mask=None):
        stats = [softmax(slot, hh, m, tile_max) for hh, (m, tile_max, _, _) in enumerate(carry)]
        next_max = [scores(i + 1, 1 - slot, hh, next_mask) for hh in range(nh)]
        accs = [accumulate(i - 1, 1 - slot, hh, alpha, acc) for hh, (_, _, alpha, acc) in enumerate(carry)]
        return tuple((m, tmax, alpha, acc) for (m, alpha), tmax, acc in zip(stats, next_max, accs))

    def last(slot, carry):
        accs = []
        for hh, (m, tile_max, alpha, acc) in enumerate(carry):
            acc = accumulate(qi - 1, 1 - slot, hh, alpha, acc)
            m, alpha = softmax(slot, hh, m, tile_max)
            accs.append(accumulate(qi, slot, hh, alpha, acc))
        return tuple(accs)

    first_mask = diag_mask | (qi > 0)
    init = []
    for hh in range(nh):
        p_refs[1][hh][...] = jnp.zeros((tq, tq), BF16)
        init.append((jnp.full((1, tq), NEG_BIG, F32), scores(0, 0, hh, first_mask), jnp.ones((1, tq), F32),
                     jnp.zeros((LANE, tq), F32)))
    carry = lax.fori_loop(0, (qi - 1) // 2, lambda j, c: iteration(2 * j + 1, 1, iteration(2 * j, 0, c)),
                          tuple(init))
    tails = [lambda c: last(0, c),
             lambda c: last(1, iteration(qi - 1, 0, c, diag_mask)),
             lambda c: last(0, iteration(qi - 1, 1, iteration(qi - 2, 0, c), diag_mask))]
    accs = lax.switch(jnp.where(qi == 0, 0, 2 - qi % 2), tails, carry)
    out_t = jnp.concatenate([a[:V_DIM] / a[V_DIM:V_DIM + 1] for a in accs], axis=0)
    o_ref[0] = out_t.T.astype(o_ref.dtype)


def _attn_sample_kernel(q_ref, kp_ref, vp_ref, kn_ref, vn_ref, o_ref):
    nt = (((1,), (1,)), ((), ()))
    slabs = [slice(hd * LANE, (hd + 1) * LANE) for hd in range(N_HEADS)]
    s_past = [lax.dot_general(q_ref[0, :, sl], kp_ref[0, :, sl], nt, preferred_element_type=F32) for sl in slabs]
    s_new = [lax.dot_general(q_ref[0, :, sl], kn_ref[0, :, sl], nt, preferred_element_type=F32) for sl in slabs]
    accs = []
    for sl, sp, sn in zip(slabs, s_past, s_new):
        m = jnp.maximum(jnp.max(sp, axis=-1, keepdims=True), jnp.max(sn, axis=-1, keepdims=True))
        accs.append(jnp.dot(jnp.exp2(sp - m).astype(BF16), vp_ref[0, :, sl], preferred_element_type=F32)
                    + jnp.dot(jnp.exp2(sn - m).astype(BF16), vn_ref[0, :, sl], preferred_element_type=F32))
    for pair in range(N_HEADS // 2):
        o_ref[0, :, pair * LANE:(pair + 1) * LANE] = _finish_pair(accs[2 * pair:2 * pair + 2]).astype(o_ref.dtype)


def _post_kernel(xl_ref, al_ref, cl_ref, xs_ref, as_ref, cs_ref, *rest, n_long, tiles_per_seq, seq):
    shared, (x1_ref, route_ref, cnewl_ref, cnews_ref, cnt_ref, carry_ref) = rest[:-6], rest[-6:]
    t = pl.program_id(0)

    @pl.when(t == 0)
    def _():
        cnt_ref[...] = jnp.zeros_like(cnt_ref)

    @pl.when(t < n_long)
    def _():
        _post_tile(xl_ref, al_ref, cl_ref, *shared, x1_ref, route_ref, cnewl_ref, cnt_ref, carry_ref,
                   first=t % tiles_per_seq == 0, seq=None)

    @pl.when(t >= n_long)
    def _():
        _post_tile(xs_ref, as_ref, cs_ref, *shared, x1_ref, route_ref, cnews_ref, cnt_ref, carry_ref,
                   first=None, seq=seq)


def _post_tile(x_ref, attn_ref, cinit_ref, gmix_ref, wb_ref, convw_ref, woa_ref, woc_ref, wo_ref,
               gffn_ref, wrt_ref, brt_ref, tri_ref,
               x1_ref, route_ref, cnew_ref, cnt_ref, carry_ref, *, first, seq):
    tm = x_ref.shape[1]

    if seq is None:
        @pl.when(first)
        def _():
            carry_ref[...] = cinit_ref[0]

    d = x_ref.shape[-1]
    hm = tm // 2
    groups = (0, hm)
    xs, zs = [], []
    for r0 in groups:
        x = x_ref[0, r0:r0 + hm, :]
        h = x * _rms_scale(x, d) * gmix_ref[...]
        xs.append(x)
        zs.append(jnp.dot(h.astype(BF16), wb_ref[...], preferred_element_type=F32))

    cw = convw_ref[...]
    row = lax.broadcasted_iota(jnp.int32, (hm, CONV_DIM), 0)
    if seq is None:
        c1, c2 = carry_ref[SUBLANE - 1:SUBLANE, :], carry_ref[SUBLANE - 2:SUBLANE - 1, :]
    merged = []
    for r0, z in zip(groups, zs):
        conv_b = z[:, :CONV_DIM]
        u = z[:, CONV_DIM:2 * CONV_DIM] * z[:, 2 * CONV_DIM:3 * CONV_DIM]
        if seq is None:
            u_m1 = jnp.where(row == 0, c1, pltpu.roll(u, 1, 0))
            u_m2 = jnp.where(row == 0, c2, jnp.where(row == 1, c1, pltpu.roll(u, 2, 0)))
            c1, c2 = u[hm - 1:, :], u[hm - 2:hm - 1, :]
            if r0 + hm == tm:
                carry_ref[...] = u[hm - SUBLANE:, :]
                cnew_ref[0] = u[hm - 2:, :]
        else:
            pos = row % seq
            u_m1 = jnp.where(pos == 0, cinit_ref[0, r0:r0 + hm, :], pltpu.roll(u, 1, 0))
            u_m2 = jnp.where(pos <= 1, cinit_ref[0, tm + r0:tm + r0 + hm, :], pltpu.roll(u, 2, 0))
            cnew_ref[0, r0 // seq:(r0 + hm) // seq] = u.reshape(hm // seq, seq, CONV_DIM)[:, seq - 2:, :]
        cv = cw[0:1, :] * u_m2 + cw[1:2, :] * u_m1 + cw[2:3, :] * u
        y_a = jnp.dot(attn_ref[0, r0:r0 + hm, :], woa_ref[...], preferred_element_type=F32)
        y_c = jnp.dot((conv_b * cv).astype(BF16), woc_ref[...], preferred_element_type=F32)
        gate_a = z[:, 3 * CONV_DIM:3 * CONV_DIM + d]
        gate_c = z[:, 3 * CONV_DIM + d:]
        merged.append((jax.nn.sigmoid(gate_a) * y_a + jax.nn.sigmoid(gate_c) * y_c).astype(BF16))
    x1s = [x + jnp.dot(mrg, wo_ref[...], preferred_element_type=F32) for x, mrg in zip(xs, merged)]
    nchunk = d // LANE
    for r0, x1 in zip(groups, x1s):
        for c in range(nchunk):
            x1_ref[pl.ds(r0 * nchunk + c, hm, stride=nchunk), :] = x1[:, c * LANE:(c + 1) * LANE]

    nt = (((1,), (1,)), ((), ()))
    n_rows = brt_ref.shape[0]
    logits = []
    for x1 in x1s:
        h2 = x1 * _rms_scale(x1, d) * gffn_ref[...]
        h2_hi = h2.astype(BF16)
        h2_lo = (h2 - h2_hi.astype(F32)).astype(BF16)
        lt2 = lax.dot_general(wrt_ref[...], h2_hi, nt, preferred_element_type=F32)
        logits.append(lt2[:n_rows] + lt2[n_rows:]
                      + lax.dot_general(wrt_ref[:n_rows, :], h2_lo, nt, preferred_element_type=F32) + brt_ref[...])
    sub = lax.broadcasted_iota(jnp.int32, (EXPERTS_PER_GROUP, hm), 0).astype(F32)
    none = float(EXPERTS_PER_GROUP)

    def first_argmax(v):
        vmax = jnp.max(v, axis=0, keepdims=True)
        return jnp.min(jnp.where(v == vmax, sub, none), axis=0, keepdims=True)

    n_keys = cnt_ref.shape[0]
    keys = lax.broadcasted_iota(jnp.int32, (n_keys, hm), 0).astype(F32)
    row8 = lax.broadcasted_iota(jnp.int32, (8, hm), 0)
    seen = cnt_ref[...]
    for r0, lt in zip(groups, logits):
        g_idx = first_argmax(lt[:EXPERTS_PER_GROUP])
        el = lt[EXPERTS_PER_GROUP * N_GROUPS:]
        for g in range(N_GROUPS - 2, -1, -1):
            el = jnp.where(g_idx == g, lt[EXPERTS_PER_GROUP * (g + 1):EXPERTS_PER_GROUP * (g + 2)], el)
        i1 = first_argmax(el)
        i2 = first_argmax(jnp.where(sub == i1, NEG_BIG, el))
        key = (g_idx * (EXPERTS_PER_GROUP * EXPERTS_PER_GROUP) + jnp.minimum(i1, i2) * EXPERTS_PER_GROUP
               + jnp.maximum(i1, i2))
        onehot = jnp.where(keys == key, 1.0, 0.0)
        before = jnp.dot(onehot.astype(BF16), tri_ref[...], preferred_element_type=F32)
        rank = jnp.sum(onehot * (before + seen), axis=0, keepdims=True)
        seen = seen + jnp.sum(onehot, axis=1, keepdims=True)
        route_ref[0, :, r0:r0 + hm] = jnp.where(row8 == 0, key, jnp.where(row8 == 1, rank, 0.0))
    cnt_ref[...] = seen


def _rows_loop(n, fn, unroll=8):
    def group(j, c):
        for u in range(unroll):
            fn(j * unroll + u)
        return c
    lax.fori_loop(0, n // unroll, group, 0)

    def single(i, c):
        fn(i)
        return c
    lax.fori_loop(n // unroll * unroll, n, single, 0)


def _moe_kernel(lo_ref, hi_ref, cnt_ref, dest_ref,
                x1_hbm, gffn_ref, wr_ref, br_ref, w13lo_ref, w13hi_ref, w2lo_ref, w2hi_ref,
                x2_hbm, tok_ref, xbuf, obuf, gsem, ssem):
    nb = pl.program_id(0)
    nblk = pl.num_programs(0)
    nchunk = xbuf.shape[1] // MOE_ROWS
    slot = nb % 2
    cnt = cnt_ref[nb]

    def start_gathers(blk, sl):
        base = blk * MOE_ROWS

        def one(i):
            src = x1_hbm.at[pl.ds(tok_ref[base + i] * nchunk, nchunk)]
            pltpu.make_async_copy(src, xbuf.at[sl, pl.ds(i * nchunk, nchunk)], gsem.at[sl]).start()
        _rows_loop(cnt_ref[blk], one)

    def start_scatters(blk, sl):
        base = blk * MOE_ROWS

        def one(i):
            dst = x2_hbm.at[pl.ds(tok_ref[base + i] * nchunk, nchunk)]
            pltpu.make_async_copy(obuf.at[sl, pl.ds(i * nchunk, nchunk)], dst, ssem.at[sl]).start()
        _rows_loop(cnt_ref[blk], one)

    def wait_gathers(blk, sl):
        rows = cnt_ref[blk] * nchunk
        pltpu.make_async_copy(x1_hbm.at[pl.ds(0, rows)], xbuf.at[sl, pl.ds(0, rows)], gsem.at[sl]).wait()

    def wait_scatters(blk, sl):
        rows = cnt_ref[blk] * nchunk
        pltpu.make_async_copy(obuf.at[sl, pl.ds(0, rows)], x2_hbm.at[pl.ds(0, rows)], ssem.at[sl]).wait()

    @pl.when(nb == 0)
    def _():
        def invert(i, c):
            tok_ref[dest_ref[i]] = i
            return c
        lax.fori_loop(0, dest_ref.shape[0], invert, 0, unroll=16)
        xbuf[...] = jnp.zeros_like(xbuf)

        @pl.when(cnt > 0)
        def _():
            start_gathers(0, 0)

    @pl.when(nb + 1 < nblk)
    def _():
        @pl.when(cnt_ref[nb + 1] > 0)
        def _():
            start_gathers(nb + 1, 1 - slot)

    @pl.when(nb >= 2)
    def _():
        @pl.when(cnt_ref[nb - 2] > 0)
        def _():
            wait_scatters(nb - 2, slot)

    @pl.when(cnt > 0)
    def _():
        wait_gathers(nb, slot)
        xg = jnp.concatenate([xbuf[slot, pl.ds(c, MOE_ROWS, stride=nchunk), :] for c in range(nchunk)], axis=1)
        h = (xg * _rms_scale(xg, xg.shape[-1]) * gffn_ref[...]).astype(BF16)

        lo, hi = lo_ref[nb], hi_ref[nb]
        logit = jnp.dot(h, wr_ref[...], preferred_element_type=F32) + br_ref[...]
        lane = _lane_iota(logit.shape)
        pick = lambda j: jnp.sum(jnp.where(lane == j, logit, 0.0), axis=-1, keepdims=True)
        is_g = lane < N_GROUPS
        gmax = jnp.max(jnp.where(is_g, logit, NEG_BIG), axis=-1, keepdims=True)
        g_den = jnp.sum(jnp.where(is_g, jnp.exp(logit - gmax), 0.0), axis=-1, keepdims=True)
        g_p = jnp.exp(pick(lo // EXPERTS_PER_GROUP) - gmax) / g_den
        l_lo, l_hi = pick(N_GROUPS + lo), pick(N_GROUPS + hi)
        gates = (g_p * jax.nn.sigmoid(l_lo - l_hi), g_p * jax.nn.sigmoid(l_hi - l_lo))

        abs_ = [jnp.dot(h, w13_ref[0], preferred_element_type=F32) for w13_ref in (w13lo_ref, w13hi_ref)]
        y = xg
        for ab, w2_ref, gate in zip(abs_, (w2lo_ref, w2hi_ref), gates):
            hid = jax.nn.silu(ab[:, :EXPERT_HIDDEN]) * ab[:, EXPERT_HIDDEN:]
            y = y + gate * jnp.dot(hid.astype(BF16), w2_ref[0], preferred_element_type=F32)
        for c in range(nchunk):
            obuf[slot, pl.ds(c, MOE_ROWS, stride=nchunk), :] = y[:, c * LANE:(c + 1) * LANE]
        start_scatters(nb, slot)

    @pl.when(nb == nblk - 1)
    def _():
        @pl.when(nb >= 1)
        def _():
            @pl.when(cnt_ref[nb - 1] > 0)
            def _():
                wait_scatters(nb - 1, 1 - slot)

        @pl.when(cnt > 0)
        def _():
            wait_scatters(nb, slot)


def _ple_kernel(x_ref, p_ref, gple_ref, wpg_ref, wple_ref, o_ref):
    tm = o_ref.shape[0]
    nchunk = o_ref.shape[1] // LANE
    emb = jnp.dot(p_ref[...].astype(BF16), wple_ref[...], preferred_element_type=F32)
    rows = tm // PLE_ROW_GROUPS
    for r0 in range(0, tm, rows):
        x = jnp.concatenate([x_ref[pl.ds(r0 * nchunk + c, rows, stride=nchunk), :] for c in range(nchunk)], axis=1)
        hp = (x * _rms_scale(x, x.shape[-1]) * gple_ref[...]).astype(BF16)
        gate = jax.nn.sigmoid(jnp.dot(hp, wpg_ref[...], preferred_element_type=F32))
        o_ref[r0:r0 + rows, :] = x + gate * emb[r0:r0 + rows]


def _const(shape):
    nd = len(shape)
    return pl.BlockSpec(shape, lambda *_: (0,) * nd)


def _head_slab_cols(w, width, offset=0):
    k = w.shape[0]
    w = w.reshape(k, N_HEADS, width)
    w = jnp.pad(w, ((0, 0), (0, 0), (offset, LANE - width - offset)))
    return w.reshape(k, N_HEADS * LANE)


def _prep_weights(g_mix, w_in, g_cq, w_uq, g_qn, g_qr, g_ckv, w_ukv, g_kn, g_kr, w_oa,
                  conv_w, w_oc, w_o, g_ffn, w_rg, b_rg, w_re, b_re, w1, w3, w2, g_ple, w_pg, w_ple):
    d = w_in.shape[0]
    n_mla = Q_LORA + KV_LORA
    kr_cols = jnp.pad(w_in[:, n_mla:n_mla + QK_ROPE], ((0, 0), (QK_NOPE, LANE - QK_NOPE - QK_ROPE)))
    w = {}
    w["wa"] = jnp.concatenate([w_in[:, :n_mla], kr_cols], axis=1).astype(BF16)
    w["wb"] = w_in[:, n_mla + QK_ROPE:].astype(BF16)
    w["wuq"] = _head_slab_cols(w_uq, QK_NOPE + QK_ROPE).astype(BF16)
    ukv = w_ukv.reshape(KV_LORA, N_HEADS, QK_NOPE + V_DIM)
    w["wuk"] = _head_slab_cols(ukv[:, :, :QK_NOPE].reshape(KV_LORA, -1), QK_NOPE).astype(BF16)
    w["wuv"] = _head_slab_cols(ukv[:, :, QK_NOPE:].reshape(KV_LORA, -1), V_DIM).astype(BF16)
    pad_hi = LANE - QK_NOPE - QK_ROPE
    w["gq"] = (jnp.pad(jnp.concatenate([g_qn, g_qr]), (0, pad_hi)) * (ATTN_SCALE * LOG2E))[None]
    w["gk"] = jnp.pad(g_kn, (0, LANE - QK_NOPE))[None]
    w["gkr"] = jnp.pad(g_kr, (QK_NOPE, pad_hi))[None]
    w["gmix"], w["gcq"], w["gckv"] = g_mix[None], g_cq[None], g_ckv[None]
    w["gffn"], w["gple"] = g_ffn[None], g_ple[None]
    w["convw"] = jnp.pad(conv_w, ((0, 8 - conv_w.shape[0]), (0, 0)))
    w["woa"], w["woc"], w["wo"] = w_oa.astype(BF16), w_oc.astype(BF16), w_o.astype(BF16)
    n_r = N_GROUPS + N_EXPERTS
    w["wr"] = jnp.pad(jnp.concatenate([w_rg, w_re], axis=1), ((0, 0), (0, LANE - n_r))).astype(BF16)
    w["br"] = jnp.pad(jnp.concatenate([b_rg, b_re]), (0, LANE - n_r))[None]
    pad_g = EXPERTS_PER_GROUP - N_GROUPS
    wrt = jnp.concatenate([jnp.pad(w_rg.T, ((0, pad_g), (0, 0))), w_re.T], axis=0)
    wrt_hi = wrt.astype(BF16)
    w["wrt"] = jnp.concatenate([wrt_hi, (wrt - wrt_hi.astype(F32)).astype(BF16)], axis=0)
    w["brt"] = jnp.concatenate([b_rg, jnp.full((pad_g,), NEG_BIG, F32), b_re])[:, None]
    w["w13"] = jnp.concatenate([w1, w3], axis=2).astype(BF16)
    w["w2"] = w2.astype(BF16)
    w["wpg"], w["wple"] = w_pg.astype(BF16), w_ple.astype(BF16)
    return w


def _rope_slabs(pos):
    inv = 1.0 / (ROPE_THETA ** (jnp.arange(0, QK_ROPE, 2, dtype=F32) / QK_ROPE))
    ang = pos.astype(F32)[:, None] * inv[None, :]
    cos, sin = jnp.cos(ang), jnp.sin(ang)
    n = pos.shape[0]
    half = QK_ROPE // 2
    z = lambda k: jnp.zeros((n, k), F32)
    pad_hi = LANE - QK_NOPE - QK_ROPE
    rc = jnp.concatenate([jnp.ones((n, QK_NOPE), F32), cos, cos, z(pad_hi)], axis=1)
    rs1 = jnp.concatenate([z(QK_NOPE), -sin, z(half), z(pad_hi)], axis=1)
    rs2 = jnp.concatenate([z(QK_NOPE), z(half), sin, z(pad_hi)], axis=1)
    return (rc, rs1, rs2), cos.T, sin.T


def _params(sem):
    return pltpu.CompilerParams(dimension_semantics=sem, vmem_limit_bytes=VMEM_LIMIT)


def _mla_pre(x, w, rope, tm):
    b, s, d = x.shape
    hw = N_HEADS * LANE
    tok = lambda width: pl.BlockSpec((1, tm, width), lambda i, j: (i, j, 0))
    rope_spec = pl.BlockSpec((tm, LANE), lambda i, j: (j, 0))
    consts = [w["gmix"], w["wa"], w["gcq"], w["gckv"], w["gkr"], w["wuq"], w["wuk"], w["wuv"], w["gq"], w["gk"]]
    return pl.pallas_call(
        _pre_kernel,
        grid=(b, s // tm),
        in_specs=[tok(d)] + [_const(c.shape) for c in consts] + [rope_spec] * 3,
        out_specs=[tok(hw), tok(hw), tok(hw), tok(KV_LORA), tok(QK_ROPE)],
        out_shape=[jax.ShapeDtypeStruct((b, s, hw), BF16)] * 3
        + [jax.ShapeDtypeStruct((b, s, KV_LORA), F32), jax.ShapeDtypeStruct((b, s, QK_ROPE), F32)],
        compiler_params=_params(("parallel", "parallel")),
        name="mla_pre",
    )(x, *consts, *rope)


def _mla_pre_t(x, w, rope, cos_t, sin_t, tm):
    b, s, d = x.shape
    hw = N_HEADS * LANE
    tok = lambda width: pl.BlockSpec((1, tm, width), lambda i, j: (i, j, 0))
    tiled = pl.BlockSpec((1, 1, hw, tm), lambda i, j: (i, j, 0, 0))
    rope_spec = pl.BlockSpec((tm, LANE), lambda i, j: (j, 0))
    rope_t_spec = pl.BlockSpec((QK_ROPE // 2, tm), lambda i, j: (0, j))
    gqt = jnp.broadcast_to(w["gq"].T, (LANE, tm))
    consts = [w["gmix"], w["wa"], w["gcq"], w["gckv"], w["gkr"], w["wuq"].T, w["wuk"], w["wuv"].T, gqt, w["gk"]]
    return pl.pallas_call(
        _pre_kernel_t,
        grid=(b, s // tm),
        in_specs=[tok(d)] + [_const(c.shape) for c in consts] + [rope_spec] * 3 + [rope_t_spec] * 2,
        out_specs=[tiled, tok(hw), tiled, tok(KV_LORA), tok(QK_ROPE)],
        out_shape=[jax.ShapeDtypeStruct((b, s // tm, hw, tm), BF16), jax.ShapeDtypeStruct((b, s, hw), BF16),
                   jax.ShapeDtypeStruct((b, s // tm, hw, tm), BF16),
                   jax.ShapeDtypeStruct((b, s, KV_LORA), F32), jax.ShapeDtypeStruct((b, s, QK_ROPE), F32)],
        compiler_params=_params(("parallel", "parallel")),
        name="mla_pre_t",
    )(x, *consts, *rope, cos_t, sin_t)


def _kv_past(past_lat, past_kpe, w, tm):
    b, t, _ = past_lat.shape
    hw = N_HEADS * LANE
    kpe_slab = jnp.pad(past_kpe, ((0, 0), (0, 0), (QK_NOPE, LANE - QK_NOPE - QK_ROPE)))
    tok = lambda width: pl.BlockSpec((1, tm, width), lambda i, j: (i, j, 0))
    consts = [w["wuk"], w["wuv"], w["gk"]]
    return pl.pallas_call(
        _kvpast_kernel,
        grid=(b, t // tm),
        in_specs=[tok(KV_LORA), tok(LANE)] + [_const(c.shape) for c in consts],
        out_specs=[tok(hw), tok(hw)],
        out_shape=[jax.ShapeDtypeStruct((b, t, hw), BF16)] * 2,
        compiler_params=_params(("parallel", "parallel")),
        name="kv_past",
    )(past_lat, kpe_slab, *consts)


def _attn_prompt(qt, k, vt):
    b, n_tiles, _, tq = qt.shape
    s = n_tiles * tq
    nh = ATTN_HEADS_PER_STEP
    width = nh * LANE
    return pl.pallas_call(
        functools.partial(_attn_prompt_kernel, tq=tq, nh=nh),
        grid=(b, N_HEADS // nh, n_tiles),
        in_specs=[pl.BlockSpec((1, 1, width, tq), lambda i, h, j: (i, j, h, 0)),
                  pl.BlockSpec((1, s, width), lambda i, h, j: (i, 0, h)),
                  pl.BlockSpec((1, n_tiles, width, tq), lambda i, h, j: (i, 0, h, 0))],
        out_specs=pl.BlockSpec((1, tq, nh * V_DIM), lambda i, h, j: (i, j, h)),
        out_shape=jax.ShapeDtypeStruct((b, s, N_HEADS * V_DIM), BF16),
        scratch_shapes=[pltpu.VMEM((tq, tq), F32)] * (2 * nh) + [pltpu.VMEM((tq, tq), BF16)] * (2 * nh),
        compiler_params=_params(("parallel", "parallel", "arbitrary")),
        name="attn_prompt",
    )(qt, k, vt)


def _attn_sample(q, kp, vp, kn, vn):
    b, s, _ = q.shape
    t = kp.shape[1]
    hw = N_HEADS * LANE
    blk = lambda rows: pl.BlockSpec((1, rows, hw), lambda i: (i, 0, 0))
    return pl.pallas_call(
        _attn_sample_kernel,
        grid=(b,),
        in_specs=[blk(s), blk(t), blk(t), blk(s), blk(s)],
        out_specs=pl.BlockSpec((1, s, N_HEADS * V_DIM), lambda i: (i, 0, 0)),
        out_shape=jax.ShapeDtypeStruct((b, s, N_HEADS * V_DIM), BF16),
        compiler_params=_params(("parallel",)),
        name="attn_sample",
    )(q, kp, vp, kn, vn)


def _post(x_long, attn_long, x_short, attn_short, cinit_short, w, tm, seq):
    b, s, d = x_long.shape
    nchunk = d // LANE
    n_keys = N_EXPERTS * EXPERTS_PER_GROUP
    tiles_per_seq = s // tm
    n_long = b * tiles_per_seq
    n_short = x_short.shape[1] // tm
    long_tile = lambda t: jnp.minimum(t, n_long - 1)
    short_tile = lambda t: jnp.maximum(t - n_long, 0)
    long_tok = lambda width: pl.BlockSpec(
        (1, tm, width), lambda t: (long_tile(t) // tiles_per_seq, long_tile(t) % tiles_per_seq, 0))
    short_tok = lambda width: pl.BlockSpec((1, tm, width), lambda t: (0, short_tile(t), 0))
    cinit_long = jnp.zeros((b, SUBLANE, CONV_DIM), F32)
    hm = tm // 2
    tri = (jnp.arange(hm)[:, None] < jnp.arange(hm)[None, :]).astype(BF16)
    consts = [w["gmix"], w["wb"], w["convw"], w["woa"], w["woc"], w["wo"], w["gffn"], w["wrt"], w["brt"], tri]
    in_specs = [long_tok(d), long_tok(N_HEADS * V_DIM),
                pl.BlockSpec((1, SUBLANE, CONV_DIM), lambda t: (long_tile(t) // tiles_per_seq, 0, 0)),
                short_tok(d), short_tok(N_HEADS * V_DIM),
                pl.BlockSpec((1, 2 * tm, CONV_DIM), lambda t: (short_tile(t), 0, 0))]
    return pl.pallas_call(
        functools.partial(_post_kernel, n_long=n_long, tiles_per_seq=tiles_per_seq, seq=seq),
        grid=(n_long + n_short,),
        in_specs=in_specs + [_const(c.shape) for c in consts],
        out_specs=[pl.BlockSpec((tm * nchunk, LANE), lambda t: (t, 0)),
                   pl.BlockSpec((1, 8, tm), lambda t: (t, 0, 0)),
                   pl.BlockSpec((1, 2, CONV_DIM), lambda t: (long_tile(t) // tiles_per_seq, 0, 0)),
                   pl.BlockSpec((1, tm // seq, 2, CONV_DIM), lambda t: (short_tile(t), 0, 0, 0)),
                   _const((n_keys, 1))],
        out_shape=[jax.ShapeDtypeStruct(((n_long + n_short) * tm * nchunk, LANE), F32),
                   jax.ShapeDtypeStruct((n_long + n_short, 8, tm), F32),
                   jax.ShapeDtypeStruct((b, 2, CONV_DIM), F32),
                   jax.ShapeDtypeStruct((n_short, tm // seq, 2, CONV_DIM), F32),
                   jax.ShapeDtypeStruct((n_keys, 1), F32)],
        scratch_shapes=[pltpu.VMEM((SUBLANE, CONV_DIM), F32)],
        compiler_params=_params(("arbitrary",)),
        name="post",
    )(x_long, attn_long, cinit_long, x_short, attn_short, cinit_short, *consts)


def _route_tables(key, rank, counts, n):
    n_keys = counts.shape[0]
    padded = (counts + MOE_ROWS - 1) // MOE_ROWS * MOE_ROWS
    pend = jnp.cumsum(padded)
    pstart = pend - padded
    ids = jnp.arange(n_keys, dtype=jnp.int32)
    dest = rank + jnp.sum(jnp.where(key[:, None] == ids[None, :], pstart[None, :], 0), axis=1)
    nblk = n // MOE_ROWS + N_PAIR_BUCKETS
    blk_start = jnp.arange(nblk, dtype=jnp.int32) * MOE_ROWS
    blk_hot = (blk_start[:, None] >= pstart[None, :]) & (blk_start[:, None] < pend[None, :])
    blk_key = jnp.sum(jnp.where(blk_hot, ids[None, :], 0), axis=1)
    blk_cnt = jnp.sum(jnp.where(blk_hot, jnp.minimum(counts[None, :] - (blk_start[:, None] - pstart[None, :]),
                                                      MOE_ROWS), 0), axis=1)
    any_hot = jnp.any(blk_hot, axis=1)
    blk_key = jnp.where(any_hot, blk_key, n_keys - 1)
    blk_lo = blk_key // EXPERTS_PER_GROUP
    blk_hi = blk_lo // EXPERTS_PER_GROUP * EXPERTS_PER_GROUP + blk_key % EXPERTS_PER_GROUP
    return blk_lo, blk_hi, blk_cnt.astype(jnp.int32), dest.astype(jnp.int32)


def _moe(x1_all, key, rank, counts, w):
    rows, _ = x1_all.shape
    d = w["gffn"].shape[1]
    nchunk = d // LANE
    n = rows // nchunk
    blk_lo, blk_hi, blk_cnt, dest = _route_tables(key, rank, counts, n)
    nblk = blk_lo.shape[0]
    w13_spec = lambda ref_idx: pl.BlockSpec((1, d, 2 * EXPERT_HIDDEN),
                                            lambda i, lo, hi, cnt, dst: ((lo, hi)[ref_idx][i], 0, 0))
    w2_spec = lambda ref_idx: pl.BlockSpec((1, EXPERT_HIDDEN, d),
                                           lambda i, lo, hi, cnt, dst: ((lo, hi)[ref_idx][i], 0, 0))
    buf = pltpu.VMEM((2, MOE_ROWS * nchunk, LANE), F32)
    grid_spec = pltpu.PrefetchScalarGridSpec(
        num_scalar_prefetch=4,
        grid=(nblk,),
        in_specs=[pl.BlockSpec(memory_space=pl.ANY),
                  pl.BlockSpec((1, d), lambda i, *_: (0, 0)),
                  pl.BlockSpec(w["wr"].shape, lambda i, *_: (0, 0)),
                  pl.BlockSpec(w["br"].shape, lambda i, *_: (0, 0)),
                  w13_spec(0), w13_spec(1), w2_spec(0), w2_spec(1)],
        out_specs=pl.BlockSpec(memory_space=pl.ANY),
        scratch_shapes=[pltpu.SMEM((nblk * MOE_ROWS,), jnp.int32), buf, buf,
                        pltpu.SemaphoreType.DMA((2,)), pltpu.SemaphoreType.DMA((2,))],
    )
    return pl.pallas_call(
        _moe_kernel,
        grid_spec=grid_spec,
        out_shape=jax.ShapeDtypeStruct((rows, LANE), F32),
        compiler_params=_params(("arbitrary",)),
        name="moe",
    )(blk_lo, blk_hi, blk_cnt, dest, x1_all, w["gffn"], w["wr"], w["br"], w["w13"], w["w13"], w["w2"], w["w2"])


def _ple(x2_all, row_off, p, w, tm):
    n, pd = p.shape
    d = w["gple"].shape[1]
    nchunk = d // LANE
    off = row_off // tm
    consts = [w["gple"], w["wpg"], w["wple"]]
    return pl.pallas_call(
        _ple_kernel,
        grid=(n // tm,),
        in_specs=[pl.BlockSpec((tm * nchunk, LANE), lambda i: (off + i, 0)), pl.BlockSpec((tm, pd), lambda i: (i, 0))]
        + [_const(c.shape) for c in consts],
        out_specs=pl.BlockSpec((tm, d), lambda i: (i, 0)),
        out_shape=jax.ShapeDtypeStruct((n, d), F32),
        compiler_params=_params(("parallel",)),
        name="ple",
    )(x2_all, p, *consts)


def _layer(xp, xs, pp, ps, past_lat, past_kpe, past_conv, w):
    bp, sp, d = xp.shape
    bs, ss, _ = xs.shape
    n_p, n_s = bp * sp, bs * ss
    past_len = past_lat.shape[1]
    tm_p, tm_s = min(TOKEN_TILE, sp), min(TOKEN_TILE, n_s)
    tm_post = min(tm_p, tm_s)

    rope, cos_t, sin_t = _rope_slabs(jnp.arange(sp))
    qt, k, vt, lat_p, kpe_p = _mla_pre_t(xp, w, rope, cos_t, sin_t, tm_p)
    attn_p = _attn_prompt(qt, k, vt)

    rope, _, _ = _rope_slabs(past_len + jnp.arange(n_s) % ss)
    xs_rows = xs.reshape(1, n_s, d)
    q, k, v, lat_s, kpe_s = _mla_pre(xs_rows, w, rope, tm_s)
    by_seq = lambda a: a.reshape(bs, ss, a.shape[-1])
    kp, vp = _kv_past(past_lat, past_kpe, w, min(TOKEN_TILE, past_len))
    attn_s = _attn_sample(by_seq(q), kp, vp, by_seq(k), by_seq(v)).reshape(1, n_s, -1)
    lat_s, kpe_s = by_seq(lat_s), by_seq(kpe_s)
    in_tiles = lambda a: a.reshape(n_s // tm_post, tm_post, CONV_DIM)
    n_hist = past_conv.shape[1]
    cinit_s = jnp.concatenate(
        [in_tiles(jnp.pad(past_conv[:, n_hist - 1:], ((0, 0), (0, ss - 1), (0, 0)))),
         in_tiles(jnp.pad(past_conv, ((0, 0), (0, ss - n_hist), (0, 0))))], axis=1)

    x1_all, route, conv_p, conv_s, cnt = _post(xp, attn_p, xs_rows, attn_s, cinit_s, w, tm_post, ss)
    conv_s = conv_s.reshape(bs, n_hist, CONV_DIM)
    key = route[:, 0].reshape(-1).astype(jnp.int32)
    rank = route[:, 1].reshape(-1).astype(jnp.int32)
    x2_all = _moe(x1_all, key, rank, cnt[:, 0].astype(jnp.int32), w)
    yp = _ple(x2_all, 0, pp.reshape(n_p, -1), w, min(TOKEN_TILE, n_p)).reshape(bp, sp, d)
    ys = _ple(x2_all, n_p, ps.reshape(n_s, -1), w, min(TOKEN_TILE, n_s)).reshape(bs, ss, d)
    return yp, ys, (lat_p, kpe_p, conv_p, lat_s, kpe_s, conv_s)


def kernel(x_prompt, x_sample, cache_kv_latent, cache_k_rope, state_conv, p_prompt, p_sample,
           g_mix, w_in, g_cq, w_uq, g_qn, g_qr, g_ckv, w_ukv, g_kn, g_kr, w_oa,
           conv_w, w_oc, w_o, g_ffn, w_rg, b_rg, w_re, b_re, w1, w3, w2, g_ple, w_pg, w_ple):
    depth = g_mix.shape[0]
    xp, xs = x_prompt, x_sample
    outs = [[] for _ in range(6)]
    for i in range(depth):
        w = _prep_weights(g_mix[i], w_in[i], g_cq[i], w_uq[i], g_qn[i], g_qr[i], g_ckv[i], w_ukv[i],
                          g_kn[i], g_kr[i], w_oa[i], conv_w[i], w_oc[i], w_o[i], g_ffn[i], w_rg[i], b_rg[i],
                          w_re[i], b_re[i], w1[i], w3[i], w2[i], g_ple[i], w_pg[i], w_ple[i])
        xp, xs, new = _layer(xp, xs, p_prompt[i], p_sample[i], cache_kv_latent[i], cache_k_rope[i], state_conv[i], w)
        for o, a in zip(outs, new):
            o.append(a)
    return (xp, xs) + tuple(jnp.stack(o, axis=0) for o in outs)
```

```python
import functools
import math

import jax
import jax.numpy as jnp
from jax import lax
from jax.experimental import pallas as pl
from jax.experimental.pallas import tpu as pltpu

F32 = jnp.float32
BF16 = jnp.bfloat16

LANE = 128
SUBLANE = 8
TOKEN_TILE = 512
CHUNK = 64
N_HEADS = 8
QK_NOPE = 64
QK_ROPE = 32
V_DIM = 64
Q_LORA = 256
KV_LORA = 256
CONV_DIM = 512
N_GROUPS = 4
EXPERTS_PER_GROUP = 8
N_EXPERTS = N_GROUPS * EXPERTS_PER_GROUP
EXPERT_HIDDEN = 256
ROPE_THETA = 10000.0
EPS = 1e-6
ATTN_SCALE = (QK_NOPE + QK_ROPE) ** -0.5
LOG2E = math.log2(math.e)
NEG_BIG = -1e30
MOE_ROWS = 128
ATTN_HEADS_PER_STEP = 4
PLE_ROW_GROUPS = 2
N_PAIR_BUCKETS = N_GROUPS * (EXPERTS_PER_GROUP * (EXPERTS_PER_GROUP - 1) // 2)
VMEM_LIMIT = 56 * 1024 * 1024


def _rms_scale(x, n):
    return lax.rsqrt(jnp.sum(x * x, axis=-1, keepdims=True) * (1.0 / n) + EPS)


def _lane_iota(shape):
    return lax.broadcasted_iota(jnp.int32, shape, len(shape) - 1)


def _rope(t, rc, rs1, rs2):
    return t * rc + pltpu.roll(t, LANE - QK_ROPE // 2, 1) * rs1 + pltpu.roll(t, QK_ROPE // 2, 1) * rs2


def _pre_latents(x_ref, gmix_ref, wa_ref, gcq_ref, gckv_ref, gkr_ref, rope_refs, lat_ref, kpe_ref):
    x = x_ref[0]
    h = x * _rms_scale(x, x.shape[-1]) * gmix_ref[...]
    z = jnp.dot(h.astype(BF16), wa_ref[...], preferred_element_type=F32)
    cq = z[:, :Q_LORA]
    ckv = z[:, Q_LORA:Q_LORA + KV_LORA]
    kr = z[:, Q_LORA + KV_LORA:]
    cqn = cq * _rms_scale(cq, Q_LORA) * gcq_ref[...]
    lat = ckv * _rms_scale(ckv, KV_LORA) * gckv_ref[...]
    lat_ref[0] = lat
    krn = kr * _rms_scale(kr, QK_ROPE) * gkr_ref[...]
    kpe = _rope(krn, *(r[...] for r in rope_refs))
    kpe_ref[0] = kpe[:, QK_NOPE:QK_NOPE + QK_ROPE]
    return cqn, lat, kpe


def _store_keys(k_ref, kf, gk, kpe):
    for hd in range(N_HEADS):
        sl = slice(hd * LANE, (hd + 1) * LANE)
        ks = kf[:, sl]
        k_ref[0, :, sl] = (ks * _rms_scale(ks, QK_NOPE) * gk + kpe).astype(BF16)


def _pre_kernel(x_ref, gmix_ref, wa_ref, gcq_ref, gckv_ref, gkr_ref, wuq_ref, wuk_ref, wuv_ref,
                gq_ref, gk_ref, rc_ref, rs1_ref, rs2_ref,
                q_ref, k_ref, v_ref, lat_ref, kpe_ref):
    rope_refs = (rc_ref, rs1_ref, rs2_ref)
    cqn, lat, kpe = _pre_latents(x_ref, gmix_ref, wa_ref, gcq_ref, gckv_ref, gkr_ref, rope_refs, lat_ref, kpe_ref)
    rc, rs1, rs2 = (r[...] for r in rope_refs)
    is_nope = _lane_iota(kpe.shape) < QK_NOPE
    latb = lat.astype(BF16)
    qf = jnp.dot(cqn.astype(BF16), wuq_ref[...], preferred_element_type=F32)
    vf = jnp.dot(latb, wuv_ref[...], preferred_element_type=F32)
    _store_keys(k_ref, jnp.dot(latb, wuk_ref[...], preferred_element_type=F32), gk_ref[...], kpe)
    gq = gq_ref[...]
    ones_hi = jnp.where(is_nope, 0.0, 1.0)
    for hd in range(N_HEADS):
        sl = slice(hd * LANE, (hd + 1) * LANE)
        qs = qf[:, sl]
        sq = qs * qs
        ss_all = jnp.sum(sq, axis=-1, keepdims=True)
        ss_n = jnp.sum(jnp.where(is_nope, sq, 0.0), axis=-1, keepdims=True)
        r = jnp.where(is_nope, lax.rsqrt(ss_n * (1.0 / QK_NOPE) + EPS),
                      lax.rsqrt((ss_all - ss_n) * (1.0 / QK_ROPE) + EPS))
        q_ref[0, :, sl] = _rope(qs * r * gq, rc, rs1, rs2).astype(BF16)
        v_ref[0, :, sl] = (vf[:, sl] + ones_hi).astype(BF16)


def _pre_kernel_t(x_ref, gmix_ref, wa_ref, gcq_ref, gckv_ref, gkr_ref, wuqt_ref, wuk_ref, wuvt_ref,
                  gqt_ref, gk_ref, rc_ref, rs1_ref, rs2_ref, cos_ref, sin_ref,
                  qt_ref, k_ref, vt_ref, lat_ref, kpe_ref):
    rope_refs = (rc_ref, rs1_ref, rs2_ref)
    cqn, lat, kpe = _pre_latents(x_ref, gmix_ref, wa_ref, gcq_ref, gckv_ref, gkr_ref, rope_refs, lat_ref, kpe_ref)
    latb = lat.astype(BF16)
    _store_keys(k_ref, jnp.dot(latb, wuk_ref[...], preferred_element_type=F32), gk_ref[...], kpe)
    tm = cqn.shape[0]
    qft = jnp.dot(wuqt_ref[...], cqn.T.astype(BF16), preferred_element_type=F32)
    vft = jnp.dot(wuvt_ref[...], lat.T.astype(BF16), preferred_element_type=F32)
    gq, cos, sin = gqt_ref[...], cos_ref[...], sin_ref[...]
    half = QK_ROPE // 2
    ones_lo = jnp.where(lax.broadcasted_iota(jnp.int32, (LANE, tm), 0) < V_DIM, 0.0, 1.0)
    pad = jnp.zeros((LANE - QK_NOPE - QK_ROPE, tm), F32)
    for hd in range(N_HEADS):
        rows = slice(hd * LANE, (hd + 1) * LANE)
        qs = qft[rows]
        sq = qs * qs
        r_n = lax.rsqrt(jnp.sum(sq[:QK_NOPE], axis=0, keepdims=True) * (1.0 / QK_NOPE) + EPS)
        r_p = lax.rsqrt(jnp.sum(sq[QK_NOPE:QK_NOPE + QK_ROPE], axis=0, keepdims=True) * (1.0 / QK_ROPE) + EPS)
        nope = qs[:QK_NOPE] * r_n * gq[:QK_NOPE]
        x1 = qs[QK_NOPE:QK_NOPE + half] * r_p * gq[QK_NOPE:QK_NOPE + half]
        x2 = qs[QK_NOPE + half:QK_NOPE + QK_ROPE] * r_p * gq[QK_NOPE + half:QK_NOPE + QK_ROPE]
        slab = jnp.concatenate([nope, x1 * cos - x2 * sin, x1 * sin + x2 * cos, pad], axis=0)
        qt_ref[0, 0, rows, :] = slab.astype(BF16)
        vt_ref[0, 0, rows, :] = (vft[rows] + ones_lo).astype(BF16)


def _kvpast_kernel(lat_ref, kpe_ref, wuk_ref, wuv_ref, gk_ref, k_ref, v_ref):
    latb = lat_ref[0].astype(BF16)
    kf = jnp.dot(latb, wuk_ref[...], preferred_element_type=F32)
    vf = jnp.dot(latb, wuv_ref[...], preferred_element_type=F32)
    kpe = kpe_ref[0]
    lane = _lane_iota(kpe.shape)
    ones_hi = jnp.where(lane < QK_NOPE, 0.0, 1.0)
    gk = gk_ref[...]
    for hd in range(N_HEADS):
        sl = slice(hd * LANE, (hd + 1) * LANE)
        ks = kf[:, sl]
        k_ref[0, :, sl] = (ks * _rms_scale(ks, QK_NOPE) * gk + kpe).astype(BF16)
        v_ref[0, :, sl] = (vf[:, sl] + ones_hi).astype(BF16)


def _finish_pair(accs):
    outs = [a / pltpu.roll(a, V_DIM, 1) for a in accs]
    lane = _lane_iota(outs[0].shape)
    return jnp.where(lane < V_DIM, outs[0], pltpu.roll(outs[1], V_DIM, 1))


def _attn_prompt_kernel(qt_ref, k_ref, vt_ref, o_ref, *scratch, tq, nh):
    s_refs = (scratch[:nh], scratch[nh:2 * nh])
    p_refs = (scratch[2 * nh:3 * nh], scratch[3 * nh:])
    qi = pl.program_id(2)
    key_pos = lax.broadcasted_iota(jnp.int32, (tq, tq), 0)
    query_pos = lax.broadcasted_iota(jnp.int32, (tq, tq), 1)
    diag_mask = (key_pos // CHUNK) <= (query_pos // CHUNK)
    heads = [slice(hh * LANE, (hh + 1) * LANE) for hh in range(nh)]

    def scores(i, slot, hh, mask):
        start = pl.multiple_of(i * tq, tq)
        s = jnp.dot(k_ref[0, pl.ds(start, tq), heads[hh]], qt_ref[0, 0, heads[hh], :], preferred_element_type=F32)
        if mask is not None:
            s = jnp.where(mask, s, NEG_BIG)
        s_refs[slot][hh][...] = s
        return jnp.max(s, axis=0, keepdims=True)

    def softmax(slot, hh, m, tile_max):
        m_new = jnp.maximum(m, tile_max)
        p_refs[slot][hh][...] = jnp.exp2(s_refs[slot][hh][...] - m_new).astype(BF16)
        return m_new, jnp.exp2(m - m_new)

    def accumulate(i, slot, hh, alpha, acc):
        pv = jnp.dot(vt_ref[0, jnp.maximum(i, 0), heads[hh], :], p_refs[slot][hh][...],
                     preferred_element_type=F32)
        return acc * alpha + pv

    def iteration(i, slot, carry, next_mask=None):
        stats = [softmax(slot, hh, m, tile_max) for hh, (m, tile_max, _, _) in enumerate(carry)]
        next_max = [scores(i + 1, 1 - slot, hh, next_mask) for hh in range(nh)]
        accs = [accumulate(i - 1, 1 - slot, hh, alpha, acc) for hh, (_, _, alpha, acc) in enumerate(carry)]
        return tuple((m, tmax, alpha, acc) for (m, alpha), tmax, acc in zip(stats, next_max, accs))

    def last(slot, carry):
        accs = []
        for hh, (m, tile_max, alpha, acc) in enumerate(carry):
            acc = accumulate(qi - 1, 1 - slot, hh, alpha, acc)
            m, alpha = softmax(slot, hh, m, tile_max)
            accs.append(accumulate(qi, slot, hh, alpha, acc))
        return tuple(accs)

    first_mask = diag_mask | (qi > 0)
    init = []
    for hh in range(nh):
        p_refs[1][hh][...] = jnp.zeros((tq, tq), BF16)
        init.append((jnp.full((1, tq), NEG_BIG, F32), scores(0, 0, hh, first_mask), jnp.ones((1, tq), F32),
                     jnp.zeros((LANE, tq), F32)))
    carry = lax.fori_loop(0, (qi - 1) // 2, lambda j, c: iteration(2 * j + 1, 1, iteration(2 * j, 0, c)),
                          tuple(init))
    tails = [lambda c: last(0, c),
             lambda c: last(1, iteration(qi - 1, 0, c, diag_mask)),
             lambda c: last(0, iteration(qi - 1, 1, iteration(qi - 2, 0, c), diag_mask))]
    accs = lax.switch(jnp.where(qi == 0, 0, 2 - qi % 2), tails, carry)
    out_t = jnp.concatenate([a[:V_DIM] / a[V_DIM:V_DIM + 1] for a in accs], axis=0)
    o_ref[0] = out_t.T.astype(o_ref.dtype)


def _attn_sample_kernel(q_ref, kp_ref, vp_ref, kn_ref, vn_ref, o_ref):
    nt = (((1,), (1,)), ((), ()))
    slabs = [slice(hd * LANE, (hd + 1) * LANE) for hd in range(N_HEADS)]
    s_past = [lax.dot_general(q_ref[0, :, sl], kp_ref[0, :, sl], nt, preferred_element_type=F32) for sl in slabs]
    s_new = [lax.dot_general(q_ref[0, :, sl], kn_ref[0, :, sl], nt, preferred_element_type=F32) for sl in slabs]
    accs = []
    for sl, sp, sn in zip(slabs, s_past, s_new):
        m = jnp.maximum(jnp.max(sp, axis=-1, keepdims=True), jnp.max(sn, axis=-1, keepdims=True))
        accs.append(jnp.dot(jnp.exp2(sp - m).astype(BF16), vp_ref[0, :, sl], preferred_element_type=F32)
                    + jnp.dot(jnp.exp2(sn - m).astype(BF16), vn_ref[0, :, sl], preferred_element_type=F32))
    for pair in range(N_HEADS // 2):
        o_ref[0, :, pair * LANE:(pair + 1) * LANE] = _finish_pair(accs[2 * pair:2 * pair + 2]).astype(o_ref.dtype)


def _post_kernel(xl_ref, al_ref, cl_ref, xs_ref, as_ref, cs_ref, *rest, n_long, tiles_per_seq, seq):
    shared, (x1_ref, route_ref, cnewl_ref, cnews_ref, cnt_ref, carry_ref) = rest[:-6], rest[-6:]
    t = pl.program_id(0)

    @pl.when(t == 0)
    def _():
        cnt_ref[...] = jnp.zeros_like(cnt_ref)

    @pl.when(t < n_long)
    def _():
        _post_tile(xl_ref, al_ref, cl_ref, *shared, x1_ref, route_ref, cnewl_ref, cnt_ref, carry_ref,
                   first=t % tiles_per_seq == 0, seq=None)

    @pl.when(t >= n_long)
    def _():
        _post_tile(xs_ref, as_ref, cs_ref, *shared, x1_ref, route_ref, cnews_ref, cnt_ref, carry_ref,
                   first=None, seq=seq)


def _post_tile(x_ref, attn_ref, cinit_ref, gmix_ref, wb_ref, convw_ref, woa_ref, woc_ref, wo_ref,
               gffn_ref, wrt_ref, brt_ref, tri_ref,
               x1_ref, route_ref, cnew_ref, cnt_ref, carry_ref, *, first, seq):
    tm = x_ref.shape[1]

    if seq is None:
        @pl.when(first)
        def _():
            carry_ref[...] = cinit_ref[0]

    d = x_ref.shape[-1]
    hm = tm // 2
    groups = (0, hm)
    xs, zs = [], []
    for r0 in groups:
        x = x_ref[0, r0:r0 + hm, :]
        h = x * _rms_scale(x, d) * gmix_ref[...]
        xs.append(x)
        zs.append(jnp.dot(h.astype(BF16), wb_ref[...], preferred_element_type=F32))

    cw = convw_ref[...]
    row = lax.broadcasted_iota(jnp.int32, (hm, CONV_DIM), 0)
    if seq is None:
        c1, c2 = carry_ref[SUBLANE - 1:SUBLANE, :], carry_ref[SUBLANE - 2:SUBLANE - 1, :]
    merged = []
    for r0, z in zip(groups, zs):
        conv_b = z[:, :CONV_DIM]
        u = z[:, CONV_DIM:2 * CONV_DIM] * z[:, 2 * CONV_DIM:3 * CONV_DIM]
        if seq is None:
            u_m1 = jnp.where(row == 0, c1, pltpu.roll(u, 1, 0))
            u_m2 = jnp.where(row == 0, c2, jnp.where(row == 1, c1, pltpu.roll(u, 2, 0)))
            c1, c2 = u[hm - 1:, :], u[hm - 2:hm - 1, :]
            if r0 + hm == tm:
                carry_ref[...] = u[hm - SUBLANE:, :]
                cnew_ref[0] = u[hm - 2:, :]
        else:
            pos = row % seq
            u_m1 = jnp.where(pos == 0, cinit_ref[0, r0:r0 + hm, :], pltpu.roll(u, 1, 0))
            u_m2 = jnp.where(pos <= 1, cinit_ref[0, tm + r0:tm + r0 + hm, :], pltpu.roll(u, 2, 0))
            cnew_ref[0, r0 // seq:(r0 + hm) // seq] = u.reshape(hm // seq, seq, CONV_DIM)[:, seq - 2:, :]
        cv = cw[0:1, :] * u_m2 + cw[1:2, :] * u_m1 + cw[2:3, :] * u
        y_a = jnp.dot(attn_ref[0, r0:r0 + hm, :], woa_ref[...], preferred_element_type=F32)
        y_c = jnp.dot((conv_b * cv).astype(BF16), woc_ref[...], preferred_element_type=F32)
        gate_a = z[:, 3 * CONV_DIM:3 * CONV_DIM + d]
        gate_c = z[:, 3 * CONV_DIM + d:]
        merged.append((jax.nn.sigmoid(gate_a) * y_a + jax.nn.sigmoid(gate_c) * y_c).astype(BF16))
    x1s = [x + jnp.dot(mrg, wo_ref[...], preferred_element_type=F32) for x, mrg in zip(xs, merged)]
    nchunk = d // LANE
    for r0, x1 in zip(groups, x1s):
        for c in range(nchunk):
            x1_ref[pl.ds(r0 * nchunk + c, hm, stride=nchunk), :] = x1[:, c * LANE:(c + 1) * LANE]

    nt = (((1,), (1,)), ((), ()))
    n_rows = brt_ref.shape[0]
    logits = []
    for x1 in x1s:
        h2 = x1 * _rms_scale(x1, d) * gffn_ref[...]
        h2_hi = h2.astype(BF16)
        h2_lo = (h2 - h2_hi.astype(F32)).astype(BF16)
        lt2 = lax.dot_general(wrt_ref[...], h2_hi, nt, preferred_element_type=F32)
        logits.append(lt2[:n_rows] + lt2[n_rows:]
                      + lax.dot_general(wrt_ref[:n_rows, :], h2_lo, nt, preferred_element_type=F32) + brt_ref[...])
    sub = lax.broadcasted_iota(jnp.int32, (EXPERTS_PER_GROUP, hm), 0).astype(F32)
    none = float(EXPERTS_PER_GROUP)

    def first_argmax(v):
        vmax = jnp.max(v, axis=0, keepdims=True)
        return jnp.min(jnp.where(v == vmax, sub, none), axis=0, keepdims=True)

    n_keys = cnt_ref.shape[0]
    keys = lax.broadcasted_iota(jnp.int32, (n_keys, hm), 0).astype(F32)
    row8 = lax.broadcasted_iota(jnp.int32, (8, hm), 0)
    seen = cnt_ref[...]
    for r0, lt in zip(groups, logits):
        g_idx = first_argmax(lt[:EXPERTS_PER_GROUP])
        el = lt[EXPERTS_PER_GROUP * N_GROUPS:]
        for g in range(N_GROUPS - 2, -1, -1):
            el = jnp.where(g_idx == g, lt[EXPERTS_PER_GROUP * (g + 1):EXPERTS_PER_GROUP * (g + 2)], el)
        i1 = first_argmax(el)
        i2 = first_argmax(jnp.where(sub == i1, NEG_BIG, el))
        key = (g_idx * (EXPERTS_PER_GROUP * EXPERTS_PER_GROUP) + jnp.minimum(i1, i2) * EXPERTS_PER_GROUP
               + jnp.maximum(i1, i2))
        onehot = jnp.where(keys == key, 1.0, 0.0)
        before = jnp.dot(onehot.astype(BF16), tri_ref[...], preferred_element_type=F32)
        rank = jnp.sum(onehot * (before + seen), axis=0, keepdims=True)
        seen = seen + jnp.sum(onehot, axis=1, keepdims=True)
        route_ref[0, :, r0:r0 + hm] = jnp.where(row8 == 0, key, jnp.where(row8 == 1, rank, 0.0))
    cnt_ref[...] = seen


def _rows_loop(n, fn, unroll=8):
    def group(j, c):
        for u in range(unroll):
            fn(j * unroll + u)
        return c
    lax.fori_loop(0, n // unroll, group, 0)

    def single(i, c):
        fn(i)
        return c
    lax.fori_loop(n // unroll * unroll, n, single, 0)


def _moe_kernel(lo_ref, hi_ref, cnt_ref, dest_ref,
                x1_hbm, gffn_ref, wr_ref, br_ref, w13lo_ref, w13hi_ref, w2lo_ref, w2hi_ref,
                x2_hbm, tok_ref, xbuf, obuf, gsem, ssem):
    nb = pl.program_id(0)
    nblk = pl.num_programs(0)
    nchunk = xbuf.shape[1] // MOE_ROWS
    slot = nb % 2
    cnt = cnt_ref[nb]

    def start_gathers(blk, sl):
        base = blk * MOE_ROWS

        def one(i):
            src = x1_hbm.at[pl.ds(tok_ref[base + i] * nchunk, nchunk)]
            pltpu.make_async_copy(src, xbuf.at[sl, pl.ds(i * nchunk, nchunk)], gsem.at[sl]).start()
        _rows_loop(cnt_ref[blk], one)

    def start_scatters(blk, sl):
        base = blk * MOE_ROWS

        def one(i):
            dst = x2_hbm.at[pl.ds(tok_ref[base + i] * nchunk, nchunk)]
            pltpu.make_async_copy(obuf.at[sl, pl.ds(i * nchunk, nchunk)], dst, ssem.at[sl]).start()
        _rows_loop(cnt_ref[blk], one)

    def wait_gathers(blk, sl):
        rows = cnt_ref[blk] * nchunk
        pltpu.make_async_copy(x1_hbm.at[pl.ds(0, rows)], xbuf.at[sl, pl.ds(0, rows)], gsem.at[sl]).wait()

    def wait_scatters(blk, sl):
        rows = cnt_ref[blk] * nchunk
        pltpu.make_async_copy(obuf.at[sl, pl.ds(0, rows)], x2_hbm.at[pl.ds(0, rows)], ssem.at[sl]).wait()

    @pl.when(nb == 0)
    def _():
        def invert(i, c):
            tok_ref[dest_ref[i]] = i
            return c
        lax.fori_loop(0, dest_ref.shape[0], invert, 0, unroll=16)
        xbuf[...] = jnp.zeros_like(xbuf)

        @pl.when(cnt > 0)
        def _():
            start_gathers(0, 0)

    @pl.when(nb + 1 < nblk)
    def _():
        @pl.when(cnt_ref[nb + 1] > 0)
        def _():
            start_gathers(nb + 1, 1 - slot)

    @pl.when(nb >= 2)
    def _():
        @pl.when(cnt_ref[nb - 2] > 0)
        def _():
            wait_scatters(nb - 2, slot)

    @pl.when(cnt > 0)
    def _():
        wait_gathers(nb, slot)
        xg = jnp.concatenate([xbuf[slot, pl.ds(c, MOE_ROWS, stride=nchunk), :] for c in range(nchunk)], axis=1)
        h = (xg * _rms_scale(xg, xg.shape[-1]) * gffn_ref[...]).astype(BF16)

        lo, hi = lo_ref[nb], hi_ref[nb]
        logit = jnp.dot(h, wr_ref[...], preferred_element_type=F32) + br_ref[...]
        lane = _lane_iota(logit.shape)
        pick = lambda j: jnp.sum(jnp.where(lane == j, logit, 0.0), axis=-1, keepdims=True)
        is_g = lane < N_GROUPS
        gmax = jnp.max(jnp.where(is_g, logit, NEG_BIG), axis=-1, keepdims=True)
        g_den = jnp.sum(jnp.where(is_g, jnp.exp(logit - gmax), 0.0), axis=-1, keepdims=True)
        g_p = jnp.exp(pick(lo // EXPERTS_PER_GROUP) - gmax) / g_den
        l_lo, l_hi = pick(N_GROUPS + lo), pick(N_GROUPS + hi)
        gates = (g_p * jax.nn.sigmoid(l_lo - l_hi), g_p * jax.nn.sigmoid(l_hi - l_lo))

        abs_ = [jnp.dot(h, w13_ref[0], preferred_element_type=F32) for w13_ref in (w13lo_ref, w13hi_ref)]
        y = xg
        for ab, w2_ref, gate in zip(abs_, (w2lo_ref, w2hi_ref), gates):
            hid = jax.nn.silu(ab[:, :EXPERT_HIDDEN]) * ab[:, EXPERT_HIDDEN:]
            y = y + gate * jnp.dot(hid.astype(BF16), w2_ref[0], preferred_element_type=F32)
        for c in range(nchunk):
            obuf[slot, pl.ds(c, MOE_ROWS, stride=nchunk), :] = y[:, c * LANE:(c + 1) * LANE]
        start_scatters(nb, slot)

    @pl.when(nb == nblk - 1)
    def _():
        @pl.when(nb >= 1)
        def _():
            @pl.when(cnt_ref[nb - 1] > 0)
            def _():
                wait_scatters(nb - 1, 1 - slot)

        @pl.when(cnt > 0)
        def _():
            wait_scatters(nb, slot)


def _ple_kernel(x_ref, p_ref, gple_ref, wpg_ref, wple_ref, o_ref):
    tm = o_ref.shape[0]
    nchunk = o_ref.shape[1] // LANE
    emb = jnp.dot(p_ref[...].astype(BF16), wple_ref[...], preferred_element_type=F32)
    rows = tm // PLE_ROW_GROUPS
    for r0 in range(0, tm, rows):
        x = jnp.concatenate([x_ref[pl.ds(r0 * nchunk + c, rows, stride=nchunk), :] for c in range(nchunk)], axis=1)
        hp = (x * _rms_scale(x, x.shape[-1]) * gple_ref[...]).astype(BF16)
        gate = jax.nn.sigmoid(jnp.dot(hp, wpg_ref[...], preferred_element_type=F32))
        o_ref[r0:r0 + rows, :] = x + gate * emb[r0:r0 + rows]


def _const(shape):
    nd = len(shape)
    return pl.BlockSpec(shape, lambda *_: (0,) * nd)


def _head_slab_cols(w, width, offset=0):
    k = w.shape[0]
    w = w.reshape(k, N_HEADS, width)
    w = jnp.pad(w, ((0, 0), (0, 0), (offset, LANE - width - offset)))
    return w.reshape(k, N_HEADS * LANE)


def _prep_weights(g_mix, w_in, g_cq, w_uq, g_qn, g_qr, g_ckv, w_ukv, g_kn, g_kr, w_oa,
                  conv_w, w_oc, w_o, g_ffn, w_rg, b_rg, w_re, b_re, w1, w3, w2, g_ple, w_pg, w_ple):
    d = w_in.shape[0]
    n_mla = Q_LORA + KV_LORA
    kr_cols = jnp.pad(w_in[:, n_mla:n_mla + QK_ROPE], ((0, 0), (QK_NOPE, LANE - QK_NOPE - QK_ROPE)))
    w = {}
    w["wa"] = jnp.concatenate([w_in[:, :n_mla], kr_cols], axis=1).astype(BF16)
    w["wb"] = w_in[:, n_mla + QK_ROPE:].astype(BF16)
    w["wuq"] = _head_slab_cols(w_uq, QK_NOPE + QK_ROPE).astype(BF16)
    ukv = w_ukv.reshape(KV_LORA, N_HEADS, QK_NOPE + V_DIM)
    w["wuk"] = _head_slab_cols(ukv[:, :, :QK_NOPE].reshape(KV_LORA, -1), QK_NOPE).astype(BF16)
    w["wuv"] = _head_slab_cols(ukv[:, :, QK_NOPE:].reshape(KV_LORA, -1), V_DIM).astype(BF16)
    pad_hi = LANE - QK_NOPE - QK_ROPE
    w["gq"] = (jnp.pad(jnp.concatenate([g_qn, g_qr]), (0, pad_hi)) * (ATTN_SCALE * LOG2E))[None]
    w["gk"] = jnp.pad(g_kn, (0, LANE - QK_NOPE))[None]
    w["gkr"] = jnp.pad(g_kr, (QK_NOPE, pad_hi))[None]
    w["gmix"], w["gcq"], w["gckv"] = g_mix[None], g_cq[None], g_ckv[None]
    w["gffn"], w["gple"] = g_ffn[None], g_ple[None]
    w["convw"] = jnp.pad(conv_w, ((0, 8 - conv_w.shape[0]), (0, 0)))
    w["woa"], w["woc"], w["wo"] = w_oa.astype(BF16), w_oc.astype(BF16), w_o.astype(BF16)
    n_r = N_GROUPS + N_EXPERTS
    w["wr"] = jnp.pad(jnp.concatenate([w_rg, w_re], axis=1), ((0, 0), (0, LANE - n_r))).astype(BF16)
    w["br"] = jnp.pad(jnp.concatenate([b_rg, b_re]), (0, LANE - n_r))[None]
    pad_g = EXPERTS_PER_GROUP - N_GROUPS
    wrt = jnp.concatenate([jnp.pad(w_rg.T, ((0, pad_g), (0, 0))), w_re.T], axis=0)
    wrt_hi = wrt.astype(BF16)
    w["wrt"] = jnp.concatenate([wrt_hi, (wrt - wrt_hi.astype(F32)).astype(BF16)], axis=0)
    w["brt"] = jnp.concatenate([b_rg, jnp.full((pad_g,), NEG_BIG, F32), b_re])[:, None]
    w["w13"] = jnp.concatenate([w1, w3], axis=2).astype(BF16)
    w["w2"] = w2.astype(BF16)
    w["wpg"], w["wple"] = w_pg.astype(BF16), w_ple.astype(BF16)
    return w


def _rope_slabs(pos):
    inv = 1.0 / (ROPE_THETA ** (jnp.arange(0, QK_ROPE, 2, dtype=F32) / QK_ROPE))
    ang = pos.astype(F32)[:, None] * inv[None, :]
    cos, sin = jnp.cos(ang), jnp.sin(ang)
    n = pos.shape[0]
    half = QK_ROPE // 2
    z = lambda k: jnp.zeros((n, k), F32)
    pad_hi = LANE - QK_NOPE - QK_ROPE
    rc = jnp.concatenate([jnp.ones((n, QK_NOPE), F32), cos, cos, z(pad_hi)], axis=1)
    rs1 = jnp.concatenate([z(QK_NOPE), -sin, z(half), z(pad_hi)], axis=1)
    rs2 = jnp.concatenate([z(QK_NOPE), z(half), sin, z(pad_hi)], axis=1)
    return (rc, rs1, rs2), cos.T, sin.T


def _params(sem):
    return pltpu.CompilerParams(dimension_semantics=sem, vmem_limit_bytes=VMEM_LIMIT)


def _mla_pre(x, w, rope, tm):
    b, s, d = x.shape
    hw = N_HEADS * LANE
    tok = lambda width: pl.BlockSpec((1, tm, width), lambda i, j: (i, j, 0))
    rope_spec = pl.BlockSpec((tm, LANE), lambda i, j: (j, 0))
    consts = [w["gmix"], w["wa"], w["gcq"], w["gckv"], w["gkr"], w["wuq"], w["wuk"], w["wuv"], w["gq"], w["gk"]]
    return pl.pallas_call(
        _pre_kernel,
        grid=(b, s // tm),
        in_specs=[tok(d)] + [_const(c.shape) for c in consts] + [rope_spec] * 3,
        out_specs=[tok(hw), tok(hw), tok(hw), tok(KV_LORA), tok(QK_ROPE)],
        out_shape=[jax.ShapeDtypeStruct((b, s, hw), BF16)] * 3
        + [jax.ShapeDtypeStruct((b, s, KV_LORA), F32), jax.ShapeDtypeStruct((b, s, QK_ROPE), F32)],
        compiler_params=_params(("parallel", "parallel")),
        name="mla_pre",
    )(x, *consts, *rope)


def _mla_pre_t(x, w, rope, cos_t, sin_t, tm):
    b, s, d = x.shape
    hw = N_HEADS * LANE
    tok = lambda width: pl.BlockSpec((1, tm, width), lambda i, j: (i, j, 0))
    tiled = pl.BlockSpec((1, 1, hw, tm), lambda i, j: (i, j, 0, 0))
    rope_spec = pl.BlockSpec((tm, LANE), lambda i, j: (j, 0))
    rope_t_spec = pl.BlockSpec((QK_ROPE // 2, tm), lambda i, j: (0, j))
    gqt = jnp.broadcast_to(w["gq"].T, (LANE, tm))
    consts = [w["gmix"], w["wa"], w["gcq"], w["gckv"], w["gkr"], w["wuq"].T, w["wuk"], w["wuv"].T, gqt, w["gk"]]
    return pl.pallas_call(
        _pre_kernel_t,
        grid=(b, s // tm),
        in_specs=[tok(d)] + [_const(c.shape) for c in consts] + [rope_spec] * 3 + [rope_t_spec] * 2,
        out_specs=[tiled, tok(hw), tiled, tok(KV_LORA), tok(QK_ROPE)],
        out_shape=[jax.ShapeDtypeStruct((b, s // tm, hw, tm), BF16), jax.ShapeDtypeStruct((b, s, hw), BF16),
                   jax.ShapeDtypeStruct((b, s // tm, hw, tm), BF16),
                   jax.ShapeDtypeStruct((b, s, KV_LORA), F32), jax.ShapeDtypeStruct((b, s, QK_ROPE), F32)],
        compiler_params=_params(("parallel", "parallel")),
        name="mla_pre_t",
    )(x, *consts, *rope, cos_t, sin_t)


def _kv_past(past_lat, past_kpe, w, tm):
    b, t, _ = past_lat.shape
    hw = N_HEADS * LANE
    kpe_slab = jnp.pad(past_kpe, ((0, 0), (0, 0), (QK_NOPE, LANE - QK_NOPE - QK_ROPE)))
    tok = lambda width: pl.BlockSpec((1, tm, width), lambda i, j: (i, j, 0))
    consts = [w["wuk"], w["wuv"], w["gk"]]
    return pl.pallas_call(
        _kvpast_kernel,
        grid=(b, t // tm),
        in_specs=[tok(KV_LORA), tok(LANE)] + [_const(c.shape) for c in consts],
        out_specs=[tok(hw), tok(hw)],
        out_shape=[jax.ShapeDtypeStruct((b, t, hw), BF16)] * 2,
        compiler_params=_params(("parallel", "parallel")),
        name="kv_past",
    )(past_lat, kpe_slab, *consts)


def _attn_prompt(qt, k, vt):
    b, n_tiles, _, tq = qt.shape
    s = n_tiles * tq
    nh = ATTN_HEADS_PER_STEP
    width = nh * LANE
    return pl.pallas_call(
        functools.partial(_attn_prompt_kernel, tq=tq, nh=nh),
        grid=(b, N_HEADS // nh, n_tiles),
        in_specs=[pl.BlockSpec((1, 1, width, tq), lambda i, h, j: (i, j, h, 0)),
                  pl.BlockSpec((1, s, width), lambda i, h, j: (i, 0, h)),
                  pl.BlockSpec((1, n_tiles, width, tq), lambda i, h, j: (i, 0, h, 0))],
        out_specs=pl.BlockSpec((1, tq, nh * V_DIM), lambda i, h, j: (i, j, h)),
        out_shape=jax.ShapeDtypeStruct((b, s, N_HEADS * V_DIM), BF16),
        scratch_shapes=[pltpu.VMEM((tq, tq), F32)] * (2 * nh) + [pltpu.VMEM((tq, tq), BF16)] * (2 * nh),
        compiler_params=_params(("parallel", "parallel", "arbitrary")),
        name="attn_prompt",
    )(qt, k, vt)


def _attn_sample(q, kp, vp, kn, vn):
    b, s, _ = q.shape
    t = kp.shape[1]
    hw = N_HEADS * LANE
    blk = lambda rows: pl.BlockSpec((1, rows, hw), lambda i: (i, 0, 0))
    return pl.pallas_call(
        _attn_sample_kernel,
        grid=(b,),
        in_specs=[blk(s), blk(t), blk(t), blk(s), blk(s)],
        out_specs=pl.BlockSpec((1, s, N_HEADS * V_DIM), lambda i: (i, 0, 0)),
        out_shape=jax.ShapeDtypeStruct((b, s, N_HEADS * V_DIM), BF16),
        compiler_params=_params(("parallel",)),
        name="attn_sample",
    )(q, kp, vp, kn, vn)


def _post(x_long, attn_long, x_short, attn_short, cinit_short, w, tm, seq):
    b, s, d = x_long.shape
    nchunk = d // LANE
    n_keys = N_EXPERTS * EXPERTS_PER_GROUP
    tiles_per_seq = s // tm
    n_long = b * tiles_per_seq
    n_short = x_short.shape[1] // tm
    long_tile = lambda t: jnp.minimum(t, n_long - 1)
    short_tile = lambda t: jnp.maximum(t - n_long, 0)
    long_tok = lambda width: pl.BlockSpec(
        (1, tm, width), lambda t: (long_tile(t) // tiles_per_seq, long_tile(t) % tiles_per_seq, 0))
    short_tok = lambda width: pl.BlockSpec((1, tm, width), lambda t: (0, short_tile(t), 0))
    cinit_long = jnp.zeros((b, SUBLANE, CONV_DIM), F32)
    hm = tm // 2
    tri = (jnp.arange(hm)[:, None] < jnp.arange(hm)[None, :]).astype(BF16)
    consts = [w["gmix"], w["wb"], w["convw"], w["woa"], w["woc"], w["wo"], w["gffn"], w["wrt"], w["brt"], tri]
    in_specs = [long_tok(d), long_tok(N_HEADS * V_DIM),
                pl.BlockSpec((1, SUBLANE, CONV_DIM), lambda t: (long_tile(t) // tiles_per_seq, 0, 0)),
                short_tok(d), short_tok(N_HEADS * V_DIM),
                pl.BlockSpec((1, 2 * tm, CONV_DIM), lambda t: (short_tile(t), 0, 0))]
    return pl.pallas_call(
        functools.partial(_post_kernel, n_long=n_long, tiles_per_seq=tiles_per_seq, seq=seq),
        grid=(n_long + n_short,),
        in_specs=in_specs + [_const(c.shape) for c in consts],
        out_specs=[pl.BlockSpec((tm * nchunk, LANE), lambda t: (t, 0)),
                   pl.BlockSpec((1, 8, tm), lambda t: (t, 0, 0)),
                   pl.BlockSpec((1, 2, CONV_DIM), lambda t: (long_tile(t) // tiles_per_seq, 0, 0)),
                   pl.BlockSpec((1, tm // seq, 2, CONV_DIM), lambda t: (short_tile(t), 0, 0, 0)),
                   _const((n_keys, 1))],
        out_shape=[jax.ShapeDtypeStruct(((n_long + n_short) * tm * nchunk, LANE), F32),
                   jax.ShapeDtypeStruct((n_long + n_short, 8, tm), F32),
                   jax.ShapeDtypeStruct((b, 2, CONV_DIM), F32),
                   jax.ShapeDtypeStruct((n_short, tm // seq, 2, CONV_DIM), F32),
                   jax.ShapeDtypeStruct((n_keys, 1), F32)],
        scratch_shapes=[pltpu.VMEM((SUBLANE, CONV_DIM), F32)],
        compiler_params=_params(("arbitrary",)),
        name="post",
    )(x_long, attn_long, cinit_long, x_short, attn_short, cinit_short, *consts)


def _route_tables(key, rank, counts, n):
    n_keys = counts.shape[0]
    padded = (counts + MOE_ROWS - 1) // MOE_ROWS * MOE_ROWS
    pend = jnp.cumsum(padded)
    pstart = pend - padded
    ids = jnp.arange(n_keys, dtype=jnp.int32)
    dest = rank + jnp.sum(jnp.where(key[:, None] == ids[None, :], pstart[None, :], 0), axis=1)
    nblk = n // MOE_ROWS + N_PAIR_BUCKETS
    blk_start = jnp.arange(nblk, dtype=jnp.int32) * MOE_ROWS
    blk_hot = (blk_start[:, None] >= pstart[None, :]) & (blk_start[:, None] < pend[None, :])
    blk_key = jnp.sum(jnp.where(blk_hot, ids[None, :], 0), axis=1)
    blk_cnt = jnp.sum(jnp.where(blk_hot, jnp.minimum(counts[None, :] - (blk_start[:, None] - pstart[None, :]),
                                                      MOE_ROWS), 0), axis=1)
    any_hot = jnp.any(blk_hot, axis=1)
    blk_key = jnp.where(any_hot, blk_key, n_keys - 1)
    blk_lo = blk_key // EXPERTS_PER_GROUP
    blk_hi = blk_lo // EXPERTS_PER_GROUP * EXPERTS_PER_GROUP + blk_key % EXPERTS_PER_GROUP
    return blk_lo, blk_hi, blk_cnt.astype(jnp.int32), dest.astype(jnp.int32)


def _moe(x1_all, key, rank, counts, w):
    rows, _ = x1_all.shape
    d = w["gffn"].shape[1]
    nchunk = d // LANE
    n = rows // nchunk
    blk_lo, blk_hi, blk_cnt, dest = _route_tables(key, rank, counts, n)
    nblk = blk_lo.shape[0]
    w13_spec = lambda ref_idx: pl.BlockSpec((1, d, 2 * EXPERT_HIDDEN),
                                            lambda i, lo, hi, cnt, dst: ((lo, hi)[ref_idx][i], 0, 0))
    w2_spec = lambda ref_idx: pl.BlockSpec((1, EXPERT_HIDDEN, d),
                                           lambda i, lo, hi, cnt, dst: ((lo, hi)[ref_idx][i], 0, 0))
    buf = pltpu.VMEM((2, MOE_ROWS * nchunk, LANE), F32)
    grid_spec = pltpu.PrefetchScalarGridSpec(
        num_scalar_prefetch=4,
        grid=(nblk,),
        in_specs=[pl.BlockSpec(memory_space=pl.ANY),
                  pl.BlockSpec((1, d), lambda i, *_: (0, 0)),
                  pl.BlockSpec(w["wr"].shape, lambda i, *_: (0, 0)),
                  pl.BlockSpec(w["br"].shape, lambda i, *_: (0, 0)),
                  w13_spec(0), w13_spec(1), w2_spec(0), w2_spec(1)],
        out_specs=pl.BlockSpec(memory_space=pl.ANY),
        scratch_shapes=[pltpu.SMEM((nblk * MOE_ROWS,), jnp.int32), buf, buf,
                        pltpu.SemaphoreType.DMA((2,)), pltpu.SemaphoreType.DMA((2,))],
    )
    return pl.pallas_call(
        _moe_kernel,
        grid_spec=grid_spec,
        out_shape=jax.ShapeDtypeStruct((rows, LANE), F32),
        compiler_params=_params(("arbitrary",)),
        name="moe",
    )(blk_lo, blk_hi, blk_cnt, dest, x1_all, w["gffn"], w["wr"], w["br"], w["w13"], w["w13"], w["w2"], w["w2"])


def _ple(x2_all, row_off, p, w, tm):
    n, pd = p.shape
    d = w["gple"].shape[1]
    nchunk = d // LANE
    off = row_off // tm
    consts = [w["gple"], w["wpg"], w["wple"]]
    return pl.pallas_call(
        _ple_kernel,
        grid=(n // tm,),
        in_specs=[pl.BlockSpec((tm * nchunk, LANE), lambda i: (off + i, 0)), pl.BlockSpec((tm, pd), lambda i: (i, 0))]
        + [_const(c.shape) for c in consts],
        out_specs=pl.BlockSpec((tm, d), lambda i: (i, 0)),
        out_shape=jax.ShapeDtypeStruct((n, d), F32),
        compiler_params=_params(("parallel",)),
        name="ple",
    )(x2_all, p, *consts)


def _layer(xp, xs, pp, ps, past_lat, past_kpe, past_conv, w):
    bp, sp, d = xp.shape
    bs, ss, _ = xs.shape
    n_p, n_s = bp * sp, bs * ss
    past_len = past_lat.shape[1]
    tm_p, tm_s = min(TOKEN_TILE, sp), min(TOKEN_TILE, n_s)
    tm_post = min(tm_p, tm_s)

    rope, cos_t, sin_t = _rope_slabs(jnp.arange(sp))
    qt, k, vt, lat_p, kpe_p = _mla_pre_t(xp, w, rope, cos_t, sin_t, tm_p)
    attn_p = _attn_prompt(qt, k, vt)

    rope, _, _ = _rope_slabs(past_len + jnp.arange(n_s) % ss)
    xs_rows = xs.reshape(1, n_s, d)
    q, k, v, lat_s, kpe_s = _mla_pre(xs_rows, w, rope, tm_s)
    by_seq = lambda a: a.reshape(bs, ss, a.shape[-1])
    kp, vp = _kv_past(past_lat, past_kpe, w, min(TOKEN_TILE, past_len))
    attn_s = _attn_sample(by_seq(q), kp, vp, by_seq(k), by_seq(v)).reshape(1, n_s, -1)
    lat_s, kpe_s = by_seq(lat_s), by_seq(kpe_s)
    in_tiles = lambda a: a.reshape(n_s // tm_post, tm_post, CONV_DIM)
    n_hist = past_conv.shape[1]
    cinit_s = jnp.concatenate(
        [in_tiles(jnp.pad(past_conv[:, n_hist - 1:], ((0, 0), (0, ss - 1), (0, 0)))),
         in_tiles(jnp.pad(past_conv, ((0, 0), (0, ss - n_hist), (0, 0))))], axis=1)

    x1_all, route, conv_p, conv_s, cnt = _post(xp, attn_p, xs_rows, attn_s, cinit_s, w, tm_post, ss)
    conv_s = conv_s.reshape(bs, n_hist, CONV_DIM)
    key = route[:, 0].reshape(-1).astype(jnp.int32)
    rank = route[:, 1].reshape(-1).astype(jnp.int32)
    x2_all = _moe(x1_all, key, rank, cnt[:, 0].astype(jnp.int32), w)
    yp = _ple(x2_all, 0, pp.reshape(n_p, -1), w, min(TOKEN_TILE, n_p)).reshape(bp, sp, d)
    ys = _ple(x2_all, n_p, ps.reshape(n_s, -1), w, min(TOKEN_TILE, n_s)).reshape(bs, ss, d)
    return yp, ys, (lat_p, kpe_p, conv_p, lat_s, kpe_s, conv_s)


def kernel(x_prompt, x_sample, cache_kv_latent, cache_k_rope, state_conv, p_prompt, p_sample,
           g_mix, w_in, g_cq, w_uq, g_qn, g_qr, g_ckv, w_ukv, g_kn, g_kr, w_oa,
           conv_w, w_oc, w_o, g_ffn, w_rg, b_rg, w_re, b_re, w1, w3, w2, g_ple, w_pg, w_ple):
    depth = g_mix.shape[0]
    xp, xs = x_prompt, x_sample
    outs = [[] for _ in range(6)]
    weights = (g_mix, w_in, g_cq, w_uq, g_qn, g_qr, g_ckv, w_ukv, g_kn, g_kr, w_oa, conv_w, w_oc, w_o, g_ffn,
               w_rg, b_rg, w_re, b_re, w1, w3, w2, g_ple, w_pg, w_ple)
    states = (p_prompt, p_sample, cache_kv_latent, cache_k_rope, state_conv)
    for i in range(depth):
        of_layer = (lambda a: a.reshape(a.shape[1:])) if depth == 1 else (lambda a: a[i])
        w = _prep_weights(*map(of_layer, weights))
        xp, xs, new = _layer(xp, xs, *map(of_layer, states), w)
        for o, a in zip(outs, new):
            o.append(a)
    return (xp, xs) + tuple(o[0][None] if depth == 1 else jnp.stack(o, axis=0) for o in outs)
```

```python
import functools
import math

import jax
import jax.numpy as jnp
from jax import lax
from jax.experimental import pallas as pl
from jax.experimental.pallas import tpu as pltpu

F32 = jnp.float32
BF16 = jnp.bfloat16

LANE = 128
SUBLANE = 8
TOKEN_TILE = 512
CHUNK = 64
N_HEADS = 8
QK_NOPE = 64
QK_ROPE = 32
V_DIM = 64
Q_LORA = 256
KV_LORA = 256
CONV_DIM = 512
N_GROUPS = 4
EXPERTS_PER_GROUP = 8
N_EXPERTS = N_GROUPS * EXPERTS_PER_GROUP
EXPERT_HIDDEN = 256
ROPE_THETA = 10000.0
EPS = 1e-6
ATTN_SCALE = (QK_NOPE + QK_ROPE) ** -0.5
LOG2E = math.log2(math.e)
NEG_BIG = -1e30
MOE_ROWS = 128
ATTN_HEADS_PER_STEP = 4
PLE_ROW_GROUPS = 2
N_PAIR_BUCKETS = N_GROUPS * (EXPERTS_PER_GROUP * (EXPERTS_PER_GROUP - 1) // 2)
VMEM_LIMIT = 56 * 1024 * 1024


def _rms_scale(x, n):
    return lax.rsqrt(jnp.sum(x * x, axis=-1, keepdims=True) * (1.0 / n) + EPS)


def _lane_iota(shape):
    return lax.broadcasted_iota(jnp.int32, shape, len(shape) - 1)


def _rope(t, rc, rs1, rs2):
    return t * rc + pltpu.roll(t, LANE - QK_ROPE // 2, 1) * rs1 + pltpu.roll(t, QK_ROPE // 2, 1) * rs2


def _pre_latents(x_ref, gmix_ref, wa_ref, gcq_ref, gckv_ref, gkr_ref, rope_refs, lat_ref, kpe_ref):
    x = x_ref[0]
    h = x * _rms_scale(x, x.shape[-1]) * gmix_ref[...]
    z = jnp.dot(h.astype(BF16), wa_ref[...], preferred_element_type=F32)
    cq = z[:, :Q_LORA]
    ckv = z[:, Q_LORA:Q_LORA + KV_LORA]
    kr = z[:, Q_LORA + KV_LORA:]
    cqn = cq * _rms_scale(cq, Q_LORA) * gcq_ref[...]
    lat = ckv * _rms_scale(ckv, KV_LORA) * gckv_ref[...]
    lat_ref[0] = lat
    krn = kr * _rms_scale(kr, QK_ROPE) * gkr_ref[...]
    kpe = _rope(krn, *(r[...] for r in rope_refs))
    kpe_ref[0] = kpe[:, QK_NOPE:QK_NOPE + QK_ROPE]
    return cqn, lat, kpe


def _store_keys(k_ref, kf, gk, kpe):
    for hd in range(N_HEADS):
        sl = slice(hd * LANE, (hd + 1) * LANE)
        ks = kf[:, sl]
        k_ref[0, :, sl] = (ks * _rms_scale(ks, QK_NOPE) * gk + kpe).astype(BF16)


def _pre_kernel(x_ref, gmix_ref, wa_ref, gcq_ref, gckv_ref, gkr_ref, wuq_ref, wuk_ref, wuv_ref,
                gq_ref, gk_ref, rc_ref, rs1_ref, rs2_ref,
                q_ref, k_ref, v_ref, lat_ref, kpe_ref):
    rope_refs = (rc_ref, rs1_ref, rs2_ref)
    cqn, lat, kpe = _pre_latents(x_ref, gmix_ref, wa_ref, gcq_ref, gckv_ref, gkr_ref, rope_refs, lat_ref, kpe_ref)
    rc, rs1, rs2 = (r[...] for r in rope_refs)
    is_nope = _lane_iota(kpe.shape) < QK_NOPE
    latb = lat.astype(BF16)
    qf = jnp.dot(cqn.astype(BF16), wuq_ref[...], preferred_element_type=F32)
    vf = jnp.dot(latb, wuv_ref[...], preferred_element_type=F32)
    _store_keys(k_ref, jnp.dot(latb, wuk_ref[...], preferred_element_type=F32), gk_ref[...], kpe)
    gq = gq_ref[...]
    ones_hi = jnp.where(is_nope, 0.0, 1.0)
    for hd in range(N_HEADS):
        sl = slice(hd * LANE, (hd + 1) * LANE)
        qs = qf[:, sl]
        sq = qs * qs
        ss_all = jnp.sum(sq, axis=-1, keepdims=True)
        ss_n = jnp.sum(jnp.where(is_nope, sq, 0.0), axis=-1, keepdims=True)
        r = jnp.where(is_nope, lax.rsqrt(ss_n * (1.0 / QK_NOPE) + EPS),
                      lax.rsqrt((ss_all - ss_n) * (1.0 / QK_ROPE) + EPS))
        q_ref[0, :, sl] = _rope(qs * r * gq, rc, rs1, rs2).astype(BF16)
        v_ref[0, :, sl] = (vf[:, sl] + ones_hi).astype(BF16)


def _pre_kernel_t(x_ref, gmix_ref, wa_ref, gcq_ref, gckv_ref, gkr_ref, wuqt_ref, wuk_ref, wuvt_ref,
                  gqt_ref, gk_ref, rc_ref, rs1_ref, rs2_ref, cos_ref, sin_ref,
                  qt_ref, k_ref, vt_ref, lat_ref, kpe_ref):
    rope_refs = (rc_ref, rs1_ref, rs2_ref)
    cqn, lat, kpe = _pre_latents(x_ref, gmix_ref, wa_ref, gcq_ref, gckv_ref, gkr_ref, rope_refs, lat_ref, kpe_ref)
    latb = lat.astype(BF16)
    _store_keys(k_ref, jnp.dot(latb, wuk_ref[...], preferred_element_type=F32), gk_ref[...], kpe)
    tm = cqn.shape[0]
    qft = jnp.dot(wuqt_ref[...], cqn.T.astype(BF16), preferred_element_type=F32)
    vft = jnp.dot(wuvt_ref[...], lat.T.astype(BF16), preferred_element_type=F32)
    gq, cos, sin = gqt_ref[...], cos_ref[...], sin_ref[...]
    half = QK_ROPE // 2
    ones_lo = jnp.where(lax.broadcasted_iota(jnp.int32, (LANE, tm), 0) < V_DIM, 0.0, 1.0)
    pad = jnp.zeros((LANE - QK_NOPE - QK_ROPE, tm), F32)
    for hd in range(N_HEADS):
        rows = slice(hd * LANE, (hd + 1) * LANE)
        qs = qft[rows]
        sq = qs * qs
        r_n = lax.rsqrt(jnp.sum(sq[:QK_NOPE], axis=0, keepdims=True) * (1.0 / QK_NOPE) + EPS)
        r_p = lax.rsqrt(jnp.sum(sq[QK_NOPE:QK_NOPE + QK_ROPE], axis=0, keepdims=True) * (1.0 / QK_ROPE) + EPS)
        nope = qs[:QK_NOPE] * r_n * gq[:QK_NOPE]
        x1 = qs[QK_NOPE:QK_NOPE + half] * r_p * gq[QK_NOPE:QK_NOPE + half]
        x2 = qs[QK_NOPE + half:QK_NOPE + QK_ROPE] * r_p * gq[QK_NOPE + half:QK_NOPE + QK_ROPE]
        slab = jnp.concatenate([nope, x1 * cos - x2 * sin, x1 * sin + x2 * cos, pad], axis=0)
        qt_ref[0, 0, rows, :] = slab.astype(BF16)
        vt_ref[0, 0, rows, :] = (vft[rows] + ones_lo).astype(BF16)


def _finish_pair(accs):
    outs = [a / pltpu.roll(a, V_DIM, 1) for a in accs]
    lane = _lane_iota(outs[0].shape)
    return jnp.where(lane < V_DIM, outs[0], pltpu.roll(outs[1], V_DIM, 1))


def _attn_prompt_kernel(qt_ref, k_ref, vt_ref, o_ref, *scratch, tq, nh):
    s_refs = (scratch[:nh], scratch[nh:2 * nh])
    p_refs = (scratch[2 * nh:3 * nh], scratch[3 * nh:])
    qi = pl.program_id(2)
    key_pos = lax.broadcasted_iota(jnp.int32, (tq, tq), 0)
    query_pos = lax.broadcasted_iota(jnp.int32, (tq, tq), 1)
    diag_mask = (key_pos // CHUNK) <= (query_pos // CHUNK)
    heads = [slice(hh * LANE, (hh + 1) * LANE) for hh in range(nh)]

    def scores(i, slot, hh, mask):
        start = pl.multiple_of(i * tq, tq)
        s = jnp.dot(k_ref[0, pl.ds(start, tq), heads[hh]], qt_ref[0, 0, heads[hh], :], preferred_element_type=F32)
        if mask is not None:
            s = jnp.where(mask, s, NEG_BIG)
        s_refs[slot][hh][...] = s
        return jnp.max(s, axis=0, keepdims=True)

    def softmax(slot, hh, m, tile_max):
        m_new = jnp.maximum(m, tile_max)
        p_refs[slot][hh][...] = jnp.exp2(s_refs[slot][hh][...] - m_new).astype(BF16)
        return m_new, jnp.exp2(m - m_new)

    def accumulate(i, slot, hh, alpha, acc):
        pv = jnp.dot(vt_ref[0, jnp.maximum(i, 0), heads[hh], :], p_refs[slot][hh][...],
                     preferred_element_type=F32)
        return acc * alpha + pv

    def iteration(i, slot, carry, next_mask=None):
        stats = [softmax(slot, hh, m, tile_max) for hh, (m, tile_max, _, _) in enumerate(carry)]
        next_max = [scores(i + 1, 1 - slot, hh, next_mask) for hh in range(nh)]
        accs = [accumulate(i - 1, 1 - slot, hh, alpha, acc) for hh, (_, _, alpha, acc) in enumerate(carry)]
        return tuple((m, tmax, alpha, acc) for (m, alpha), tmax, acc in zip(stats, next_max, accs))

    def last(slot, carry):
        accs = []
        for hh, (m, tile_max, alpha, acc) in enumerate(carry):
            acc = accumulate(qi - 1, 1 - slot, hh, alpha, acc)
            m, alpha = softmax(slot, hh, m, tile_max)
            accs.append(accumulate(qi, slot, hh, alpha, acc))
        return tuple(accs)

    first_mask = diag_mask | (qi > 0)
    init = []
    for hh in range(nh):
        p_refs[1][hh][...] = jnp.zeros((tq, tq), BF16)
        init.append((jnp.full((1, tq), NEG_BIG, F32), scores(0, 0, hh, first_mask), jnp.ones((1, tq), F32),
                     jnp.zeros((LANE, tq), F32)))
    carry = lax.fori_loop(0, (qi - 1) // 2, lambda j, c: iteration(2 * j + 1, 1, iteration(2 * j, 0, c)),
                          tuple(init))
    tails = [lambda c: last(0, c),
             lambda c: last(1, iteration(qi - 1, 0, c, diag_mask)),
             lambda c: last(0, iteration(qi - 1, 1, iteration(qi - 2, 0, c), diag_mask))]
    accs = lax.switch(jnp.where(qi == 0, 0, 2 - qi % 2), tails, carry)
    out_t = jnp.concatenate([a[:V_DIM] / a[V_DIM:V_DIM + 1] for a in accs], axis=0)
    o_ref[0] = out_t.T.astype(o_ref.dtype)


def _attn_sample_kernel(q_ref, lat_ref, kpe_ref, kn_ref, vn_ref, wuk_ref, wuv_ref, gk_ref, o_ref):
    nt = (((1,), (1,)), ((), ()))
    slabs = [slice(hd * LANE, (hd + 1) * LANE) for hd in range(N_HEADS)]
    latb = lat_ref[0].astype(BF16)
    kf = jnp.dot(latb, wuk_ref[...], preferred_element_type=F32)
    vf = jnp.dot(latb, wuv_ref[...], preferred_element_type=F32)
    kpe = kpe_ref[0]
    ones_hi = jnp.where(_lane_iota(kpe.shape) < QK_NOPE, 0.0, 1.0)
    gk = gk_ref[...]
    kp = [(kf[:, sl] * _rms_scale(kf[:, sl], QK_NOPE) * gk + kpe).astype(BF16) for sl in slabs]
    vp = [(vf[:, sl] + ones_hi).astype(BF16) for sl in slabs]
    s_past = [lax.dot_general(q_ref[0, :, sl], k, nt, preferred_element_type=F32) for sl, k in zip(slabs, kp)]
    s_new = [lax.dot_general(q_ref[0, :, sl], kn_ref[0, :, sl], nt, preferred_element_type=F32) for sl in slabs]
    accs = []
    for sl, sp, sn, v in zip(slabs, s_past, s_new, vp):
        m = jnp.maximum(jnp.max(sp, axis=-1, keepdims=True), jnp.max(sn, axis=-1, keepdims=True))
        accs.append(jnp.dot(jnp.exp2(sp - m).astype(BF16), v, preferred_element_type=F32)
                    + jnp.dot(jnp.exp2(sn - m).astype(BF16), vn_ref[0, :, sl], preferred_element_type=F32))
    for pair in range(N_HEADS // 2):
        o_ref[0, :, pair * LANE:(pair + 1) * LANE] = _finish_pair(accs[2 * pair:2 * pair + 2]).astype(o_ref.dtype)


def _post_kernel(xl_ref, al_ref, cl_ref, xs_ref, as_ref, cs_ref, *rest, n_long, tiles_per_seq, seq):
    shared, (x1_ref, route_ref, cnewl_ref, cnews_ref, cnt_ref, carry_ref) = rest[:-6], rest[-6:]
    t = pl.program_id(0)

    @pl.when(t == 0)
    def _():
        cnt_ref[...] = jnp.zeros_like(cnt_ref)

    @pl.when(t < n_long)
    def _():
        _post_tile(xl_ref, al_ref, cl_ref, *shared, x1_ref, route_ref, cnewl_ref, cnt_ref, carry_ref,
                   first=t % tiles_per_seq == 0, seq=None)

    @pl.when(t >= n_long)
    def _():
        _post_tile(xs_ref, as_ref, cs_ref, *shared, x1_ref, route_ref, cnews_ref, cnt_ref, carry_ref,
                   first=None, seq=seq)


def _post_tile(x_ref, attn_ref, cinit_ref, gmix_ref, wb_ref, convw_ref, woa_ref, woc_ref, wo_ref,
               gffn_ref, wrt_ref, brt_ref, tri_ref,
               x1_ref, route_ref, cnew_ref, cnt_ref, carry_ref, *, first, seq):
    tm = x_ref.shape[1]

    if seq is None:
        @pl.when(first)
        def _():
            carry_ref[...] = cinit_ref[0]

    d = x_ref.shape[-1]
    hm = tm // 2
    groups = (0, hm)
    xs, zs = [], []
    for r0 in groups:
        x = x_ref[0, r0:r0 + hm, :]
        h = x * _rms_scale(x, d) * gmix_ref[...]
        xs.append(x)
        zs.append(jnp.dot(h.astype(BF16), wb_ref[...], preferred_element_type=F32))

    cw = convw_ref[...]
    row = lax.broadcasted_iota(jnp.int32, (hm, CONV_DIM), 0)
    if seq is None:
        c1, c2 = carry_ref[SUBLANE - 1:SUBLANE, :], carry_ref[SUBLANE - 2:SUBLANE - 1, :]
    merged = []
    for r0, z in zip(groups, zs):
        conv_b = z[:, :CONV_DIM]
        u = z[:, CONV_DIM:2 * CONV_DIM] * z[:, 2 * CONV_DIM:3 * CONV_DIM]
        if seq is None:
            u_m1 = jnp.where(row == 0, c1, pltpu.roll(u, 1, 0))
            u_m2 = jnp.where(row == 0, c2, jnp.where(row == 1, c1, pltpu.roll(u, 2, 0)))
            c1, c2 = u[hm - 1:, :], u[hm - 2:hm - 1, :]
            if r0 + hm == tm:
                carry_ref[...] = u[hm - SUBLANE:, :]
                cnew_ref[0] = u[hm - 2:, :]
        else:
            pos = row % seq
            u_m1 = jnp.where(pos == 0, cinit_ref[0, r0:r0 + hm, :], pltpu.roll(u, 1, 0))
            u_m2 = jnp.where(pos <= 1, cinit_ref[0, tm + r0:tm + r0 + hm, :], pltpu.roll(u, 2, 0))
            cnew_ref[0, r0 // seq:(r0 + hm) // seq] = u.reshape(hm // seq, seq, CONV_DIM)[:, seq - 2:, :]
        cv = cw[0:1, :] * u_m2 + cw[1:2, :] * u_m1 + cw[2:3, :] * u
        y_a = jnp.dot(attn_ref[0, r0:r0 + hm, :], woa_ref[...], preferred_element_type=F32)
        y_c = jnp.dot((conv_b * cv).astype(BF16), woc_ref[...], preferred_element_type=F32)
        gate_a = z[:, 3 * CONV_DIM:3 * CONV_DIM + d]
        gate_c = z[:, 3 * CONV_DIM + d:]
        merged.append((jax.nn.sigmoid(gate_a) * y_a + jax.nn.sigmoid(gate_c) * y_c).astype(BF16))
    x1s = [x + jnp.dot(mrg, wo_ref[...], preferred_element_type=F32) for x, mrg in zip(xs, merged)]
    nchunk = d // LANE
    for r0, x1 in zip(groups, x1s):
        for c in range(nchunk):
            x1_ref[pl.ds(r0 * nchunk + c, hm, stride=nchunk), :] = x1[:, c * LANE:(c + 1) * LANE]

    nt = (((1,), (1,)), ((), ()))
    n_rows = brt_ref.shape[0]
    logits = []
    for x1 in x1s:
        h2 = x1 * _rms_scale(x1, d) * gffn_ref[...]
        h2_hi = h2.astype(BF16)
        h2_lo = (h2 - h2_hi.astype(F32)).astype(BF16)
        lt2 = lax.dot_general(wrt_ref[...], h2_hi, nt, preferred_element_type=F32)
        logits.append(lt2[:n_rows] + lt2[n_rows:]
                      + lax.dot_general(wrt_ref[:n_rows, :], h2_lo, nt, preferred_element_type=F32) + brt_ref[...])
    sub = lax.broadcasted_iota(jnp.int32, (EXPERTS_PER_GROUP, hm), 0).astype(F32)
    none = float(EXPERTS_PER_GROUP)

    def first_argmax(v):
        vmax = jnp.max(v, axis=0, keepdims=True)
        return jnp.min(jnp.where(v == vmax, sub, none), axis=0, keepdims=True)

    n_keys = cnt_ref.shape[0]
    keys = lax.broadcasted_iota(jnp.int32, (n_keys, hm), 0).astype(F32)
    row8 = lax.broadcasted_iota(jnp.int32, (8, hm), 0)
    seen = cnt_ref[...]
    for r0, lt in zip(groups, logits):
        g_idx = first_argmax(lt[:EXPERTS_PER_GROUP])
        el = lt[EXPERTS_PER_GROUP * N_GROUPS:]
        for g in range(N_GROUPS - 2, -1, -1):
            el = jnp.where(g_idx == g, lt[EXPERTS_PER_GROUP * (g + 1):EXPERTS_PER_GROUP * (g + 2)], el)
        i1 = first_argmax(el)
        i2 = first_argmax(jnp.where(sub == i1, NEG_BIG, el))
        key = (g_idx * (EXPERTS_PER_GROUP * EXPERTS_PER_GROUP) + jnp.minimum(i1, i2) * EXPERTS_PER_GROUP
               + jnp.maximum(i1, i2))
        onehot = jnp.where(keys == key, 1.0, 0.0)
        before = jnp.dot(onehot.astype(BF16), tri_ref[...], preferred_element_type=F32)
        rank = jnp.sum(onehot * (before + seen), axis=0, keepdims=True)
        seen = seen + jnp.sum(onehot, axis=1, keepdims=True)
        route_ref[0, :, r0:r0 + hm] = jnp.where(row8 == 0, key, jnp.where(row8 == 1, rank, 0.0))
    cnt_ref[...] = seen


def _rows_loop(n, fn, unroll=8):
    def group(j, c):
        for u in range(unroll):
            fn(j * unroll + u)
        return c
    lax.fori_loop(0, n // unroll, group, 0)

    def single(i, c):
        fn(i)
        return c
    lax.fori_loop(n // unroll * unroll, n, single, 0)


def _moe_kernel(lo_ref, hi_ref, cnt_ref, dest_ref,
                x1_hbm, gffn_ref, wr_ref, br_ref, w13lo_ref, w13hi_ref, w2lo_ref, w2hi_ref,
                x2_hbm, tok_ref, xbuf, obuf, gsem, ssem):
    nb = pl.program_id(0)
    nblk = pl.num_programs(0)
    nchunk = xbuf.shape[1] // MOE_ROWS
    slot = nb % 2
    cnt = cnt_ref[nb]

    def start_gathers(blk, sl):
        base = blk * MOE_ROWS

        def one(i):
            src = x1_hbm.at[pl.ds(tok_ref[base + i] * nchunk, nchunk)]
            pltpu.make_async_copy(src, xbuf.at[sl, pl.ds(i * nchunk, nchunk)], gsem.at[sl]).start()
        _rows_loop(cnt_ref[blk], one)

    def start_scatters(blk, sl):
        base = blk * MOE_ROWS

        def one(i):
            dst = x2_hbm.at[pl.ds(tok_ref[base + i] * nchunk, nchunk)]
            pltpu.make_async_copy(obuf.at[sl, pl.ds(i * nchunk, nchunk)], dst, ssem.at[sl]).start()
        _rows_loop(cnt_ref[blk], one)

    def wait_gathers(blk, sl):
        rows = cnt_ref[blk] * nchunk
        pltpu.make_async_copy(x1_hbm.at[pl.ds(0, rows)], xbuf.at[sl, pl.ds(0, rows)], gsem.at[sl]).wait()

    def wait_scatters(blk, sl):
        rows = cnt_ref[blk] * nchunk
        pltpu.make_async_copy(obuf.at[sl, pl.ds(0, rows)], x2_hbm.at[pl.ds(0, rows)], ssem.at[sl]).wait()

    @pl.when(nb == 0)
    def _():
        def invert(i, c):
            tok_ref[dest_ref[i]] = i
            return c
        lax.fori_loop(0, dest_ref.shape[0], invert, 0, unroll=16)
        xbuf[...] = jnp.zeros_like(xbuf)

        @pl.when(cnt > 0)
        def _():
            start_gathers(0, 0)

    @pl.when(nb + 1 < nblk)
    def _():
        @pl.when(cnt_ref[nb + 1] > 0)
        def _():
            start_gathers(nb + 1, 1 - slot)

    @pl.when(nb >= 2)
    def _():
        @pl.when(cnt_ref[nb - 2] > 0)
        def _():
            wait_scatters(nb - 2, slot)

    @pl.when(cnt > 0)
    def _():
        wait_gathers(nb, slot)
        xg = jnp.concatenate([xbuf[slot, pl.ds(c, MOE_ROWS, stride=nchunk), :] for c in range(nchunk)], axis=1)
        h = (xg * _rms_scale(xg, xg.shape[-1]) * gffn_ref[...]).astype(BF16)

        lo, hi = lo_ref[nb], hi_ref[nb]
        logit = jnp.dot(h, wr_ref[...], preferred_element_type=F32) + br_ref[...]
        lane = _lane_iota(logit.shape)
        pick = lambda j: jnp.sum(jnp.where(lane == j, logit, 0.0), axis=-1, keepdims=True)
        is_g = lane < N_GROUPS
        gmax = jnp.max(jnp.where(is_g, logit, NEG_BIG), axis=-1, keepdims=True)
        g_den = jnp.sum(jnp.where(is_g, jnp.exp(logit - gmax), 0.0), axis=-1, keepdims=True)
        g_p = jnp.exp(pick(lo // EXPERTS_PER_GROUP) - gmax) / g_den
        l_lo, l_hi = pick(N_GROUPS + lo), pick(N_GROUPS + hi)
        gates = (g_p * jax.nn.sigmoid(l_lo - l_hi), g_p * jax.nn.sigmoid(l_hi - l_lo))

        abs_ = [jnp.dot(h, w13_ref[0], preferred_element_type=F32) for w13_ref in (w13lo_ref, w13hi_ref)]
        y = xg
        for ab, w2_ref, gate in zip(abs_, (w2lo_ref, w2hi_ref), gates):
            hid = jax.nn.silu(ab[:, :EXPERT_HIDDEN]) * ab[:, EXPERT_HIDDEN:]
            y = y + gate * jnp.dot(hid.astype(BF16), w2_ref[0], preferred_element_type=F32)
        for c in range(nchunk):
            obuf[slot, pl.ds(c, MOE_ROWS, stride=nchunk), :] = y[:, c * LANE:(c + 1) * LANE]
        start_scatters(nb, slot)

    @pl.when(nb == nblk - 1)
    def _():
        @pl.when(nb >= 1)
        def _():
            @pl.when(cnt_ref[nb - 1] > 0)
            def _():
                wait_scatters(nb - 1, 1 - slot)

        @pl.when(cnt > 0)
        def _():
            wait_scatters(nb, slot)


def _ple_kernel(x_ref, p_ref, gple_ref, wpg_ref, wple_ref, o_ref):
    tm = o_ref.shape[0]
    nchunk = o_ref.shape[1] // LANE
    emb = jnp.dot(p_ref[...].astype(BF16), wple_ref[...], preferred_element_type=F32)
    rows = tm // PLE_ROW_GROUPS
    for r0 in range(0, tm, rows):
        x = jnp.concatenate([x_ref[pl.ds(r0 * nchunk + c, rows, stride=nchunk), :] for c in range(nchunk)], axis=1)
        hp = (x * _rms_scale(x, x.shape[-1]) * gple_ref[...]).astype(BF16)
        gate = jax.nn.sigmoid(jnp.dot(hp, wpg_ref[...], preferred_element_type=F32))
        o_ref[r0:r0 + rows, :] = x + gate * emb[r0:r0 + rows]


def _const(shape):
    nd = len(shape)
    return pl.BlockSpec(shape, lambda *_: (0,) * nd)


def _head_slab_cols(w, width, offset=0):
    k = w.shape[0]
    w = w.reshape(k, N_HEADS, width)
    w = jnp.pad(w, ((0, 0), (0, 0), (offset, LANE - width - offset)))
    return w.reshape(k, N_HEADS * LANE)


def _prep_weights(g_mix, w_in, g_cq, w_uq, g_qn, g_qr, g_ckv, w_ukv, g_kn, g_kr, w_oa,
                  conv_w, w_oc, w_o, g_ffn, w_rg, b_rg, w_re, b_re, w1, w3, w2, g_ple, w_pg, w_ple):
    d = w_in.shape[0]
    n_mla = Q_LORA + KV_LORA
    kr_cols = jnp.pad(w_in[:, n_mla:n_mla + QK_ROPE], ((0, 0), (QK_NOPE, LANE - QK_NOPE - QK_ROPE)))
    w = {}
    w["wa"] = jnp.concatenate([w_in[:, :n_mla], kr_cols], axis=1).astype(BF16)
    w["wb"] = w_in[:, n_mla + QK_ROPE:].astype(BF16)
    w["wuq"] = _head_slab_cols(w_uq, QK_NOPE + QK_ROPE).astype(BF16)
    ukv = w_ukv.reshape(KV_LORA, N_HEADS, QK_NOPE + V_DIM)
    w["wuk"] = _head_slab_cols(ukv[:, :, :QK_NOPE].reshape(KV_LORA, -1), QK_NOPE).astype(BF16)
    w["wuv"] = _head_slab_cols(ukv[:, :, QK_NOPE:].reshape(KV_LORA, -1), V_DIM).astype(BF16)
    pad_hi = LANE - QK_NOPE - QK_ROPE
    w["gq"] = (jnp.pad(jnp.concatenate([g_qn, g_qr]), (0, pad_hi)) * (ATTN_SCALE * LOG2E))[None]
    w["gk"] = jnp.pad(g_kn, (0, LANE - QK_NOPE))[None]
    w["gkr"] = jnp.pad(g_kr, (QK_NOPE, pad_hi))[None]
    w["gmix"], w["gcq"], w["gckv"] = g_mix[None], g_cq[None], g_ckv[None]
    w["gffn"], w["gple"] = g_ffn[None], g_ple[None]
    w["convw"] = jnp.pad(conv_w, ((0, 8 - conv_w.shape[0]), (0, 0)))
    w["woa"], w["woc"], w["wo"] = w_oa.astype(BF16), w_oc.astype(BF16), w_o.astype(BF16)
    n_r = N_GROUPS + N_EXPERTS
    w["wr"] = jnp.pad(jnp.concatenate([w_rg, w_re], axis=1), ((0, 0), (0, LANE - n_r))).astype(BF16)
    w["br"] = jnp.pad(jnp.concatenate([b_rg, b_re]), (0, LANE - n_r))[None]
    pad_g = EXPERTS_PER_GROUP - N_GROUPS
    wrt = jnp.concatenate([jnp.pad(w_rg.T, ((0, pad_g), (0, 0))), w_re.T], axis=0)
    wrt_hi = wrt.astype(BF16)
    w["wrt"] = jnp.concatenate([wrt_hi, (wrt - wrt_hi.astype(F32)).astype(BF16)], axis=0)
    w["brt"] = jnp.concatenate([b_rg, jnp.full((pad_g,), NEG_BIG, F32), b_re])[:, None]
    w["w13"] = jnp.concatenate([w1, w3], axis=2).astype(BF16)
    w["w2"] = w2.astype(BF16)
    w["wpg"], w["wple"] = w_pg.astype(BF16), w_ple.astype(BF16)
    return w


def _rope_slabs(pos):
    inv = 1.0 / (ROPE_THETA ** (jnp.arange(0, QK_ROPE, 2, dtype=F32) / QK_ROPE))
    ang = pos.astype(F32)[:, None] * inv[None, :]
    cos, sin = jnp.cos(ang), jnp.sin(ang)
    n = pos.shape[0]
    half = QK_ROPE // 2
    z = lambda k: jnp.zeros((n, k), F32)
    pad_hi = LANE - QK_NOPE - QK_ROPE
    rc = jnp.concatenate([jnp.ones((n, QK_NOPE), F32), cos, cos, z(pad_hi)], axis=1)
    rs1 = jnp.concatenate([z(QK_NOPE), -sin, z(half), z(pad_hi)], axis=1)
    rs2 = jnp.concatenate([z(QK_NOPE), z(half), sin, z(pad_hi)], axis=1)
    return (rc, rs1, rs2), cos.T, sin.T


def _params(sem):
    return pltpu.CompilerParams(dimension_semantics=sem, vmem_limit_bytes=VMEM_LIMIT)


def _mla_pre(x, w, rope, tm):
    b, s, d = x.shape
    hw = N_HEADS * LANE
    tok = lambda width: pl.BlockSpec((1, tm, width), lambda i, j: (i, j, 0))
    rope_spec = pl.BlockSpec((tm, LANE), lambda i, j: (j, 0))
    consts = [w["gmix"], w["wa"], w["gcq"], w["gckv"], w["gkr"], w["wuq"], w["wuk"], w["wuv"], w["gq"], w["gk"]]
    return pl.pallas_call(
        _pre_kernel,
        grid=(b, s // tm),
        in_specs=[tok(d)] + [_const(c.shape) for c in consts] + [rope_spec] * 3,
        out_specs=[tok(hw), tok(hw), tok(hw), tok(KV_LORA), tok(QK_ROPE)],
        out_shape=[jax.ShapeDtypeStruct((b, s, hw), BF16)] * 3
        + [jax.ShapeDtypeStruct((b, s, KV_LORA), F32), jax.ShapeDtypeStruct((b, s, QK_ROPE), F32)],
        compiler_params=_params(("parallel", "parallel")),
        name="mla_pre",
    )(x, *consts, *rope)


def _mla_pre_t(x, w, rope, cos_t, sin_t, tm):
    b, s, d = x.shape
    hw = N_HEADS * LANE
    tok = lambda width: pl.BlockSpec((1, tm, width), lambda i, j: (i, j, 0))
    tiled = pl.BlockSpec((1, 1, hw, tm), lambda i, j: (i, j, 0, 0))
    rope_spec = pl.BlockSpec((tm, LANE), lambda i, j: (j, 0))
    rope_t_spec = pl.BlockSpec((QK_ROPE // 2, tm), lambda i, j: (0, j))
    gqt = jnp.broadcast_to(w["gq"].T, (LANE, tm))
    consts = [w["gmix"], w["wa"], w["gcq"], w["gckv"], w["gkr"], w["wuq"].T, w["wuk"], w["wuv"].T, gqt, w["gk"]]
    return pl.pallas_call(
        _pre_kernel_t,
        grid=(b, s // tm),
        in_specs=[tok(d)] + [_const(c.shape) for c in consts] + [rope_spec] * 3 + [rope_t_spec] * 2,
        out_specs=[tiled, tok(hw), tiled, tok(KV_LORA), tok(QK_ROPE)],
        out_shape=[jax.ShapeDtypeStruct((b, s // tm, hw, tm), BF16), jax.ShapeDtypeStruct((b, s, hw), BF16),
                   jax.ShapeDtypeStruct((b, s // tm, hw, tm), BF16),
                   jax.ShapeDtypeStruct((b, s, KV_LORA), F32), jax.ShapeDtypeStruct((b, s, QK_ROPE), F32)],
        compiler_params=_params(("parallel", "parallel")),
        name="mla_pre_t",
    )(x, *consts, *rope, cos_t, sin_t)


def _attn_prompt(qt, k, vt):
    b, n_tiles, _, tq = qt.shape
    s = n_tiles * tq
    nh = ATTN_HEADS_PER_STEP
    width = nh * LANE
    return pl.pallas_call(
        functools.partial(_attn_prompt_kernel, tq=tq, nh=nh),
        grid=(b, N_HEADS // nh, n_tiles),
        in_specs=[pl.BlockSpec((1, 1, width, tq), lambda i, h, j: (i, j, h, 0)),
                  pl.BlockSpec((1, s, width), lambda i, h, j: (i, 0, h)),
                  pl.BlockSpec((1, n_tiles, width, tq), lambda i, h, j: (i, 0, h, 0))],
        out_specs=pl.BlockSpec((1, tq, nh * V_DIM), lambda i, h, j: (i, j, h)),
        out_shape=jax.ShapeDtypeStruct((b, s, N_HEADS * V_DIM), BF16),
        scratch_shapes=[pltpu.VMEM((tq, tq), F32)] * (2 * nh) + [pltpu.VMEM((tq, tq), BF16)] * (2 * nh),
        compiler_params=_params(("parallel", "parallel", "arbitrary")),
        name="attn_prompt",
    )(qt, k, vt)


def _attn_sample(q, past_lat, past_kpe, kn, vn, w):
    b, s, _ = q.shape
    t = past_lat.shape[1]
    hw = N_HEADS * LANE
    kpe_slab = jnp.pad(past_kpe, ((0, 0), (0, 0), (QK_NOPE, LANE - QK_NOPE - QK_ROPE)))
    blk = lambda rows, width: pl.BlockSpec((1, rows, width), lambda i: (i, 0, 0))
    consts = [w["wuk"], w["wuv"], w["gk"]]
    return pl.pallas_call(
        _attn_sample_kernel,
        grid=(b,),
        in_specs=[blk(s, hw), blk(t, KV_LORA), blk(t, LANE), blk(s, hw), blk(s, hw)]
        + [_const(c.shape) for c in consts],
        out_specs=pl.BlockSpec((1, s, N_HEADS * V_DIM), lambda i: (i, 0, 0)),
        out_shape=jax.ShapeDtypeStruct((b, s, N_HEADS * V_DIM), BF16),
        compiler_params=_params(("parallel",)),
        name="attn_sample",
    )(q, past_lat, kpe_slab, kn, vn, *consts)


def _post(x_long, attn_long, x_short, attn_short, cinit_short, w, tm, seq):
    b, s, d = x_long.shape
    nchunk = d // LANE
    n_keys = N_EXPERTS * EXPERTS_PER_GROUP
    tiles_per_seq = s // tm
    n_long = b * tiles_per_seq
    n_short = x_short.shape[1] // tm
    long_tile = lambda t: jnp.minimum(t, n_long - 1)
    short_tile = lambda t: jnp.maximum(t - n_long, 0)
    long_tok = lambda width: pl.BlockSpec(
        (1, tm, width), lambda t: (long_tile(t) // tiles_per_seq, long_tile(t) % tiles_per_seq, 0))
    short_tok = lambda width: pl.BlockSpec((1, tm, width), lambda t: (0, short_tile(t), 0))
    cinit_long = jnp.zeros((b, SUBLANE, CONV_DIM), F32)
    hm = tm // 2
    tri = (jnp.arange(hm)[:, None] < jnp.arange(hm)[None, :]).astype(BF16)
    consts = [w["gmix"], w["wb"], w["convw"], w["woa"], w["woc"], w["wo"], w["gffn"], w["wrt"], w["brt"], tri]
    in_specs = [long_tok(d), long_tok(N_HEADS * V_DIM),
                pl.BlockSpec((1, SUBLANE, CONV_DIM), lambda t: (long_tile(t) // tiles_per_seq, 0, 0)),
                short_tok(d), short_tok(N_HEADS * V_DIM),
                pl.BlockSpec((1, 2 * tm, CONV_DIM), lambda t: (short_tile(t), 0, 0))]
    return pl.pallas_call(
        functools.partial(_post_kernel, n_long=n_long, tiles_per_seq=tiles_per_seq, seq=seq),
        grid=(n_long + n_short,),
        in_specs=in_specs + [_const(c.shape) for c in consts],
        out_specs=[pl.BlockSpec((tm * nchunk, LANE), lambda t: (t, 0)),
                   pl.BlockSpec((1, 8, tm), lambda t: (t, 0, 0)),
                   pl.BlockSpec((1, 2, CONV_DIM), lambda t: (long_tile(t) // tiles_per_seq, 0, 0)),
                   pl.BlockSpec((1, tm // seq, 2, CONV_DIM), lambda t: (short_tile(t), 0, 0, 0)),
                   _const((n_keys, 1))],
        out_shape=[jax.ShapeDtypeStruct(((n_long + n_short) * tm * nchunk, LANE), F32),
                   jax.ShapeDtypeStruct((n_long + n_short, 8, tm), F32),
                   jax.ShapeDtypeStruct((b, 2, CONV_DIM), F32),
                   jax.ShapeDtypeStruct((n_short, tm // seq, 2, CONV_DIM), F32),
                   jax.ShapeDtypeStruct((n_keys, 1), F32)],
        scratch_shapes=[pltpu.VMEM((SUBLANE, CONV_DIM), F32)],
        compiler_params=_params(("arbitrary",)),
        name="post",
    )(x_long, attn_long, cinit_long, x_short, attn_short, cinit_short, *consts)


def _route_tables(key, rank, counts, n):
    n_keys = counts.shape[0]
    padded = (counts + MOE_ROWS - 1) // MOE_ROWS * MOE_ROWS
    pend = jnp.cumsum(padded)
    pstart = pend - padded
    ids = jnp.arange(n_keys, dtype=jnp.int32)
    dest = rank + jnp.sum(jnp.where(key[:, None] == ids[None, :], pstart[None, :], 0), axis=1)
    nblk = n // MOE_ROWS + N_PAIR_BUCKETS
    blk_start = jnp.arange(nblk, dtype=jnp.int32) * MOE_ROWS
    blk_hot = (blk_start[:, None] >= pstart[None, :]) & (blk_start[:, None] < pend[None, :])
    blk_key = jnp.sum(jnp.where(blk_hot, ids[None, :], 0), axis=1)
    blk_cnt = jnp.sum(jnp.where(blk_hot, jnp.minimum(counts[None, :] - (blk_start[:, None] - pstart[None, :]),
                                                      MOE_ROWS), 0), axis=1)
    any_hot = jnp.any(blk_hot, axis=1)
    blk_key = jnp.where(any_hot, blk_key, n_keys - 1)
    blk_lo = blk_key // EXPERTS_PER_GROUP
    blk_hi = blk_lo // EXPERTS_PER_GROUP * EXPERTS_PER_GROUP + blk_key % EXPERTS_PER_GROUP
    return blk_lo, blk_hi, blk_cnt.astype(jnp.int32), dest.astype(jnp.int32)


def _moe(x1_all, key, rank, counts, w):
    rows, _ = x1_all.shape
    d = w["gffn"].shape[1]
    nchunk = d // LANE
    n = rows // nchunk
    blk_lo, blk_hi, blk_cnt, dest = _route_tables(key, rank, counts, n)
    nblk = blk_lo.shape[0]
    w13_spec = lambda ref_idx: pl.BlockSpec((1, d, 2 * EXPERT_HIDDEN),
                                            lambda i, lo, hi, cnt, dst: ((lo, hi)[ref_idx][i], 0, 0))
    w2_spec = lambda ref_idx: pl.BlockSpec((1, EXPERT_HIDDEN, d),
                                           lambda i, lo, hi, cnt, dst: ((lo, hi)[ref_idx][i], 0, 0))
    buf = pltpu.VMEM((2, MOE_ROWS * nchunk, LANE), F32)
    grid_spec = pltpu.PrefetchScalarGridSpec(
        num_scalar_prefetch=4,
        grid=(nblk,),
        in_specs=[pl.BlockSpec(memory_space=pl.ANY),
                  pl.BlockSpec((1, d), lambda i, *_: (0, 0)),
                  pl.BlockSpec(w["wr"].shape, lambda i, *_: (0, 0)),
                  pl.BlockSpec(w["br"].shape, lambda i, *_: (0, 0)),
                  w13_spec(0), w13_spec(1), w2_spec(0), w2_spec(1)],
        out_specs=pl.BlockSpec(memory_space=pl.ANY),
        scratch_shapes=[pltpu.SMEM((nblk * MOE_ROWS,), jnp.int32), buf, buf,
                        pltpu.SemaphoreType.DMA((2,)), pltpu.SemaphoreType.DMA((2,))],
    )
    return pl.pallas_call(
        _moe_kernel,
        grid_spec=grid_spec,
        out_shape=jax.ShapeDtypeStruct((rows, LANE), F32),
        compiler_params=_params(("arbitrary",)),
        name="moe",
    )(blk_lo, blk_hi, blk_cnt, dest, x1_all, w["gffn"], w["wr"], w["br"], w["w13"], w["w13"], w["w2"], w["w2"])


def _ple(x2_all, row_off, p, w, tm):
    n, pd = p.shape
    d = w["gple"].shape[1]
    nchunk = d // LANE
    off = row_off // tm
    consts = [w["gple"], w["wpg"], w["wple"]]
    return pl.pallas_call(
        _ple_kernel,
        grid=(n // tm,),
        in_specs=[pl.BlockSpec((tm * nchunk, LANE), lambda i: (off + i, 0)), pl.BlockSpec((tm, pd), lambda i: (i, 0))]
        + [_const(c.shape) for c in consts],
        out_specs=pl.BlockSpec((tm, d), lambda i: (i, 0)),
        out_shape=jax.ShapeDtypeStruct((n, d), F32),
        compiler_params=_params(("parallel",)),
        name="ple",
    )(x2_all, p, *consts)


def _layer(xp, xs, pp, ps, past_lat, past_kpe, past_conv, w):
    bp, sp, d = xp.shape
    bs, ss, _ = xs.shape
    n_p, n_s = bp * sp, bs * ss
    past_len = past_lat.shape[1]
    tm_p, tm_s = min(TOKEN_TILE, sp), min(TOKEN_TILE, n_s)
    tm_post = min(tm_p, tm_s)

    rope, cos_t, sin_t = _rope_slabs(jnp.arange(sp))
    qt, k, vt, lat_p, kpe_p = _mla_pre_t(xp, w, rope, cos_t, sin_t, tm_p)
    attn_p = _attn_prompt(qt, k, vt)

    rope, _, _ = _rope_slabs(past_len + jnp.arange(n_s) % ss)
    xs_rows = xs.reshape(1, n_s, d)
    q, k, v, lat_s, kpe_s = _mla_pre(xs_rows, w, rope, tm_s)
    by_seq = lambda a: a.reshape(bs, ss, a.shape[-1])
    attn_s = _attn_sample(by_seq(q), past_lat, past_kpe, by_seq(k), by_seq(v), w).reshape(1, n_s, -1)
    lat_s, kpe_s = by_seq(lat_s), by_seq(kpe_s)
    in_tiles = lambda a: a.reshape(n_s // tm_post, tm_post, CONV_DIM)
    n_hist = past_conv.shape[1]
    cinit_s = jnp.concatenate(
        [in_tiles(jnp.pad(past_conv[:, n_hist - 1:], ((0, 0), (0, ss - 1), (0, 0)))),
         in_tiles(jnp.pad(past_conv, ((0, 0), (0, ss - n_hist), (0, 0))))], axis=1)

    x1_all, route, conv_p, conv_s, cnt = _post(xp, attn_p, xs_rows, attn_s, cinit_s, w, tm_post, ss)
    conv_s = conv_s.reshape(bs, n_hist, CONV_DIM)
    key = route[:, 0].reshape(-1).astype(jnp.int32)
    rank = route[:, 1].reshape(-1).astype(jnp.int32)
    x2_all = _moe(x1_all, key, rank, cnt[:, 0].astype(jnp.int32), w)
    yp = _ple(x2_all, 0, pp.reshape(n_p, -1), w, min(TOKEN_TILE, n_p)).reshape(bp, sp, d)
    ys = _ple(x2_all, n_p, ps.reshape(n_s, -1), w, min(TOKEN_TILE, n_s)).reshape(bs, ss, d)
    return yp, ys, (lat_p, kpe_p, conv_p, lat_s, kpe_s, conv_s)


def kernel(x_prompt, x_sample, cache_kv_latent, cache_k_rope, state_conv, p_prompt, p_sample,
           g_mix, w_in, g_cq, w_uq, g_qn, g_qr, g_ckv, w_ukv, g_kn, g_kr, w_oa,
           conv_w, w_oc, w_o, g_ffn, w_rg, b_rg, w_re, b_re, w1, w3, w2, g_ple, w_pg, w_ple):
    depth = g_mix.shape[0]
    xp, xs = x_prompt, x_sample
    outs = [[] for _ in range(6)]
    for i in range(depth):
        w = _prep_weights(g_mix[i], w_in[i], g_cq[i], w_uq[i], g_qn[i], g_qr[i], g_ckv[i], w_ukv[i],
                          g_kn[i], g_kr[i], w_oa[i], conv_w[i], w_oc[i], w_o[i], g_ffn[i], w_rg[i], b_rg[i],
                          w_re[i], b_re[i], w1[i], w3[i], w2[i], g_ple[i], w_pg[i], w_ple[i])
        xp, xs, new = _layer(xp, xs, p_prompt[i], p_sample[i], cache_kv_latent[i], cache_k_rope[i], state_conv[i], w)
        for o, a in zip(outs, new):
            o.append(a)
    return (xp, xs) + tuple(jnp.stack(o, axis=0) for o in outs)
```

```python
import functools
import math

import jax
import jax.numpy as jnp
from jax import lax
from jax.experimental import pallas as pl
from jax.experimental.pallas import tpu as pltpu

F32 = jnp.float32
BF16 = jnp.bfloat16

LANE = 128
SUBLANE = 8
TOKEN_TILE = 512
CHUNK = 64
N_HEADS = 8
QK_NOPE = 64
QK_ROPE = 32
V_DIM = 64
Q_LORA = 256
KV_LORA = 256
CONV_DIM = 512
N_GROUPS = 4
EXPERTS_PER_GROUP = 8
N_EXPERTS = N_GROUPS * EXPERTS_PER_GROUP
EXPERT_HIDDEN = 256
ROPE_THETA = 10000.0
EPS = 1e-6
ATTN_SCALE = (QK_NOPE + QK_ROPE) ** -0.5
LOG2E = math.log2(math.e)
NEG_BIG = -1e30
MOE_ROWS = 128
ATTN_HEADS_PER_STEP = 4
PLE_ROW_GROUPS = 2
N_PAIR_BUCKETS = N_GROUPS * (EXPERTS_PER_GROUP * (EXPERTS_PER_GROUP - 1) // 2)
VMEM_LIMIT = 56 * 1024 * 1024


def _rms_scale(x, n):
    return lax.rsqrt(jnp.sum(x * x, axis=-1, keepdims=True) * (1.0 / n) + EPS)


def _lane_iota(shape):
    return lax.broadcasted_iota(jnp.int32, shape, len(shape) - 1)


def _rope(t, rc, rs1, rs2):
    return t * rc + pltpu.roll(t, LANE - QK_ROPE // 2, 1) * rs1 + pltpu.roll(t, QK_ROPE // 2, 1) * rs2


def _pre_latents(x_ref, gmix_ref, wa_ref, gcq_ref, gckv_ref, gkr_ref, rope_refs, lat_ref, kpe_ref):
    x = x_ref[0]
    h = x * _rms_scale(x, x.shape[-1]) * gmix_ref[...]
    z = jnp.dot(h.astype(BF16), wa_ref[...], preferred_element_type=F32)
    cq = z[:, :Q_LORA]
    ckv = z[:, Q_LORA:Q_LORA + KV_LORA]
    kr = z[:, Q_LORA + KV_LORA:]
    cqn = cq * _rms_scale(cq, Q_LORA) * gcq_ref[...]
    lat = ckv * _rms_scale(ckv, KV_LORA) * gckv_ref[...]
    lat_ref[0] = lat
    krn = kr * _rms_scale(kr, QK_ROPE) * gkr_ref[...]
    kpe = _rope(krn, *(r[...] for r in rope_refs))
    kpe_ref[0] = kpe[:, QK_NOPE:QK_NOPE + QK_ROPE]
    return cqn, lat, kpe


def _store_keys(k_ref, kf, gk, kpe):
    for hd in range(N_HEADS):
        sl = slice(hd * LANE, (hd + 1) * LANE)
        ks = kf[:, sl]
        k_ref[0, :, sl] = (ks * _rms_scale(ks, QK_NOPE) * gk + kpe).astype(BF16)


def _pre_kernel(x_ref, gmix_ref, wa_ref, gcq_ref, gckv_ref, gkr_ref, wuq_ref, wuk_ref, wuv_ref,
                gq_ref, gk_ref, rc_ref, rs1_ref, rs2_ref,
                q_ref, k_ref, v_ref, lat_ref, kpe_ref):
    rope_refs = (rc_ref, rs1_ref, rs2_ref)
    cqn, lat, kpe = _pre_latents(x_ref, gmix_ref, wa_ref, gcq_ref, gckv_ref, gkr_ref, rope_refs, lat_ref, kpe_ref)
    rc, rs1, rs2 = (r[...] for r in rope_refs)
    is_nope = _lane_iota(kpe.shape) < QK_NOPE
    latb = lat.astype(BF16)
    qf = jnp.dot(cqn.astype(BF16), wuq_ref[...], preferred_element_type=F32)
    vf = jnp.dot(latb, wuv_ref[...], preferred_element_type=F32)
    _store_keys(k_ref, jnp.dot(latb, wuk_ref[...], preferred_element_type=F32), gk_ref[...], kpe)
    gq = gq_ref[...]
    ones_hi = jnp.where(is_nope, 0.0, 1.0)
    for hd in range(N_HEADS):
        sl = slice(hd * LANE, (hd + 1) * LANE)
        qs = qf[:, sl]
        sq = qs * qs
        ss_all = jnp.sum(sq, axis=-1, keepdims=True)
        ss_n = jnp.sum(jnp.where(is_nope, sq, 0.0), axis=-1, keepdims=True)
        r = jnp.where(is_nope, lax.rsqrt(ss_n * (1.0 / QK_NOPE) + EPS),
                      lax.rsqrt((ss_all - ss_n) * (1.0 / QK_ROPE) + EPS))
        q_ref[0, :, sl] = _rope(qs * r * gq, rc, rs1, rs2).astype(BF16)
        v_ref[0, :, sl] = (vf[:, sl] + ones_hi).astype(BF16)


def _pre_kernel_t(x_ref, gmix_ref, wa_ref, gcq_ref, gckv_ref, gkr_ref, wuqt_ref, wuk_ref, wuvt_ref,
                  gqt_ref, gk_ref, rc_ref, rs1_ref, rs2_ref, cos_ref, sin_ref,
                  qt_ref, k_ref, vt_ref, lat_ref, kpe_ref):
    rope_refs = (rc_ref, rs1_ref, rs2_ref)
    cqn, lat, kpe = _pre_latents(x_ref, gmix_ref, wa_ref, gcq_ref, gckv_ref, gkr_ref, rope_refs, lat_ref, kpe_ref)
    latb = lat.astype(BF16)
    _store_keys(k_ref, jnp.dot(latb, wuk_ref[...], preferred_element_type=F32), gk_ref[...], kpe)
    tm = cqn.shape[0]
    qft = jnp.dot(wuqt_ref[...], cqn.T.astype(BF16), preferred_element_type=F32)
    vft = jnp.dot(wuvt_ref[...], lat.T.astype(BF16), preferred_element_type=F32)
    gq, cos, sin = gqt_ref[...], cos_ref[...], sin_ref[...]
    half = QK_ROPE // 2
    ones_lo = jnp.where(lax.broadcasted_iota(jnp.int32, (LANE, tm), 0) < V_DIM, 0.0, 1.0)
    pad = jnp.zeros((LANE - QK_NOPE - QK_ROPE, tm), F32)
    for hd in range(N_HEADS):
        rows = slice(hd * LANE, (hd + 1) * LANE)
        qs = qft[rows]
        sq = qs * qs
        r_n = lax.rsqrt(jnp.sum(sq[:QK_NOPE], axis=0, keepdims=True) * (1.0 / QK_NOPE) + EPS)
        r_p = lax.rsqrt(jnp.sum(sq[QK_NOPE:QK_NOPE + QK_ROPE], axis=0, keepdims=True) * (1.0 / QK_ROPE) + EPS)
        nope = qs[:QK_NOPE] * r_n * gq[:QK_NOPE]
        x1 = qs[QK_NOPE:QK_NOPE + half] * r_p * gq[QK_NOPE:QK_NOPE + half]
        x2 = qs[QK_NOPE + half:QK_NOPE + QK_ROPE] * r_p * gq[QK_NOPE + half:QK_NOPE + QK_ROPE]
        slab = jnp.concatenate([nope, x1 * cos - x2 * sin, x1 * sin + x2 * cos, pad], axis=0)
        qt_ref[0, 0, rows, :] = slab.astype(BF16)
        vt_ref[0, 0, rows, :] = (vft[rows] + ones_lo).astype(BF16)


def _finish_pair(accs):
    outs = [a / pltpu.roll(a, V_DIM, 1) for a in accs]
    lane = _lane_iota(outs[0].shape)
    return jnp.where(lane < V_DIM, outs[0], pltpu.roll(outs[1], V_DIM, 1))


def _attn_prompt_kernel(qt_ref, k_ref, vt_ref, o_ref, *scratch, tq, nh):
    s_refs = (scratch[:nh], scratch[nh:2 * nh])
    p_refs = (scratch[2 * nh:3 * nh], scratch[3 * nh:])
    qi = pl.program_id(2)
    key_pos = lax.broadcasted_iota(jnp.int32, (tq, tq), 0)
    query_pos = lax.broadcasted_iota(jnp.int32, (tq, tq), 1)
    diag_mask = (key_pos // CHUNK) <= (query_pos // CHUNK)
    heads = [slice(hh * LANE, (hh + 1) * LANE) for hh in range(nh)]

    def scores(i, slot, hh, mask):
        start = pl.multiple_of(i * tq, tq)
        s = jnp.dot(k_ref[0, pl.ds(start, tq), heads[hh]], qt_ref[0, 0, heads[hh], :], preferred_element_type=F32)
        if mask is not None:
            s = jnp.where(mask, s, NEG_BIG)
        s_refs[slot][hh][...] = s
        return jnp.max(s, axis=0, keepdims=True)

    def softmax(slot, hh, m, tile_max):
        m_new = jnp.maximum(m, tile_max)
        p_refs[slot][hh][...] = jnp.exp2(s_refs[slot][hh][...] - m_new).astype(BF16)
        return m_new, jnp.exp2(m - m_new)

    def accumulate(i, slot, hh, alpha, acc):
        pv = jnp.dot(vt_ref[0, jnp.maximum(i, 0), heads[hh], :], p_refs[slot][hh][...],
                     preferred_element_type=F32)
        return acc * alpha + pv

    def iteration(i, slot, carry, next_mask=None):
        stats = [softmax(slot, hh, m, tile_max) for hh, (m, tile_max, _, _) in enumerate(carry)]
        next_max = [scores(i + 1, 1 - slot, hh, next_mask) for hh in range(nh)]
        accs = [accumulate(i - 1, 1 - slot, hh, alpha, acc) for hh, (_, _, alpha, acc) in enumerate(carry)]
        return tuple((m, tmax, alpha, acc) for (m, alpha), tmax, acc in zip(stats, next_max, accs))

    def last(slot, carry):
        accs = []
        for hh, (m, tile_max, alpha, acc) in enumerate(carry):
            acc = accumulate(qi - 1, 1 - slot, hh, alpha, acc)
            m, alpha = softmax(slot, hh, m, tile_max)
            accs.append(accumulate(qi, slot, hh, alpha, acc))
        return tuple(accs)

    first_mask = diag_mask | (qi > 0)
    init = []
    for hh in range(nh):
        p_refs[1][hh][...] = jnp.zeros((tq, tq), BF16)
        init.append((jnp.full((1, tq), NEG_BIG, F32), scores(0, 0, hh, first_mask), jnp.ones((1, tq), F32),
                     jnp.zeros((LANE, tq), F32)))
    carry = lax.fori_loop(0, (qi - 1) // 2, lambda j, c: iteration(2 * j + 1, 1, iteration(2 * j, 0, c)),
                          tuple(init))
    tails = [lambda c: last(0, c),
             lambda c: last(1, iteration(qi - 1, 0, c, diag_mask)),
             lambda c: last(0, iteration(qi - 1, 1, iteration(qi - 2, 0, c), diag_mask))]
    accs = lax.switch(jnp.where(qi == 0, 0, 2 - qi % 2), tails, carry)
    out_t = jnp.concatenate([a[:V_DIM] / a[V_DIM:V_DIM + 1] for a in accs], axis=0)
    o_ref[0] = out_t.T.astype(o_ref.dtype)


def _attn_sample_kernel(q_ref, lat_ref, kpe_ref, kn_ref, vn_ref, wuk_ref, wuv_ref, gk_ref, o_ref):
    nt = (((1,), (1,)), ((), ()))
    slabs = [slice(hd * LANE, (hd + 1) * LANE) for hd in range(N_HEADS)]
    latb = lat_ref[0].astype(BF16)
    kf = jnp.dot(latb, wuk_ref[...], preferred_element_type=F32)
    vf = jnp.dot(latb, wuv_ref[...], preferred_element_type=F32)
    kpe = kpe_ref[0]
    ones_hi = jnp.where(_lane_iota(kpe.shape) < QK_NOPE, 0.0, 1.0)
    gk = gk_ref[...]
    kp = [(kf[:, sl] * _rms_scale(kf[:, sl], QK_NOPE) * gk + kpe).astype(BF16) for sl in slabs]
    vp = [(vf[:, sl] + ones_hi).astype(BF16) for sl in slabs]
    s_past = [lax.dot_general(q_ref[0, :, sl], k, nt, preferred_element_type=F32) for sl, k in zip(slabs, kp)]
    s_new = [lax.dot_general(q_ref[0, :, sl], kn_ref[0, :, sl], nt, preferred_element_type=F32) for sl in slabs]
    accs = []
    for sl, sp, sn, v in zip(slabs, s_past, s_new, vp):
        m = jnp.maximum(jnp.max(sp, axis=-1, keepdims=True), jnp.max(sn, axis=-1, keepdims=True))
        accs.append(jnp.dot(jnp.exp2(sp - m).astype(BF16), v, preferred_element_type=F32)
                    + jnp.dot(jnp.exp2(sn - m).astype(BF16), vn_ref[0, :, sl], preferred_element_type=F32))
    for pair in range(N_HEADS // 2):
        o_ref[0, :, pair * LANE:(pair + 1) * LANE] = _finish_pair(accs[2 * pair:2 * pair + 2]).astype(o_ref.dtype)


def _post_kernel(xl_ref, al_ref, cl_ref, xs_ref, as_ref, cs_ref, *rest, n_long, tiles_per_seq, seq):
    shared, (x1_ref, route_ref, cnewl_ref, cnews_ref, cnt_ref, carry_ref) = rest[:-6], rest[-6:]
    t = pl.program_id(0)

    @pl.when(t == 0)
    def _():
        cnt_ref[...] = jnp.zeros_like(cnt_ref)

    @pl.when(t < n_long)
    def _():
        _post_tile(xl_ref, al_ref, cl_ref, *shared, x1_ref, route_ref, cnewl_ref, cnt_ref, carry_ref,
                   first=t % tiles_per_seq == 0, seq=None)

    @pl.when(t >= n_long)
    def _():
        _post_tile(xs_ref, as_ref, cs_ref, *shared, x1_ref, route_ref, cnews_ref, cnt_ref, carry_ref,
                   first=None, seq=seq)


def _post_tile(x_ref, attn_ref, cinit_ref, gmix_ref, wb_ref, convw_ref, woa_ref, woc_ref, wo_ref,
               gffn_ref, wrt_ref, brt_ref, tri_ref,
               x1_ref, route_ref, cnew_ref, cnt_ref, carry_ref, *, first, seq):
    tm = x_ref.shape[1]

    if seq is None:
        @pl.when(first)
        def _():
            carry_ref[...] = cinit_ref[0]

    d = x_ref.shape[-1]
    hm = tm // 2
    groups = (0, hm)
    xs, zs = [], []
    for r0 in groups:
        x = x_ref[0, r0:r0 + hm, :]
        h = x * _rms_scale(x, d) * gmix_ref[...]
        xs.append(x)
        zs.append(jnp.dot(h.astype(BF16), wb_ref[...], preferred_element_type=F32))

    cw = convw_ref[...]
    row = lax.broadcasted_iota(jnp.int32, (hm, CONV_DIM), 0)
    if seq is None:
        c1, c2 = carry_ref[SUBLANE - 1:SUBLANE, :], carry_ref[SUBLANE - 2:SUBLANE - 1, :]
    merged = []
    for r0, z in zip(groups, zs):
        conv_b = z[:, :CONV_DIM]
        u = z[:, CONV_DIM:2 * CONV_DIM] * z[:, 2 * CONV_DIM:3 * CONV_DIM]
        if seq is None:
            u_m1 = jnp.where(row == 0, c1, pltpu.roll(u, 1, 0))
            u_m2 = jnp.where(row == 0, c2, jnp.where(row == 1, c1, pltpu.roll(u, 2, 0)))
            c1, c2 = u[hm - 1:, :], u[hm - 2:hm - 1, :]
            if r0 + hm == tm:
                carry_ref[...] = u[hm - SUBLANE:, :]
                cnew_ref[0] = u[hm - 2:, :]
        else:
            pos = row % seq
            u_m1 = jnp.where(pos == 0, cinit_ref[0, r0:r0 + hm, :], pltpu.roll(u, 1, 0))
            u_m2 = jnp.where(pos <= 1, cinit_ref[0, tm + r0:tm + r0 + hm, :], pltpu.roll(u, 2, 0))
            cnew_ref[0, r0 // seq:(r0 + hm) // seq] = u.reshape(hm // seq, seq, CONV_DIM)[:, seq - 2:, :]
        cv = cw[0:1, :] * u_m2 + cw[1:2, :] * u_m1 + cw[2:3, :] * u
        y_a = jnp.dot(attn_ref[0, r0:r0 + hm, :], woa_ref[...], preferred_element_type=F32)
        y_c = jnp.dot((conv_b * cv).astype(BF16), woc_ref[...], preferred_element_type=F32)
        gate_a = z[:, 3 * CONV_DIM:3 * CONV_DIM + d]
        gate_c = z[:, 3 * CONV_DIM + d:]
        merged.append((jax.nn.sigmoid(gate_a) * y_a + jax.nn.sigmoid(gate_c) * y_c).astype(BF16))
    x1s = [x + jnp.dot(mrg, wo_ref[...], preferred_element_type=F32) for x, mrg in zip(xs, merged)]
    nchunk = d // LANE
    for r0, x1 in zip(groups, x1s):
        for c in range(nchunk):
            x1_ref[pl.ds(r0 * nchunk + c, hm, stride=nchunk), :] = x1[:, c * LANE:(c + 1) * LANE]

    nt = (((1,), (1,)), ((), ()))
    n_rows = brt_ref.shape[0]
    logits = []
    for x1 in x1s:
        h2 = x1 * _rms_scale(x1, d) * gffn_ref[...]
        h2_hi = h2.astype(BF16)
        h2_lo = (h2 - h2_hi.astype(F32)).astype(BF16)
        lt2 = lax.dot_general(wrt_ref[...], h2_hi, nt, preferred_element_type=F32)
        logits.append(lt2[:n_rows] + lt2[n_rows:]
                      + lax.dot_general(wrt_ref[:n_rows, :], h2_lo, nt, preferred_element_type=F32) + brt_ref[...])
    sub = lax.broadcasted_iota(jnp.int32, (EXPERTS_PER_GROUP, hm), 0).astype(F32)
    none = float(EXPERTS_PER_GROUP)

    def first_argmax(v):
        vmax = jnp.max(v, axis=0, keepdims=True)
        return jnp.min(jnp.where(v == vmax, sub, none), axis=0, keepdims=True)

    n_keys = cnt_ref.shape[0]
    keys = lax.broadcasted_iota(jnp.int32, (n_keys, hm), 0).astype(F32)
    row8 = lax.broadcasted_iota(jnp.int32, (8, hm), 0)
    seen = cnt_ref[...]
    for r0, lt in zip(groups, logits):
        g_idx = first_argmax(lt[:EXPERTS_PER_GROUP])
        el = lt[EXPERTS_PER_GROUP * N_GROUPS:]
        for g in range(N_GROUPS - 2, -1, -1):
            el = jnp.where(g_idx == g, lt[EXPERTS_PER_GROUP * (g + 1):EXPERTS_PER_GROUP * (g + 2)], el)
        i1 = first_argmax(el)
        i2 = first_argmax(jnp.where(sub == i1, NEG_BIG, el))
        key = (g_idx * (EXPERTS_PER_GROUP * EXPERTS_PER_GROUP) + jnp.minimum(i1, i2) * EXPERTS_PER_GROUP
               + jnp.maximum(i1, i2))
        onehot = jnp.where(keys == key, 1.0, 0.0)
        before = jnp.dot(onehot.astype(BF16), tri_ref[...], preferred_element_type=F32)
        rank = jnp.sum(onehot * (before + seen), axis=0, keepdims=True)
        seen = seen + jnp.sum(onehot, axis=1, keepdims=True)
        route_ref[0, :, r0:r0 + hm] = jnp.where(row8 == 0, key, jnp.where(row8 == 1, rank, 0.0))
    cnt_ref[...] = seen


def _rows_loop(n, fn, unroll=8):
    def group(j, c):
        for u in range(unroll):
            fn(j * unroll + u, u)
        return c
    lax.fori_loop(0, n // unroll, group, 0)

    def single(i, c):
        fn(i, 0)
        return c
    lax.fori_loop(n // unroll * unroll, n, single, 0)


def _moe_kernel(lo_ref, hi_ref, cnt_ref, dest_ref,
                x1_hbm, gffn_ref, wr_ref, br_ref, w13lo_ref, w13hi_ref, w2lo_ref, w2hi_ref,
                x2_hbm, tok_ref, xbuf, obuf, gsem, ssem):
    nb = pl.program_id(0)
    nblk = pl.num_programs(0)
    nchunk = xbuf.shape[1] // MOE_ROWS
    slot = nb % 2
    cnt = cnt_ref[nb]

    def start_gathers(blk, sl):
        base = blk * MOE_ROWS

        def one(i, lane):
            del lane
            src = x1_hbm.at[pl.ds(tok_ref[base + i] * nchunk, nchunk)]
            pltpu.make_async_copy(src, xbuf.at[sl, pl.ds(i * nchunk, nchunk)], gsem.at[sl]).start()
        _rows_loop(cnt_ref[blk], one)

    def start_scatters(blk, sl):
        base = blk * MOE_ROWS

        def one(i, lane):
            dst = x2_hbm.at[pl.ds(tok_ref[base + i] * nchunk, nchunk)]
            pltpu.make_async_copy(obuf.at[sl, pl.ds(i * nchunk, nchunk)], dst, ssem.at[sl]).start(priority=lane % 2)
        _rows_loop(cnt_ref[blk], one)

    def wait_gathers(blk, sl):
        rows = cnt_ref[blk] * nchunk
        pltpu.make_async_copy(x1_hbm.at[pl.ds(0, rows)], xbuf.at[sl, pl.ds(0, rows)], gsem.at[sl]).wait()

    def wait_scatters(blk, sl):
        rows = cnt_ref[blk] * nchunk
        pltpu.make_async_copy(obuf.at[sl, pl.ds(0, rows)], x2_hbm.at[pl.ds(0, rows)], ssem.at[sl]).wait()

    @pl.when(nb == 0)
    def _():
        def invert(i, c):
            tok_ref[dest_ref[i]] = i
            return c
        lax.fori_loop(0, dest_ref.shape[0], invert, 0, unroll=16)
        xbuf[...] = jnp.zeros_like(xbuf)

        @pl.when(cnt > 0)
        def _():
            start_gathers(0, 0)

    @pl.when(nb + 1 < nblk)
    def _():
        @pl.when(cnt_ref[nb + 1] > 0)
        def _():
            start_gathers(nb + 1, 1 - slot)

    @pl.when(nb >= 2)
    def _():
        @pl.when(cnt_ref[nb - 2] > 0)
        def _():
            wait_scatters(nb - 2, slot)

    @pl.when(cnt > 0)
    def _():
        wait_gathers(nb, slot)
        xg = jnp.concatenate([xbuf[slot, pl.ds(c, MOE_ROWS, stride=nchunk), :] for c in range(nchunk)], axis=1)
        h = (xg * _rms_scale(xg, xg.shape[-1]) * gffn_ref[...]).astype(BF16)

        lo, hi = lo_ref[nb], hi_ref[nb]
        logit = jnp.dot(h, wr_ref[...], preferred_element_type=F32) + br_ref[...]
        lane = _lane_iota(logit.shape)
        pick = lambda j: jnp.sum(jnp.where(lane == j, logit, 0.0), axis=-1, keepdims=True)
        is_g = lane < N_GROUPS
        gmax = jnp.max(jnp.where(is_g, logit, NEG_BIG), axis=-1, keepdims=True)
        g_den = jnp.sum(jnp.where(is_g, jnp.exp(logit - gmax), 0.0), axis=-1, keepdims=True)
        g_p = jnp.exp(pick(lo // EXPERTS_PER_GROUP) - gmax) / g_den
        l_lo, l_hi = pick(N_GROUPS + lo), pick(N_GROUPS + hi)
        gates = (g_p * jax.nn.sigmoid(l_lo - l_hi), g_p * jax.nn.sigmoid(l_hi - l_lo))

        abs_ = [jnp.dot(h, w13_ref[0], preferred_element_type=F32) for w13_ref in (w13lo_ref, w13hi_ref)]
        y = xg
        for ab, w2_ref, gate in zip(abs_, (w2lo_ref, w2hi_ref), gates):
            hid = jax.nn.silu(ab[:, :EXPERT_HIDDEN]) * ab[:, EXPERT_HIDDEN:]
            y = y + gate * jnp.dot(hid.astype(BF16), w2_ref[0], preferred_element_type=F32)
        for c in range(nchunk):
            obuf[slot, pl.ds(c, MOE_ROWS, stride=nchunk), :] = y[:, c * LANE:(c + 1) * LANE]
        start_scatters(nb, slot)

    @pl.when(nb == nblk - 1)
    def _():
        @pl.when(nb >= 1)
        def _():
            @pl.when(cnt_ref[nb - 1] > 0)
            def _():
                wait_scatters(nb - 1, 1 - slot)

        @pl.when(cnt > 0)
        def _():
            wait_scatters(nb, slot)


def _ple_kernel(x_ref, p_ref, gple_ref, wpg_ref, wple_ref, o_ref):
    tm = o_ref.shape[0]
    nchunk = o_ref.shape[1] // LANE
    emb = jnp.dot(p_ref[...].astype(BF16), wple_ref[...], preferred_element_type=F32)
    rows = tm // PLE_ROW_GROUPS
    for r0 in range(0, tm, rows):
        x = jnp.concatenate([x_ref[pl.ds(r0 * nchunk + c, rows, stride=nchunk), :] for c in range(nchunk)], axis=1)
        hp = (x * _rms_scale(x, x.shape[-1]) * gple_ref[...]).astype(BF16)
        gate = jax.nn.sigmoid(jnp.dot(hp, wpg_ref[...], preferred_element_type=F32))
        o_ref[r0:r0 + rows, :] = x + gate * emb[r0:r0 + rows]


def _const(shape):
    nd = len(shape)
    return pl.BlockSpec(shape, lambda *_: (0,) * nd)


def _head_slab_cols(w, width, offset=0):
    k = w.shape[0]
    w = w.reshape(k, N_HEADS, width)
    w = jnp.pad(w, ((0, 0), (0, 0), (offset, LANE - width - offset)))
    return w.reshape(k, N_HEADS * LANE)


def _prep_weights(g_mix, w_in, g_cq, w_uq, g_qn, g_qr, g_ckv, w_ukv, g_kn, g_kr, w_oa,
                  conv_w, w_oc, w_o, g_ffn, w_rg, b_rg, w_re, b_re, w1, w3, w2, g_ple, w_pg, w_ple):
    d = w_in.shape[0]
    n_mla = Q_LORA + KV_LORA
    kr_cols = jnp.pad(w_in[:, n_mla:n_mla + QK_ROPE], ((0, 0), (QK_NOPE, LANE - QK_NOPE - QK_ROPE)))
    w = {}
    w["wa"] = jnp.concatenate([w_in[:, :n_mla], kr_cols], axis=1).astype(BF16)
    w["wb"] = w_in[:, n_mla + QK_ROPE:].astype(BF16)
    w["wuq"] = _head_slab_cols(w_uq, QK_NOPE + QK_ROPE).astype(BF16)
    ukv = w_ukv.reshape(KV_LORA, N_HEADS, QK_NOPE + V_DIM)
    w["wuk"] = _head_slab_cols(ukv[:, :, :QK_NOPE].reshape(KV_LORA, -1), QK_NOPE).astype(BF16)
    w["wuv"] = _head_slab_cols(ukv[:, :, QK_NOPE:].reshape(KV_LORA, -1), V_DIM).astype(BF16)
    pad_hi = LANE - QK_NOPE - QK_ROPE
    w["gq"] = (jnp.pad(jnp.concatenate([g_qn, g_qr]), (0, pad_hi)) * (ATTN_SCALE * LOG2E))[None]
    w["gk"] = jnp.pad(g_kn, (0, LANE - QK_NOPE))[None]
    w["gkr"] = jnp.pad(g_kr, (QK_NOPE, pad_hi))[None]
    w["gmix"], w["gcq"], w["gckv"] = g_mix[None], g_cq[None], g_ckv[None]
    w["gffn"], w["gple"] = g_ffn[None], g_ple[None]
    w["convw"] = jnp.pad(conv_w, ((0, 8 - conv_w.shape[0]), (0, 0)))
    w["woa"], w["woc"], w["wo"] = w_oa.astype(BF16), w_oc.astype(BF16), w_o.astype(BF16)
    n_r = N_GROUPS + N_EXPERTS
    w["wr"] = jnp.pad(jnp.concatenate([w_rg, w_re], axis=1), ((0, 0), (0, LANE - n_r))).astype(BF16)
    w["br"] = jnp.pad(jnp.concatenate([b_rg, b_re]), (0, LANE - n_r))[None]
    pad_g = EXPERTS_PER_GROUP - N_GROUPS
    wrt = jnp.concatenate([jnp.pad(w_rg.T, ((0, pad_g), (0, 0))), w_re.T], axis=0)
    wrt_hi = wrt.astype(BF16)
    w["wrt"] = jnp.concatenate([wrt_hi, (wrt - wrt_hi.astype(F32)).astype(BF16)], axis=0)
    w["brt"] = jnp.concatenate([b_rg, jnp.full((pad_g,), NEG_BIG, F32), b_re])[:, None]
    w["w13"] = jnp.concatenate([w1, w3], axis=2).astype(BF16)
    w["w2"] = w2.astype(BF16)
    w["wpg"], w["wple"] = w_pg.astype(BF16), w_ple.astype(BF16)
    return w


def _rope_slabs(pos):
    inv = 1.0 / (ROPE_THETA ** (jnp.arange(0, QK_ROPE, 2, dtype=F32) / QK_ROPE))
    ang = pos.astype(F32)[:, None] * inv[None, :]
    cos, sin = jnp.cos(ang), jnp.sin(ang)
    n = pos.shape[0]
    half = QK_ROPE // 2
    z = lambda k: jnp.zeros((n, k), F32)
    pad_hi = LANE - QK_NOPE - QK_ROPE
    rc = jnp.concatenate([jnp.ones((n, QK_NOPE), F32), cos, cos, z(pad_hi)], axis=1)
    rs1 = jnp.concatenate([z(QK_NOPE), -sin, z(half), z(pad_hi)], axis=1)
    rs2 = jnp.concatenate([z(QK_NOPE), z(half), sin, z(pad_hi)], axis=1)
    return (rc, rs1, rs2), cos.T, sin.T


def _params(sem):
    return pltpu.CompilerParams(dimension_semantics=sem, vmem_limit_bytes=VMEM_LIMIT)


def _mla_pre(x, w, rope, tm):
    b, s, d = x.shape
    hw = N_HEADS * LANE
    tok = lambda width: pl.BlockSpec((1, tm, width), lambda i, j: (i, j, 0))
    rope_spec = pl.BlockSpec((tm, LANE), lambda i, j: (j, 0))
    consts = [w["gmix"], w["wa"], w["gcq"], w["gckv"], w["gkr"], w["wuq"], w["wuk"], w["wuv"], w["gq"], w["gk"]]
    return pl.pallas_call(
        _pre_kernel,
        grid=(b, s // tm),
        in_specs=[tok(d)] + [_const(c.shape) for c in consts] + [rope_spec] * 3,
        out_specs=[tok(hw), tok(hw), tok(hw), tok(KV_LORA), tok(QK_ROPE)],
        out_shape=[jax.ShapeDtypeStruct((b, s, hw), BF16)] * 3
        + [jax.ShapeDtypeStruct((b, s, KV_LORA), F32), jax.ShapeDtypeStruct((b, s, QK_ROPE), F32)],
        compiler_params=_params(("parallel", "parallel")),
        name="mla_pre",
    )(x, *consts, *rope)


def _mla_pre_t(x, w, rope, cos_t, sin_t, tm):
    b, s, d = x.shape
    hw = N_HEADS * LANE
    tok = lambda width: pl.BlockSpec((1, tm, width), lambda i, j: (i, j, 0))
    tiled = pl.BlockSpec((1, 1, hw, tm), lambda i, j: (i, j, 0, 0))
    rope_spec = pl.BlockSpec((tm, LANE), lambda i, j: (j, 0))
    rope_t_spec = pl.BlockSpec((QK_ROPE // 2, tm), lambda i, j: (0, j))
    gqt = jnp.broadcast_to(w["gq"].T, (LANE, tm))
    consts = [w["gmix"], w["wa"], w["gcq"], w["gckv"], w["gkr"], w["wuq"].T, w["wuk"], w["wuv"].T, gqt, w["gk"]]
    return pl.pallas_call(
        _pre_kernel_t,
        grid=(b, s // tm),
        in_specs=[tok(d)] + [_const(c.shape) for c in consts] + [rope_spec] * 3 + [rope_t_spec] * 2,
        out_specs=[tiled, tok(hw), tiled, tok(KV_LORA), tok(QK_ROPE)],
        out_shape=[jax.ShapeDtypeStruct((b, s // tm, hw, tm), BF16), jax.ShapeDtypeStruct((b, s, hw), BF16),
                   jax.ShapeDtypeStruct((b, s // tm, hw, tm), BF16),
                   jax.ShapeDtypeStruct((b, s, KV_LORA), F32), jax.ShapeDtypeStruct((b, s, QK_ROPE), F32)],
        compiler_params=_params(("parallel", "parallel")),
        name="mla_pre_t",
    )(x, *consts, *rope, cos_t, sin_t)


def _attn_prompt(qt, k, vt):
    b, n_tiles, _, tq = qt.shape
    s = n_tiles * tq
    nh = ATTN_HEADS_PER_STEP
    width = nh * LANE
    return pl.pallas_call(
        functools.partial(_attn_prompt_kernel, tq=tq, nh=nh),
        grid=(b, N_HEADS // nh, n_tiles),
        in_specs=[pl.BlockSpec((1, 1, width, tq), lambda i, h, j: (i, j, h, 0)),
                  pl.BlockSpec((1, s, width), lambda i, h, j: (i, 0, h)),
                  pl.BlockSpec((1, n_tiles, width, tq), lambda i, h, j: (i, 0, h, 0))],
        out_specs=pl.BlockSpec((1, tq, nh * V_DIM), lambda i, h, j: (i, j, h)),
        out_shape=jax.ShapeDtypeStruct((b, s, N_HEADS * V_DIM), BF16),
        scratch_shapes=[pltpu.VMEM((tq, tq), F32)] * (2 * nh) + [pltpu.VMEM((tq, tq), BF16)] * (2 * nh),
        compiler_params=_params(("parallel", "parallel", "arbitrary")),
        name="attn_prompt",
    )(qt, k, vt)


def _attn_sample(q, past_lat, past_kpe, kn, vn, w):
    b, s, _ = q.shape
    t = past_lat.shape[1]
    hw = N_HEADS * LANE
    kpe_slab = jnp.pad(past_kpe, ((0, 0), (0, 0), (QK_NOPE, LANE - QK_NOPE - QK_ROPE)))
    blk = lambda rows, width: pl.BlockSpec((1, rows, width), lambda i: (i, 0, 0))
    consts = [w["wuk"], w["wuv"], w["gk"]]
    return pl.pallas_call(
        _attn_sample_kernel,
        grid=(b,),
        in_specs=[blk(s, hw), blk(t, KV_LORA), blk(t, LANE), blk(s, hw), blk(s, hw)]
        + [_const(c.shape) for c in consts],
        out_specs=pl.BlockSpec((1, s, N_HEADS * V_DIM), lambda i: (i, 0, 0)),
        out_shape=jax.ShapeDtypeStruct((b, s, N_HEADS * V_DIM), BF16),
        compiler_params=_params(("parallel",)),
        name="attn_sample",
    )(q, past_lat, kpe_slab, kn, vn, *consts)


def _post(x_long, attn_long, x_short, attn_short, cinit_short, w, tm, seq):
    b, s, d = x_long.shape
    nchunk = d // LANE
    n_keys = N_EXPERTS * EXPERTS_PER_GROUP
    tiles_per_seq = s // tm
    n_long = b * tiles_per_seq
    n_short = x_short.shape[1] // tm
    long_tile = lambda t: jnp.minimum(t, n_long - 1)
    short_tile = lambda t: jnp.maximum(t - n_long, 0)
    long_tok = lambda width: pl.BlockSpec(
        (1, tm, width), lambda t: (long_tile(t) // tiles_per_seq, long_tile(t) % tiles_per_seq, 0))
    short_tok = lambda width: pl.BlockSpec((1, tm, width), lambda t: (0, short_tile(t), 0))
    cinit_long = jnp.zeros((b, SUBLANE, CONV_DIM), F32)
    hm = tm // 2
    tri = (jnp.arange(hm)[:, None] < jnp.arange(hm)[None, :]).astype(BF16)
    consts = [w["gmix"], w["wb"], w["convw"], w["woa"], w["woc"], w["wo"], w["gffn"], w["wrt"], w["brt"], tri]
    in_specs = [long_tok(d), long_tok(N_HEADS * V_DIM),
                pl.BlockSpec((1, SUBLANE, CONV_DIM), lambda t: (long_tile(t) // tiles_per_seq, 0, 0)),
                short_tok(d), short_tok(N_HEADS * V_DIM),
                pl.BlockSpec((1, 2 * tm, CONV_DIM), lambda t: (short_tile(t), 0, 0))]
    return pl.pallas_call(
        functools.partial(_post_kernel, n_long=n_long, tiles_per_seq=tiles_per_seq, seq=seq),
        grid=(n_long + n_short,),
        in_specs=in_specs + [_const(c.shape) for c in consts],
        out_specs=[pl.BlockSpec((tm * nchunk, LANE), lambda t: (t, 0)),
                   pl.BlockSpec((1, 8, tm), lambda t: (t, 0, 0)),
                   pl.BlockSpec((1, 2, CONV_DIM), lambda t: (long_tile(t) // tiles_per_seq, 0, 0)),
                   pl.BlockSpec((1, tm // seq, 2, CONV_DIM), lambda t: (short_tile(t), 0, 0, 0)),
                   _const((n_keys, 1))],
        out_shape=[jax.ShapeDtypeStruct(((n_long + n_short) * tm * nchunk, LANE), F32),
                   jax.ShapeDtypeStruct((n_long + n_short, 8, tm), F32),
                   jax.ShapeDtypeStruct((b, 2, CONV_DIM), F32),
                   jax.ShapeDtypeStruct((n_short, tm // seq, 2, CONV_DIM), F32),
                   jax.ShapeDtypeStruct((n_keys, 1), F32)],
        scratch_shapes=[pltpu.VMEM((SUBLANE, CONV_DIM), F32)],
        compiler_params=_params(("arbitrary",)),
        name="post",
    )(x_long, attn_long, cinit_long, x_short, attn_short, cinit_short, *consts)


def _route_tables(key, rank, counts, n):
    n_keys = counts.shape[0]
    padded = (counts + MOE_ROWS - 1) // MOE_ROWS * MOE_ROWS
    pend = jnp.cumsum(padded)
    pstart = pend - padded
    ids = jnp.arange(n_keys, dtype=jnp.int32)
    dest = rank + jnp.sum(jnp.where(key[:, None] == ids[None, :], pstart[None, :], 0), axis=1)
    nblk = n // MOE_ROWS + N_PAIR_BUCKETS
    blk_start = jnp.arange(nblk, dtype=jnp.int32) * MOE_ROWS
    blk_hot = (blk_start[:, None] >= pstart[None, :]) & (blk_start[:, None] < pend[None, :])
    blk_key = jnp.sum(jnp.where(blk_hot, ids[None, :], 0), axis=1)
    blk_cnt = jnp.sum(jnp.where(blk_hot, jnp.minimum(counts[None, :] - (blk_start[:, None] - pstart[None, :]),
                                                      MOE_ROWS), 0), axis=1)
    any_hot = jnp.any(blk_hot, axis=1)
    blk_key = jnp.where(any_hot, blk_key, n_keys - 1)
    blk_lo = blk_key // EXPERTS_PER_GROUP
    blk_hi = blk_lo // EXPERTS_PER_GROUP * EXPERTS_PER_GROUP + blk_key % EXPERTS_PER_GROUP
    return blk_lo, blk_hi, blk_cnt.astype(jnp.int32), dest.astype(jnp.int32)


def _moe(x1_all, key, rank, counts, w):
    rows, _ = x1_all.shape
    d = w["gffn"].shape[1]
    nchunk = d // LANE
    n = rows // nchunk
    blk_lo, blk_hi, blk_cnt, dest = _route_tables(key, rank, counts, n)
    nblk = blk_lo.shape[0]
    w13_spec = lambda ref_idx: pl.BlockSpec((1, d, 2 * EXPERT_HIDDEN),
                                            lambda i, lo, hi, cnt, dst: ((lo, hi)[ref_idx][i], 0, 0))
    w2_spec = lambda ref_idx: pl.BlockSpec((1, EXPERT_HIDDEN, d),
                                           lambda i, lo, hi, cnt, dst: ((lo, hi)[ref_idx][i], 0, 0))
    buf = pltpu.VMEM((2, MOE_ROWS * nchunk, LANE), F32)
    grid_spec = pltpu.PrefetchScalarGridSpec(
        num_scalar_prefetch=4,
        grid=(nblk,),
        in_specs=[pl.BlockSpec(memory_space=pl.ANY),
                  pl.BlockSpec((1, d), lambda i, *_: (0, 0)),
                  pl.BlockSpec(w["wr"].shape, lambda i, *_: (0, 0)),
                  pl.BlockSpec(w["br"].shape, lambda i, *_: (0, 0)),
                  w13_spec(0), w13_spec(1), w2_spec(0), w2_spec(1)],
        out_specs=pl.BlockSpec(memory_space=pl.ANY),
        scratch_shapes=[pltpu.SMEM((nblk * MOE_ROWS,), jnp.int32), buf, buf,
                        pltpu.SemaphoreType.DMA((2,)), pltpu.SemaphoreType.DMA((2,))],
    )
    return pl.pallas_call(
        _moe_kernel,
        grid_spec=grid_spec,
        out_shape=jax.ShapeDtypeStruct((rows, LANE), F32),
        compiler_params=_params(("arbitrary",)),
        name="moe",
    )(blk_lo, blk_hi, blk_cnt, dest, x1_all, w["gffn"], w["wr"], w["br"], w["w13"], w["w13"], w["w2"], w["w2"])


def _ple(x2_all, row_off, p, w, tm):
    n, pd = p.shape
    d = w["gple"].shape[1]
    nchunk = d // LANE
    off = row_off // tm
    consts = [w["gple"], w["wpg"], w["wple"]]
    return pl.pallas_call(
        _ple_kernel,
        grid=(n // tm,),
        in_specs=[pl.BlockSpec((tm * nchunk, LANE), lambda i: (off + i, 0)), pl.BlockSpec((tm, pd), lambda i: (i, 0))]
        + [_const(c.shape) for c in consts],
        out_specs=pl.BlockSpec((tm, d), lambda i: (i, 0)),
        out_shape=jax.ShapeDtypeStruct((n, d), F32),
        compiler_params=_params(("parallel",)),
        name="ple",
    )(x2_all, p, *consts)


def _layer(xp, xs, pp, ps, past_lat, past_kpe, past_conv, w):
    bp, sp, d = xp.shape
    bs, ss, _ = xs.shape
    n_p, n_s = bp * sp, bs * ss
    past_len = past_lat.shape[1]
    tm_p, tm_s = min(TOKEN_TILE, sp), min(TOKEN_TILE, n_s)
    tm_post = min(tm_p, tm_s)

    rope, cos_t, sin_t = _rope_slabs(jnp.arange(sp))
    qt, k, vt, lat_p, kpe_p = _mla_pre_t(xp, w, rope, cos_t, sin_t, tm_p)
    attn_p = _attn_prompt(qt, k, vt)

    rope, _, _ = _rope_slabs(past_len + jnp.arange(n_s) % ss)
    xs_rows = xs.reshape(1, n_s, d)
    q, k, v, lat_s, kpe_s = _mla_pre(xs_rows, w, rope, tm_s)
    by_seq = lambda a: a.reshape(bs, ss, a.shape[-1])
    attn_s = _attn_sample(by_seq(q), past_lat, past_kpe, by_seq(k), by_seq(v), w).reshape(1, n_s, -1)
    lat_s, kpe_s = by_seq(lat_s), by_seq(kpe_s)
    in_tiles = lambda a: a.reshape(n_s // tm_post, tm_post, CONV_DIM)
    n_hist = past_conv.shape[1]
    cinit_s = jnp.concatenate(
        [in_tiles(jnp.pad(past_conv[:, n_hist - 1:], ((0, 0), (0, ss - 1), (0, 0)))),
         in_tiles(jnp.pad(past_conv, ((0, 0), (0, ss - n_hist), (0, 0))))], axis=1)

    x1_all, route, conv_p, conv_s, cnt = _post(xp, attn_p, xs_rows, attn_s, cinit_s, w, tm_post, ss)
    conv_s = conv_s.reshape(bs, n_hist, CONV_DIM)
    key = route[:, 0].reshape(-1).astype(jnp.int32)
    rank = route[:, 1].reshape(-1).astype(jnp.int32)
    x2_all = _moe(x1_all, key, rank, cnt[:, 0].astype(jnp.int32), w)
    yp = _ple(x2_all, 0, pp.reshape(n_p, -1), w, min(TOKEN_TILE, n_p)).reshape(bp, sp, d)
    ys = _ple(x2_all, n_p, ps.reshape(n_s, -1), w, min(TOKEN_TILE, n_s)).reshape(bs, ss, d)
    return yp, ys, (lat_p, kpe_p, conv_p, lat_s, kpe_s, conv_s)


def kernel(x_prompt, x_sample, cache_kv_latent, cache_k_rope, state_conv, p_prompt, p_sample,
           g_mix, w_in, g_cq, w_uq, g_qn, g_qr, g_ckv, w_ukv, g_kn, g_kr, w_oa,
           conv_w, w_oc, w_o, g_ffn, w_rg, b_rg, w_re, b_re, w1, w3, w2, g_ple, w_pg, w_ple):
    depth = g_mix.shape[0]
    xp, xs = x_prompt, x_sample
    outs = [[] for _ in range(6)]
    for i in range(depth):
        w = _prep_weights(g_mix[i], w_in[i], g_cq[i], w_uq[i], g_qn[i], g_qr[i], g_ckv[i], w_ukv[i],
                          g_kn[i], g_kr[i], w_oa[i], conv_w[i], w_oc[i], w_o[i], g_ffn[i], w_rg[i], b_rg[i],
                          w_re[i], b_re[i], w1[i], w3[i], w2[i], g_ple[i], w_pg[i], w_ple[i])
        xp, xs, new = _layer(xp, xs, p_prompt[i], p_sample[i], cache_kv_latent[i], cache_k_rope[i], state_conv[i], w)
        for o, a in zip(outs, new):
            o.append(a)
    return (xp, xs) + tuple(jnp.stack(o, axis=0) for o in outs)
```

```python
import functools
import math

import jax
import jax.numpy as jnp
from jax import lax
from jax.experimental import pallas as pl
from jax.experimental.pallas import tpu as pltpu

F32 = jnp.float32
BF16 = jnp.bfloat16

LANE = 128
SUBLANE = 8
TOKEN_TILE = 512
CHUNK = 64
N_HEADS = 8
QK_NOPE = 64
QK_ROPE = 32
V_DIM = 64
Q_LORA = 256
KV_LORA = 256
CONV_DIM = 512
N_GROUPS = 4
EXPERTS_PER_GROUP = 8
N_EXPERTS = N_GROUPS * EXPERTS_PER_GROUP
EXPERT_HIDDEN = 256
ROPE_THETA = 10000.0
EPS = 1e-6
ATTN_SCALE = (QK_NOPE + QK_ROPE) ** -0.5
LOG2E = math.log2(math.e)
NEG_BIG = -1e30
MOE_ROWS = 128
ATTN_HEADS_PER_STEP = 4
PLE_ROW_GROUPS = 2
PLE_RING = 3
N_PAIR_BUCKETS = N_GROUPS * (EXPERTS_PER_GROUP * (EXPERTS_PER_GROUP - 1) // 2)
VMEM_LIMIT = 56 * 1024 * 1024


def _rms_scale(x, n):
    return lax.rsqrt(jnp.sum(x * x, axis=-1, keepdims=True) * (1.0 / n) + EPS)


def _lane_iota(shape):
    return lax.broadcasted_iota(jnp.int32, shape, len(shape) - 1)


def _rope(t, rc, rs1, rs2):
    return t * rc + pltpu.roll(t, LANE - QK_ROPE // 2, 1) * rs1 + pltpu.roll(t, QK_ROPE // 2, 1) * rs2


def _pre_latents(x_ref, gmix_ref, wa_ref, gcq_ref, gckv_ref, gkr_ref, rope_refs, lat_ref, kpe_ref):
    x = x_ref[0]
    h = x * _rms_scale(x, x.shape[-1]) * gmix_ref[...]
    z = jnp.dot(h.astype(BF16), wa_ref[...], preferred_element_type=F32)
    cq = z[:, :Q_LORA]
    ckv = z[:, Q_LORA:Q_LORA + KV_LORA]
    kr = z[:, Q_LORA + KV_LORA:]
    cqn = cq * _rms_scale(cq, Q_LORA) * gcq_ref[...]
    lat = ckv * _rms_scale(ckv, KV_LORA) * gckv_ref[...]
    lat_ref[0] = lat
    krn = kr * _rms_scale(kr, QK_ROPE) * gkr_ref[...]
    kpe = _rope(krn, *(r[...] for r in rope_refs))
    kpe_ref[0] = kpe[:, QK_NOPE:QK_NOPE + QK_ROPE]
    return cqn, lat, kpe


def _store_keys(k_ref, kf, gk, kpe):
    for hd in range(N_HEADS):
        sl = slice(hd * LANE, (hd + 1) * LANE)
        ks = kf[:, sl]
        k_ref[0, :, sl] = (ks * _rms_scale(ks, QK_NOPE) * gk + kpe).astype(BF16)


def _pre_kernel(x_ref, gmix_ref, wa_ref, gcq_ref, gckv_ref, gkr_ref, wuq_ref, wuk_ref, wuv_ref,
                gq_ref, gk_ref, rc_ref, rs1_ref, rs2_ref,
                q_ref, k_ref, v_ref, lat_ref, kpe_ref):
    rope_refs = (rc_ref, rs1_ref, rs2_ref)
    cqn, lat, kpe = _pre_latents(x_ref, gmix_ref, wa_ref, gcq_ref, gckv_ref, gkr_ref, rope_refs, lat_ref, kpe_ref)
    rc, rs1, rs2 = (r[...] for r in rope_refs)
    is_nope = _lane_iota(kpe.shape) < QK_NOPE
    latb = lat.astype(BF16)
    qf = jnp.dot(cqn.astype(BF16), wuq_ref[...], preferred_element_type=F32)
    vf = jnp.dot(latb, wuv_ref[...], preferred_element_type=F32)
    _store_keys(k_ref, jnp.dot(latb, wuk_ref[...], preferred_element_type=F32), gk_ref[...], kpe)
    gq = gq_ref[...]
    ones_hi = jnp.where(is_nope, 0.0, 1.0)
    for hd in range(N_HEADS):
        sl = slice(hd * LANE, (hd + 1) * LANE)
        qs = qf[:, sl]
        sq = qs * qs
        ss_all = jnp.sum(sq, axis=-1, keepdims=True)
        ss_n = jnp.sum(jnp.where(is_nope, sq, 0.0), axis=-1, keepdims=True)
        r = jnp.where(is_nope, lax.rsqrt(ss_n * (1.0 / QK_NOPE) + EPS),
                      lax.rsqrt((ss_all - ss_n) * (1.0 / QK_ROPE) + EPS))
        q_ref[0, :, sl] = _rope(qs * r * gq, rc, rs1, rs2).astype(BF16)
        v_ref[0, :, sl] = (vf[:, sl] + ones_hi).astype(BF16)


def _pre_kernel_t(x_ref, gmix_ref, wa_ref, gcq_ref, gckv_ref, gkr_ref, wuqt_ref, wuk_ref, wuvt_ref,
                  gqt_ref, gk_ref, rc_ref, rs1_ref, rs2_ref, cos_ref, sin_ref,
                  qt_ref, k_ref, vt_ref, lat_ref, kpe_ref):
    rope_refs = (rc_ref, rs1_ref, rs2_ref)
    cqn, lat, kpe = _pre_latents(x_ref, gmix_ref, wa_ref, gcq_ref, gckv_ref, gkr_ref, rope_refs, lat_ref, kpe_ref)
    latb = lat.astype(BF16)
    _store_keys(k_ref, jnp.dot(latb, wuk_ref[...], preferred_element_type=F32), gk_ref[...], kpe)
    tm = cqn.shape[0]
    qft = jnp.dot(wuqt_ref[...], cqn.T.astype(BF16), preferred_element_type=F32)
    vft = jnp.dot(wuvt_ref[...], lat.T.astype(BF16), preferred_element_type=F32)
    gq, cos, sin = gqt_ref[...], cos_ref[...], sin_ref[...]
    half = QK_ROPE // 2
    ones_lo = jnp.where(lax.broadcasted_iota(jnp.int32, (LANE, tm), 0) < V_DIM, 0.0, 1.0)
    pad = jnp.zeros((LANE - QK_NOPE - QK_ROPE, tm), F32)
    for hd in range(N_HEADS):
        rows = slice(hd * LANE, (hd + 1) * LANE)
        qs = qft[rows]
        sq = qs * qs
        r_n = lax.rsqrt(jnp.sum(sq[:QK_NOPE], axis=0, keepdims=True) * (1.0 / QK_NOPE) + EPS)
        r_p = lax.rsqrt(jnp.sum(sq[QK_NOPE:QK_NOPE + QK_ROPE], axis=0, keepdims=True) * (1.0 / QK_ROPE) + EPS)
        nope = qs[:QK_NOPE] * r_n * gq[:QK_NOPE]
        x1 = qs[QK_NOPE:QK_NOPE + half] * r_p * gq[QK_NOPE:QK_NOPE + half]
        x2 = qs[QK_NOPE + half:QK_NOPE + QK_ROPE] * r_p * gq[QK_NOPE + half:QK_NOPE + QK_ROPE]
        slab = jnp.concatenate([nope, x1 * cos - x2 * sin, x1 * sin + x2 * cos, pad], axis=0)
        qt_ref[0, 0, rows, :] = slab.astype(BF16)
        vt_ref[0, 0, rows, :] = (vft[rows] + ones_lo).astype(BF16)


def _finish_pair(accs):
    outs = [a / pltpu.roll(a, V_DIM, 1) for a in accs]
    lane = _lane_iota(outs[0].shape)
    return jnp.where(lane < V_DIM, outs[0], pltpu.roll(outs[1], V_DIM, 1))


def _attn_prompt_kernel(qt_ref, k_ref, vt_ref, o_ref, *scratch, tq, nh):
    s_refs = (scratch[:nh], scratch[nh:2 * nh])
    p_refs = (scratch[2 * nh:3 * nh], scratch[3 * nh:])
    qi = pl.program_id(2)
    key_pos = lax.broadcasted_iota(jnp.int32, (tq, tq), 0)
    query_pos = lax.broadcasted_iota(jnp.int32, (tq, tq), 1)
    diag_mask = (key_pos // CHUNK) <= (query_pos // CHUNK)
    heads = [slice(hh * LANE, (hh + 1) * LANE) for hh in range(nh)]

    def scores(i, slot, hh, mask):
        start = pl.multiple_of(i * tq, tq)
        s = jnp.dot(k_ref[0, pl.ds(start, tq), heads[hh]], qt_ref[0, 0, heads[hh], :], preferred_element_type=F32)
        if mask is not None:
            s = jnp.where(mask, s, NEG_BIG)
        s_refs[slot][hh][...] = s
        return jnp.max(s, axis=0, keepdims=True)

    def softmax(slot, hh, m, tile_max):
        m_new = jnp.maximum(m, tile_max)
        p_refs[slot][hh][...] = jnp.exp2(s_refs[slot][hh][...] - m_new).astype(BF16)
        return m_new, jnp.exp2(m - m_new)

    def accumulate(i, slot, hh, alpha, acc):
        pv = jnp.dot(vt_ref[0, jnp.maximum(i, 0), heads[hh], :], p_refs[slot][hh][...],
                     preferred_element_type=F32)
        return acc * alpha + pv

    def iteration(i, slot, carry, next_mask=None):
        stats = [softmax(slot, hh, m, tile_max) for hh, (m, tile_max, _, _) in enumerate(carry)]
        next_max = [scores(i + 1, 1 - slot, hh, next_mask) for hh in range(nh)]
        accs = [accumulate(i - 1, 1 - slot, hh, alpha, acc) for hh, (_, _, alpha, acc) in enumerate(carry)]
        return tuple((m, tmax, alpha, acc) for (m, alpha), tmax, acc in zip(stats, next_max, accs))

    def last(slot, carry):
        accs = []
        for hh, (m, tile_max, alpha, acc) in enumerate(carry):
            acc = accumulate(qi - 1, 1 - slot, hh, alpha, acc)
            m, alpha = softmax(slot, hh, m, tile_max)
            accs.append(accumulate(qi, slot, hh, alpha, acc))
        return tuple(accs)

    first_mask = diag_mask | (qi > 0)
    init = []
    for hh in range(nh):
        p_refs[1][hh][...] = jnp.zeros((tq, tq), BF16)
        init.append((jnp.full((1, tq), NEG_BIG, F32), scores(0, 0, hh, first_mask), jnp.ones((1, tq), F32),
                     jnp.zeros((LANE, tq), F32)))
    carry = lax.fori_loop(0, (qi - 1) // 2, lambda j, c: iteration(2 * j + 1, 1, iteration(2 * j, 0, c)),
                          tuple(init))
    tails = [lambda c: last(0, c),
             lambda c: last(1, iteration(qi - 1, 0, c, diag_mask)),
             lambda c: last(0, iteration(qi - 1, 1, iteration(qi - 2, 0, c), diag_mask))]
    accs = lax.switch(jnp.where(qi == 0, 0, 2 - qi % 2), tails, carry)
    out_t = jnp.concatenate([a[:V_DIM] / a[V_DIM:V_DIM + 1] for a in accs], axis=0)
    o_ref[0] = out_t.T.astype(o_ref.dtype)


def _attn_sample_kernel(q_ref, lat_ref, kpe_ref, kn_ref, vn_ref, wuk_ref, wuv_ref, gk_ref, o_ref):
    nt = (((1,), (1,)), ((), ()))
    slabs = [slice(hd * LANE, (hd + 1) * LANE) for hd in range(N_HEADS)]
    latb = lat_ref[0].astype(BF16)
    kf = jnp.dot(latb, wuk_ref[...], preferred_element_type=F32)
    vf = jnp.dot(latb, wuv_ref[...], preferred_element_type=F32)
    kpe = kpe_ref[0]
    ones_hi = jnp.where(_lane_iota(kpe.shape) < QK_NOPE, 0.0, 1.0)
    gk = gk_ref[...]
    kp = [(kf[:, sl] * _rms_scale(kf[:, sl], QK_NOPE) * gk + kpe).astype(BF16) for sl in slabs]
    vp = [(vf[:, sl] + ones_hi).astype(BF16) for sl in slabs]
    s_past = [lax.dot_general(q_ref[0, :, sl], k, nt, preferred_element_type=F32) for sl, k in zip(slabs, kp)]
    s_new = [lax.dot_general(q_ref[0, :, sl], kn_ref[0, :, sl], nt, preferred_element_type=F32) for sl in slabs]
    accs = []
    for sl, sp, sn, v in zip(slabs, s_past, s_new, vp):
        m = jnp.maximum(jnp.max(sp, axis=-1, keepdims=True), jnp.max(sn, axis=-1, keepdims=True))
        accs.append(jnp.dot(jnp.exp2(sp - m).astype(BF16), v, preferred_element_type=F32)
                    + jnp.dot(jnp.exp2(sn - m).astype(BF16), vn_ref[0, :, sl], preferred_element_type=F32))
    for pair in range(N_HEADS // 2):
        o_ref[0, :, pair * LANE:(pair + 1) * LANE] = _finish_pair(accs[2 * pair:2 * pair + 2]).astype(o_ref.dtype)


def _post_kernel(xl_ref, al_ref, cl_ref, xs_ref, as_ref, cs_ref, *rest, n_long, tiles_per_seq, seq):
    shared, (x1_ref, route_ref, cnewl_ref, cnews_ref, cnt_ref, carry_ref) = rest[:-6], rest[-6:]
    t = pl.program_id(0)

    @pl.when(t == 0)
    def _():
        cnt_ref[...] = jnp.zeros_like(cnt_ref)

    @pl.when(t < n_long)
    def _():
        _post_tile(xl_ref, al_ref, cl_ref, *shared, x1_ref, route_ref, cnewl_ref, cnt_ref, carry_ref,
                   first=t % tiles_per_seq == 0, seq=None)

    @pl.when(t >= n_long)
    def _():
        _post_tile(xs_ref, as_ref, cs_ref, *shared, x1_ref, route_ref, cnews_ref, cnt_ref, carry_ref,
                   first=None, seq=seq)


def _post_tile(x_ref, attn_ref, cinit_ref, gmix_ref, wb_ref, convw_ref, woa_ref, woc_ref, wo_ref,
               gffn_ref, wrt_ref, brt_ref, tri_ref,
               x1_ref, route_ref, cnew_ref, cnt_ref, carry_ref, *, first, seq):
    tm = x_ref.shape[1]

    if seq is None:
        @pl.when(first)
        def _():
            carry_ref[...] = cinit_ref[0]

    d = x_ref.shape[-1]
    hm = tm // 2
    groups = (0, hm)
    xs, zs = [], []
    for r0 in groups:
        x = x_ref[0, r0:r0 + hm, :]
        h = x * _rms_scale(x, d) * gmix_ref[...]
        xs.append(x)
        zs.append(jnp.dot(h.astype(BF16), wb_ref[...], preferred_element_type=F32))

    cw = convw_ref[...]
    row = lax.broadcasted_iota(jnp.int32, (hm, CONV_DIM), 0)
    if seq is None:
        c1, c2 = carry_ref[SUBLANE - 1:SUBLANE, :], carry_ref[SUBLANE - 2:SUBLANE - 1, :]
    merged = []
    for r0, z in zip(groups, zs):
        conv_b = z[:, :CONV_DIM]
        u = z[:, CONV_DIM:2 * CONV_DIM] * z[:, 2 * CONV_DIM:3 * CONV_DIM]
        if seq is None:
            u_m1 = jnp.where(row == 0, c1, pltpu.roll(u, 1, 0))
            u_m2 = jnp.where(row == 0, c2, jnp.where(row == 1, c1, pltpu.roll(u, 2, 0)))
            c1, c2 = u[hm - 1:, :], u[hm - 2:hm - 1, :]
            if r0 + hm == tm:
                carry_ref[...] = u[hm - SUBLANE:, :]
                cnew_ref[0] = u[hm - 2:, :]
        else:
            pos = row % seq
            u_m1 = jnp.where(pos == 0, cinit_ref[0, r0:r0 + hm, :], pltpu.roll(u, 1, 0))
            u_m2 = jnp.where(pos <= 1, cinit_ref[0, tm + r0:tm + r0 + hm, :], pltpu.roll(u, 2, 0))
            cnew_ref[0, r0 // seq:(r0 + hm) // seq] = u.reshape(hm // seq, seq, CONV_DIM)[:, seq - 2:, :]
        cv = cw[0:1, :] * u_m2 + cw[1:2, :] * u_m1 + cw[2:3, :] * u
        y_a = jnp.dot(attn_ref[0, r0:r0 + hm, :], woa_ref[...], preferred_element_type=F32)
        y_c = jnp.dot((conv_b * cv).astype(BF16), woc_ref[...], preferred_element_type=F32)
        gate_a = z[:, 3 * CONV_DIM:3 * CONV_DIM + d]
        gate_c = z[:, 3 * CONV_DIM + d:]
        merged.append((jax.nn.sigmoid(gate_a) * y_a + jax.nn.sigmoid(gate_c) * y_c).astype(BF16))
    x1s = [x + jnp.dot(mrg, wo_ref[...], preferred_element_type=F32) for x, mrg in zip(xs, merged)]
    nchunk = d // LANE
    for r0, x1 in zip(groups, x1s):
        for c in range(nchunk):
            x1_ref[pl.ds(r0 * nchunk + c, hm, stride=nchunk), :] = x1[:, c * LANE:(c + 1) * LANE]

    nt = (((1,), (1,)), ((), ()))
    n_rows = brt_ref.shape[0]
    logits = []
    for x1 in x1s:
        h2 = x1 * _rms_scale(x1, d) * gffn_ref[...]
        h2_hi = h2.astype(BF16)
        h2_lo = (h2 - h2_hi.astype(F32)).astype(BF16)
        lt2 = lax.dot_general(wrt_ref[...], h2_hi, nt, preferred_element_type=F32)
        logits.append(lt2[:n_rows] + lt2[n_rows:]
                      + lax.dot_general(wrt_ref[:n_rows, :], h2_lo, nt, preferred_element_type=F32) + brt_ref[...])
    sub = lax.broadcasted_iota(jnp.int32, (EXPERTS_PER_GROUP, hm), 0).astype(F32)
    none = float(EXPERTS_PER_GROUP)

    def first_argmax(v):
        vmax = jnp.max(v, axis=0, keepdims=True)
        return jnp.min(jnp.where(v == vmax, sub, none), axis=0, keepdims=True)

    n_keys = cnt_ref.shape[0]
    keys = lax.broadcasted_iota(jnp.int32, (n_keys, hm), 0).astype(F32)
    row8 = lax.broadcasted_iota(jnp.int32, (8, hm), 0)
    seen = cnt_ref[...]
    for r0, lt in zip(groups, logits):
        g_idx = first_argmax(lt[:EXPERTS_PER_GROUP])
        el = lt[EXPERTS_PER_GROUP * N_GROUPS:]
        for g in range(N_GROUPS - 2, -1, -1):
            el = jnp.where(g_idx == g, lt[EXPERTS_PER_GROUP * (g + 1):EXPERTS_PER_GROUP * (g + 2)], el)
        i1 = first_argmax(el)
        i2 = first_argmax(jnp.where(sub == i1, NEG_BIG, el))
        key = (g_idx * (EXPERTS_PER_GROUP * EXPERTS_PER_GROUP) + jnp.minimum(i1, i2) * EXPERTS_PER_GROUP
               + jnp.maximum(i1, i2))
        onehot = jnp.where(keys == key, 1.0, 0.0)
        before = jnp.dot(onehot.astype(BF16), tri_ref[...], preferred_element_type=F32)
        rank = jnp.sum(onehot * (before + seen), axis=0, keepdims=True)
        seen = seen + jnp.sum(onehot, axis=1, keepdims=True)
        route_ref[0, :, r0:r0 + hm] = jnp.where(row8 == 0, key, jnp.where(row8 == 1, rank, 0.0))
    cnt_ref[...] = seen


def _rows_loop(n, fn, unroll=8):
    def group(j, c):
        for u in range(unroll):
            fn(j * unroll + u)
        return c
    lax.fori_loop(0, n // unroll, group, 0)

    def single(i, c):
        fn(i)
        return c
    lax.fori_loop(n // unroll * unroll, n, single, 0)


def _moe_kernel(lo_ref, hi_ref, cnt_ref, dest_ref,
                x1_hbm, gffn_ref, wr_ref, br_ref, w13lo_ref, w13hi_ref, w2lo_ref, w2hi_ref,
                x2_hbm, tok_ref, xbuf, obuf, gsem, ssem):
    nb = pl.program_id(0)
    nblk = pl.num_programs(0)
    nchunk = xbuf.shape[1] // MOE_ROWS
    slot = nb % 2
    cnt = cnt_ref[nb]

    def start_gathers(blk, sl):
        base = blk * MOE_ROWS

        def one(i):
            src = x1_hbm.at[pl.ds(tok_ref[base + i] * nchunk, nchunk)]
            pltpu.make_async_copy(src, xbuf.at[sl, pl.ds(i * nchunk, nchunk)], gsem.at[sl]).start()
        _rows_loop(cnt_ref[blk], one)

    def start_scatters(blk, sl):
        base = blk * MOE_ROWS

        def one(i):
            dst = x2_hbm.at[pl.ds(tok_ref[base + i] * nchunk, nchunk)]
            pltpu.make_async_copy(obuf.at[sl, pl.ds(i * nchunk, nchunk)], dst, ssem.at[sl]).start()
        _rows_loop(cnt_ref[blk], one)

    def wait_gathers(blk, sl):
        rows = cnt_ref[blk] * nchunk
        pltpu.make_async_copy(x1_hbm.at[pl.ds(0, rows)], xbuf.at[sl, pl.ds(0, rows)], gsem.at[sl]).wait()

    def wait_scatters(blk, sl):
        rows = cnt_ref[blk] * nchunk
        pltpu.make_async_copy(obuf.at[sl, pl.ds(0, rows)], x2_hbm.at[pl.ds(0, rows)], ssem.at[sl]).wait()

    @pl.when(nb == 0)
    def _():
        def invert(i, c):
            tok_ref[dest_ref[i]] = i
            return c
        lax.fori_loop(0, dest_ref.shape[0], invert, 0, unroll=16)
        xbuf[...] = jnp.zeros_like(xbuf)

        @pl.when(cnt > 0)
        def _():
            start_gathers(0, 0)

    @pl.when(nb + 1 < nblk)
    def _():
        @pl.when(cnt_ref[nb + 1] > 0)
        def _():
            start_gathers(nb + 1, 1 - slot)

    @pl.when(nb >= 2)
    def _():
        @pl.when(cnt_ref[nb - 2] > 0)
        def _():
            wait_scatters(nb - 2, slot)

    @pl.when(cnt > 0)
    def _():
        wait_gathers(nb, slot)
        xg = jnp.concatenate([xbuf[slot, pl.ds(c, MOE_ROWS, stride=nchunk), :] for c in range(nchunk)], axis=1)
        h = (xg * _rms_scale(xg, xg.shape[-1]) * gffn_ref[...]).astype(BF16)

        lo, hi = lo_ref[nb], hi_ref[nb]
        logit = jnp.dot(h, wr_ref[...], preferred_element_type=F32) + br_ref[...]
        lane = _lane_iota(logit.shape)
        pick = lambda j: jnp.sum(jnp.where(lane == j, logit, 0.0), axis=-1, keepdims=True)
        is_g = lane < N_GROUPS
        gmax = jnp.max(jnp.where(is_g, logit, NEG_BIG), axis=-1, keepdims=True)
        g_den = jnp.sum(jnp.where(is_g, jnp.exp(logit - gmax), 0.0), axis=-1, keepdims=True)
        g_p = jnp.exp(pick(lo // EXPERTS_PER_GROUP) - gmax) / g_den
        l_lo, l_hi = pick(N_GROUPS + lo), pick(N_GROUPS + hi)
        gates = (g_p * jax.nn.sigmoid(l_lo - l_hi), g_p * jax.nn.sigmoid(l_hi - l_lo))

        abs_ = [jnp.dot(h, w13_ref[0], preferred_element_type=F32) for w13_ref in (w13lo_ref, w13hi_ref)]
        y = xg
        for ab, w2_ref, gate in zip(abs_, (w2lo_ref, w2hi_ref), gates):
            hid = jax.nn.silu(ab[:, :EXPERT_HIDDEN]) * ab[:, EXPERT_HIDDEN:]
            y = y + gate * jnp.dot(hid.astype(BF16), w2_ref[0], preferred_element_type=F32)
        for c in range(nchunk):
            obuf[slot, pl.ds(c, MOE_ROWS, stride=nchunk), :] = y[:, c * LANE:(c + 1) * LANE]
        start_scatters(nb, slot)

    @pl.when(nb == nblk - 1)
    def _():
        @pl.when(nb >= 1)
        def _():
            @pl.when(cnt_ref[nb - 1] > 0)
            def _():
                wait_scatters(nb - 1, 1 - slot)

        @pl.when(cnt > 0)
        def _():
            wait_scatters(nb, slot)


def _ple_kernel(x_hbm, p_ref, gple_ref, wpg_ref, wple_ref, o_ref, ring, sem, *, first_tile):
    tm = o_ref.shape[0]
    nchunk = o_ref.shape[1] // LANE
    s = pl.program_id(0)
    n_steps = pl.num_programs(0)
    tile_rows = tm * nchunk

    def fetch(step):
        slot = step % PLE_RING
        return pltpu.make_async_copy(x_hbm.at[pl.ds((first_tile + step) * tile_rows, tile_rows)],
                                     ring.at[slot], sem.at[slot])

    @pl.when(s == 0)
    def _():
        for ahead in range(PLE_RING - 1):
            @pl.when(ahead < n_steps)
            def _(ahead=ahead):
                fetch(ahead).start()

    @pl.when(s + PLE_RING - 1 < n_steps)
    def _():
        fetch(s + PLE_RING - 1).start()

    fetch(s).wait()
    slot = s % PLE_RING
    emb = jnp.dot(p_ref[...].astype(BF16), wple_ref[...], preferred_element_type=F32)
    rows = tm // PLE_ROW_GROUPS
    for r0 in range(0, tm, rows):
        x = jnp.concatenate([ring[slot, pl.ds(r0 * nchunk + c, rows, stride=nchunk), :] for c in range(nchunk)],
                            axis=1)
        hp = (x * _rms_scale(x, x.shape[-1]) * gple_ref[...]).astype(BF16)
        gate = jax.nn.sigmoid(jnp.dot(hp, wpg_ref[...], preferred_element_type=F32))
        o_ref[r0:r0 + rows, :] = x + gate * emb[r0:r0 + rows]


def _const(shape):
    nd = len(shape)
    return pl.BlockSpec(shape, lambda *_: (0,) * nd)


def _head_slab_cols(w, width, offset=0):
    k = w.shape[0]
    w = w.reshape(k, N_HEADS, width)
    w = jnp.pad(w, ((0, 0), (0, 0), (offset, LANE - width - offset)))
    return w.reshape(k, N_HEADS * LANE)


def _prep_weights(g_mix, w_in, g_cq, w_uq, g_qn, g_qr, g_ckv, w_ukv, g_kn, g_kr, w_oa,
                  conv_w, w_oc, w_o, g_ffn, w_rg, b_rg, w_re, b_re, w1, w3, w2, g_ple, w_pg, w_ple):
    d = w_in.shape[0]
    n_mla = Q_LORA + KV_LORA
    kr_cols = jnp.pad(w_in[:, n_mla:n_mla + QK_ROPE], ((0, 0), (QK_NOPE, LANE - QK_NOPE - QK_ROPE)))
    w = {}
    w["wa"] = jnp.concatenate([w_in[:, :n_mla], kr_cols], axis=1).astype(BF16)
    w["wb"] = w_in[:, n_mla + QK_ROPE:].astype(BF16)
    w["wuq"] = _head_slab_cols(w_uq, QK_NOPE + QK_ROPE).astype(BF16)
    ukv = w_ukv.reshape(KV_LORA, N_HEADS, QK_NOPE + V_DIM)
    w["wuk"] = _head_slab_cols(ukv[:, :, :QK_NOPE].reshape(KV_LORA, -1), QK_NOPE).astype(BF16)
    w["wuv"] = _head_slab_cols(ukv[:, :, QK_NOPE:].reshape(KV_LORA, -1), V_DIM).astype(BF16)
    pad_hi = LANE - QK_NOPE - QK_ROPE
    w["gq"] = (jnp.pad(jnp.concatenate([g_qn, g_qr]), (0, pad_hi)) * (ATTN_SCALE * LOG2E))[None]
    w["gk"] = jnp.pad(g_kn, (0, LANE - QK_NOPE))[None]
    w["gkr"] = jnp.pad(g_kr, (QK_NOPE, pad_hi))[None]
    w["gmix"], w["gcq"], w["gckv"] = g_mix[None], g_cq[None], g_ckv[None]
    w["gffn"], w["gple"] = g_ffn[None], g_ple[None]
    w["convw"] = jnp.pad(conv_w, ((0, 8 - conv_w.shape[0]), (0, 0)))
    w["woa"], w["woc"], w["wo"] = w_oa.astype(BF16), w_oc.astype(BF16), w_o.astype(BF16)
    n_r = N_GROUPS + N_EXPERTS
    w["wr"] = jnp.pad(jnp.concatenate([w_rg, w_re], axis=1), ((0, 0), (0, LANE - n_r))).astype(BF16)
    w["br"] = jnp.pad(jnp.concatenate([b_rg, b_re]), (0, LANE - n_r))[None]
    pad_g = EXPERTS_PER_GROUP - N_GROUPS
    wrt = jnp.concatenate([jnp.pad(w_rg.T, ((0, pad_g), (0, 0))), w_re.T], axis=0)
    wrt_hi = wrt.astype(BF16)
    w["wrt"] = jnp.concatenate([wrt_hi, (wrt - wrt_hi.astype(F32)).astype(BF16)], axis=0)
    w["brt"] = jnp.concatenate([b_rg, jnp.full((pad_g,), NEG_BIG, F32), b_re])[:, None]
    w["w13"] = jnp.concatenate([w1, w3], axis=2).astype(BF16)
    w["w2"] = w2.astype(BF16)
    w["wpg"], w["wple"] = w_pg.astype(BF16), w_ple.astype(BF16)
    return w


def _rope_slabs(pos):
    inv = 1.0 / (ROPE_THETA ** (jnp.arange(0, QK_ROPE, 2, dtype=F32) / QK_ROPE))
    ang = pos.astype(F32)[:, None] * inv[None, :]
    cos, sin = jnp.cos(ang), jnp.sin(ang)
    n = pos.shape[0]
    half = QK_ROPE // 2
    z = lambda k: jnp.zeros((n, k), F32)
    pad_hi = LANE - QK_NOPE - QK_ROPE
    rc = jnp.concatenate([jnp.ones((n, QK_NOPE), F32), cos, cos, z(pad_hi)], axis=1)
    rs1 = jnp.concatenate([z(QK_NOPE), -sin, z(half), z(pad_hi)], axis=1)
    rs2 = jnp.concatenate([z(QK_NOPE), z(half), sin, z(pad_hi)], axis=1)
    return (rc, rs1, rs2), cos.T, sin.T


def _params(sem):
    return pltpu.CompilerParams(dimension_semantics=sem, vmem_limit_bytes=VMEM_LIMIT)


def _mla_pre(x, w, rope, tm):
    b, s, d = x.shape
    hw = N_HEADS * LANE
    tok = lambda width: pl.BlockSpec((1, tm, width), lambda i, j: (i, j, 0))
    rope_spec = pl.BlockSpec((tm, LANE), lambda i, j: (j, 0))
    consts = [w["gmix"], w["wa"], w["gcq"], w["gckv"], w["gkr"], w["wuq"], w["wuk"], w["wuv"], w["gq"], w["gk"]]
    return pl.pallas_call(
        _pre_kernel,
        grid=(b, s // tm),
        in_specs=[tok(d)] + [_const(c.shape) for c in consts] + [rope_spec] * 3,
        out_specs=[tok(hw), tok(hw), tok(hw), tok(KV_LORA), tok(QK_ROPE)],
        out_shape=[jax.ShapeDtypeStruct((b, s, hw), BF16)] * 3
        + [jax.ShapeDtypeStruct((b, s, KV_LORA), F32), jax.ShapeDtypeStruct((b, s, QK_ROPE), F32)],
        compiler_params=_params(("parallel", "parallel")),
        name="mla_pre",
    )(x, *consts, *rope)


def _mla_pre_t(x, w, rope, cos_t, sin_t, tm):
    b, s, d = x.shape
    hw = N_HEADS * LANE
    tok = lambda width: pl.BlockSpec((1, tm, width), lambda i, j: (i, j, 0))
    tiled = pl.BlockSpec((1, 1, hw, tm), lambda i, j: (i, j, 0, 0))
    rope_spec = pl.BlockSpec((tm, LANE), lambda i, j: (j, 0))
    rope_t_spec = pl.BlockSpec((QK_ROPE // 2, tm), lambda i, j: (0, j))
    gqt = jnp.broadcast_to(w["gq"].T, (LANE, tm))
    consts = [w["gmix"], w["wa"], w["gcq"], w["gckv"], w["gkr"], w["wuq"].T, w["wuk"], w["wuv"].T, gqt, w["gk"]]
    return pl.pallas_call(
        _pre_kernel_t,
        grid=(b, s // tm),
        in_specs=[tok(d)] + [_const(c.shape) for c in consts] + [rope_spec] * 3 + [rope_t_spec] * 2,
        out_specs=[tiled, tok(hw), tiled, tok(KV_LORA), tok(QK_ROPE)],
        out_shape=[jax.ShapeDtypeStruct((b, s // tm, hw, tm), BF16), jax.ShapeDtypeStruct((b, s, hw), BF16),
                   jax.ShapeDtypeStruct((b, s // tm, hw, tm), BF16),
                   jax.ShapeDtypeStruct((b, s, KV_LORA), F32), jax.ShapeDtypeStruct((b, s, QK_ROPE), F32)],
        compiler_params=_params(("parallel", "parallel")),
        name="mla_pre_t",
    )(x, *consts, *rope, cos_t, sin_t)


def _attn_prompt(qt, k, vt):
    b, n_tiles, _, tq = qt.shape
    s = n_tiles * tq
    nh = ATTN_HEADS_PER_STEP
    width = nh * LANE
    return pl.pallas_call(
        functools.partial(_attn_prompt_kernel, tq=tq, nh=nh),
        grid=(b, N_HEADS // nh, n_tiles),
        in_specs=[pl.BlockSpec((1, 1, width, tq), lambda i, h, j: (i, j, h, 0)),
                  pl.BlockSpec((1, s, width), lambda i, h, j: (i, 0, h)),
                  pl.BlockSpec((1, n_tiles, width, tq), lambda i, h, j: (i, 0, h, 0))],
        out_specs=pl.BlockSpec((1, tq, nh * V_DIM), lambda i, h, j: (i, j, h)),
        out_shape=jax.ShapeDtypeStruct((b, s, N_HEADS * V_DIM), BF16),
        scratch_shapes=[pltpu.VMEM((tq, tq), F32)] * (2 * nh) + [pltpu.VMEM((tq, tq), BF16)] * (2 * nh),
        compiler_params=_params(("parallel", "parallel", "arbitrary")),
        name="attn_prompt",
    )(qt, k, vt)


def _attn_sample(q, past_lat, past_kpe, kn, vn, w):
    b, s, _ = q.shape
    t = past_lat.shape[1]
    hw = N_HEADS * LANE
    kpe_slab = jnp.pad(past_kpe, ((0, 0), (0, 0), (QK_NOPE, LANE - QK_NOPE - QK_ROPE)))
    blk = lambda rows, width: pl.BlockSpec((1, rows, width), lambda i: (i, 0, 0))
    consts = [w["wuk"], w["wuv"], w["gk"]]
    return pl.pallas_call(
        _attn_sample_kernel,
        grid=(b,),
        in_specs=[blk(s, hw), blk(t, KV_LORA), blk(t, LANE), blk(s, hw), blk(s, hw)]
        + [_const(c.shape) for c in consts],
        out_specs=pl.BlockSpec((1, s, N_HEADS * V_DIM), lambda i: (i, 0, 0)),
        out_shape=jax.ShapeDtypeStruct((b, s, N_HEADS * V_DIM), BF16),
        compiler_params=_params(("parallel",)),
        name="attn_sample",
    )(q, past_lat, kpe_slab, kn, vn, *consts)


def _post(x_long, attn_long, x_short, attn_short, cinit_short, w, tm, seq):
    b, s, d = x_long.shape
    nchunk = d // LANE
    n_keys = N_EXPERTS * EXPERTS_PER_GROUP
    tiles_per_seq = s // tm
    n_long = b * tiles_per_seq
    n_short = x_short.shape[1] // tm
    long_tile = lambda t: jnp.minimum(t, n_long - 1)
    short_tile = lambda t: jnp.maximum(t - n_long, 0)
    long_tok = lambda width: pl.BlockSpec(
        (1, tm, width), lambda t: (long_tile(t) // tiles_per_seq, long_tile(t) % tiles_per_seq, 0))
    short_tok = lambda width: pl.BlockSpec((1, tm, width), lambda t: (0, short_tile(t), 0))
    cinit_long = jnp.zeros((b, SUBLANE, CONV_DIM), F32)
    hm = tm // 2
    tri = (jnp.arange(hm)[:, None] < jnp.arange(hm)[None, :]).astype(BF16)
    consts = [w["gmix"], w["wb"], w["convw"], w["woa"], w["woc"], w["wo"], w["gffn"], w["wrt"], w["brt"], tri]
    in_specs = [long_tok(d), long_tok(N_HEADS * V_DIM),
                pl.BlockSpec((1, SUBLANE, CONV_DIM), lambda t: (long_tile(t) // tiles_per_seq, 0, 0)),
                short_tok(d), short_tok(N_HEADS * V_DIM),
                pl.BlockSpec((1, 2 * tm, CONV_DIM), lambda t: (short_tile(t), 0, 0))]
    return pl.pallas_call(
        functools.partial(_post_kernel, n_long=n_long, tiles_per_seq=tiles_per_seq, seq=seq),
        grid=(n_long + n_short,),
        in_specs=in_specs + [_const(c.shape) for c in consts],
        out_specs=[pl.BlockSpec((tm * nchunk, LANE), lambda t: (t, 0)),
                   pl.BlockSpec((1, 8, tm), lambda t: (t, 0, 0)),
                   pl.BlockSpec((1, 2, CONV_DIM), lambda t: (long_tile(t) // tiles_per_seq, 0, 0)),
                   pl.BlockSpec((1, tm // seq, 2, CONV_DIM), lambda t: (short_tile(t), 0, 0, 0)),
                   _const((n_keys, 1))],
        out_shape=[jax.ShapeDtypeStruct(((n_long + n_short) * tm * nchunk, LANE), F32),
                   jax.ShapeDtypeStruct((n_long + n_short, 8, tm), F32),
                   jax.ShapeDtypeStruct((b, 2, CONV_DIM), F32),
                   jax.ShapeDtypeStruct((n_short, tm // seq, 2, CONV_DIM), F32),
                   jax.ShapeDtypeStruct((n_keys, 1), F32)],
        scratch_shapes=[pltpu.VMEM((SUBLANE, CONV_DIM), F32)],
        compiler_params=_params(("arbitrary",)),
        name="post",
    )(x_long, attn_long, cinit_long, x_short, attn_short, cinit_short, *consts)


def _route_tables(key, rank, counts, n):
    n_keys = counts.shape[0]
    padded = (counts + MOE_ROWS - 1) // MOE_ROWS * MOE_ROWS
    pend = jnp.cumsum(padded)
    pstart = pend - padded
    ids = jnp.arange(n_keys, dtype=jnp.int32)
    dest = rank + jnp.sum(jnp.where(key[:, None] == ids[None, :], pstart[None, :], 0), axis=1)
    nblk = n // MOE_ROWS + N_PAIR_BUCKETS
    blk_start = jnp.arange(nblk, dtype=jnp.int32) * MOE_ROWS
    blk_hot = (blk_start[:, None] >= pstart[None, :]) & (blk_start[:, None] < pend[None, :])
    blk_key = jnp.sum(jnp.where(blk_hot, ids[None, :], 0), axis=1)
    blk_cnt = jnp.sum(jnp.where(blk_hot, jnp.minimum(counts[None, :] - (blk_start[:, None] - pstart[None, :]),
                                                      MOE_ROWS), 0), axis=1)
    any_hot = jnp.any(blk_hot, axis=1)
    blk_key = jnp.where(any_hot, blk_key, n_keys - 1)
    blk_lo = blk_key // EXPERTS_PER_GROUP
    blk_hi = blk_lo // EXPERTS_PER_GROUP * EXPERTS_PER_GROUP + blk_key % EXPERTS_PER_GROUP
    return blk_lo, blk_hi, blk_cnt.astype(jnp.int32), dest.astype(jnp.int32)


def _moe(x1_all, key, rank, counts, w):
    rows, _ = x1_all.shape
    d = w["gffn"].shape[1]
    nchunk = d // LANE
    n = rows // nchunk
    blk_lo, blk_hi, blk_cnt, dest = _route_tables(key, rank, counts, n)
    nblk = blk_lo.shape[0]
    w13_spec = lambda ref_idx: pl.BlockSpec((1, d, 2 * EXPERT_HIDDEN),
                                            lambda i, lo, hi, cnt, dst: ((lo, hi)[ref_idx][i], 0, 0))
    w2_spec = lambda ref_idx: pl.BlockSpec((1, EXPERT_HIDDEN, d),
                                           lambda i, lo, hi, cnt, dst: ((lo, hi)[ref_idx][i], 0, 0))
    buf = pltpu.VMEM((2, MOE_ROWS * nchunk, LANE), F32)
    grid_spec = pltpu.PrefetchScalarGridSpec(
        num_scalar_prefetch=4,
        grid=(nblk,),
        in_specs=[pl.BlockSpec(memory_space=pl.ANY),
                  pl.BlockSpec((1, d), lambda i, *_: (0, 0)),
                  pl.BlockSpec(w["wr"].shape, lambda i, *_: (0, 0)),
                  pl.BlockSpec(w["br"].shape, lambda i, *_: (0, 0)),
                  w13_spec(0), w13_spec(1), w2_spec(0), w2_spec(1)],
        out_specs=pl.BlockSpec(memory_space=pl.ANY),
        scratch_shapes=[pltpu.SMEM((nblk * MOE_ROWS,), jnp.int32), buf, buf,
                        pltpu.SemaphoreType.DMA((2,)), pltpu.SemaphoreType.DMA((2,))],
    )
    return pl.pallas_call(
        _moe_kernel,
        grid_spec=grid_spec,
        out_shape=jax.ShapeDtypeStruct((rows, LANE), F32),
        compiler_params=_params(("arbitrary",)),
        name="moe",
    )(blk_lo, blk_hi, blk_cnt, dest, x1_all, w["gffn"], w["wr"], w["br"], w["w13"], w["w13"], w["w2"], w["w2"])


def _ple(x2_all, row_off, p, w, tm):
    n, pd = p.shape
    d = w["gple"].shape[1]
    nchunk = d // LANE
    consts = [w["gple"], w["wpg"], w["wple"]]
    return pl.pallas_call(
        functools.partial(_ple_kernel, first_tile=row_off // tm),
        grid=(n // tm,),
        in_specs=[pl.BlockSpec(memory_space=pl.ANY), pl.BlockSpec((tm, pd), lambda i: (i, 0))]
        + [_const(c.shape) for c in consts],
        out_specs=pl.BlockSpec((tm, d), lambda i: (i, 0)),
        out_shape=jax.ShapeDtypeStruct((n, d), F32),
        scratch_shapes=[pltpu.VMEM((PLE_RING, tm * nchunk, LANE), F32), pltpu.SemaphoreType.DMA((PLE_RING,))],
        compiler_params=_params(("arbitrary",)),
        name="ple",
    )(x2_all, p, *consts)


def _layer(xp, xs, pp, ps, past_lat, past_kpe, past_conv, w):
    bp, sp, d = xp.shape
    bs, ss, _ = xs.shape
    n_p, n_s = bp * sp, bs * ss
    past_len = past_lat.shape[1]
    tm_p, tm_s = min(TOKEN_TILE, sp), min(TOKEN_TILE, n_s)
    tm_post = min(tm_p, tm_s)

    rope, cos_t, sin_t = _rope_slabs(jnp.arange(sp))
    qt, k, vt, lat_p, kpe_p = _mla_pre_t(xp, w, rope, cos_t, sin_t, tm_p)
    attn_p = _attn_prompt(qt, k, vt)

    rope, _, _ = _rope_slabs(past_len + jnp.arange(n_s) % ss)
    xs_rows = xs.reshape(1, n_s, d)
    q, k, v, lat_s, kpe_s = _mla_pre(xs_rows, w, rope, tm_s)
    by_seq = lambda a: a.reshape(bs, ss, a.shape[-1])
    attn_s = _attn_sample(by_seq(q), past_lat, past_kpe, by_seq(k), by_seq(v), w).reshape(1, n_s, -1)
    lat_s, kpe_s = by_seq(lat_s), by_seq(kpe_s)
    in_tiles = lambda a: a.reshape(n_s // tm_post, tm_post, CONV_DIM)
    n_hist = past_conv.shape[1]
    cinit_s = jnp.concatenate(
        [in_tiles(jnp.pad(past_conv[:, n_hist - 1:], ((0, 0), (0, ss - 1), (0, 0)))),
         in_tiles(jnp.pad(past_conv, ((0, 0), (0, ss - n_hist), (0, 0))))], axis=1)

    x1_all, route, conv_p, conv_s, cnt = _post(xp, attn_p, xs_rows, attn_s, cinit_s, w, tm_post, ss)
    conv_s = conv_s.reshape(bs, n_hist, CONV_DIM)
    key = route[:, 0].reshape(-1).astype(jnp.int32)
    rank = route[:, 1].reshape(-1).astype(jnp.int32)
    x2_all = _moe(x1_all, key, rank, cnt[:, 0].astype(jnp.int32), w)
    yp = _ple(x2_all, 0, pp.reshape(n_p, -1), w, min(TOKEN_TILE, n_p)).reshape(bp, sp, d)
    ys = _ple(x2_all, n_p, ps.reshape(n_s, -1), w, min(TOKEN_TILE, n_s)).reshape(bs, ss, d)
    return yp, ys, (lat_p, kpe_p, conv_p, lat_s, kpe_s, conv_s)


def kernel(x_prompt, x_sample, cache_kv_latent, cache_k_rope, state_conv, p_prompt, p_sample,
           g_mix, w_in, g_cq, w_uq, g_qn, g_qr, g_ckv, w_ukv, g_kn, g_kr, w_oa,
           conv_w, w_oc, w_o, g_ffn, w_rg, b_rg, w_re, b_re, w1, w3, w2, g_ple, w_pg, w_ple):
    depth = g_mix.shape[0]
    xp, xs = x_prompt, x_sample
    outs = [[] for _ in range(6)]
    for i in range(depth):
        w = _prep_weights(g_mix[i], w_in[i], g_cq[i], w_uq[i], g_qn[i], g_qr[i], g_ckv[i], w_ukv[i],
                          g_kn[i], g_kr[i], w_oa[i], conv_w[i], w_oc[i], w_o[i], g_ffn[i], w_rg[i], b_rg[i],
                          w_re[i], b_re[i], w1[i], w3[i], w2[i], g_ple[i], w_pg[i], w_ple[i])
        xp, xs, new = _layer(xp, xs, p_prompt[i], p_sample[i], cache_kv_latent[i], cache_k_rope[i], state_conv[i], w)
        for o, a in zip(outs, new):
            o.append(a)
    return (xp, xs) + tuple(jnp.stack(o, axis=0) for o in outs)
```
